```python
import jax
import jax.numpy as jnp
from jax import lax
import numpy as np

D_MODEL = 1024
BATCH = 8
SEQ = 2048
DEPTH = 4
DEC_BATCH = 32
DEC_SEQ = 4
PAST_LEN = 8192
PAGE_SIZE = 128

N_A_LAYERS = DEPTH // 2
N_B_LAYERS = DEPTH - N_A_LAYERS
N_DENSE = (DEPTH + 1) // 2
N_MOE = DEPTH // 2
CONV_W = 3
HEAD_DIM = 128
ROT_DIM = HEAD_DIM // 4
ROPE_THETA = 500000.0
DIL_GROUPS = ((128, 1), (512, 4), (2048, 16))
N_GROUPS = len(DIL_GROUPS)
HQ_G = 8
HKV_G = 2
Q_WIDTH = N_GROUPS * HQ_G * HEAD_DIM
KV_WIDTH = N_GROUPS * HKV_G * HEAD_DIM
O_WIDTH = HQ_G * HEAD_DIM
D_FF = 2816
N_EXPERTS = 8
TOP_K = 2
D_FF_E = 3584
Q_BLOCK = 128
EPS = 1e-6
ATTN_SCALE = HEAD_DIM ** -0.5
NEG = -1e30

kernel_name = 'yoco_shortconv_dilated_swa_step'


def _rmsnorm(x, g):
    xf = x.astype(jnp.float32)
    y = xf * lax.rsqrt(jnp.mean(xf * xf, axis=-1, keepdims=True) + EPS)
    return (y * g.astype(jnp.float32)).astype(x.dtype)


def _ada(c, w, b):
    return (jax.nn.silu(c) @ w + b)[:, None, :]


def _modulate(h, shift, scale):
    return h * (1 + scale) + shift


def _rope_partial(x, pos):
    half = ROT_DIM // 2
    inv = jnp.float32(ROPE_THETA) ** (-jnp.arange(half, dtype=jnp.float32) / half)
    ang = pos.astype(jnp.float32)[:, None] * inv[None, :]
    cos = jnp.cos(ang)[None, :, None, :]
    sin = jnp.sin(ang)[None, :, None, :]
    xr = x[..., :ROT_DIM].astype(jnp.float32)
    x1, x2 = xr[..., :half], xr[..., half:]
    rot = jnp.concatenate([x1 * cos - x2 * sin, x2 * cos + x1 * sin], axis=-1)
    return jnp.concatenate([rot.astype(x.dtype), x[..., ROT_DIM:]], axis=-1)


def _short_conv(h, prev, w_in, w_conv, w_out):
    t = h.shape[1]
    b_gate, c_gate, v = jnp.split(h @ w_in, 3, axis=-1)
    u = c_gate * v
    u_ext = jnp.concatenate([prev.astype(u.dtype), u], axis=1)
    y = u_ext[:, 0:t] * w_conv[0]
    for k in range(1, CONV_W):
        y = y + u_ext[:, k:k + t] * w_conv[k]
    return (b_gate * y) @ w_out, u_ext[:, t:]


def _swiglu(h, w_gu, w_down):
    g, u = jnp.split(h @ w_gu, 2, axis=-1)
    return (jax.nn.silu(g) * u) @ w_down


def _moe(h, w_router, b_router, w_gu_e, w_down_e):
    logits = (h @ w_router).astype(jnp.float32) + b_router.astype(jnp.float32)
    top_v, top_i = lax.top_k(logits, TOP_K)
    top_w = jax.nn.softmax(top_v, axis=-1)
    gate = jnp.einsum('btk,btke->bte', top_w,
                      jax.nn.one_hot(top_i, N_EXPERTS, dtype=jnp.float32)).astype(h.dtype)
    out = jnp.zeros_like(h)
    for e in range(N_EXPERTS):
        out = out + gate[..., e:e + 1] * _swiglu(h, w_gu_e[e], w_down_e[e])
    return out


def _shared_kv(x, c, pos, g_norm_kv, w_ada_kv, b_ada_kv, w_kv, g_k):
    b, t, _ = x.shape
    shift, scale = jnp.split(_ada(c, w_ada_kv, b_ada_kv), 2, axis=-1)
    h = _modulate(_rmsnorm(x, g_norm_kv), shift, scale)
    k, v = jnp.split(h @ w_kv, 2, axis=-1)
    k = _rope_partial(_rmsnorm(k.reshape(b, t, N_GROUPS * HKV_G, HEAD_DIM), g_k), pos)
    v = v.reshape(b, t, N_GROUPS * HKV_G, HEAD_DIM)
    ks = [k[:, :, g * HKV_G:(g + 1) * HKV_G] for g in range(N_GROUPS)]
    vs = [v[:, :, g * HKV_G:(g + 1) * HKV_G] for g in range(N_GROUPS)]
    return ks, vs


def _dilated_block(q, q_rel, ks, vs, offsets):
    b, tq = q.shape[:2]
    qg = q.reshape(b, tq, N_GROUPS, HKV_G, HQ_G // HKV_G, HEAD_DIM)
    outs, lses = [], []
    for g, (window, dil) in enumerate(DIL_GROUPS):
        n_keys = window // dil + 1
        idx = offsets[g] + q_rel[:, None] - dil * jnp.arange(n_keys, dtype=jnp.int32)[None, :]
        valid = idx >= 0
        idx = jnp.maximum(idx, 0)
        kg = jnp.take(ks[g], idx, axis=1)
        vg = jnp.take(vs[g], idx, axis=1)
        s = jnp.einsum('btkrd,btjkd->btkrj', qg[:, :, g], kg).astype(jnp.float32) * ATTN_SCALE
        s = jnp.where(valid[None, :, None, None, :], s, NEG)
        m = jnp.max(s, axis=-1, keepdims=True)
        p = jnp.exp(s - m)
        den = jnp.sum(p, axis=-1, keepdims=True)
        outs.append(jnp.einsum('btkrj,btjkd->btkrd', (p / den).astype(vg.dtype), vg))
        lses.append(m[..., 0] + jnp.log(den[..., 0]))
    alpha = jax.nn.softmax(jnp.stack(lses, axis=0), axis=0)
    o = jnp.einsum('gbtkr,gbtkrd->btkrd', alpha.astype(outs[0].dtype), jnp.stack(outs, axis=0))
    return o.reshape(b, tq, O_WIDTH)


def _dilated_mixer(h, pos, ks, vs, offsets, w_q, g_q, w_o):
    b, t, _ = h.shape
    q = _rope_partial(_rmsnorm((h @ w_q).reshape(b, t, N_GROUPS * HQ_G, HEAD_DIM), g_q), pos)
    rel = jnp.arange(t, dtype=jnp.int32)
    if t % Q_BLOCK == 0:
        nb = t // Q_BLOCK
        qb = jnp.moveaxis(q.reshape(b, nb, Q_BLOCK, N_GROUPS * HQ_G, HEAD_DIM), 1, 0)
        ob = lax.map(lambda a: _dilated_block(a[0], a[1], ks, vs, offsets),
                     (qb, rel.reshape(nb, Q_BLOCK)))
        o = jnp.moveaxis(ob, 0, 1).reshape(b, t, O_WIDTH)
    else:
        o = _dilated_block(q, rel, ks, vs, offsets)
    return o @ w_o


def _trunk(x, c, pos, conv_prev, kv_prev, p):
    (g_norm_mix, g_norm_ffn, w_ada, b_ada, w_in, w_conv, w_out_conv, g_norm_kv, w_ada_kv, b_ada_kv,
     w_kv, g_k, w_q, g_q, w_o, w_gu_dense, w_down_dense, w_router, b_router, w_gu_moe, w_down_moe) = p
    t = x.shape[1]
    conv_new = []
    ks = vs = offsets = kv_state = None
    for layer in range(DEPTH):
        if layer == N_A_LAYERS:
            k_new, v_new = _shared_kv(x, c, pos, g_norm_kv, w_ada_kv, b_ada_kv, w_kv, g_k)
            if kv_prev is None:
                ks, vs = k_new, v_new
                offsets = (0,) * N_GROUPS
                kv_state = ([k[:, -min(w, t):] for k, (w, _) in zip(ks, DIL_GROUPS)],
                            [v[:, -min(w, t):] for v, (w, _) in zip(vs, DIL_GROUPS)])
            else:
                ks = [jnp.concatenate([kp.astype(kn.dtype), kn], axis=1) for kp, kn in zip(kv_prev[0], k_new)]
                vs = [jnp.concatenate([vp.astype(vn.dtype), vn], axis=1) for vp, vn in zip(kv_prev[1], v_new)]
                offsets = tuple(kp.shape[1] for kp in kv_prev[0])
                kv_state = ([k[:, -kp.shape[1]:] for k, kp in zip(ks, kv_prev[0])],
                            [v[:, -vp.shape[1]:] for v, vp in zip(vs, kv_prev[1])])
        shift1, scale1, gate1, shift2, scale2, gate2 = jnp.split(_ada(c, w_ada[layer], b_ada[layer]), 6, axis=-1)
        h = _modulate(_rmsnorm(x, g_norm_mix[layer]), shift1, scale1)
        if layer < N_A_LAYERS:
            y, st = _short_conv(h, conv_prev[layer], w_in[layer], w_conv[layer], w_out_conv[layer])
            conv_new.append(st)
        else:
            lb = layer - N_A_LAYERS
            y = _dilated_mixer(h, pos, ks, vs, offsets, w_q[lb], g_q[lb], w_o[lb])
        x = x + gate1 * y
        h = _modulate(_rmsnorm(x, g_norm_ffn[layer]), shift2, scale2)
        if layer % 2 == 0:
            y = _swiglu(h, w_gu_dense[layer // 2], w_down_dense[layer // 2])
        else:
            y = _moe(h, w_router[layer // 2], b_router[layer // 2], w_gu_moe[layer // 2], w_down_moe[layer // 2])
        x = x + gate2 * y
    return x, jnp.stack(conv_new, axis=0), kv_state


def setup_inputs(seed: int = 0) -> dict:
    key = jax.random.key(seed)
    keys = iter(jax.random.split(key, 48))
    d = D_MODEL

    def nrm(shape, scale):
        return jax.random.normal(next(keys), shape, jnp.float32) * scale

    buf = [min(w, PAST_LEN) for w, _ in DIL_GROUPS]
    return {
        'x_prompt': nrm((BATCH, SEQ, d), 1.0),
        'x_sample': nrm((DEC_BATCH, DEC_SEQ, d), 1.0),
        'state_conv': nrm((N_A_LAYERS, DEC_BATCH, CONV_W - 1, d), 1.0),
        'cache_k_g0': nrm((DEC_BATCH, buf[0], HKV_G, HEAD_DIM), 1.0),
        'cache_v_g0': nrm((DEC_BATCH, buf[0], HKV_G, HEAD_DIM), 1.0),
        'cache_k_g1': nrm((DEC_BATCH, buf[1], HKV_G, HEAD_DIM), 1.0),
        'cache_v_g1': nrm((DEC_BATCH, buf[1], HKV_G, HEAD_DIM), 1.0),
        'cache_k_g2': nrm((DEC_BATCH, buf[2], HKV_G, HEAD_DIM), 1.0),
        'cache_v_g2': nrm((DEC_BATCH, buf[2], HKV_G, HEAD_DIM), 1.0),
        'c_prompt': nrm((BATCH, d), 1.0),
        'c_sample': nrm((DEC_BATCH, d), 1.0),
        'g_norm_mix': 1.0 + nrm((DEPTH, d), 0.02),
        'g_norm_ffn': 1.0 + nrm((DEPTH, d), 0.02),
        'w_ada': nrm((DEPTH, d, 6 * d), 0.5 * d ** -0.5),
        'b_ada': nrm((DEPTH, 6 * d), 0.02),
        'w_in': nrm((N_A_LAYERS, d, 3 * d), d ** -0.5),
        'w_conv': nrm((N_A_LAYERS, CONV_W, d), CONV_W ** -0.5),
        'w_out_conv': nrm((N_A_LAYERS, d, d), d ** -0.5),
        'g_norm_kv': 1.0 + nrm((d,), 0.02),
        'w_ada_kv': nrm((d, 2 * d), 0.5 * d ** -0.5),
        'b_ada_kv': nrm((2 * d,), 0.02),
        'w_kv': nrm((d, 2 * KV_WIDTH), d ** -0.5),
        'g_k': 1.0 + nrm((HEAD_DIM,), 0.02),
        'w_q': nrm((N_B_LAYERS, d, Q_WIDTH), d ** -0.5),
        'g_q': 1.0 + nrm((N_B_LAYERS, HEAD_DIM), 0.02),
        'w_o': nrm((N_B_LAYERS, O_WIDTH, d), O_WIDTH ** -0.5),
        'w_gu_dense': nrm((N_DENSE, d, 2 * D_FF), d ** -0.5),
        'w_down_dense': nrm((N_DENSE, D_FF, d), D_FF ** -0.5),
        'w_router': nrm((N_MOE, d, N_EXPERTS), d ** -0.5),
        'b_router': nrm((N_MOE, N_EXPERTS), 0.01),
        'w_gu_moe': nrm((N_MOE, N_EXPERTS, d, 2 * D_FF_E), d ** -0.5),
        'w_down_moe': nrm((N_MOE, N_EXPERTS, D_FF_E, d), D_FF_E ** -0.5),
    }


def reference(x_prompt, x_sample, state_conv, cache_k_g0, cache_v_g0, cache_k_g1, cache_v_g1,
              cache_k_g2, cache_v_g2, c_prompt, c_sample, g_norm_mix, g_norm_ffn, w_ada, b_ada,
              w_in, w_conv, w_out_conv, g_norm_kv, w_ada_kv, b_ada_kv, w_kv, g_k, w_q, g_q, w_o,
              w_gu_dense, w_down_dense, w_router, b_router, w_gu_moe, w_down_moe):
    params = (g_norm_mix, g_norm_ffn, w_ada, b_ada, w_in, w_conv, w_out_conv, g_norm_kv, w_ada_kv,
              b_ada_kv, w_kv, g_k, w_q, g_q, w_o, w_gu_dense, w_down_dense, w_router, b_router,
              w_gu_moe, w_down_moe)
    b_p, t_p, d = x_prompt.shape
    pos_p = jnp.arange(t_p, dtype=jnp.int32)
    conv0 = jnp.zeros((N_A_LAYERS, b_p, CONV_W - 1, d), x_prompt.dtype)
    y_prompt, conv_p, (kp, vp) = _trunk(x_prompt, c_prompt, pos_p, conv0, None, params)

    pos_s = PAST_LEN + jnp.arange(x_sample.shape[1], dtype=jnp.int32)
    kv_prev = ([cache_k_g0, cache_k_g1, cache_k_g2], [cache_v_g0, cache_v_g1, cache_v_g2])
    y_sample, conv_s, (ksn, vsn) = _trunk(x_sample, c_sample, pos_s, state_conv, kv_prev, params)
    return (y_prompt, y_sample, conv_p, conv_s,
            kp[0], vp[0], kp[1], vp[1], kp[2], vp[2],
            ksn[0], vsn[0], ksn[1], vsn[1], ksn[2], vsn[2])
```

```python
import functools

import jax
import jax.numpy as jnp
from jax import lax
from jax.experimental import pallas as pl
from jax.experimental.pallas import tpu as pltpu

D_MODEL = 1024
PAST_LEN = 8192
CONV_W = 3
HEAD_DIM = 128
ROT_DIM = HEAD_DIM // 4
ROPE_THETA = 500000.0
DIL_GROUPS = ((128, 1), (512, 4), (2048, 16))
N_GROUPS = len(DIL_GROUPS)
HQ_G = 8
HKV_G = 2
REP = HQ_G // HKV_G
N_Q_HEADS = N_GROUPS * HQ_G
N_KV_HEADS = N_GROUPS * HKV_G
Q_WIDTH = N_Q_HEADS * HEAD_DIM
KV_WIDTH = N_KV_HEADS * HEAD_DIM
O_WIDTH = HQ_G * HEAD_DIM
GROUP_KV_WIDTH = HKV_G * HEAD_DIM
N_EXPERTS = 8
EPS = 1e-6
ATTN_SCALE = HEAD_DIM ** -0.5
NEG = -1e30
Q_BLOCK = 128

F32 = jnp.float32
BF16 = jnp.bfloat16
HIGHEST = lax.Precision.HIGHEST

V7X_VMEM_LIMIT_BYTES = 56 * 1024 * 1024
PROMPT_TILE = 512
MOE_TILE = 1024
MOE_FF_CHUNK = 512


def _cparams(*sem):
    return pltpu.CompilerParams(dimension_semantics=sem, vmem_limit_bytes=V7X_VMEM_LIMIT_BYTES)


def _resident(shape, index_map):
    return pl.BlockSpec(shape, index_map, pipeline_mode=pl.Buffered(1))


def _silu(x):
    return x * jax.nn.sigmoid(x)


def _norm_mod(x, g, shift, scale):
    y = x * lax.rsqrt(jnp.mean(x * x, axis=-1, keepdims=True) + EPS) * g
    return y * (1.0 + scale) + shift


def _mod_part(mod_ref, k):
    return mod_ref[0, :, k * D_MODEL:(k + 1) * D_MODEL]


def _ada_kernel(c_ref, w_ref, b_ref, o_ref):
    c = c_ref[...]
    o_ref[0] = jnp.dot(_silu(c), w_ref[0], preferred_element_type=F32, precision=HIGHEST) + b_ref[0]


def _ada(c, w, b):
    n_layers, d, n = w.shape
    m = c.shape[0]
    tn = 1024
    return pl.pallas_call(
        _ada_kernel,
        grid=(n_layers, n // tn),
        in_specs=[pl.BlockSpec((m, d), lambda l, j: (0, 0)),
                  pl.BlockSpec((1, d, tn), lambda l, j: (l, 0, j)),
                  pl.BlockSpec((1, 1, tn), lambda l, j: (l, 0, j))],
        out_specs=pl.BlockSpec((1, m, tn), lambda l, j: (l, 0, j)),
        out_shape=jax.ShapeDtypeStruct((n_layers, m, n), F32),
        compiler_params=_cparams("parallel", "parallel"),
        name="ada",
    )(c, w, b.reshape(n_layers, 1, n))


def _conv_kernel(*refs, tm, tiles_per_seq, seq_rows):
    if seq_rows is None:
        x_ref, mod_ref, g_ref, win_ref, wc_ref, wout_ref, xo_ref, ust_ref, carry_ref = refs
    else:
        x_ref, mod_ref, g_ref, win_ref, wc_ref, wout_ref, pa_ref, pb_ref, xo_ref, ust_ref = refs
    x = x_ref[...]
    h = _norm_mod(x, g_ref[...], _mod_part(mod_ref, 0), _mod_part(mod_ref, 1)).astype(BF16)
    b_gate = jnp.dot(h, win_ref[:, 0:D_MODEL], preferred_element_type=F32)
    c_gate = jnp.dot(h, win_ref[:, D_MODEL:2 * D_MODEL], preferred_element_type=F32)
    v = jnp.dot(h, win_ref[:, 2 * D_MODEL:3 * D_MODEL], preferred_element_type=F32)
    u = c_gate * v
    row = lax.broadcasted_iota(jnp.int32, (tm, 1), 0)
    r1 = pltpu.roll(u, 1, 0)
    r2 = pltpu.roll(u, 2, 0)
    if seq_rows is None:
        @pl.when(pl.program_id(0) % tiles_per_seq == 0)
        def _():
            carry_ref[...] = jnp.zeros_like(carry_ref)
        um1 = jnp.where(row == 0, carry_ref[7:8, :], r1)
        um2 = jnp.where(row == 0, carry_ref[6:7, :], jnp.where(row == 1, carry_ref[7:8, :], r2))
        carry_ref[...] = u[tm - 8:tm, :]
        ust_ref[0] = u[tm - 8:tm, :]
    else:
        rowm = row & (seq_rows - 1)
        um1 = jnp.where(rowm == 0, pa_ref[...], r1)
        um2 = jnp.where(rowm < 2, pb_ref[...], r2)
        ust_ref[...] = u
    y = um2 * wc_ref[0:1, :] + um1 * wc_ref[1:2, :] + u * wc_ref[2:3, :]
    z = (b_gate * y).astype(BF16)
    out = jnp.dot(z, wout_ref[...], preferred_element_type=F32)
    xo_ref[...] = x + _mod_part(mod_ref, 2) * out


def _conv_layer(x, mod, tiles_per_seq, g, w_in, w_conv, w_out, tm, prev=None):
    rows, d = x.shape
    n_tiles = rows // tm
    mr = mod.shape[1]
    row_spec = pl.BlockSpec((tm, d), lambda i: (i, 0))
    in_specs = [row_spec,
                pl.BlockSpec((1, mr, 6 * d), lambda i: (i // tiles_per_seq, 0, 0)),
                pl.BlockSpec((1, d), lambda i: (0, 0)),
                _resident((d, 3 * d), lambda i: (0, 0)),
                pl.BlockSpec((CONV_W, d), lambda i: (0, 0)),
                _resident((d, d), lambda i: (0, 0))]
    args = [x, mod, g.reshape(1, d), w_in, w_conv, w_out]
    if prev is None:
        seq_rows = None
        n_seq = n_tiles // tiles_per_seq
        ust_shape = jax.ShapeDtypeStruct((n_seq, 8, d), F32)
        ust_spec = pl.BlockSpec((1, 8, d), lambda i: (i // tiles_per_seq, 0, 0))
        scratch = [pltpu.VMEM((8, d), F32)]
    else:
        seq_rows, pa, pb = prev
        in_specs += [row_spec, row_spec]
        args += [pa, pb]
        ust_shape = jax.ShapeDtypeStruct((rows, d), F32)
        ust_spec = row_spec
        scratch = []
    return pl.pallas_call(
        functools.partial(_conv_kernel, tm=tm, tiles_per_seq=tiles_per_seq, seq_rows=seq_rows),
        grid=(n_tiles,),
        in_specs=in_specs,
        out_specs=[row_spec, ust_spec],
        out_shape=[jax.ShapeDtypeStruct((rows, d), F32), ust_shape],
        scratch_shapes=scratch,
        compiler_params=_cparams("arbitrary"),
        name="conv_layer",
    )(*args)


def _ffn_kernel(x_ref, mod_ref, g_ref, wgu_ref, wd_ref, xo_ref, *, d_ff, n_chunks):
    x = x_ref[...]
    h = _norm_mod(x, g_ref[...], _mod_part(mod_ref, 3), _mod_part(mod_ref, 4)).astype(BF16)
    fc = d_ff // n_chunks
    acc = None
    for c in range(n_chunks):
        gate = jnp.dot(h, wgu_ref[:, c * fc:(c + 1) * fc], preferred_element_type=F32)
        up = jnp.dot(h, wgu_ref[:, d_ff + c * fc:d_ff + (c + 1) * fc], preferred_element_type=F32)
        a = (_silu(gate) * up).astype(BF16)
        part = jnp.dot(a, wd_ref[c * fc:(c + 1) * fc, :], preferred_element_type=F32)
        acc = part if acc is None else acc + part
    xo_ref[...] = x + _mod_part(mod_ref, 5) * acc


def _ffn_layer(x, mod, tiles_per_seq, g, w_gu, w_down, tm):
    rows, d = x.shape
    d_ff = w_down.shape[0]
    mr = mod.shape[1]
    row_spec = pl.BlockSpec((tm, d), lambda i: (i, 0))
    return pl.pallas_call(
        functools.partial(_ffn_kernel, d_ff=d_ff, n_chunks=2),
        grid=(rows // tm,),
        in_specs=[row_spec,
                  pl.BlockSpec((1, mr, 6 * d), lambda i: (i // tiles_per_seq, 0, 0)),
                  pl.BlockSpec((1, d), lambda i: (0, 0)),
                  _resident((d, 2 * d_ff), lambda i: (0, 0)),
                  _resident((d_ff, d), lambda i: (0, 0))],
        out_specs=row_spec,
        out_shape=jax.ShapeDtypeStruct((rows, d), F32),
        compiler_params=_cparams("parallel"),
        name="ffn_dense",
    )(x, mod, g.reshape(1, d), w_gu, w_down)


def _router_kernel(x_ref, mod_ref, g_ref, wr_ref, br_ref, h_ref, gates_ref):
    x = x_ref[...]
    h = _norm_mod(x, g_ref[...], _mod_part(mod_ref, 3), _mod_part(mod_ref, 4))
    h_ref[...] = h.astype(BF16)
    logits = jnp.dot(h, wr_ref[...], preferred_element_type=F32, precision=HIGHEST) + br_ref[...]
    idx = lax.broadcasted_iota(jnp.int32, logits.shape, 1).astype(F32)
    m1 = jnp.max(logits, axis=1, keepdims=True)
    i1 = jnp.min(jnp.where(logits == m1, idx, float(N_EXPERTS)), axis=1, keepdims=True)
    sel1 = idx == i1
    rest = jnp.where(sel1, -jnp.inf, logits)
    m2 = jnp.max(rest, axis=1, keepdims=True)
    i2 = jnp.min(jnp.where(rest == m2, idx, float(N_EXPERTS)), axis=1, keepdims=True)
    sel2 = idx == i2
    e2 = jnp.exp(m2 - m1)
    den = 1.0 + e2
    gates_ref[...] = jnp.where(sel1, 1.0 / den, 0.0) + jnp.where(sel2, e2 / den, 0.0)


def _router(x, mod, tiles_per_seq, g, w_router, b_router, tm):
    rows, d = x.shape
    mr = mod.shape[1]
    row_spec = pl.BlockSpec((tm, d), lambda i: (i, 0))
    return pl.pallas_call(
        _router_kernel,
        grid=(rows // tm,),
        in_specs=[row_spec,
                  pl.BlockSpec((1, mr, 6 * d), lambda i: (i // tiles_per_seq, 0, 0)),
                  pl.BlockSpec((1, d), lambda i: (0, 0)),
                  pl.BlockSpec((d, N_EXPERTS), lambda i: (0, 0)),
                  pl.BlockSpec((1, N_EXPERTS), lambda i: (0, 0))],
        out_specs=[row_spec, pl.BlockSpec((tm, N_EXPERTS), lambda i: (i, 0))],
        out_shape=[jax.ShapeDtypeStruct((rows, d), BF16),
                   jax.ShapeDtypeStruct((rows, N_EXPERTS), F32)],
        compiler_params=_cparams("parallel"),
        name="moe_router",
    )(x, mod, g.reshape(1, d), w_router, b_router.reshape(1, N_EXPERTS))


def _moe_kernel(x_ref, mod_ref, h_ref, gates_ref, wg_ref, wu_ref, wd_ref, xo_ref, acc_ref):
    e = pl.program_id(1)
    f = pl.program_id(2)

    @pl.when(jnp.logical_and(e == 0, f == 0))
    def _():
        acc_ref[...] = jnp.zeros_like(acc_ref)

    h = h_ref[...]
    gate = jnp.dot(h, wg_ref[0], preferred_element_type=F32)
    up = jnp.dot(h, wu_ref[0], preferred_element_type=F32)
    a = (_silu(gate) * up).astype(BF16)
    y = jnp.dot(a, wd_ref[0], preferred_element_type=F32)
    gates = gates_ref[...]
    lane = lax.broadcasted_iota(jnp.int32, gates.shape, 1)
    w_e = jnp.sum(jnp.where(lane == e, gates, 0.0), axis=1, keepdims=True)
    acc_ref[...] += w_e * y

    @pl.when(jnp.logical_and(e == pl.num_programs(1) - 1, f == pl.num_programs(2) - 1))
    def _():
        xo_ref[...] = x_ref[...] + _mod_part(mod_ref, 5) * acc_ref[...]


def _moe_layer(x, mod, tiles_per_seq, g, w_router, b_router, w_gu, w_down, tm):
    rows, d = x.shape
    mr = mod.shape[1]
    d_ff = w_down.shape[1]
    fc = MOE_FF_CHUNK
    nf = d_ff // fc
    h, gates = _router(x, mod, tiles_per_seq, g, w_router, b_router, tm)
    row_spec = pl.BlockSpec((tm, d), lambda i, e, f: (i, 0))
    return pl.pallas_call(
        _moe_kernel,
        grid=(rows // tm, N_EXPERTS, nf),
        in_specs=[row_spec,
                  pl.BlockSpec((1, mr, 6 * d), lambda i, e, f: (i // tiles_per_seq, 0, 0)),
                  row_spec,
                  pl.BlockSpec((tm, N_EXPERTS), lambda i, e, f: (i, 0)),
                  pl.BlockSpec((1, d, fc), lambda i, e, f: (e, 0, f)),
                  pl.BlockSpec((1, d, fc), lambda i, e, f: (e, 0, nf + f)),
                  pl.BlockSpec((1, fc, d), lambda i, e, f: (e, f, 0))],
        out_specs=row_spec,
        out_shape=jax.ShapeDtypeStruct((rows, d), F32),
        scratch_shapes=[pltpu.VMEM((tm, d), F32)],
        compiler_params=_cparams("parallel", "arbitrary", "arbitrary"),
        name="moe_experts",
    )(x, mod, h, gates, w_gu, w_gu, w_down)


def _rope_tables(pos):
    half = ROT_DIM // 2
    inv = jnp.float32(ROPE_THETA) ** (-jnp.arange(half, dtype=jnp.float32) / half)
    ang = pos.astype(jnp.float32)[:, None] * inv[None, :]
    cos, sin = jnp.cos(ang), jnp.sin(ang)
    t = pos.shape[0]
    pad1 = jnp.ones((t, HEAD_DIM - ROT_DIM), F32)
    pad0 = jnp.zeros((t, HEAD_DIM - ROT_DIM), F32)
    return (jnp.concatenate([cos, cos, pad1], axis=1), jnp.concatenate([-sin, sin, pad0], axis=1))


def _head_norm_rope(xh, g, cos, sin):
    half = ROT_DIM // 2
    xn = xh * lax.rsqrt(jnp.mean(xh * xh, axis=-1, keepdims=True) + EPS) * g
    lane = lax.broadcasted_iota(jnp.int32, xn.shape, 1)
    swapped = jnp.where(lane < half, pltpu.roll(xn, HEAD_DIM - half, 1), pltpu.roll(xn, half, 1))
    return xn * cos + swapped * sin


def _kv_kernel(x_ref, mod_ref, g_ref, wkv_ref, gk_ref, cos_ref, sin_ref, *out_refs):
    x = x_ref[...]
    h = _norm_mod(x, g_ref[...], mod_ref[0, :, 0:D_MODEL], mod_ref[0, :, D_MODEL:2 * D_MODEL]).astype(BF16)
    cos, sin, gk = cos_ref[...], sin_ref[...], gk_ref[...]
    k_refs, v_refs = out_refs[:N_GROUPS], out_refs[N_GROUPS:]
    for hd in range(N_KV_HEADS):
        kh = jnp.dot(h, wkv_ref[:, hd * HEAD_DIM:(hd + 1) * HEAD_DIM], preferred_element_type=F32)
        vh = jnp.dot(h, wkv_ref[:, KV_WIDTH + hd * HEAD_DIM:KV_WIDTH + (hd + 1) * HEAD_DIM],
                     preferred_element_type=F32)
        g_idx, k_idx = divmod(hd, HKV_G)
        sl = slice(k_idx * HEAD_DIM, (k_idx + 1) * HEAD_DIM)
        k_refs[g_idx][:, sl] = _head_norm_rope(kh, gk, cos, sin)
        v_refs[g_idx][:, sl] = vh


def _kv_proj(x, mod_kv, tiles_per_seq, g, w_kv, g_k, cos, sin, tm):
    rows, d = x.shape
    mr = mod_kv.shape[1]
    row_spec = pl.BlockSpec((tm, d), lambda i: (i, 0))
    tab_spec = pl.BlockSpec((tm, HEAD_DIM), lambda i: (i % tiles_per_seq, 0))
    out_spec = pl.BlockSpec((tm, GROUP_KV_WIDTH), lambda i: (i, 0))
    return pl.pallas_call(
        _kv_kernel,
        grid=(rows // tm,),
        in_specs=[row_spec,
                  pl.BlockSpec((1, mr, 2 * d), lambda i: (i // tiles_per_seq, 0, 0)),
                  pl.BlockSpec((1, d), lambda i: (0, 0)),
                  _resident((d, 2 * KV_WIDTH), lambda i: (0, 0)),
                  pl.BlockSpec((1, HEAD_DIM), lambda i: (0, 0)),
                  tab_spec, tab_spec],
        out_specs=[out_spec] * (2 * N_GROUPS),
        out_shape=[jax.ShapeDtypeStruct((rows, GROUP_KV_WIDTH), F32)] * (2 * N_GROUPS),
        compiler_params=_cparams("parallel"),
        name="kv_proj",
    )(x, mod_kv, g.reshape(1, d), w_kv, g_k.reshape(1, HEAD_DIM), cos, sin)


def _q_kernel(x_ref, mod_ref, g_ref, wq_ref, gq_ref, cos_ref, sin_ref, q_ref):
    x = x_ref[...]
    h = _norm_mod(x, g_ref[...], _mod_part(mod_ref, 0), _mod_part(mod_ref, 1)).astype(BF16)
    cos, sin, gq = cos_ref[...], sin_ref[...], gq_ref[...]
    for hd in range(N_Q_HEADS):
        sl = slice(hd * HEAD_DIM, (hd + 1) * HEAD_DIM)
        qh = jnp.dot(h, wq_ref[:, sl], preferred_element_type=F32)
        q_ref[:, sl] = (_head_norm_rope(qh, gq, cos, sin) * ATTN_SCALE).astype(BF16)


def _q_proj(x, mod, tiles_per_seq, g, w_q, g_q, cos, sin, tm):
    rows, d = x.shape
    mr = mod.shape[1]
    row_spec = pl.BlockSpec((tm, d), lambda i: (i, 0))
    tab_spec = pl.BlockSpec((tm, HEAD_DIM), lambda i: (i % tiles_per_seq, 0))
    return pl.pallas_call(
        _q_kernel,
        grid=(rows // tm,),
        in_specs=[row_spec,
                  pl.BlockSpec((1, mr, 6 * d), lambda i: (i // tiles_per_seq, 0, 0)),
                  pl.BlockSpec((1, d), lambda i: (0, 0)),
                  _resident((d, Q_WIDTH), lambda i: (0, 0)),
                  pl.BlockSpec((1, HEAD_DIM), lambda i: (0, 0)),
                  tab_spec, tab_spec],
        out_specs=pl.BlockSpec((tm, Q_WIDTH), lambda i: (i, 0)),
        out_shape=jax.ShapeDtypeStruct((rows, Q_WIDTH), BF16),
        compiler_params=_cparams("parallel"),
        name="q_proj",
    )(x, mod, g.reshape(1, d), w_q, g_q.reshape(1, HEAD_DIM), cos, sin)


def _attn_prompt_kernel(q_ref, kp_ref, kc_ref, vp_ref, vc_ref, o_ref, st_ref):
    blk = pl.program_id(2)
    nq = REP * Q_BLOCK
    row = lax.broadcasted_iota(jnp.int32, (nq, 2 * Q_BLOCK), 0) & (Q_BLOCK - 1)
    col = lax.broadcasted_iota(jnp.int32, (nq, 2 * Q_BLOCK), 1)
    valid = (col >= row) & (col <= row + Q_BLOCK) & ((blk > 0) | (col >= Q_BLOCK))
    lane = lax.broadcasted_iota(jnp.int32, (Q_BLOCK, HEAD_DIM), 1)
    for k in range(HKV_G):
        ksl = slice(k * HEAD_DIM, (k + 1) * HEAD_DIM)
        q = jnp.concatenate(
            [q_ref[0, :, (k * REP + r) * HEAD_DIM:(k * REP + r + 1) * HEAD_DIM] for r in range(REP)], axis=0)
        keys = jnp.concatenate([kp_ref[0, :, ksl], kc_ref[0, :, ksl]], axis=0).astype(BF16)
        vals = jnp.concatenate([vp_ref[0, :, ksl], vc_ref[0, :, ksl]], axis=0).astype(BF16)
        s = lax.dot_general(q, keys, (((1,), (1,)), ((), ())), preferred_element_type=F32)
        s = jnp.where(valid, s, NEG)
        m = jnp.max(s, axis=1, keepdims=True)
        p = jnp.exp(s - m)
        den = jnp.sum(p, axis=1, keepdims=True)
        out = jnp.dot(p.astype(BF16), vals, preferred_element_type=F32) / den
        lse = m + jnp.log(den)
        stats = jnp.zeros((Q_BLOCK, HEAD_DIM), F32)
        for r in range(REP):
            rs = slice(r * Q_BLOCK, (r + 1) * Q_BLOCK)
            o_ref[0, :, (k * REP + r) * HEAD_DIM:(k * REP + r + 1) * HEAD_DIM] = out[rs].astype(BF16)
            stats = jnp.where(lane == r, lse[rs], stats)
        st_ref[0, :, ksl] = stats


def _attn_prompt_group(q, k_g, v_g, group, batch, seq):
    _, dil = DIL_GROUPS[group]
    sub = seq // dil
    n_blk = sub // Q_BLOCK
    qv = q.reshape(batch, sub, dil * Q_WIDTH)
    kv = k_g.reshape(batch, sub, dil * GROUP_KV_WIDTH)
    vv = v_g.reshape(batch, sub, dil * GROUP_KV_WIDTH)
    q_cols = Q_WIDTH // O_WIDTH
    cur = pl.BlockSpec((1, Q_BLOCK, GROUP_KV_WIDTH), lambda b, r, j: (b, j, r))
    prev = pl.BlockSpec((1, Q_BLOCK, GROUP_KV_WIDTH), lambda b, r, j: (b, jnp.maximum(j - 1, 0), r))
    out, stats = pl.pallas_call(
        _attn_prompt_kernel,
        grid=(batch, dil, n_blk),
        in_specs=[pl.BlockSpec((1, Q_BLOCK, O_WIDTH), lambda b, r, j: (b, j, r * q_cols + group)),
                  prev, cur, prev, cur],
        out_specs=[pl.BlockSpec((1, Q_BLOCK, O_WIDTH), lambda b, r, j: (b, j, r)), cur],
        out_shape=[jax.ShapeDtypeStruct((batch, sub, dil * O_WIDTH), BF16),
                   jax.ShapeDtypeStruct((batch, sub, dil * GROUP_KV_WIDTH), F32)],
        compiler_params=_cparams("parallel", "parallel", "arbitrary"),
        name=f"attn_prompt_g{group}",
    )(qv, kv, kv, vv, vv)
    return out.reshape(batch * seq, O_WIDTH), stats.reshape(batch * seq, GROUP_KV_WIDTH)


def _merge_o_kernel(x_ref, mod_ref, o0_ref, o1_ref, o2_ref, s0_ref, s1_ref, s2_ref, wo_ref, xo_ref):
    o_refs = (o0_ref, o1_ref, o2_ref)
    s_refs = (s0_ref, s1_ref, s2_ref)
    heads = []
    for k in range(HKV_G):
        for r in range(REP):
            c = k * HEAD_DIM + r
            lses = [s[:, c:c + 1] for s in s_refs]
            m = jnp.maximum(jnp.maximum(lses[0], lses[1]), lses[2])
            es = [jnp.exp(l - m) for l in lses]
            den = es[0] + es[1] + es[2]
            sl = slice((k * REP + r) * HEAD_DIM, (k * REP + r + 1) * HEAD_DIM)
            o = sum((e / den) * o_ref[:, sl].astype(F32) for e, o_ref in zip(es, o_refs))
            heads.append(o.astype(BF16))
    o = jnp.concatenate(heads, axis=1)
    y = jnp.dot(o, wo_ref[...], preferred_element_type=F32)
    xo_ref[...] = x_ref[...] + _mod_part(mod_ref, 2) * y


def _merge_o(x, mod, tiles_per_seq, outs, stats, w_o, tm):
    rows, d = x.shape
    mr = mod.shape[1]
    row_spec = pl.BlockSpec((tm, d), lambda i: (i, 0))
    st_spec = pl.BlockSpec((tm, GROUP_KV_WIDTH), lambda i: (i, 0))
    return pl.pallas_call(
        _merge_o_kernel,
        grid=(rows // tm,),
        in_specs=[row_spec,
                  pl.BlockSpec((1, mr, 6 * d), lambda i: (i // tiles_per_seq, 0, 0)),
                  pl.BlockSpec((tm, O_WIDTH), lambda i: (i, 0))] + [pl.BlockSpec((tm, O_WIDTH), lambda i: (i, 0))] * 2
                 + [st_spec] * 3 + [_resident((O_WIDTH, d), lambda i: (0, 0))],
        out_specs=row_spec,
        out_shape=jax.ShapeDtypeStruct((rows, d), F32),
        compiler_params=_cparams("parallel"),
        name="merge_o_proj",
    )(x, mod, *outs, *stats, w_o)


def _attn_sample_kernel(*refs, n_new):
    q_ref = refs[0]
    nk_refs = refs[1:1 + N_GROUPS]
    nv_refs = refs[1 + N_GROUPS:1 + 2 * N_GROUPS]
    ck_refs = refs[1 + 2 * N_GROUPS:1 + 3 * N_GROUPS]
    cv_refs = refs[1 + 3 * N_GROUPS:1 + 4 * N_GROUPS]
    o_ref = refs[1 + 4 * N_GROUPS]
    ok_refs = refs[2 + 4 * N_GROUPS:2 + 5 * N_GROUPS]
    ov_refs = refs[2 + 5 * N_GROUPS:2 + 6 * N_GROUPS]
    nq = REP * n_new
    qi = lax.broadcasted_iota(jnp.int32, (nq, 1), 0) & (n_new - 1)

    for g in range(N_GROUPS):
        for c_ref, n_ref, o_buf in ((ck_refs[g], nk_refs[g], ok_refs[g]), (cv_refs[g], nv_refs[g], ov_refs[g])):
            length = c_ref.shape[1]
            rolled = pltpu.roll(c_ref[0], length - n_new, 0)
            o_buf[0, 0:length - 8, :] = rolled[0:length - 8, :]
            row8 = lax.broadcasted_iota(jnp.int32, (8, 1), 0)
            o_buf[0, length - 8:length, :] = jnp.where(row8 >= 8 - n_new, n_ref[0], rolled[length - 8:length, :])

    for k in range(HKV_G):
        ksl = slice(k * HEAD_DIM, (k + 1) * HEAD_DIM)
        outs, lses = [], []
        for g, (window, dil) in enumerate(DIL_GROUPS):
            length = ck_refs[g].shape[1]
            q = q_ref[0, g, k]
            qf = q.astype(F32)
            keys = ck_refs[g][0, :, ksl].astype(BF16)
            vals = cv_refs[g][0, :, ksl].astype(BF16)
            s = lax.dot_general(q, keys, (((1,), (1,)), ((), ())), preferred_element_type=F32)
            idx = lax.broadcasted_iota(jnp.int32, (nq, length), 1)
            diff = length + qi - idx
            s = jnp.where(((diff & (dil - 1)) == 0) & (diff <= window), s, NEG)
            new_k = nk_refs[g][0, :, ksl]
            new_v = nv_refs[g][0, :, ksl]
            s_new = []
            for j in range(n_new):
                sj = jnp.sum(qf * new_k[8 - n_new + j:8 - n_new + j + 1, :], axis=1, keepdims=True)
                ok = (qi >= j) & (((qi - j) & (dil - 1)) == 0)
                s_new.append(jnp.where(ok, sj, NEG))
            m = jnp.max(s, axis=1, keepdims=True)
            for sj in s_new:
                m = jnp.maximum(m, sj)
            p = jnp.exp(s - m)
            den = jnp.sum(p, axis=1, keepdims=True)
            acc = jnp.dot(p.astype(BF16), vals, preferred_element_type=F32)
            for j, sj in enumerate(s_new):
                pj = jnp.exp(sj - m)
                den = den + pj
                acc = acc + pj * new_v[8 - n_new + j:8 - n_new + j + 1, :]
            outs.append(acc / den)
            lses.append(m + jnp.log(den))
        m = jnp.maximum(jnp.maximum(lses[0], lses[1]), lses[2])
        es = [jnp.exp(l - m) for l in lses]
        den = es[0] + es[1] + es[2]
        o_ref[0, k] = sum((e / den) * o for e, o in zip(es, outs))


def _attn_sample(q, new_k, new_v, cache_k, cache_v, n_seq, n_new):
    assert n_new <= 8
    q5 = q.reshape(n_seq, n_new, N_GROUPS, HKV_G, REP, HEAD_DIM)
    q5 = jnp.transpose(q5, (0, 2, 3, 4, 1, 5)).reshape(n_seq, N_GROUPS, HKV_G, REP * n_new, HEAD_DIM)
    pad = lambda a: jnp.pad(a.reshape(n_seq, n_new, GROUP_KV_WIDTH), ((0, 0), (8 - n_new, 0), (0, 0)))
    new_k8 = [pad(a) for a in new_k]
    new_v8 = [pad(a) for a in new_v]
    ck = [c.reshape(n_seq, c.shape[1], GROUP_KV_WIDTH) for c in cache_k]
    cv = [c.reshape(n_seq, c.shape[1], GROUP_KV_WIDTH) for c in cache_v]
    new_spec = pl.BlockSpec((1, 8, GROUP_KV_WIDTH), lambda b: (b, 0, 0))
    cache_specs = [pl.BlockSpec((1, c.shape[1], GROUP_KV_WIDTH), lambda b: (b, 0, 0)) for c in ck]
    cache_shapes = [jax.ShapeDtypeStruct(c.shape, F32) for c in ck]
    res = pl.pallas_call(
        functools.partial(_attn_sample_kernel, n_new=n_new),
        grid=(n_seq,),
        in_specs=[pl.BlockSpec((1, N_GROUPS, HKV_G, REP * n_new, HEAD_DIM), lambda b: (b, 0, 0, 0, 0))]
                 + [new_spec] * (2 * N_GROUPS) + cache_specs + cache_specs,
        out_specs=[pl.BlockSpec((1, HKV_G, REP * n_new, HEAD_DIM), lambda b: (b, 0, 0, 0))]
                  + cache_specs + cache_specs,
        out_shape=[jax.ShapeDtypeStruct((n_seq, HKV_G, REP * n_new, HEAD_DIM), F32)]
                  + cache_shapes + cache_shapes,
        compiler_params=_cparams("parallel"),
        name="attn_sample",
    )(q5, *new_k8, *new_v8, *ck, *cv)
    o = res[0].reshape(n_seq, HKV_G, REP, n_new, HEAD_DIM)
    o = jnp.transpose(o, (0, 3, 1, 2, 4)).reshape(n_seq * n_new, O_WIDTH)
    shape4 = lambda a: a.reshape(n_seq, a.shape[1], HKV_G, HEAD_DIM)
    new_ck = [shape4(a) for a in res[1:1 + N_GROUPS]]
    new_cv = [shape4(a) for a in res[1 + N_GROUPS:1 + 2 * N_GROUPS]]
    return o, new_ck, new_cv


def _o_proj_kernel(x_ref, mod_ref, o_ref, wo_ref, xo_ref):
    y = jnp.dot(o_ref[...].astype(BF16), wo_ref[...], preferred_element_type=F32)
    xo_ref[...] = x_ref[...] + _mod_part(mod_ref, 2) * y


def _o_proj(x, mod, o, w_o):
    rows, d = x.shape
    mr = mod.shape[1]
    full = pl.BlockSpec((rows, d), lambda i: (0, 0))
    return pl.pallas_call(
        _o_proj_kernel,
        grid=(1,),
        in_specs=[full, pl.BlockSpec((1, mr, 6 * d), lambda i: (0, 0, 0)),
                  pl.BlockSpec((rows, O_WIDTH), lambda i: (0, 0)),
                  pl.BlockSpec((O_WIDTH, d), lambda i: (0, 0))],
        out_specs=full,
        out_shape=jax.ShapeDtypeStruct((rows, d), F32),
        compiler_params=_cparams("arbitrary"),
        name="o_proj_sample",
    )(x, mod, o, w_o)


def _trunk(x, mods, mod_kv, pos, weights, tm, conv_prev, kv_prev):
    (g_norm_mix, g_norm_ffn, w_in, w_conv, w_out_conv, g_norm_kv, w_kv, g_k, w_q, g_q, w_o,
     w_gu_dense, w_down_dense, w_router, b_router, w_gu_moe, w_down_moe) = weights
    batch, seq, d = x.shape
    rows = batch * seq
    depth = g_norm_mix.shape[0]
    n_conv = w_in.shape[0]
    sample = kv_prev is not None
    tiles_per_seq = 1 if sample else seq // tm
    x = x.reshape(rows, d)
    cos, sin = _rope_tables(pos)
    if sample:
        cos, sin = jnp.tile(cos, (batch, 1)), jnp.tile(sin, (batch, 1))
    conv_state = []
    kv_state = None
    for layer in range(depth):
        mod = mods[layer]
        if layer == n_conv:
            kvs = _kv_proj(x, mod_kv, tiles_per_seq, g_norm_kv, w_kv, g_k, cos, sin, tm)
            k_new, v_new = kvs[:N_GROUPS], kvs[N_GROUPS:]
        if layer < n_conv:
            if sample:
                st = conv_prev[layer]
                zero = jnp.zeros((batch, seq - 2, d), F32)
                pa = jnp.concatenate([st[:, 1:2], jnp.zeros((batch, seq - 1, d), F32)], axis=1)
                pb = jnp.concatenate([st, zero], axis=1)
                prev = (seq, pa.reshape(rows, d), pb.reshape(rows, d))
            else:
                prev = None
            x, u_tail = _conv_layer(x, mod, tiles_per_seq, g_norm_mix[layer], w_in[layer], w_conv[layer],
                                    w_out_conv[layer], tm, prev)
            if sample:
                conv_state.append(u_tail.reshape(batch, seq, d)[:, seq - (CONV_W - 1):])
            else:
                conv_state.append(u_tail[:, 8 - (CONV_W - 1):])
        else:
            lb = layer - n_conv
            q = _q_proj(x, mod, tiles_per_seq, g_norm_mix[layer], w_q[lb], g_q[lb], cos, sin, tm)
            if sample:
                o, new_ck, new_cv = _attn_sample(q, k_new, v_new, kv_prev[0], kv_prev[1], batch, seq)
                kv_state = (new_ck, new_cv)
                x = _o_proj(x, mod, o, w_o[lb])
            else:
                res = [_attn_prompt_group(q, k_new[g], v_new[g], g, batch, seq) for g in range(N_GROUPS)]
                x = _merge_o(x, mod, tiles_per_seq, [r[0] for r in res], [r[1] for r in res], w_o[lb], tm)
        if layer % 2 == 0:
            x = _ffn_layer(x, mod, tiles_per_seq, g_norm_ffn[layer], w_gu_dense[layer // 2],
                           w_down_dense[layer // 2], tm)
        else:
            tm_moe = tm if sample else min(MOE_TILE, seq)
            x = _moe_layer(x, mod, 1 if sample else seq // tm_moe, g_norm_ffn[layer], w_router[layer // 2],
                           b_router[layer // 2], w_gu_moe[layer // 2], w_down_moe[layer // 2], tm_moe)
    if not sample:
        shape4 = lambda a: a.reshape(batch, seq, HKV_G, HEAD_DIM)
        kv_state = ([shape4(k)[:, -min(w, seq):] for k, (w, _) in zip(k_new, DIL_GROUPS)],
                    [shape4(v)[:, -min(w, seq):] for v, (w, _) in zip(v_new, DIL_GROUPS)])
    return x.reshape(batch, seq, d), jnp.stack(conv_state, axis=0), kv_state


def kernel(x_prompt, x_sample, state_conv, cache_k_g0, cache_v_g0, cache_k_g1, cache_v_g1, cache_k_g2,
           cache_v_g2, c_prompt, c_sample, g_norm_mix, g_norm_ffn, w_ada, b_ada, w_in, w_conv, w_out_conv,
           g_norm_kv, w_ada_kv, b_ada_kv, w_kv, g_k, w_q, g_q, w_o, w_gu_dense, w_down_dense, w_router,
           b_router, w_gu_moe, w_down_moe):
    b_p, t_p, d = x_prompt.shape
    b_s, t_s, _ = x_sample.shape
    bf = lambda w: w.astype(BF16)
    weights = (g_norm_mix, g_norm_ffn, bf(w_in), w_conv, bf(w_out_conv), g_norm_kv, bf(w_kv), g_k, bf(w_q),
               g_q, bf(w_o), bf(w_gu_dense), bf(w_down_dense), w_router, b_router, bf(w_gu_moe),
               bf(w_down_moe))

    c_all = jnp.concatenate([c_prompt, c_sample], axis=0)
    mods_all = _ada(c_all, w_ada, b_ada)
    mod_kv_all = _ada(c_all, w_ada_kv[None], b_ada_kv[None])[0]
    depth = w_ada.shape[0]
    mods_p = [mods_all[l, :b_p, None, :] for l in range(depth)]
    mods_s = [jnp.repeat(mods_all[l, b_p:], t_s, axis=0)[None] for l in range(depth)]
    mod_kv_p = mod_kv_all[:b_p, None, :]
    mod_kv_s = jnp.repeat(mod_kv_all[b_p:], t_s, axis=0)[None]

    pos_p = jnp.arange(t_p, dtype=jnp.int32)
    y_prompt, conv_p, (kp, vp) = _trunk(x_prompt, mods_p, mod_kv_p, pos_p, weights,
                                        min(PROMPT_TILE, t_p), None, None)

    pos_s = PAST_LEN + jnp.arange(t_s, dtype=jnp.int32)
    kv_prev = ([cache_k_g0, cache_k_g1, cache_k_g2], [cache_v_g0, cache_v_g1, cache_v_g2])
    y_sample, conv_s, (ksn, vsn) = _trunk(x_sample, mods_s, mod_kv_s, pos_s, weights, b_s * t_s,
                                          state_conv, kv_prev)
    return (y_prompt, y_sample, conv_p, conv_s,
            kp[0], vp[0], kp[1], vp[1], kp[2], vp[2],
            ksn[0], vsn[0], ksn[1], vsn[1], ksn[2], vsn[2])
```

```python
import functools

import jax
import jax.numpy as jnp
from jax import lax
from jax.experimental import pallas as pl
from jax.experimental.pallas import tpu as pltpu

D_MODEL = 1024
PAST_LEN = 8192
CONV_W = 3
HEAD_DIM = 128
ROT_DIM = HEAD_DIM // 4
ROPE_THETA = 500000.0
DIL_GROUPS = ((128, 1), (512, 4), (2048, 16))
N_GROUPS = len(DIL_GROUPS)
HQ_G = 8
HKV_G = 2
REP = HQ_G // HKV_G
N_Q_HEADS = N_GROUPS * HQ_G
N_KV_HEADS = N_GROUPS * HKV_G
Q_WIDTH = N_Q_HEADS * HEAD_DIM
KV_WIDTH = N_KV_HEADS * HEAD_DIM
O_WIDTH = HQ_G * HEAD_DIM
GROUP_KV_WIDTH = HKV_G * HEAD_DIM
N_EXPERTS = 8
EPS = 1e-6
ATTN_SCALE = HEAD_DIM ** -0.5
NEG = -1e30
Q_BLOCK = 128

F32 = jnp.float32
BF16 = jnp.bfloat16
HIGHEST = lax.Precision.HIGHEST

V7X_VMEM_LIMIT_BYTES = 56 * 1024 * 1024
PROMPT_TILE = 512
MOE_TILE = 1024
MOE_FF_CHUNK = 512


def _cparams(*sem):
    return pltpu.CompilerParams(dimension_semantics=sem, vmem_limit_bytes=V7X_VMEM_LIMIT_BYTES)


def _resident(shape, index_map):
    return pl.BlockSpec(shape, index_map, pipeline_mode=pl.Buffered(1))


def _silu(x):
    return x * jax.nn.sigmoid(x)


def _norm_mod(x, g, shift, scale):
    y = x * lax.rsqrt(jnp.mean(x * x, axis=-1, keepdims=True) + EPS) * g
    return y * (1.0 + scale) + shift


def _mod_part(mod_ref, k):
    return mod_ref[0, :, k * D_MODEL:(k + 1) * D_MODEL]


def _ada_kernel(c_ref, w_ref, b_ref, o_ref):
    c = c_ref[...]
    o_ref[0] = jnp.dot(_silu(c), w_ref[0], preferred_element_type=F32, precision=HIGHEST) + b_ref[0]


def _ada(c, w, b):
    n_layers, d, n = w.shape
    m = c.shape[0]
    tn = 1024
    return pl.pallas_call(
        _ada_kernel,
        grid=(n_layers, n // tn),
        in_specs=[pl.BlockSpec((m, d), lambda l, j: (0, 0)),
                  pl.BlockSpec((1, d, tn), lambda l, j: (l, 0, j)),
                  pl.BlockSpec((1, 1, tn), lambda l, j: (l, 0, j))],
        out_specs=pl.BlockSpec((1, m, tn), lambda l, j: (l, 0, j)),
        out_shape=jax.ShapeDtypeStruct((n_layers, m, n), F32),
        compiler_params=_cparams("parallel", "parallel"),
        name="ada",
    )(c, w, b.reshape(n_layers, 1, n))


def _conv_kernel(*refs, tm, tiles_per_seq, seq_rows):
    if seq_rows is None:
        x_ref, mod_ref, g_ref, win_ref, wc_ref, wout_ref, xo_ref, ust_ref, carry_ref = refs
    else:
        x_ref, mod_ref, g_ref, win_ref, wc_ref, wout_ref, pa_ref, pb_ref, xo_ref, ust_ref = refs
    x = x_ref[...]
    h = _norm_mod(x, g_ref[...], _mod_part(mod_ref, 0), _mod_part(mod_ref, 1)).astype(BF16)
    b_gate = jnp.dot(h, win_ref[:, 0:D_MODEL], preferred_element_type=F32)
    c_gate = jnp.dot(h, win_ref[:, D_MODEL:2 * D_MODEL], preferred_element_type=F32)
    v = jnp.dot(h, win_ref[:, 2 * D_MODEL:3 * D_MODEL], preferred_element_type=F32)
    u = c_gate * v
    row = lax.broadcasted_iota(jnp.int32, (tm, 1), 0)
    r1 = pltpu.roll(u, 1, 0)
    r2 = pltpu.roll(u, 2, 0)
    if seq_rows is None:
        @pl.when(pl.program_id(0) % tiles_per_seq == 0)
        def _():
            carry_ref[...] = jnp.zeros_like(carry_ref)
        um1 = jnp.where(row == 0, carry_ref[7:8, :], r1)
        um2 = jnp.where(row == 0, carry_ref[6:7, :], jnp.where(row == 1, carry_ref[7:8, :], r2))
        carry_ref[...] = u[tm - 8:tm, :]
        ust_ref[0] = u[tm - 8:tm, :]
    else:
        rowm = row & (seq_rows - 1)
        um1 = jnp.where(rowm == 0, pa_ref[...], r1)
        um2 = jnp.where(rowm < 2, pb_ref[...], r2)
        ust_ref[...] = u
    y = um2 * wc_ref[0:1, :] + um1 * wc_ref[1:2, :] + u * wc_ref[2:3, :]
    z = (b_gate * y).astype(BF16)
    out = jnp.dot(z, wout_ref[...], preferred_element_type=F32)
    xo_ref[...] = x + _mod_part(mod_ref, 2) * out


def _conv_layer(x, mod, tiles_per_seq, g, w_in, w_conv, w_out, tm, prev=None):
    rows, d = x.shape
    n_tiles = rows // tm
    mr = mod.shape[1]
    row_spec = pl.BlockSpec((tm, d), lambda i: (i, 0))
    in_specs = [row_spec,
                pl.BlockSpec((1, mr, 6 * d), lambda i: (i // tiles_per_seq, 0, 0)),
                pl.BlockSpec((1, d), lambda i: (0, 0)),
                _resident((d, 3 * d), lambda i: (0, 0)),
                pl.BlockSpec((CONV_W, d), lambda i: (0, 0)),
                _resident((d, d), lambda i: (0, 0))]
    args = [x, mod, g.reshape(1, d), w_in, w_conv, w_out]
    if prev is None:
        seq_rows = None
        n_seq = n_tiles // tiles_per_seq
        ust_shape = jax.ShapeDtypeStruct((n_seq, 8, d), F32)
        ust_spec = pl.BlockSpec((1, 8, d), lambda i: (i // tiles_per_seq, 0, 0))
        scratch = [pltpu.VMEM((8, d), F32)]
    else:
        seq_rows, pa, pb = prev
        in_specs += [row_spec, row_spec]
        args += [pa, pb]
        ust_shape = jax.ShapeDtypeStruct((rows, d), F32)
        ust_spec = row_spec
        scratch = []
    return pl.pallas_call(
        functools.partial(_conv_kernel, tm=tm, tiles_per_seq=tiles_per_seq, seq_rows=seq_rows),
        grid=(n_tiles,),
        in_specs=in_specs,
        out_specs=[row_spec, ust_spec],
        out_shape=[jax.ShapeDtypeStruct((rows, d), F32), ust_shape],
        scratch_shapes=scratch,
        compiler_params=_cparams("arbitrary"),
        name="conv_layer",
    )(*args)


def _ffn_kernel(x_ref, mod_ref, g_ref, wgu_ref, wd_ref, xo_ref, *, d_ff, n_chunks):
    x = x_ref[...]
    h = _norm_mod(x, g_ref[...], _mod_part(mod_ref, 3), _mod_part(mod_ref, 4)).astype(BF16)
    fc = d_ff // n_chunks
    acc = None
    for c in range(n_chunks):
        gate = jnp.dot(h, wgu_ref[:, c * fc:(c + 1) * fc], preferred_element_type=F32)
        up = jnp.dot(h, wgu_ref[:, d_ff + c * fc:d_ff + (c + 1) * fc], preferred_element_type=F32)
        a = (_silu(gate) * up).astype(BF16)
        part = jnp.dot(a, wd_ref[c * fc:(c + 1) * fc, :], preferred_element_type=F32)
        acc = part if acc is None else acc + part
    xo_ref[...] = x + _mod_part(mod_ref, 5) * acc


def _ffn_layer(x, mod, tiles_per_seq, g, w_gu, w_down, tm):
    rows, d = x.shape
    d_ff = w_down.shape[0]
    mr = mod.shape[1]
    row_spec = pl.BlockSpec((tm, d), lambda i: (i, 0))
    return pl.pallas_call(
        functools.partial(_ffn_kernel, d_ff=d_ff, n_chunks=2),
        grid=(rows // tm,),
        in_specs=[row_spec,
                  pl.BlockSpec((1, mr, 6 * d), lambda i: (i // tiles_per_seq, 0, 0)),
                  pl.BlockSpec((1, d), lambda i: (0, 0)),
                  _resident((d, 2 * d_ff), lambda i: (0, 0)),
                  _resident((d_ff, d), lambda i: (0, 0))],
        out_specs=row_spec,
        out_shape=jax.ShapeDtypeStruct((rows, d), F32),
        compiler_params=_cparams("parallel"),
        name="ffn_dense",
    )(x, mod, g.reshape(1, d), w_gu, w_down)


def _router_kernel(x_ref, mod_ref, g_ref, wr_ref, br_ref, h_ref, gates_ref):
    x = x_ref[...]
    h = _norm_mod(x, g_ref[...], _mod_part(mod_ref, 3), _mod_part(mod_ref, 4))
    h_ref[...] = h.astype(BF16)
    logits = jnp.dot(h, wr_ref[...], preferred_element_type=F32, precision=HIGHEST) + br_ref[...]
    idx = lax.broadcasted_iota(jnp.int32, logits.shape, 1).astype(F32)
    m1 = jnp.max(logits, axis=1, keepdims=True)
    i1 = jnp.min(jnp.where(logits == m1, idx, float(N_EXPERTS)), axis=1, keepdims=True)
    sel1 = idx == i1
    rest = jnp.where(sel1, -jnp.inf, logits)
    m2 = jnp.max(rest, axis=1, keepdims=True)
    i2 = jnp.min(jnp.where(rest == m2, idx, float(N_EXPERTS)), axis=1, keepdims=True)
    sel2 = idx == i2
    e2 = jnp.exp(m2 - m1)
    den = 1.0 + e2
    gates_ref[...] = jnp.where(sel1, 1.0 / den, 0.0) + jnp.where(sel2, e2 / den, 0.0)


def _router(x, mod, tiles_per_seq, g, w_router, b_router, tm):
    rows, d = x.shape
    mr = mod.shape[1]
    row_spec = pl.BlockSpec((tm, d), lambda i: (i, 0))
    return pl.pallas_call(
        _router_kernel,
        grid=(rows // tm,),
        in_specs=[row_spec,
                  pl.BlockSpec((1, mr, 6 * d), lambda i: (i // tiles_per_seq, 0, 0)),
                  pl.BlockSpec((1, d), lambda i: (0, 0)),
                  pl.BlockSpec((d, N_EXPERTS), lambda i: (0, 0)),
                  pl.BlockSpec((1, N_EXPERTS), lambda i: (0, 0))],
        out_specs=[row_spec, pl.BlockSpec((tm, N_EXPERTS), lambda i: (i, 0))],
        out_shape=[jax.ShapeDtypeStruct((rows, d), BF16),
                   jax.ShapeDtypeStruct((rows, N_EXPERTS), F32)],
        compiler_params=_cparams("parallel"),
        name="moe_router",
    )(x, mod, g.reshape(1, d), w_router, b_router.reshape(1, N_EXPERTS))


def _moe_kernel(x_ref, mod_ref, h_ref, gates_ref, wg_ref, wu_ref, wd_ref, xo_ref, acc_ref):
    e = pl.program_id(1)
    f = pl.program_id(2)

    @pl.when(jnp.logical_and(e == 0, f == 0))
    def _():
        acc_ref[...] = jnp.zeros_like(acc_ref)

    h = h_ref[...]
    gate = jnp.dot(h, wg_ref[0], preferred_element_type=F32)
    up = jnp.dot(h, wu_ref[0], preferred_element_type=F32)
    a = (_silu(gate) * up).astype(BF16)
    y = jnp.dot(a, wd_ref[0], preferred_element_type=F32)
    gates = gates_ref[...]
    lane = lax.broadcasted_iota(jnp.int32, gates.shape, 1)
    w_e = jnp.sum(jnp.where(lane == e, gates, 0.0), axis=1, keepdims=True)
    acc_ref[...] += w_e * y

    @pl.when(jnp.logical_and(e == pl.num_programs(1) - 1, f == pl.num_programs(2) - 1))
    def _():
        xo_ref[...] = x_ref[...] + _mod_part(mod_ref, 5) * acc_ref[...]


def _moe_layer(x, mod, tiles_per_seq, g, w_router, b_router, w_gu, w_down, tm):
    rows, d = x.shape
    mr = mod.shape[1]
    d_ff = w_down.shape[1]
    fc = MOE_FF_CHUNK
    nf = d_ff // fc
    h, gates = _router(x, mod, tiles_per_seq, g, w_router, b_router, tm)
    row_spec = pl.BlockSpec((tm, d), lambda i, e, f: (i, 0))
    return pl.pallas_call(
        _moe_kernel,
        grid=(rows // tm, N_EXPERTS, nf),
        in_specs=[row_spec,
                  pl.BlockSpec((1, mr, 6 * d), lambda i, e, f: (i // tiles_per_seq, 0, 0)),
                  row_spec,
                  pl.BlockSpec((tm, N_EXPERTS), lambda i, e, f: (i, 0)),
                  pl.BlockSpec((1, d, fc), lambda i, e, f: (e, 0, f)),
                  pl.BlockSpec((1, d, fc), lambda i, e, f: (e, 0, nf + f)),
                  pl.BlockSpec((1, fc, d), lambda i, e, f: (e, f, 0))],
        out_specs=row_spec,
        out_shape=jax.ShapeDtypeStruct((rows, d), F32),
        scratch_shapes=[pltpu.VMEM((tm, d), F32)],
        compiler_params=_cparams("parallel", "arbitrary", "arbitrary"),
        name="moe_experts",
    )(x, mod, h, gates, w_gu, w_gu, w_down)


SEG_ALIGN = 16
SEG_SIZES = (512, 256, 128, 64, 32, 16)
SPARSE_TILE = 512
EXPERT_ROW_TILE = 256


def _route_kernel(x_ref, mod_ref, g_ref, wrt_ref, br_ref, h_ref, slots_ref, gw_ref, seg_ref, *, tm):
    x = x_ref[...]
    h = _norm_mod(x, g_ref[...], _mod_part(mod_ref, 3), _mod_part(mod_ref, 4))
    h_ref[...] = h.astype(BF16)
    logits = lax.dot_general(wrt_ref[...], h, (((1,), (1,)), ((), ())), preferred_element_type=F32,
                             precision=HIGHEST) + br_ref[...]
    eidx = lax.broadcasted_iota(jnp.int32, logits.shape, 0).astype(F32)
    m1 = jnp.max(logits, axis=0, keepdims=True)
    i1 = jnp.min(jnp.where(logits == m1, eidx, float(N_EXPERTS)), axis=0, keepdims=True)
    sel1 = eidx == i1
    rest = jnp.where(sel1, -jnp.inf, logits)
    m2 = jnp.max(rest, axis=0, keepdims=True)
    i2 = jnp.min(jnp.where(rest == m2, eidx, float(N_EXPERTS)), axis=0, keepdims=True)
    sel2 = eidx == i2
    e2 = jnp.exp(m2 - m1)
    den = 1.0 + e2
    gw_ref[0] = jnp.concatenate([1.0 / den, e2 / den], axis=0)
    mask = jnp.where(sel1 | sel2, 1.0, 0.0)
    before = (lax.broadcasted_iota(jnp.int32, (tm, tm), 0) < lax.broadcasted_iota(jnp.int32, (tm, tm), 1))
    rank = jnp.dot(mask.astype(BF16), jnp.where(before, 1.0, 0.0).astype(BF16), preferred_element_type=F32)
    count = jnp.sum(mask, axis=1, keepdims=True)
    seg = jnp.floor((count + (SEG_ALIGN - 1)) * (1.0 / SEG_ALIGN)) * SEG_ALIGN
    start = jnp.zeros_like(seg)
    for e in range(N_EXPERTS - 1):
        start = start + jnp.where(eidx[:, 0:1] > e, seg[e:e + 1, :], 0.0)
    slot = start + rank
    slot1 = jnp.sum(jnp.where(sel1, slot, 0.0), axis=0, keepdims=True)
    slot2 = jnp.sum(jnp.where(sel2, slot, 0.0), axis=0, keepdims=True)
    slots_ref[0] = jnp.concatenate([slot1, slot2], axis=0).astype(jnp.int32)
    seg_ref[0] = jnp.broadcast_to(seg, (N_EXPERTS, HEAD_DIM)).astype(jnp.int32)


def _route(x, mod, tiles_per_seq, g, w_router, b_router, tm):
    rows, d = x.shape
    mr = mod.shape[1]
    n_tiles = rows // tm
    row_spec = pl.BlockSpec((tm, d), lambda i: (i, 0))
    pair_spec = pl.BlockSpec((1, 2, tm), lambda i: (i, 0, 0))
    return pl.pallas_call(
        functools.partial(_route_kernel, tm=tm),
        grid=(n_tiles,),
        in_specs=[row_spec,
                  pl.BlockSpec((1, mr, 6 * d), lambda i: (i // tiles_per_seq, 0, 0)),
                  pl.BlockSpec((1, d), lambda i: (0, 0)),
                  pl.BlockSpec((N_EXPERTS, d), lambda i: (0, 0)),
                  pl.BlockSpec((N_EXPERTS, 1), lambda i: (0, 0))],
        out_specs=[row_spec, pair_spec, pair_spec,
                   pl.BlockSpec((1, N_EXPERTS, HEAD_DIM), lambda i: (i, 0, 0))],
        out_shape=[jax.ShapeDtypeStruct((rows, d), BF16),
                   jax.ShapeDtypeStruct((n_tiles, 2, tm), jnp.int32),
                   jax.ShapeDtypeStruct((n_tiles, 2, tm), F32),
                   jax.ShapeDtypeStruct((n_tiles, N_EXPERTS, HEAD_DIM), jnp.int32)],
        compiler_params=_cparams("parallel"),
        name="moe_route",
    )(x, mod, g.reshape(1, d), w_router.T, b_router.reshape(N_EXPERTS, 1))


def _segment_copies(tile, seg_ref, loc_ref, goff_ref, packed, sorted_hbm, sem, to_sorted):
    copies = []
    max_seg = (packed.shape[0] - N_EXPERTS * SEG_ALIGN) // 2
    for e in range(N_EXPERTS):
        seg = seg_ref[tile * N_EXPERTS + e]
        loc = loc_ref[tile * N_EXPERTS + e]
        goff = goff_ref[tile * N_EXPERTS + e]
        done = jnp.int32(0)
        for size in (s for s in SEG_SIZES if s <= max_seg):
            vm = packed.at[pl.ds(pl.multiple_of(loc + done, SEG_ALIGN), size), :]
            hb = sorted_hbm.at[pl.ds(pl.multiple_of(goff + done, SEG_ALIGN), size), :]
            src, dst = (vm, hb) if to_sorted else (hb, vm)
            copies.append(((seg & size) != 0, pltpu.make_async_copy(src, dst, sem)))
            done = done + (seg & size)
    return copies


def _start_all(copies):
    for cond, cp in copies:
        pl.when(cond)(cp.start)


def _wait_all(copies):
    for cond, cp in copies:
        pl.when(cond)(cp.wait)


def _dispatch_kernel(seg_ref, loc_ref, goff_ref, h_ref, slots_ref, sorted_in, sorted_out, stage_ref, sems,
                     *, n_slots, n_tiles):
    del sorted_in
    i = pl.program_id(0)
    par = i % 2

    def copies(tile, parity):
        return _segment_copies(tile, seg_ref, loc_ref, goff_ref, stage_ref.at[parity], sorted_out,
                               sems.at[parity], True)

    @pl.when(i >= 2)
    def _():
        _wait_all(copies(i - 2, par))

    tm = h_ref.shape[0]
    srow = lax.broadcasted_iota(jnp.int32, (n_slots, tm), 0)
    onehot = (srow == slots_ref[0, 0:1, :]) | (srow == slots_ref[0, 1:2, :])
    packed = jnp.dot(jnp.where(onehot, 1.0, 0.0).astype(BF16), h_ref[...], preferred_element_type=F32)
    stage_ref[par] = packed.astype(BF16)
    _start_all(copies(i, par))

    @pl.when(i == n_tiles - 1)
    def _():
        if n_tiles >= 2:
            _wait_all(copies(i - 1, 1 - par))
        _wait_all(copies(i, par))


def _dispatch(h, slots, seg, loc, goff, n_sorted_rows, tm):
    rows, d = h.shape
    n_tiles = rows // tm
    n_slots = 2 * tm + N_EXPERTS * SEG_ALIGN
    sorted_zero = jnp.zeros((n_sorted_rows, d), BF16)
    return pl.pallas_call(
        functools.partial(_dispatch_kernel, n_slots=n_slots, n_tiles=n_tiles),
        grid_spec=pltpu.PrefetchScalarGridSpec(
            num_scalar_prefetch=3,
            grid=(n_tiles,),
            in_specs=[pl.BlockSpec((tm, d), lambda i, *_: (i, 0)),
                      pl.BlockSpec((1, 2, tm), lambda i, *_: (i, 0, 0)),
                      pl.BlockSpec(memory_space=pl.ANY)],
            out_specs=pl.BlockSpec(memory_space=pl.ANY),
            scratch_shapes=[pltpu.VMEM((2, n_slots, d), BF16), pltpu.SemaphoreType.DMA((2,))]),
        out_shape=jax.ShapeDtypeStruct((n_sorted_rows, d), BF16),
        input_output_aliases={5: 0},
        compiler_params=_cparams("arbitrary"),
        name="moe_dispatch",
    )(seg, loc, goff, h, slots, sorted_zero)


def _experts_kernel(te_ref, jb_ref, nact_ref, h_ref, wgu_ref, wd_ref, y_ref, *, d_ff, fc):
    del te_ref, jb_ref

    @pl.when(pl.program_id(0) < nact_ref[0])
    def _():
        h = h_ref[...]
        acc = None
        for c in range(d_ff // fc):
            gate = jnp.dot(h, wgu_ref[0, :, c * fc:(c + 1) * fc], preferred_element_type=F32)
            up = jnp.dot(h, wgu_ref[0, :, d_ff + c * fc:d_ff + (c + 1) * fc], preferred_element_type=F32)
            a = (_silu(gate) * up).astype(BF16)
            part = jnp.dot(a, wd_ref[0, c * fc:(c + 1) * fc, :], preferred_element_type=F32)
            acc = part if acc is None else acc + part
        y_ref[...] = acc.astype(BF16)

    @pl.when(pl.program_id(0) >= nact_ref[0])
    def _():
        y_ref[...] = jnp.zeros_like(y_ref)


def _experts(h_sorted, tile_expert, tile_block, n_active, w_gu, w_down, tmf):
    n_rows, d = h_sorted.shape
    d_ff = w_down.shape[1]
    return pl.pallas_call(
        functools.partial(_experts_kernel, d_ff=d_ff, fc=MOE_FF_CHUNK),
        grid_spec=pltpu.PrefetchScalarGridSpec(
            num_scalar_prefetch=3,
            grid=(n_rows // tmf,),
            in_specs=[pl.BlockSpec((tmf, d), lambda j, te, jb, na: (jb[j], 0)),
                      pl.BlockSpec((1, d, 2 * d_ff), lambda j, te, jb, na: (te[j], 0, 0)),
                      pl.BlockSpec((1, d_ff, d), lambda j, te, jb, na: (te[j], 0, 0))],
            out_specs=pl.BlockSpec((tmf, d), lambda j, te, jb, na: (j, 0))),
        out_shape=jax.ShapeDtypeStruct((n_rows, d), BF16),
        compiler_params=pltpu.CompilerParams(dimension_semantics=("arbitrary",),
                                             vmem_limit_bytes=60 * 1024 * 1024),
        name="moe_experts_sorted",
    )(tile_expert, tile_block, n_active, h_sorted, w_gu, w_down)


def _combine_kernel(seg_ref, loc_ref, goff_ref, x_ref, mod_ref, slots_ref, gw_ref, sorted_hbm, xo_ref,
                    stage_ref, sems, *, n_slots, n_tiles):
    i = pl.program_id(0)
    par = i % 2

    def copies(tile, parity):
        return _segment_copies(tile, seg_ref, loc_ref, goff_ref, stage_ref.at[parity], sorted_hbm,
                               sems.at[parity], False)

    @pl.when(i == 0)
    def _():
        stage_ref[...] = jnp.zeros_like(stage_ref)
        _start_all(copies(i, par))

    @pl.when(i + 1 < n_tiles)
    def _():
        _start_all(copies(i + 1, 1 - par))

    _wait_all(copies(i, par))
    y = stage_ref[par]
    tm = x_ref.shape[0]
    scol = lax.broadcasted_iota(jnp.int32, (tm, n_slots), 1)
    moe = None
    for k in range(2):
        onehot = jnp.where(scol == slots_ref[:, k:k + 1], 1.0, 0.0).astype(BF16)
        part = gw_ref[:, k:k + 1] * jnp.dot(onehot, y, preferred_element_type=F32)
        moe = part if moe is None else moe + part
    xo_ref[...] = x_ref[...] + _mod_part(mod_ref, 5) * moe


def _combine(x, mod, tiles_per_seq, slots_col, gw_col, y_sorted, seg, loc, goff, tm):
    rows, d = x.shape
    mr = mod.shape[1]
    n_tiles = rows // tm
    n_slots = 2 * tm + N_EXPERTS * SEG_ALIGN
    row_spec = pl.BlockSpec((tm, d), lambda i, *_: (i, 0))
    pair_spec = pl.BlockSpec((tm, 2), lambda i, *_: (i, 0))
    return pl.pallas_call(
        functools.partial(_combine_kernel, n_slots=n_slots, n_tiles=n_tiles),
        grid_spec=pltpu.PrefetchScalarGridSpec(
            num_scalar_prefetch=3,
            grid=(n_tiles,),
            in_specs=[row_spec,
                      pl.BlockSpec((1, mr, 6 * d), lambda i, *_: (i // tiles_per_seq, 0, 0)),
                      pair_spec, pair_spec,
                      pl.BlockSpec(memory_space=pl.ANY)],
            out_specs=row_spec,
            scratch_shapes=[pltpu.VMEM((2, n_slots, d), BF16), pltpu.SemaphoreType.DMA((2,))]),
        out_shape=jax.ShapeDtypeStruct((rows, d), F32),
        compiler_params=_cparams("arbitrary"),
        name="moe_combine",
    )(seg, loc, goff, x, mod, slots_col, gw_col, y_sorted)


def _moe_sparse_layer(x, mod, tiles_per_seq, g, w_router, b_router, w_gu, w_down, tm):
    rows, d = x.shape
    n_tiles = rows // tm
    tmf = EXPERT_ROW_TILE
    h, slots, gw, seg3 = _route(x, mod, tiles_per_seq, g, w_router, b_router, tm)
    seg = seg3[:, :, 0]
    loc = jnp.cumsum(seg, axis=1) - seg
    region = ((jnp.sum(seg, axis=0) + tmf - 1) // tmf) * tmf
    region_end = jnp.cumsum(region)
    goff = (region_end - region)[None, :] + jnp.cumsum(seg, axis=0) - seg
    n_sorted = 2 * rows + n_tiles * N_EXPERTS * (SEG_ALIGN - 1) + N_EXPERTS * (tmf - SEG_ALIGN)
    n_sorted = ((n_sorted + tmf - 1) // tmf) * tmf
    n_row_tiles = n_sorted // tmf
    n_active = region_end[-1] // tmf
    tile_ids = jnp.arange(n_row_tiles, dtype=jnp.int32)
    tile_block = jnp.minimum(tile_ids, n_active - 1)
    tile_expert = jnp.minimum(jnp.searchsorted(region_end // tmf, tile_block, side="right"),
                              N_EXPERTS - 1).astype(jnp.int32)
    flat = lambda a: a.reshape(-1).astype(jnp.int32)
    seg_f, loc_f, goff_f = flat(seg), flat(loc), flat(goff)
    h_sorted = _dispatch(h, slots, seg_f, loc_f, goff_f, n_sorted, tm)
    y_sorted = _experts(h_sorted, tile_expert, tile_block.astype(jnp.int32),
                        n_active.reshape(1).astype(jnp.int32), w_gu, w_down, tmf)
    slots_col = jnp.transpose(slots, (0, 2, 1)).reshape(rows, 2)
    gw_col = jnp.transpose(gw, (0, 2, 1)).reshape(rows, 2)
    return _combine(x, mod, tiles_per_seq, slots_col, gw_col, y_sorted, seg_f, loc_f, goff_f, tm)


def _rope_tables(pos):
    half = ROT_DIM // 2
    inv = jnp.float32(ROPE_THETA) ** (-jnp.arange(half, dtype=jnp.float32) / half)
    ang = pos.astype(jnp.float32)[:, None] * inv[None, :]
    cos, sin = jnp.cos(ang), jnp.sin(ang)
    t = pos.shape[0]
    pad1 = jnp.ones((t, HEAD_DIM - ROT_DIM), F32)
    pad0 = jnp.zeros((t, HEAD_DIM - ROT_DIM), F32)
    return (jnp.concatenate([cos, cos, pad1], axis=1), jnp.concatenate([-sin, sin, pad0], axis=1))


def _head_norm_rope(xh, g, cos, sin):
    half = ROT_DIM // 2
    xn = xh * lax.rsqrt(jnp.mean(xh * xh, axis=-1, keepdims=True) + EPS) * g
    lane = lax.broadcasted_iota(jnp.int32, xn.shape, 1)
    swapped = jnp.where(lane < half, pltpu.roll(xn, HEAD_DIM - half, 1), pltpu.roll(xn, half, 1))
    return xn * cos + swapped * sin


def _kv_kernel(x_ref, mod_ref, g_ref, wkv_ref, gk_ref, cos_ref, sin_ref, *out_refs):
    x = x_ref[...]
    h = _norm_mod(x, g_ref[...], mod_ref[0, :, 0:D_MODEL], mod_ref[0, :, D_MODEL:2 * D_MODEL]).astype(BF16)
    cos, sin, gk = cos_ref[...], sin_ref[...], gk_ref[...]
    k_refs, v_refs = out_refs[:N_GROUPS], out_refs[N_GROUPS:]
    for hd in range(N_KV_HEADS):
        kh = jnp.dot(h, wkv_ref[:, hd * HEAD_DIM:(hd + 1) * HEAD_DIM], preferred_element_type=F32)
        vh = jnp.dot(h, wkv_ref[:, KV_WIDTH + hd * HEAD_DIM:KV_WIDTH + (hd + 1) * HEAD_DIM],
                     preferred_element_type=F32)
        g_idx, k_idx = divmod(hd, HKV_G)
        sl = slice(k_idx * HEAD_DIM, (k_idx + 1) * HEAD_DIM)
        k_refs[g_idx][:, sl] = _head_norm_rope(kh, gk, cos, sin)
        v_refs[g_idx][:, sl] = vh


def _kv_proj(x, mod_kv, tiles_per_seq, g, w_kv, g_k, cos, sin, tm):
    rows, d = x.shape
    mr = mod_kv.shape[1]
    row_spec = pl.BlockSpec((tm, d), lambda i: (i, 0))
    tab_spec = pl.BlockSpec((tm, HEAD_DIM), lambda i: (i % tiles_per_seq, 0))
    out_spec = pl.BlockSpec((tm, GROUP_KV_WIDTH), lambda i: (i, 0))
    return pl.pallas_call(
        _kv_kernel,
        grid=(rows // tm,),
        in_specs=[row_spec,
                  pl.BlockSpec((1, mr, 2 * d), lambda i: (i // tiles_per_seq, 0, 0)),
                  pl.BlockSpec((1, d), lambda i: (0, 0)),
                  _resident((d, 2 * KV_WIDTH), lambda i: (0, 0)),
                  pl.BlockSpec((1, HEAD_DIM), lambda i: (0, 0)),
                  tab_spec, tab_spec],
        out_specs=[out_spec] * (2 * N_GROUPS),
        out_shape=[jax.ShapeDtypeStruct((rows, GROUP_KV_WIDTH), F32)] * (2 * N_GROUPS),
        compiler_params=_cparams("parallel"),
        name="kv_proj",
    )(x, mod_kv, g.reshape(1, d), w_kv, g_k.reshape(1, HEAD_DIM), cos, sin)


def _q_kernel(x_ref, mod_ref, g_ref, wq_ref, gq_ref, cos_ref, sin_ref, q_ref):
    x = x_ref[...]
    h = _norm_mod(x, g_ref[...], _mod_part(mod_ref, 0), _mod_part(mod_ref, 1)).astype(BF16)
    cos, sin, gq = cos_ref[...], sin_ref[...], gq_ref[...]
    for hd in range(N_Q_HEADS):
        sl = slice(hd * HEAD_DIM, (hd + 1) * HEAD_DIM)
        qh = jnp.dot(h, wq_ref[:, sl], preferred_element_type=F32)
        q_ref[:, sl] = (_head_norm_rope(qh, gq, cos, sin) * ATTN_SCALE).astype(BF16)


def _q_proj(x, mod, tiles_per_seq, g, w_q, g_q, cos, sin, tm):
    rows, d = x.shape
    mr = mod.shape[1]
    row_spec = pl.BlockSpec((tm, d), lambda i: (i, 0))
    tab_spec = pl.BlockSpec((tm, HEAD_DIM), lambda i: (i % tiles_per_seq, 0))
    return pl.pallas_call(
        _q_kernel,
        grid=(rows // tm,),
        in_specs=[row_spec,
                  pl.BlockSpec((1, mr, 6 * d), lambda i: (i // tiles_per_seq, 0, 0)),
                  pl.BlockSpec((1, d), lambda i: (0, 0)),
                  _resident((d, Q_WIDTH), lambda i: (0, 0)),
                  pl.BlockSpec((1, HEAD_DIM), lambda i: (0, 0)),
                  tab_spec, tab_spec],
        out_specs=pl.BlockSpec((tm, Q_WIDTH), lambda i: (i, 0)),
        out_shape=jax.ShapeDtypeStruct((rows, Q_WIDTH), BF16),
        compiler_params=_cparams("parallel"),
        name="q_proj",
    )(x, mod, g.reshape(1, d), w_q, g_q.reshape(1, HEAD_DIM), cos, sin)


def _attn_prompt_kernel(q_ref, kp_ref, kc_ref, vp_ref, vc_ref, o_ref, st_ref):
    blk = pl.program_id(2)
    nq = REP * Q_BLOCK
    row = lax.broadcasted_iota(jnp.int32, (nq, 2 * Q_BLOCK), 0) & (Q_BLOCK - 1)
    col = lax.broadcasted_iota(jnp.int32, (nq, 2 * Q_BLOCK), 1)
    valid = (col >= row) & (col <= row + Q_BLOCK) & ((blk > 0) | (col >= Q_BLOCK))
    lane = lax.broadcasted_iota(jnp.int32, (Q_BLOCK, HEAD_DIM), 1)
    for k in range(HKV_G):
        ksl = slice(k * HEAD_DIM, (k + 1) * HEAD_DIM)
        q = jnp.concatenate(
            [q_ref[0, :, (k * REP + r) * HEAD_DIM:(k * REP + r + 1) * HEAD_DIM] for r in range(REP)], axis=0)
        keys = jnp.concatenate([kp_ref[0, :, ksl], kc_ref[0, :, ksl]], axis=0).astype(BF16)
        vals = jnp.concatenate([vp_ref[0, :, ksl], vc_ref[0, :, ksl]], axis=0).astype(BF16)
        s = lax.dot_general(q, keys, (((1,), (1,)), ((), ())), preferred_element_type=F32)
        s = jnp.where(valid, s, NEG)
        m = jnp.max(s, axis=1, keepdims=True)
        p = jnp.exp(s - m)
        den = jnp.sum(p, axis=1, keepdims=True)
        out = jnp.dot(p.astype(BF16), vals, preferred_element_type=F32) / den
        lse = m + jnp.log(den)
        stats = jnp.zeros((Q_BLOCK, HEAD_DIM), F32)
        for r in range(REP):
            rs = slice(r * Q_BLOCK, (r + 1) * Q_BLOCK)
            o_ref[0, :, (k * REP + r) * HEAD_DIM:(k * REP + r + 1) * HEAD_DIM] = out[rs].astype(BF16)
            stats = jnp.where(lane == r, lse[rs], stats)
        st_ref[0, :, ksl] = stats


def _attn_prompt_group(q, k_g, v_g, group, batch, seq):
    _, dil = DIL_GROUPS[group]
    sub = seq // dil
    n_blk = sub // Q_BLOCK
    qv = q.reshape(batch, sub, dil * Q_WIDTH)
    kv = k_g.reshape(batch, sub, dil * GROUP_KV_WIDTH)
    vv = v_g.reshape(batch, sub, dil * GROUP_KV_WIDTH)
    q_cols = Q_WIDTH // O_WIDTH
    cur = pl.BlockSpec((1, Q_BLOCK, GROUP_KV_WIDTH), lambda b, r, j: (b, j, r))
    prev = pl.BlockSpec((1, Q_BLOCK, GROUP_KV_WIDTH), lambda b, r, j: (b, jnp.maximum(j - 1, 0), r))
    out, stats = pl.pallas_call(
        _attn_prompt_kernel,
        grid=(batch, dil, n_blk),
        in_specs=[pl.BlockSpec((1, Q_BLOCK, O_WIDTH), lambda b, r, j: (b, j, r * q_cols + group)),
                  prev, cur, prev, cur],
        out_specs=[pl.BlockSpec((1, Q_BLOCK, O_WIDTH), lambda b, r, j: (b, j, r)), cur],
        out_shape=[jax.ShapeDtypeStruct((batch, sub, dil * O_WIDTH), BF16),
                   jax.ShapeDtypeStruct((batch, sub, dil * GROUP_KV_WIDTH), F32)],
        compiler_params=_cparams("parallel", "parallel", "arbitrary"),
        name=f"attn_prompt_g{group}",
    )(qv, kv, kv, vv, vv)
    return out.reshape(batch * seq, O_WIDTH), stats.reshape(batch * seq, GROUP_KV_WIDTH)


def _merge_o_kernel(x_ref, mod_ref, o0_ref, o1_ref, o2_ref, s0_ref, s1_ref, s2_ref, wo_ref, xo_ref):
    o_refs = (o0_ref, o1_ref, o2_ref)
    s_refs = (s0_ref, s1_ref, s2_ref)
    heads = []
    for k in range(HKV_G):
        for r in range(REP):
            c = k * HEAD_DIM + r
            lses = [s[:, c:c + 1] for s in s_refs]
            m = jnp.maximum(jnp.maximum(lses[0], lses[1]), lses[2])
            es = [jnp.exp(l - m) for l in lses]
            den = es[0] + es[1] + es[2]
            sl = slice((k * REP + r) * HEAD_DIM, (k * REP + r + 1) * HEAD_DIM)
            o = sum((e / den) * o_ref[:, sl].astype(F32) for e, o_ref in zip(es, o_refs))
            heads.append(o.astype(BF16))
    o = jnp.concatenate(heads, axis=1)
    y = jnp.dot(o, wo_ref[...], preferred_element_type=F32)
    xo_ref[...] = x_ref[...] + _mod_part(mod_ref, 2) * y


def _merge_o(x, mod, tiles_per_seq, outs, stats, w_o, tm):
    rows, d = x.shape
    mr = mod.shape[1]
    row_spec = pl.BlockSpec((tm, d), lambda i: (i, 0))
    st_spec = pl.BlockSpec((tm, GROUP_KV_WIDTH), lambda i: (i, 0))
    return pl.pallas_call(
        _merge_o_kernel,
        grid=(rows // tm,),
        in_specs=[row_spec,
                  pl.BlockSpec((1, mr, 6 * d), lambda i: (i // tiles_per_seq, 0, 0)),
                  pl.BlockSpec((tm, O_WIDTH), lambda i: (i, 0))] + [pl.BlockSpec((tm, O_WIDTH), lambda i: (i, 0))] * 2
                 + [st_spec] * 3 + [_resident((O_WIDTH, d), lambda i: (0, 0))],
        out_specs=row_spec,
        out_shape=jax.ShapeDtypeStruct((rows, d), F32),
        compiler_params=_cparams("parallel"),
        name="merge_o_proj",
    )(x, mod, *outs, *stats, w_o)


def _attn_sample_kernel(*refs, n_new):
    q_ref = refs[0]
    nk_refs = refs[1:1 + N_GROUPS]
    nv_refs = refs[1 + N_GROUPS:1 + 2 * N_GROUPS]
    ck_refs = refs[1 + 2 * N_GROUPS:1 + 3 * N_GROUPS]
    cv_refs = refs[1 + 3 * N_GROUPS:1 + 4 * N_GROUPS]
    o_ref = refs[1 + 4 * N_GROUPS]
    ok_refs = refs[2 + 4 * N_GROUPS:2 + 5 * N_GROUPS]
    ov_refs = refs[2 + 5 * N_GROUPS:2 + 6 * N_GROUPS]
    nq = REP * n_new
    qi = lax.broadcasted_iota(jnp.int32, (nq, 1), 0) & (n_new - 1)

    for g in range(N_GROUPS):
        for c_ref, n_ref, o_buf in ((ck_refs[g], nk_refs[g], ok_refs[g]), (cv_refs[g], nv_refs[g], ov_refs[g])):
            length = c_ref.shape[1]
            rolled = pltpu.roll(c_ref[0], length - n_new, 0)
            o_buf[0, 0:length - 8, :] = rolled[0:length - 8, :]
            row8 = lax.broadcasted_iota(jnp.int32, (8, 1), 0)
            o_buf[0, length - 8:length, :] = jnp.where(row8 >= 8 - n_new, n_ref[0], rolled[length - 8:length, :])

    for k in range(HKV_G):
        ksl = slice(k * HEAD_DIM, (k + 1) * HEAD_DIM)
        outs, lses = [], []
        for g, (window, dil) in enumerate(DIL_GROUPS):
            length = ck_refs[g].shape[1]
            q = q_ref[0, g, k]
            qf = q.astype(F32)
            keys = ck_refs[g][0, :, ksl].astype(BF16)
            vals = cv_refs[g][0, :, ksl].astype(BF16)
            s = lax.dot_general(q, keys, (((1,), (1,)), ((), ())), preferred_element_type=F32)
            idx = lax.broadcasted_iota(jnp.int32, (nq, length), 1)
            diff = length + qi - idx
            s = jnp.where(((diff & (dil - 1)) == 0) & (diff <= window), s, NEG)
            new_k = nk_refs[g][0, :, ksl]
            new_v = nv_refs[g][0, :, ksl]
            s_new = []
            for j in range(n_new):
                sj = jnp.sum(qf * new_k[8 - n_new + j:8 - n_new + j + 1, :], axis=1, keepdims=True)
                ok = (qi >= j) & (((qi - j) & (dil - 1)) == 0)
                s_new.append(jnp.where(ok, sj, NEG))
            m = jnp.max(s, axis=1, keepdims=True)
            for sj in s_new:
                m = jnp.maximum(m, sj)
            p = jnp.exp(s - m)
            den = jnp.sum(p, axis=1, keepdims=True)
            acc = jnp.dot(p.astype(BF16), vals, preferred_element_type=F32)
            for j, sj in enumerate(s_new):
                pj = jnp.exp(sj - m)
                den = den + pj
                acc = acc + pj * new_v[8 - n_new + j:8 - n_new + j + 1, :]
            outs.append(acc / den)
            lses.append(m + jnp.log(den))
        m = jnp.maximum(jnp.maximum(lses[0], lses[1]), lses[2])
        es = [jnp.exp(l - m) for l in lses]
        den = es[0] + es[1] + es[2]
        o_ref[0, k] = sum((e / den) * o for e, o in zip(es, outs))


def _attn_sample(q, new_k, new_v, cache_k, cache_v, n_seq, n_new):
    assert n_new <= 8
    q5 = q.reshape(n_seq, n_new, N_GROUPS, HKV_G, REP, HEAD_DIM)
    q5 = jnp.transpose(q5, (0, 2, 3, 4, 1, 5)).reshape(n_seq, N_GROUPS, HKV_G, REP * n_new, HEAD_DIM)
    pad = lambda a: jnp.pad(a.reshape(n_seq, n_new, GROUP_KV_WIDTH), ((0, 0), (8 - n_new, 0), (0, 0)))
    new_k8 = [pad(a) for a in new_k]
    new_v8 = [pad(a) for a in new_v]
    ck = [c.reshape(n_seq, c.shape[1], GROUP_KV_WIDTH) for c in cache_k]
    cv = [c.reshape(n_seq, c.shape[1], GROUP_KV_WIDTH) for c in cache_v]
    new_spec = pl.BlockSpec((1, 8, GROUP_KV_WIDTH), lambda b: (b, 0, 0))
    cache_specs = [pl.BlockSpec((1, c.shape[1], GROUP_KV_WIDTH), lambda b: (b, 0, 0)) for c in ck]
    cache_shapes = [jax.ShapeDtypeStruct(c.shape, F32) for c in ck]
    res = pl.pallas_call(
        functools.partial(_attn_sample_kernel, n_new=n_new),
        grid=(n_seq,),
        in_specs=[pl.BlockSpec((1, N_GROUPS, HKV_G, REP * n_new, HEAD_DIM), lambda b: (b, 0, 0, 0, 0))]
                 + [new_spec] * (2 * N_GROUPS) + cache_specs + cache_specs,
        out_specs=[pl.BlockSpec((1, HKV_G, REP * n_new, HEAD_DIM), lambda b: (b, 0, 0, 0))]
                  + cache_specs + cache_specs,
        out_shape=[jax.ShapeDtypeStruct((n_seq, HKV_G, REP * n_new, HEAD_DIM), F32)]
                  + cache_shapes + cache_shapes,
        compiler_params=_cparams("parallel"),
        name="attn_sample",
    )(q5, *new_k8, *new_v8, *ck, *cv)
    o = res[0].reshape(n_seq, HKV_G, REP, n_new, HEAD_DIM)
    o = jnp.transpose(o, (0, 3, 1, 2, 4)).reshape(n_seq * n_new, O_WIDTH)
    shape4 = lambda a: a.reshape(n_seq, a.shape[1], HKV_G, HEAD_DIM)
    new_ck = [shape4(a) for a in res[1:1 + N_GROUPS]]
    new_cv = [shape4(a) for a in res[1 + N_GROUPS:1 + 2 * N_GROUPS]]
    return o, new_ck, new_cv


def _o_proj_kernel(x_ref, mod_ref, o_ref, wo_ref, xo_ref):
    y = jnp.dot(o_ref[...].astype(BF16), wo_ref[...], preferred_element_type=F32)
    xo_ref[...] = x_ref[...] + _mod_part(mod_ref, 2) * y


def _o_proj(x, mod, o, w_o):
    rows, d = x.shape
    mr = mod.shape[1]
    full = pl.BlockSpec((rows, d), lambda i: (0, 0))
    return pl.pallas_call(
        _o_proj_kernel,
        grid=(1,),
        in_specs=[full, pl.BlockSpec((1, mr, 6 * d), lambda i: (0, 0, 0)),
                  pl.BlockSpec((rows, O_WIDTH), lambda i: (0, 0)),
                  pl.BlockSpec((O_WIDTH, d), lambda i: (0, 0))],
        out_specs=full,
        out_shape=jax.ShapeDtypeStruct((rows, d), F32),
        compiler_params=_cparams("arbitrary"),
        name="o_proj_sample",
    )(x, mod, o, w_o)


def _trunk(x, mods, mod_kv, pos, weights, tm, conv_prev, kv_prev):
    (g_norm_mix, g_norm_ffn, w_in, w_conv, w_out_conv, g_norm_kv, w_kv, g_k, w_q, g_q, w_o,
     w_gu_dense, w_down_dense, w_router, b_router, w_gu_moe, w_down_moe) = weights
    batch, seq, d = x.shape
    rows = batch * seq
    depth = g_norm_mix.shape[0]
    n_conv = w_in.shape[0]
    sample = kv_prev is not None
    tiles_per_seq = 1 if sample else seq // tm
    x = x.reshape(rows, d)
    cos, sin = _rope_tables(pos)
    if sample:
        cos, sin = jnp.tile(cos, (batch, 1)), jnp.tile(sin, (batch, 1))
    conv_state = []
    kv_state = None
    for layer in range(depth):
        mod = mods[layer]
        if layer == n_conv:
            kvs = _kv_proj(x, mod_kv, tiles_per_seq, g_norm_kv, w_kv, g_k, cos, sin, tm)
            k_new, v_new = kvs[:N_GROUPS], kvs[N_GROUPS:]
        if layer < n_conv:
            if sample:
                st = conv_prev[layer]
                zero = jnp.zeros((batch, seq - 2, d), F32)
                pa = jnp.concatenate([st[:, 1:2], jnp.zeros((batch, seq - 1, d), F32)], axis=1)
                pb = jnp.concatenate([st, zero], axis=1)
                prev = (seq, pa.reshape(rows, d), pb.reshape(rows, d))
            else:
                prev = None
            x, u_tail = _conv_layer(x, mod, tiles_per_seq, g_norm_mix[layer], w_in[layer], w_conv[layer],
                                    w_out_conv[layer], tm, prev)
            if sample:
                conv_state.append(u_tail.reshape(batch, seq, d)[:, seq - (CONV_W - 1):])
            else:
                conv_state.append(u_tail[:, 8 - (CONV_W - 1):])
        else:
            lb = layer - n_conv
            q = _q_proj(x, mod, tiles_per_seq, g_norm_mix[layer], w_q[lb], g_q[lb], cos, sin, tm)
            if sample:
                o, new_ck, new_cv = _attn_sample(q, k_new, v_new, kv_prev[0], kv_prev[1], batch, seq)
                kv_state = (new_ck, new_cv)
                x = _o_proj(x, mod, o, w_o[lb])
            else:
                res = [_attn_prompt_group(q, k_new[g], v_new[g], g, batch, seq) for g in range(N_GROUPS)]
                x = _merge_o(x, mod, tiles_per_seq, [r[0] for r in res], [r[1] for r in res], w_o[lb], tm)
        if layer % 2 == 0:
            x = _ffn_layer(x, mod, tiles_per_seq, g_norm_ffn[layer], w_gu_dense[layer // 2],
                           w_down_dense[layer // 2], tm)
        elif sample:
            x = _moe_layer(x, mod, 1, g_norm_ffn[layer], w_router[layer // 2], b_router[layer // 2],
                           w_gu_moe[layer // 2], w_down_moe[layer // 2], tm)
        else:
            tm_moe = min(SPARSE_TILE, seq)
            x = _moe_sparse_layer(x, mod, seq // tm_moe, g_norm_ffn[layer], w_router[layer // 2],
                                  b_router[layer // 2], w_gu_moe[layer // 2], w_down_moe[layer // 2], tm_moe)
    if not sample:
        shape4 = lambda a: a.reshape(batch, seq, HKV_G, HEAD_DIM)
        kv_state = ([shape4(k)[:, -min(w, seq):] for k, (w, _) in zip(k_new, DIL_GROUPS)],
                    [shape4(v)[:, -min(w, seq):] for v, (w, _) in zip(v_new, DIL_GROUPS)])
    return x.reshape(batch, seq, d), jnp.stack(conv_state, axis=0), kv_state


def kernel(x_prompt, x_sample, state_conv, cache_k_g0, cache_v_g0, cache_k_g1, cache_v_g1, cache_k_g2,
           cache_v_g2, c_prompt, c_sample, g_norm_mix, g_norm_ffn, w_ada, b_ada, w_in, w_conv, w_out_conv,
           g_norm_kv, w_ada_kv, b_ada_kv, w_kv, g_k, w_q, g_q, w_o, w_gu_dense, w_down_dense, w_router,
           b_router, w_gu_moe, w_down_moe):
    b_p, t_p, d = x_prompt.shape
    b_s, t_s, _ = x_sample.shape
    bf = lambda w: w.astype(BF16)
    weights = (g_norm_mix, g_norm_ffn, bf(w_in), w_conv, bf(w_out_conv), g_norm_kv, bf(w_kv), g_k, bf(w_q),
               g_q, bf(w_o), bf(w_gu_dense), bf(w_down_dense), w_router, b_router, bf(w_gu_moe),
               bf(w_down_moe))

    c_all = jnp.concatenate([c_prompt, c_sample], axis=0)
    mods_all = _ada(c_all, w_ada, b_ada)
    mod_kv_all = _ada(c_all, w_ada_kv[None], b_ada_kv[None])[0]
    depth = w_ada.shape[0]
    mods_p = [mods_all[l, :b_p, None, :] for l in range(depth)]
    mods_s = [jnp.repeat(mods_all[l, b_p:], t_s, axis=0)[None] for l in range(depth)]
    mod_kv_p = mod_kv_all[:b_p, None, :]
    mod_kv_s = jnp.repeat(mod_kv_all[b_p:], t_s, axis=0)[None]

    pos_p = jnp.arange(t_p, dtype=jnp.int32)
    y_prompt, conv_p, (kp, vp) = _trunk(x_prompt, mods_p, mod_kv_p, pos_p, weights,
                                        min(PROMPT_TILE, t_p), None, None)

    pos_s = PAST_LEN + jnp.arange(t_s, dtype=jnp.int32)
    kv_prev = ([cache_k_g0, cache_k_g1, cache_k_g2], [cache_v_g0, cache_v_g1, cache_v_g2])
    y_sample, conv_s, (ksn, vsn) = _trunk(x_sample, mods_s, mod_kv_s, pos_s, weights, b_s * t_s,
                                          state_conv, kv_prev)
    return (y_prompt, y_sample, conv_p, conv_s,
            kp[0], vp[0], kp[1], vp[1], kp[2], vp[2],
            ksn[0], vsn[0], ksn[1], vsn[1], ksn[2], vsn[2])
```

```python
import functools

import jax
import jax.numpy as jnp
from jax import lax
from jax.experimental import pallas as pl
from jax.experimental.pallas import tpu as pltpu

D_MODEL = 1024
PAST_LEN = 8192
CONV_W = 3
HEAD_DIM = 128
ROT_DIM = HEAD_DIM // 4
ROPE_THETA = 500000.0
DIL_GROUPS = ((128, 1), (512, 4), (2048, 16))
N_GROUPS = len(DIL_GROUPS)
HQ_G = 8
HKV_G = 2
REP = HQ_G // HKV_G
N_Q_HEADS = N_GROUPS * HQ_G
N_KV_HEADS = N_GROUPS * HKV_G
Q_WIDTH = N_Q_HEADS * HEAD_DIM
KV_WIDTH = N_KV_HEADS * HEAD_DIM
O_WIDTH = HQ_G * HEAD_DIM
GROUP_KV_WIDTH = HKV_G * HEAD_DIM
N_EXPERTS = 8
EPS = 1e-6
ATTN_SCALE = HEAD_DIM ** -0.5
NEG = -1e30
Q_BLOCK = 128

F32 = jnp.float32
BF16 = jnp.bfloat16
HIGHEST = lax.Precision.HIGHEST

V7X_VMEM_LIMIT_BYTES = 56 * 1024 * 1024
PROMPT_TILE = 512
MOE_TILE = 1024
MOE_FF_CHUNK = 512


def _cparams(*sem):
    return pltpu.CompilerParams(dimension_semantics=sem, vmem_limit_bytes=V7X_VMEM_LIMIT_BYTES)


def _resident(shape, index_map):
    return pl.BlockSpec(shape, index_map, pipeline_mode=pl.Buffered(1))


def _silu(x):
    return x * jax.nn.sigmoid(x)


def _norm_mod(x, g, shift, scale):
    y = x * lax.rsqrt(jnp.mean(x * x, axis=-1, keepdims=True) + EPS) * g
    return y * (1.0 + scale) + shift


def _mod_part(mod_ref, k):
    return mod_ref[0, :, k * D_MODEL:(k + 1) * D_MODEL]


def _ada_kernel(c_ref, w_ref, b_ref, o_ref):
    c = c_ref[...]
    o_ref[0] = jnp.dot(_silu(c), w_ref[0], preferred_element_type=F32, precision=HIGHEST) + b_ref[0]


def _ada(c, w, b):
    n_layers, d, n = w.shape
    m = c.shape[0]
    tn = 1024
    return pl.pallas_call(
        _ada_kernel,
        grid=(n_layers, n // tn),
        in_specs=[pl.BlockSpec((m, d), lambda l, j: (0, 0)),
                  pl.BlockSpec((1, d, tn), lambda l, j: (l, 0, j)),
                  pl.BlockSpec((1, 1, tn), lambda l, j: (l, 0, j))],
        out_specs=pl.BlockSpec((1, m, tn), lambda l, j: (l, 0, j)),
        out_shape=jax.ShapeDtypeStruct((n_layers, m, n), F32),
        compiler_params=_cparams("parallel", "parallel"),
        name="ada",
    )(c, w, b.reshape(n_layers, 1, n))


def _conv_kernel(*refs, tm, tiles_per_seq, seq_rows):
    if seq_rows is None:
        x_ref, mod_ref, g_ref, win_ref, wc_ref, wout_ref, xo_ref, ust_ref, carry_ref = refs
    else:
        x_ref, mod_ref, g_ref, win_ref, wc_ref, wout_ref, pa_ref, pb_ref, xo_ref, ust_ref = refs
    x = x_ref[...]
    h = _norm_mod(x, g_ref[...], _mod_part(mod_ref, 0), _mod_part(mod_ref, 1)).astype(BF16)
    b_gate = jnp.dot(h, win_ref[:, 0:D_MODEL], preferred_element_type=F32)
    c_gate = jnp.dot(h, win_ref[:, D_MODEL:2 * D_MODEL], preferred_element_type=F32)
    v = jnp.dot(h, win_ref[:, 2 * D_MODEL:3 * D_MODEL], preferred_element_type=F32)
    u = c_gate * v
    row = lax.broadcasted_iota(jnp.int32, (tm, 1), 0)
    r1 = pltpu.roll(u, 1, 0)
    r2 = pltpu.roll(u, 2, 0)
    if seq_rows is None:
        @pl.when(pl.program_id(0) % tiles_per_seq == 0)
        def _():
            carry_ref[...] = jnp.zeros_like(carry_ref)
        um1 = jnp.where(row == 0, carry_ref[7:8, :], r1)
        um2 = jnp.where(row == 0, carry_ref[6:7, :], jnp.where(row == 1, carry_ref[7:8, :], r2))
        carry_ref[...] = u[tm - 8:tm, :]
        ust_ref[0] = u[tm - 8:tm, :]
    else:
        rowm = row & (seq_rows - 1)
        um1 = jnp.where(rowm == 0, pa_ref[...], r1)
        um2 = jnp.where(rowm < 2, pb_ref[...], r2)
        ust_ref[...] = u
    y = um2 * wc_ref[0:1, :] + um1 * wc_ref[1:2, :] + u * wc_ref[2:3, :]
    z = (b_gate * y).astype(BF16)
    out = jnp.dot(z, wout_ref[...], preferred_element_type=F32)
    xo_ref[...] = x + _mod_part(mod_ref, 2) * out


def _conv_layer(x, mod, tiles_per_seq, g, w_in, w_conv, w_out, tm, prev=None):
    rows, d = x.shape
    n_tiles = rows // tm
    mr = mod.shape[1]
    row_spec = pl.BlockSpec((tm, d), lambda i: (i, 0))
    in_specs = [row_spec,
                pl.BlockSpec((1, mr, 6 * d), lambda i: (i // tiles_per_seq, 0, 0)),
                pl.BlockSpec((1, d), lambda i: (0, 0)),
                _resident((d, 3 * d), lambda i: (0, 0)),
                pl.BlockSpec((CONV_W, d), lambda i: (0, 0)),
                _resident((d, d), lambda i: (0, 0))]
    args = [x, mod, g.reshape(1, d), w_in, w_conv, w_out]
    if prev is None:
        seq_rows = None
        n_seq = n_tiles // tiles_per_seq
        ust_shape = jax.ShapeDtypeStruct((n_seq, 8, d), F32)
        ust_spec = pl.BlockSpec((1, 8, d), lambda i: (i // tiles_per_seq, 0, 0))
        scratch = [pltpu.VMEM((8, d), F32)]
    else:
        seq_rows, pa, pb = prev
        in_specs += [row_spec, row_spec]
        args += [pa, pb]
        ust_shape = jax.ShapeDtypeStruct((rows, d), F32)
        ust_spec = row_spec
        scratch = []
    return pl.pallas_call(
        functools.partial(_conv_kernel, tm=tm, tiles_per_seq=tiles_per_seq, seq_rows=seq_rows),
        grid=(n_tiles,),
        in_specs=in_specs,
        out_specs=[row_spec, ust_spec],
        out_shape=[jax.ShapeDtypeStruct((rows, d), F32), ust_shape],
        scratch_shapes=scratch,
        compiler_params=_cparams("arbitrary"),
        name="conv_layer",
    )(*args)


def _ffn_kernel(x_ref, mod_ref, g_ref, wgu_ref, wd_ref, xo_ref, *, d_ff, n_chunks):
    x = x_ref[...]
    h = _norm_mod(x, g_ref[...], _mod_part(mod_ref, 3), _mod_part(mod_ref, 4)).astype(BF16)
    fc = d_ff // n_chunks
    acc = None
    for c in range(n_chunks):
        gate = jnp.dot(h, wgu_ref[:, c * fc:(c + 1) * fc], preferred_element_type=F32)
        up = jnp.dot(h, wgu_ref[:, d_ff + c * fc:d_ff + (c + 1) * fc], preferred_element_type=F32)
        a = (_silu(gate) * up).astype(BF16)
        part = jnp.dot(a, wd_ref[c * fc:(c + 1) * fc, :], preferred_element_type=F32)
        acc = part if acc is None else acc + part
    xo_ref[...] = x + _mod_part(mod_ref, 5) * acc


def _ffn_layer(x, mod, tiles_per_seq, g, w_gu, w_down, tm):
    rows, d = x.shape
    d_ff = w_down.shape[0]
    mr = mod.shape[1]
    row_spec = pl.BlockSpec((tm, d), lambda i: (i, 0))
    return pl.pallas_call(
        functools.partial(_ffn_kernel, d_ff=d_ff, n_chunks=2),
        grid=(rows // tm,),
        in_specs=[row_spec,
                  pl.BlockSpec((1, mr, 6 * d), lambda i: (i // tiles_per_seq, 0, 0)),
                  pl.BlockSpec((1, d), lambda i: (0, 0)),
                  _resident((d, 2 * d_ff), lambda i: (0, 0)),
                  _resident((d_ff, d), lambda i: (0, 0))],
        out_specs=row_spec,
        out_shape=jax.ShapeDtypeStruct((rows, d), F32),
        compiler_params=_cparams("parallel"),
        name="ffn_dense",
    )(x, mod, g.reshape(1, d), w_gu, w_down)


def _router_kernel(x_ref, mod_ref, g_ref, wr_ref, br_ref, h_ref, gates_ref):
    x = x_ref[...]
    h = _norm_mod(x, g_ref[...], _mod_part(mod_ref, 3), _mod_part(mod_ref, 4))
    h_ref[...] = h.astype(BF16)
    logits = jnp.dot(h, wr_ref[...], preferred_element_type=F32, precision=HIGHEST) + br_ref[...]
    idx = lax.broadcasted_iota(jnp.int32, logits.shape, 1).astype(F32)
    m1 = jnp.max(logits, axis=1, keepdims=True)
    i1 = jnp.min(jnp.where(logits == m1, idx, float(N_EXPERTS)), axis=1, keepdims=True)
    sel1 = idx == i1
    rest = jnp.where(sel1, -jnp.inf, logits)
    m2 = jnp.max(rest, axis=1, keepdims=True)
    i2 = jnp.min(jnp.where(rest == m2, idx, float(N_EXPERTS)), axis=1, keepdims=True)
    sel2 = idx == i2
    e2 = jnp.exp(m2 - m1)
    den = 1.0 + e2
    gates_ref[...] = jnp.where(sel1, 1.0 / den, 0.0) + jnp.where(sel2, e2 / den, 0.0)


def _router(x, mod, tiles_per_seq, g, w_router, b_router, tm):
    rows, d = x.shape
    mr = mod.shape[1]
    row_spec = pl.BlockSpec((tm, d), lambda i: (i, 0))
    return pl.pallas_call(
        _router_kernel,
        grid=(rows // tm,),
        in_specs=[row_spec,
                  pl.BlockSpec((1, mr, 6 * d), lambda i: (i // tiles_per_seq, 0, 0)),
                  pl.BlockSpec((1, d), lambda i: (0, 0)),
                  pl.BlockSpec((d, N_EXPERTS), lambda i: (0, 0)),
                  pl.BlockSpec((1, N_EXPERTS), lambda i: (0, 0))],
        out_specs=[row_spec, pl.BlockSpec((tm, N_EXPERTS), lambda i: (i, 0))],
        out_shape=[jax.ShapeDtypeStruct((rows, d), BF16),
                   jax.ShapeDtypeStruct((rows, N_EXPERTS), F32)],
        compiler_params=_cparams("parallel"),
        name="moe_router",
    )(x, mod, g.reshape(1, d), w_router, b_router.reshape(1, N_EXPERTS))


def _moe_kernel(x_ref, mod_ref, h_ref, gates_ref, wg_ref, wu_ref, wd_ref, xo_ref, acc_ref):
    e = pl.program_id(1)
    f = pl.program_id(2)

    @pl.when(jnp.logical_and(e == 0, f == 0))
    def _():
        acc_ref[...] = jnp.zeros_like(acc_ref)

    h = h_ref[...]
    gate = jnp.dot(h, wg_ref[0], preferred_element_type=F32)
    up = jnp.dot(h, wu_ref[0], preferred_element_type=F32)
    a = (_silu(gate) * up).astype(BF16)
    y = jnp.dot(a, wd_ref[0], preferred_element_type=F32)
    gates = gates_ref[...]
    lane = lax.broadcasted_iota(jnp.int32, gates.shape, 1)
    w_e = jnp.sum(jnp.where(lane == e, gates, 0.0), axis=1, keepdims=True)
    acc_ref[...] += w_e * y

    @pl.when(jnp.logical_and(e == pl.num_programs(1) - 1, f == pl.num_programs(2) - 1))
    def _():
        xo_ref[...] = x_ref[...] + _mod_part(mod_ref, 5) * acc_ref[...]


def _moe_layer(x, mod, tiles_per_seq, g, w_router, b_router, w_gu, w_down, tm):
    rows, d = x.shape
    mr = mod.shape[1]
    d_ff = w_down.shape[1]
    fc = MOE_FF_CHUNK
    nf = d_ff // fc
    h, gates = _router(x, mod, tiles_per_seq, g, w_router, b_router, tm)
    row_spec = pl.BlockSpec((tm, d), lambda i, e, f: (i, 0))
    return pl.pallas_call(
        _moe_kernel,
        grid=(rows // tm, N_EXPERTS, nf),
        in_specs=[row_spec,
                  pl.BlockSpec((1, mr, 6 * d), lambda i, e, f: (i // tiles_per_seq, 0, 0)),
                  row_spec,
                  pl.BlockSpec((tm, N_EXPERTS), lambda i, e, f: (i, 0)),
                  pl.BlockSpec((1, d, fc), lambda i, e, f: (e, 0, f)),
                  pl.BlockSpec((1, d, fc), lambda i, e, f: (e, 0, nf + f)),
                  pl.BlockSpec((1, fc, d), lambda i, e, f: (e, f, 0))],
        out_specs=row_spec,
        out_shape=jax.ShapeDtypeStruct((rows, d), F32),
        scratch_shapes=[pltpu.VMEM((tm, d), F32)],
        compiler_params=_cparams("parallel", "arbitrary", "arbitrary"),
        name="moe_experts",
    )(x, mod, h, gates, w_gu, w_gu, w_down)


SEG_ALIGN = 16
SEG_SIZES = (512, 256, 128, 64, 32, 16)
SPARSE_TILE = 512
EXPERT_ROW_TILE = 256


def _route_kernel(x_ref, mod_ref, g_ref, wrt_ref, br_ref, h_ref, slots_ref, gw_ref, seg_ref, *, tm):
    x = x_ref[...]
    h = _norm_mod(x, g_ref[...], _mod_part(mod_ref, 3), _mod_part(mod_ref, 4))
    h_ref[...] = h.astype(BF16)
    logits = lax.dot_general(wrt_ref[...], h, (((1,), (1,)), ((), ())), preferred_element_type=F32,
                             precision=HIGHEST) + br_ref[...]
    eidx = lax.broadcasted_iota(jnp.int32, logits.shape, 0).astype(F32)
    m1 = jnp.max(logits, axis=0, keepdims=True)
    i1 = jnp.min(jnp.where(logits == m1, eidx, float(N_EXPERTS)), axis=0, keepdims=True)
    sel1 = eidx == i1
    rest = jnp.where(sel1, -jnp.inf, logits)
    m2 = jnp.max(rest, axis=0, keepdims=True)
    i2 = jnp.min(jnp.where(rest == m2, eidx, float(N_EXPERTS)), axis=0, keepdims=True)
    sel2 = eidx == i2
    e2 = jnp.exp(m2 - m1)
    den = 1.0 + e2
    gw_ref[0] = jnp.concatenate([1.0 / den, e2 / den], axis=0)
    mask = jnp.where(sel1 | sel2, 1.0, 0.0)
    before = (lax.broadcasted_iota(jnp.int32, (tm, tm), 0) < lax.broadcasted_iota(jnp.int32, (tm, tm), 1))
    rank = jnp.dot(mask.astype(BF16), jnp.where(before, 1.0, 0.0).astype(BF16), preferred_element_type=F32)
    count = jnp.sum(mask, axis=1, keepdims=True)
    seg = jnp.floor((count + (SEG_ALIGN - 1)) * (1.0 / SEG_ALIGN)) * SEG_ALIGN
    start = jnp.zeros_like(seg)
    for e in range(N_EXPERTS - 1):
        start = start + jnp.where(eidx[:, 0:1] > e, seg[e:e + 1, :], 0.0)
    slot = start + rank
    slot1 = jnp.sum(jnp.where(sel1, slot, 0.0), axis=0, keepdims=True)
    slot2 = jnp.sum(jnp.where(sel2, slot, 0.0), axis=0, keepdims=True)
    slots_ref[0] = jnp.concatenate([slot1, slot2], axis=0).astype(jnp.int32)
    seg_ref[0] = jnp.broadcast_to(seg, (N_EXPERTS, HEAD_DIM)).astype(jnp.int32)


def _route(x, mod, tiles_per_seq, g, w_router, b_router, tm):
    rows, d = x.shape
    mr = mod.shape[1]
    n_tiles = rows // tm
    row_spec = pl.BlockSpec((tm, d), lambda i: (i, 0))
    pair_spec = pl.BlockSpec((1, 2, tm), lambda i: (i, 0, 0))
    return pl.pallas_call(
        functools.partial(_route_kernel, tm=tm),
        grid=(n_tiles,),
        in_specs=[row_spec,
                  pl.BlockSpec((1, mr, 6 * d), lambda i: (i // tiles_per_seq, 0, 0)),
                  pl.BlockSpec((1, d), lambda i: (0, 0)),
                  pl.BlockSpec((N_EXPERTS, d), lambda i: (0, 0)),
                  pl.BlockSpec((N_EXPERTS, 1), lambda i: (0, 0))],
        out_specs=[row_spec, pair_spec, pair_spec,
                   pl.BlockSpec((1, N_EXPERTS, HEAD_DIM), lambda i: (i, 0, 0))],
        out_shape=[jax.ShapeDtypeStruct((rows, d), BF16),
                   jax.ShapeDtypeStruct((n_tiles, 2, tm), jnp.int32),
                   jax.ShapeDtypeStruct((n_tiles, 2, tm), F32),
                   jax.ShapeDtypeStruct((n_tiles, N_EXPERTS, HEAD_DIM), jnp.int32)],
        compiler_params=_cparams("parallel"),
        name="moe_route",
    )(x, mod, g.reshape(1, d), w_router.T, b_router.reshape(N_EXPERTS, 1))


def _segment_copies(tile, seg_ref, loc_ref, goff_ref, packed, sorted_hbm, sem, to_sorted):
    copies = []
    max_seg = (packed.shape[0] - N_EXPERTS * SEG_ALIGN) // 2
    for e in range(N_EXPERTS):
        seg = seg_ref[tile * N_EXPERTS + e]
        loc = loc_ref[tile * N_EXPERTS + e]
        goff = goff_ref[tile * N_EXPERTS + e]
        done = jnp.int32(0)
        for size in (s for s in SEG_SIZES if s <= max_seg):
            vm = packed.at[pl.ds(pl.multiple_of(loc + done, SEG_ALIGN), size), :]
            hb = sorted_hbm.at[pl.ds(pl.multiple_of(goff + done, SEG_ALIGN), size), :]
            src, dst = (vm, hb) if to_sorted else (hb, vm)
            copies.append(((seg & size) != 0, pltpu.make_async_copy(src, dst, sem)))
            done = done + (seg & size)
    return copies


def _start_all(copies):
    for cond, cp in copies:
        pl.when(cond)(cp.start)


def _wait_all(copies):
    for cond, cp in copies:
        pl.when(cond)(cp.wait)


def _dispatch_kernel(seg_ref, loc_ref, goff_ref, h_ref, slots_ref, sorted_in, sorted_out, stage_ref, sems,
                     *, n_slots, n_tiles):
    del sorted_in
    i = pl.program_id(0)
    par = i % 2

    def copies(tile, parity):
        return _segment_copies(tile, seg_ref, loc_ref, goff_ref, stage_ref.at[parity], sorted_out,
                               sems.at[parity], True)

    @pl.when(i >= 2)
    def _():
        _wait_all(copies(i - 2, par))

    tm = h_ref.shape[0]
    srow = lax.broadcasted_iota(jnp.int32, (n_slots, tm), 0)
    onehot = (srow == slots_ref[0, 0:1, :]) | (srow == slots_ref[0, 1:2, :])
    packed = jnp.dot(jnp.where(onehot, 1.0, 0.0).astype(BF16), h_ref[...], preferred_element_type=F32)
    stage_ref[par] = packed.astype(BF16)
    _start_all(copies(i, par))

    @pl.when(i == n_tiles - 1)
    def _():
        if n_tiles >= 2:
            _wait_all(copies(i - 1, 1 - par))
        _wait_all(copies(i, par))


def _dispatch(h, slots, seg, loc, goff, n_sorted_rows, tm):
    rows, d = h.shape
    n_tiles = rows // tm
    n_slots = 2 * tm + N_EXPERTS * SEG_ALIGN
    sorted_zero = jnp.zeros((n_sorted_rows, d), BF16)
    return pl.pallas_call(
        functools.partial(_dispatch_kernel, n_slots=n_slots, n_tiles=n_tiles),
        grid_spec=pltpu.PrefetchScalarGridSpec(
            num_scalar_prefetch=3,
            grid=(n_tiles,),
            in_specs=[pl.BlockSpec((tm, d), lambda i, *_: (i, 0)),
                      pl.BlockSpec((1, 2, tm), lambda i, *_: (i, 0, 0)),
                      pl.BlockSpec(memory_space=pl.ANY)],
            out_specs=pl.BlockSpec(memory_space=pl.ANY),
            scratch_shapes=[pltpu.VMEM((2, n_slots, d), BF16), pltpu.SemaphoreType.DMA((2,))]),
        out_shape=jax.ShapeDtypeStruct((n_sorted_rows, d), BF16),
        input_output_aliases={5: 0},
        compiler_params=_cparams("arbitrary"),
        name="moe_dispatch",
    )(seg, loc, goff, h, slots, sorted_zero)


def _experts_kernel(te_ref, jb_ref, nact_ref, h_ref, wgu_ref, wd_ref, y_ref, *, d_ff, fc):
    del te_ref, jb_ref

    @pl.when(pl.program_id(0) < nact_ref[0])
    def _():
        h = h_ref[...]
        acc = None
        for c in range(d_ff // fc):
            gate = jnp.dot(h, wgu_ref[0, :, c * fc:(c + 1) * fc], preferred_element_type=F32)
            up = jnp.dot(h, wgu_ref[0, :, d_ff + c * fc:d_ff + (c + 1) * fc], preferred_element_type=F32)
            a = (_silu(gate) * up).astype(BF16)
            part = jnp.dot(a, wd_ref[0, c * fc:(c + 1) * fc, :], preferred_element_type=F32)
            acc = part if acc is None else acc + part
        y_ref[...] = acc.astype(BF16)

    @pl.when(pl.program_id(0) >= nact_ref[0])
    def _():
        y_ref[...] = jnp.zeros_like(y_ref)


def _experts(h_sorted, tile_expert, tile_block, n_active, w_gu, w_down, tmf):
    n_rows, d = h_sorted.shape
    d_ff = w_down.shape[1]
    return pl.pallas_call(
        functools.partial(_experts_kernel, d_ff=d_ff, fc=MOE_FF_CHUNK),
        grid_spec=pltpu.PrefetchScalarGridSpec(
            num_scalar_prefetch=3,
            grid=(n_rows // tmf,),
            in_specs=[pl.BlockSpec((tmf, d), lambda j, te, jb, na: (jb[j], 0)),
                      pl.BlockSpec((1, d, 2 * d_ff), lambda j, te, jb, na: (te[j], 0, 0)),
                      pl.BlockSpec((1, d_ff, d), lambda j, te, jb, na: (te[j], 0, 0))],
            out_specs=pl.BlockSpec((tmf, d), lambda j, te, jb, na: (j, 0))),
        out_shape=jax.ShapeDtypeStruct((n_rows, d), BF16),
        compiler_params=pltpu.CompilerParams(dimension_semantics=("arbitrary",),
                                             vmem_limit_bytes=60 * 1024 * 1024),
        name="moe_experts_sorted",
    )(tile_expert, tile_block, n_active, h_sorted, w_gu, w_down)


def _combine_kernel(seg_ref, loc_ref, goff_ref, x_ref, mod_ref, slots_ref, gw_ref, sorted_hbm, xo_ref,
                    stage_ref, sems, *, n_slots, n_tiles):
    i = pl.program_id(0)
    par = i % 2

    def copies(tile, parity):
        return _segment_copies(tile, seg_ref, loc_ref, goff_ref, stage_ref.at[parity], sorted_hbm,
                               sems.at[parity], False)

    @pl.when(i == 0)
    def _():
        stage_ref[...] = jnp.zeros_like(stage_ref)
        _start_all(copies(i, par))

    @pl.when(i + 1 < n_tiles)
    def _():
        _start_all(copies(i + 1, 1 - par))

    _wait_all(copies(i, par))
    y = stage_ref[par]
    tm = x_ref.shape[0]
    scol = lax.broadcasted_iota(jnp.int32, (tm, n_slots), 1)
    moe = None
    for k in range(2):
        onehot = jnp.where(scol == slots_ref[:, k:k + 1], 1.0, 0.0).astype(BF16)
        part = gw_ref[:, k:k + 1] * jnp.dot(onehot, y, preferred_element_type=F32)
        moe = part if moe is None else moe + part
    xo_ref[...] = x_ref[...] + _mod_part(mod_ref, 5) * moe


def _combine(x, mod, tiles_per_seq, slots_col, gw_col, y_sorted, seg, loc, goff, tm):
    rows, d = x.shape
    mr = mod.shape[1]
    n_tiles = rows // tm
    n_slots = 2 * tm + N_EXPERTS * SEG_ALIGN
    row_spec = pl.BlockSpec((tm, d), lambda i, *_: (i, 0))
    pair_spec = pl.BlockSpec((tm, 2), lambda i, *_: (i, 0))
    return pl.pallas_call(
        functools.partial(_combine_kernel, n_slots=n_slots, n_tiles=n_tiles),
        grid_spec=pltpu.PrefetchScalarGridSpec(
            num_scalar_prefetch=3,
            grid=(n_tiles,),
            in_specs=[row_spec,
                      pl.BlockSpec((1, mr, 6 * d), lambda i, *_: (i // tiles_per_seq, 0, 0)),
                      pair_spec, pair_spec,
                      pl.BlockSpec(memory_space=pl.ANY)],
            out_specs=row_spec,
            scratch_shapes=[pltpu.VMEM((2, n_slots, d), BF16), pltpu.SemaphoreType.DMA((2,))]),
        out_shape=jax.ShapeDtypeStruct((rows, d), F32),
        compiler_params=_cparams("arbitrary"),
        name="moe_combine",
    )(seg, loc, goff, x, mod, slots_col, gw_col, y_sorted)


def _moe_sparse_layer(x, mod, tiles_per_seq, g, w_router, b_router, w_gu, w_down, tm):
    rows, d = x.shape
    n_tiles = rows // tm
    tmf = EXPERT_ROW_TILE
    h, slots, gw, seg3 = _route(x, mod, tiles_per_seq, g, w_router, b_router, tm)
    seg = seg3[:, :, 0]
    loc = jnp.cumsum(seg, axis=1) - seg
    region = ((jnp.sum(seg, axis=0) + tmf - 1) // tmf) * tmf
    region_end = jnp.cumsum(region)
    goff = (region_end - region)[None, :] + jnp.cumsum(seg, axis=0) - seg
    n_sorted = 2 * rows + n_tiles * N_EXPERTS * (SEG_ALIGN - 1) + N_EXPERTS * (tmf - SEG_ALIGN)
    n_sorted = ((n_sorted + tmf - 1) // tmf) * tmf
    n_row_tiles = n_sorted // tmf
    n_active = region_end[-1] // tmf
    tile_ids = jnp.arange(n_row_tiles, dtype=jnp.int32)
    tile_block = jnp.maximum(jnp.minimum(tile_ids, n_active - 1), 0)
    tile_expert = jnp.sum((region_end // tmf)[None, :] <= tile_block[:, None], axis=1)
    tile_expert = jnp.minimum(tile_expert, N_EXPERTS - 1).astype(jnp.int32)
    flat = lambda a: a.reshape(-1).astype(jnp.int32)
    seg_f, loc_f, goff_f = flat(seg), flat(loc), flat(goff)
    h_sorted = _dispatch(h, slots, seg_f, loc_f, goff_f, n_sorted, tm)
    y_sorted = _experts(h_sorted, tile_expert, tile_block.astype(jnp.int32),
                        n_active.reshape(1).astype(jnp.int32), w_gu, w_down, tmf)
    slots_col = jnp.transpose(slots, (0, 2, 1)).reshape(rows, 2)
    gw_col = jnp.transpose(gw, (0, 2, 1)).reshape(rows, 2)
    return _combine(x, mod, tiles_per_seq, slots_col, gw_col, y_sorted, seg_f, loc_f, goff_f, tm)


def _rope_tables(pos):
    half = ROT_DIM // 2
    inv = jnp.float32(ROPE_THETA) ** (-jnp.arange(half, dtype=jnp.float32) / half)
    ang = pos.astype(jnp.float32)[:, None] * inv[None, :]
    cos, sin = jnp.cos(ang), jnp.sin(ang)
    t = pos.shape[0]
    pad1 = jnp.ones((t, HEAD_DIM - ROT_DIM), F32)
    pad0 = jnp.zeros((t, HEAD_DIM - ROT_DIM), F32)
    return (jnp.concatenate([cos, cos, pad1], axis=1), jnp.concatenate([-sin, sin, pad0], axis=1))


def _swap_rot_halves(v):
    half = ROT_DIM // 2
    return jnp.concatenate([v[half:ROT_DIM], v[:half], v[ROT_DIM:]])


def _rope_gain_tables(cos_ref, sin_ref, g_ref, gs_ref, scale):
    return cos_ref[...] * (g_ref[...] * scale), sin_ref[...] * (gs_ref[...] * scale)


def _head_norm_rope(xh, a, b):
    half = ROT_DIM // 2
    rs = lax.rsqrt(jnp.mean(xh * xh, axis=-1, keepdims=True) + EPS)
    lane = lax.broadcasted_iota(jnp.int32, xh.shape, 1)
    swapped = jnp.where(lane < half, pltpu.roll(xh, HEAD_DIM - half, 1), pltpu.roll(xh, half, 1))
    return (xh * a + swapped * b) * rs


SPLIT = 4
SPLIT_TILE = SPLIT * Q_BLOCK
SPLIT_GROUPS = (1, 2)


def _kv_kernel(x_ref, mod_ref, g_ref, wkv_ref, gk_ref, cos_ref, sin_ref, *out_refs, split):
    x = x_ref[...]
    h = _norm_mod(x, g_ref[...], mod_ref[0, :, 0:D_MODEL], mod_ref[0, :, D_MODEL:2 * D_MODEL]).astype(BF16)
    rope_a, rope_b = _rope_gain_tables(cos_ref, sin_ref, gk_ref.at[0:1], gk_ref.at[1:2], 1.0)
    k_refs, v_refs = out_refs[:N_GROUPS], out_refs[N_GROUPS:2 * N_GROUPS]
    kv_all = jnp.dot(h, wkv_ref[...], preferred_element_type=F32)
    for hd in range(N_KV_HEADS):
        kh = kv_all[:, hd * HEAD_DIM:(hd + 1) * HEAD_DIM]
        vh = kv_all[:, KV_WIDTH + hd * HEAD_DIM:KV_WIDTH + (hd + 1) * HEAD_DIM]
        g_idx, k_idx = divmod(hd, HKV_G)
        sl = slice(k_idx * HEAD_DIM, (k_idx + 1) * HEAD_DIM)
        kh = _head_norm_rope(kh, rope_a, rope_b)
        k_refs[g_idx][:, sl] = kh
        v_refs[g_idx][:, sl] = vh
        if split and g_idx in SPLIT_GROUPS:
            n = SPLIT_GROUPS.index(g_idx)
            scr = out_refs[-1]
            for j, (val, dst) in enumerate(((kh, out_refs[2 * N_GROUPS + 2 * n]),
                                            (vh, out_refs[2 * N_GROUPS + 2 * n + 1]))):
                slot = (n * HKV_G + k_idx) * 2 + j
                scr[slot] = val
                for c in range(SPLIT):
                    dst[c * Q_BLOCK:(c + 1) * Q_BLOCK, sl] = (
                        scr[slot, pl.ds(c, Q_BLOCK, stride=SPLIT), :].astype(BF16))


def _kv_proj(x, mod_kv, tiles_per_seq, g, w_kv, g_k, cos, sin, tm, split):
    rows, d = x.shape
    mr = mod_kv.shape[1]
    assert not split or tm == SPLIT_TILE
    row_spec = pl.BlockSpec((tm, d), lambda i: (i, 0))
    tab_spec = pl.BlockSpec((tm, HEAD_DIM), lambda i: (i % tiles_per_seq, 0))
    out_spec = pl.BlockSpec((tm, GROUP_KV_WIDTH), lambda i: (i, 0))
    n_split = 2 * len(SPLIT_GROUPS) if split else 0
    return pl.pallas_call(
        functools.partial(_kv_kernel, split=split),
        grid=(rows // tm,),
        in_specs=[row_spec,
                  pl.BlockSpec((1, mr, 2 * d), lambda i: (i // tiles_per_seq, 0, 0)),
                  pl.BlockSpec((1, d), lambda i: (0, 0)),
                  _resident((d, 2 * KV_WIDTH), lambda i: (0, 0)),
                  pl.BlockSpec((2, HEAD_DIM), lambda i: (0, 0)),
                  tab_spec, tab_spec],
        out_specs=[out_spec] * (2 * N_GROUPS + n_split),
        out_shape=[jax.ShapeDtypeStruct((rows, GROUP_KV_WIDTH), F32)] * (2 * N_GROUPS)
                  + [jax.ShapeDtypeStruct((rows, GROUP_KV_WIDTH), BF16)] * n_split,
        scratch_shapes=[pltpu.VMEM((n_split * HKV_G, tm, HEAD_DIM), F32)] if split else [],
        compiler_params=_cparams("parallel"),
        name="kv_proj",
    )(x, mod_kv, g.reshape(1, d), w_kv, jnp.stack([g_k, _swap_rot_halves(g_k)]), cos, sin)


def _q_kernel(x_ref, mod_ref, g_ref, wq_ref, gq_ref, cos_ref, sin_ref, *refs, split):
    q_refs = refs[:N_GROUPS]
    x = x_ref[...]
    h = _norm_mod(x, g_ref[...], _mod_part(mod_ref, 0), _mod_part(mod_ref, 1)).astype(BF16)
    rope_a, rope_b = _rope_gain_tables(cos_ref, sin_ref, gq_ref.at[0:1], gq_ref.at[1:2], ATTN_SCALE)
    q_all = jnp.dot(h, wq_ref[...], preferred_element_type=F32)
    for hd in range(N_Q_HEADS):
        g_idx, h_idx = divmod(hd, HQ_G)
        sl = slice(h_idx * HEAD_DIM, (h_idx + 1) * HEAD_DIM)
        qh = _head_norm_rope(q_all[:, hd * HEAD_DIM:(hd + 1) * HEAD_DIM], rope_a, rope_b)
        if split and g_idx in SPLIT_GROUPS:
            scr = refs[N_GROUPS]
            n = SPLIT_GROUPS.index(g_idx) * HQ_G + h_idx
            scr[n] = qh
            for c in range(SPLIT):
                q_refs[g_idx][c * Q_BLOCK:(c + 1) * Q_BLOCK, sl] = (
                    scr[n, pl.ds(c, Q_BLOCK, stride=SPLIT), :].astype(BF16))
        else:
            q_refs[g_idx][:, sl] = qh.astype(BF16)


def _q_proj(x, mod, tiles_per_seq, g, w_q, g_q, cos, sin, tm, split):
    rows, d = x.shape
    mr = mod.shape[1]
    assert not split or tm == SPLIT_TILE
    row_spec = pl.BlockSpec((tm, d), lambda i: (i, 0))
    tab_spec = pl.BlockSpec((tm, HEAD_DIM), lambda i: (i % tiles_per_seq, 0))
    return pl.pallas_call(
        functools.partial(_q_kernel, split=split),
        grid=(rows // tm,),
        in_specs=[row_spec,
                  pl.BlockSpec((1, mr, 6 * d), lambda i: (i // tiles_per_seq, 0, 0)),
                  pl.BlockSpec((1, d), lambda i: (0, 0)),
                  _resident((d, Q_WIDTH), lambda i: (0, 0)),
                  pl.BlockSpec((2, HEAD_DIM), lambda i: (0, 0)),
                  tab_spec, tab_spec],
        out_specs=[pl.BlockSpec((tm, O_WIDTH), lambda i: (i, 0))] * N_GROUPS,
        out_shape=[jax.ShapeDtypeStruct((rows, O_WIDTH), BF16)] * N_GROUPS,
        scratch_shapes=[pltpu.VMEM((len(SPLIT_GROUPS) * HQ_G, tm, HEAD_DIM), F32)] if split else [],
        compiler_params=_cparams("parallel"),
        name="q_proj",
    )(x, mod, g.reshape(1, d), w_q, jnp.stack([g_q, _swap_rot_halves(g_q)]), cos, sin)


def _attn_prompt_kernel(*refs, step, has_prev, n_blk):
    if has_prev:
        q_ref, k_ref, v_ref, kp_ref, vp_ref, o_ref, st_ref = refs
        not_first = pl.program_id(1) > 0
    else:
        q_ref, k_ref, v_ref, o_ref, st_ref = refs
    nq = REP * Q_BLOCK
    lane = lax.broadcasted_iota(jnp.int32, (Q_BLOCK, HEAD_DIM), 1)
    for i in range(n_blk):
        first_kb = i - 1 if step == 1 else 0
        if first_kb < 0 and not has_prev:
            first_kb = 0
        kbs = list(range(first_kb, i + 1))
        nk = len(kbs) * Q_BLOCK
        qpos = i * Q_BLOCK + (lax.broadcasted_iota(jnp.int32, (nq, nk), 0) & (Q_BLOCK - 1))
        kpos = first_kb * Q_BLOCK + lax.broadcasted_iota(jnp.int32, (nq, nk), 1)
        dist = qpos - kpos
        valid = (dist >= 0) & (dist <= step * Q_BLOCK) & ((dist & (step - 1)) == 0)
        if first_kb < 0:
            valid = valid & (not_first | (kpos >= 0))
        for k in range(HKV_G):
            ksl = slice(k * HEAD_DIM, (k + 1) * HEAD_DIM)
            q = jnp.concatenate(
                [q_ref[i, :, (k * REP + r) * HEAD_DIM:(k * REP + r + 1) * HEAD_DIM] for r in range(REP)], axis=0)
            keys = jnp.concatenate([kp_ref[:, ksl] if j < 0 else k_ref[j, :, ksl] for j in kbs], axis=0)
            vals = jnp.concatenate([vp_ref[:, ksl] if j < 0 else v_ref[j, :, ksl] for j in kbs], axis=0)
            s = lax.dot_general(q, keys.astype(BF16), (((1,), (1,)), ((), ())), preferred_element_type=F32)
            s = jnp.where(valid, s, NEG)
            m = jnp.max(s, axis=1, keepdims=True)
            p = jnp.exp(s - m)
            den = jnp.sum(p, axis=1, keepdims=True)
            out = jnp.dot(p.astype(BF16), vals.astype(BF16), preferred_element_type=F32) / den
            lse = m + jnp.log(den)
            stats = jnp.zeros((Q_BLOCK, HEAD_DIM), F32)
            for r in range(REP):
                rs = slice(r * Q_BLOCK, (r + 1) * Q_BLOCK)
                o_ref[i, :, (k * REP + r) * HEAD_DIM:(k * REP + r + 1) * HEAD_DIM] = out[rs].astype(BF16)
                stats = jnp.where(lane == r, lse[rs], stats)
            st_ref[i, :, ksl] = stats


def _attn_prompt_group(q, k_g, v_g, group, batch, seq):
    _, dil = DIL_GROUPS[group]
    rows = batch * seq
    n_tiles = seq // SPLIT_TILE
    if group in SPLIT_GROUPS:
        step = dil // SPLIT
        lead = (batch, n_tiles, SPLIT, Q_BLOCK)
        view = lambda a: a.reshape(*lead, a.shape[-1])
        spec = lambda w: pl.BlockSpec((None, n_tiles, None, Q_BLOCK, w), lambda b, c: (b, 0, c, 0, 0))
        out_shape = [jax.ShapeDtypeStruct((*lead, O_WIDTH), BF16),
                     jax.ShapeDtypeStruct((*lead, GROUP_KV_WIDTH), F32)]
        out, stats = pl.pallas_call(
            functools.partial(_attn_prompt_kernel, step=step, has_prev=False, n_blk=n_tiles),
            grid=(batch, SPLIT),
            in_specs=[spec(O_WIDTH), spec(GROUP_KV_WIDTH), spec(GROUP_KV_WIDTH)],
            out_specs=[spec(O_WIDTH), spec(GROUP_KV_WIDTH)],
            out_shape=out_shape,
            compiler_params=_cparams("parallel", "parallel"),
            name=f"attn_prompt_g{group}",
        )(view(q), view(k_g), view(v_g))
    else:
        lead = (batch * n_tiles, SPLIT, Q_BLOCK)
        view = lambda a: a.reshape(*lead, a.shape[-1])
        spec = lambda w: pl.BlockSpec((None, SPLIT, Q_BLOCK, w), lambda b, i: (b * n_tiles + i, 0, 0, 0))
        prev = pl.BlockSpec((None, None, Q_BLOCK, GROUP_KV_WIDTH),
                            lambda b, i: (jnp.maximum(b * n_tiles + i - 1, 0), SPLIT - 1, 0, 0))
        out_shape = [jax.ShapeDtypeStruct((*lead, O_WIDTH), BF16),
                     jax.ShapeDtypeStruct((*lead, GROUP_KV_WIDTH), F32)]
        out, stats = pl.pallas_call(
            functools.partial(_attn_prompt_kernel, step=dil, has_prev=True, n_blk=SPLIT),
            grid=(batch, n_tiles),
            in_specs=[spec(O_WIDTH), spec(GROUP_KV_WIDTH), spec(GROUP_KV_WIDTH), prev, prev],
            out_specs=[spec(O_WIDTH), spec(GROUP_KV_WIDTH)],
            out_shape=out_shape,
            compiler_params=_cparams("parallel", "arbitrary"),
            name=f"attn_prompt_g{group}",
        )(view(q), view(k_g), view(v_g), view(k_g), view(v_g))
    return out.reshape(rows, O_WIDTH), stats.reshape(rows, GROUP_KV_WIDTH)


def _merge_o_kernel(x_ref, mod_ref, o0_ref, o1_ref, o2_ref, s0_ref, s1_ref, s2_ref, wo_ref, xo_ref,
                    on_ref, sn_ref):
    for n, (o_ref, s_ref) in enumerate(((o1_ref, s1_ref), (o2_ref, s2_ref))):
        for c in range(SPLIT):
            rows = slice(c * Q_BLOCK, (c + 1) * Q_BLOCK)
            dst = pl.ds(c, Q_BLOCK, stride=SPLIT)
            for hd in range(HQ_G):
                on_ref[n, hd, dst, :] = o_ref[rows, hd * HEAD_DIM:(hd + 1) * HEAD_DIM].astype(F32)
            for k in range(HKV_G):
                sn_ref[n, k, dst, :] = s_ref[rows, k * HEAD_DIM:(k + 1) * HEAD_DIM]
    heads = []
    for k in range(HKV_G):
        for r in range(REP):
            hd = k * REP + r
            lses = [s0_ref[:, k * HEAD_DIM + r:k * HEAD_DIM + r + 1],
                    sn_ref[0, k, :, r:r + 1], sn_ref[1, k, :, r:r + 1]]
            m = jnp.maximum(jnp.maximum(lses[0], lses[1]), lses[2])
            es = [jnp.exp(l - m) for l in lses]
            den = es[0] + es[1] + es[2]
            sl = slice(hd * HEAD_DIM, (hd + 1) * HEAD_DIM)
            outs = [o0_ref[:, sl].astype(F32), on_ref[0, hd], on_ref[1, hd]]
            o = sum((e / den) * o_g for e, o_g in zip(es, outs))
            heads.append(o.astype(BF16))
    o = jnp.concatenate(heads, axis=1)
    y = jnp.dot(o, wo_ref[...], preferred_element_type=F32)
    xo_ref[...] = x_ref[...] + _mod_part(mod_ref, 2) * y


def _merge_o(x, mod, tiles_per_seq, outs, stats, w_o, tm):
    rows, d = x.shape
    mr = mod.shape[1]
    assert tm == SPLIT_TILE
    row_spec = pl.BlockSpec((tm, d), lambda i: (i, 0))
    o_spec = pl.BlockSpec((tm, O_WIDTH), lambda i: (i, 0))
    st_spec = pl.BlockSpec((tm, GROUP_KV_WIDTH), lambda i: (i, 0))
    return pl.pallas_call(
        _merge_o_kernel,
        grid=(rows // tm,),
        in_specs=[row_spec,
                  pl.BlockSpec((1, mr, 6 * d), lambda i: (i // tiles_per_seq, 0, 0))]
                 + [o_spec] * 3 + [st_spec] * 3 + [_resident((O_WIDTH, d), lambda i: (0, 0))],
        out_specs=row_spec,
        out_shape=jax.ShapeDtypeStruct((rows, d), F32),
        scratch_shapes=[pltpu.VMEM((len(SPLIT_GROUPS), HQ_G, tm, HEAD_DIM), F32),
                        pltpu.VMEM((len(SPLIT_GROUPS), HKV_G, tm, HEAD_DIM), F32)],
        compiler_params=_cparams("parallel"),
        name="merge_o_proj",
    )(x, mod, *outs, *stats, w_o)


def _attn_sample_kernel(*refs, n_new):
    q_ref = refs[0]
    nk_refs = refs[1:1 + N_GROUPS]
    nv_refs = refs[1 + N_GROUPS:1 + 2 * N_GROUPS]
    ck_refs = refs[1 + 2 * N_GROUPS:1 + 3 * N_GROUPS]
    cv_refs = refs[1 + 3 * N_GROUPS:1 + 4 * N_GROUPS]
    o_ref = refs[1 + 4 * N_GROUPS]
    ok_refs = refs[2 + 4 * N_GROUPS:2 + 5 * N_GROUPS]
    ov_refs = refs[2 + 5 * N_GROUPS:2 + 6 * N_GROUPS]
    nq = REP * n_new
    qi = lax.broadcasted_iota(jnp.int32, (nq, 1), 0) & (n_new - 1)

    for g in range(N_GROUPS):
        for c_ref, n_ref, o_buf in ((ck_refs[g], nk_refs[g], ok_refs[g]), (cv_refs[g], nv_refs[g], ov_refs[g])):
            length = c_ref.shape[1]
            rolled = pltpu.roll(c_ref[0], length - n_new, 0)
            o_buf[0, 0:length - 8, :] = rolled[0:length - 8, :]
            row8 = lax.broadcasted_iota(jnp.int32, (8, 1), 0)
            o_buf[0, length - 8:length, :] = jnp.where(row8 >= 8 - n_new, n_ref[0], rolled[length - 8:length, :])

    for k in range(HKV_G):
        ksl = slice(k * HEAD_DIM, (k + 1) * HEAD_DIM)
        outs, lses = [], []
        for g, (window, dil) in enumerate(DIL_GROUPS):
            length = ck_refs[g].shape[1]
            q = q_ref[0, g, k]
            qf = q.astype(F32)
            keys = ck_refs[g][0, :, ksl].astype(BF16)
            vals = cv_refs[g][0, :, ksl].astype(BF16)
            s = lax.dot_general(q, keys, (((1,), (1,)), ((), ())), preferred_element_type=F32)
            idx = lax.broadcasted_iota(jnp.int32, (nq, length), 1)
            diff = length + qi - idx
            s = jnp.where(((diff & (dil - 1)) == 0) & (diff <= window), s, NEG)
            new_k = nk_refs[g][0, :, ksl]
            new_v = nv_refs[g][0, :, ksl]
            s_new = []
            for j in range(n_new):
                sj = jnp.sum(qf * new_k[8 - n_new + j:8 - n_new + j + 1, :], axis=1, keepdims=True)
                ok = (qi >= j) & (((qi - j) & (dil - 1)) == 0)
                s_new.append(jnp.where(ok, sj, NEG))
            m = jnp.max(s, axis=1, keepdims=True)
            for sj in s_new:
                m = jnp.maximum(m, sj)
            p = jnp.exp(s - m)
            den = jnp.sum(p, axis=1, keepdims=True)
            acc = jnp.dot(p.astype(BF16), vals, preferred_element_type=F32)
            for j, sj in enumerate(s_new):
                pj = jnp.exp(sj - m)
                den = den + pj
                acc = acc + pj * new_v[8 - n_new + j:8 - n_new + j + 1, :]
            outs.append(acc / den)
            lses.append(m + jnp.log(den))
        m = jnp.maximum(jnp.maximum(lses[0], lses[1]), lses[2])
        es = [jnp.exp(l - m) for l in lses]
        den = es[0] + es[1] + es[2]
        o_ref[0, k] = sum((e / den) * o for e, o in zip(es, outs))


def _attn_sample(q, new_k, new_v, cache_k, cache_v, n_seq, n_new):
    assert n_new <= 8
    q5 = q.reshape(n_seq, n_new, N_GROUPS, HKV_G, REP, HEAD_DIM)
    q5 = jnp.transpose(q5, (0, 2, 3, 4, 1, 5)).reshape(n_seq, N_GROUPS, HKV_G, REP * n_new, HEAD_DIM)
    pad = lambda a: jnp.pad(a.reshape(n_seq, n_new, GROUP_KV_WIDTH), ((0, 0), (8 - n_new, 0), (0, 0)))
    new_k8 = [pad(a) for a in new_k]
    new_v8 = [pad(a) for a in new_v]
    ck = [c.reshape(n_seq, c.shape[1], GROUP_KV_WIDTH) for c in cache_k]
    cv = [c.reshape(n_seq, c.shape[1], GROUP_KV_WIDTH) for c in cache_v]
    new_spec = pl.BlockSpec((1, 8, GROUP_KV_WIDTH), lambda b: (b, 0, 0))
    cache_specs = [pl.BlockSpec((1, c.shape[1], GROUP_KV_WIDTH), lambda b: (b, 0, 0)) for c in ck]
    cache_shapes = [jax.ShapeDtypeStruct(c.shape, F32) for c in ck]
    res = pl.pallas_call(
        functools.partial(_attn_sample_kernel, n_new=n_new),
        grid=(n_seq,),
        in_specs=[pl.BlockSpec((1, N_GROUPS, HKV_G, REP * n_new, HEAD_DIM), lambda b: (b, 0, 0, 0, 0))]
                 + [new_spec] * (2 * N_GROUPS) + cache_specs + cache_specs,
        out_specs=[pl.BlockSpec((1, HKV_G, REP * n_new, HEAD_DIM), lambda b: (b, 0, 0, 0))]
                  + cache_specs + cache_specs,
        out_shape=[jax.ShapeDtypeStruct((n_seq, HKV_G, REP * n_new, HEAD_DIM), F32)]
                  + cache_shapes + cache_shapes,
        compiler_params=_cparams("parallel"),
        name="attn_sample",
    )(q5, *new_k8, *new_v8, *ck, *cv)
    o = res[0].reshape(n_seq, HKV_G, REP, n_new, HEAD_DIM)
    o = jnp.transpose(o, (0, 3, 1, 2, 4)).reshape(n_seq * n_new, O_WIDTH)
    shape4 = lambda a: a.reshape(n_seq, a.shape[1], HKV_G, HEAD_DIM)
    new_ck = [shape4(a) for a in res[1:1 + N_GROUPS]]
    new_cv = [shape4(a) for a in res[1 + N_GROUPS:1 + 2 * N_GROUPS]]
    return o, new_ck, new_cv


def _o_proj_kernel(x_ref, mod_ref, o_ref, wo_ref, xo_ref):
    y = jnp.dot(o_ref[...].astype(BF16), wo_ref[...], preferred_element_type=F32)
    xo_ref[...] = x_ref[...] + _mod_part(mod_ref, 2) * y


def _o_proj(x, mod, o, w_o):
    rows, d = x.shape
    mr = mod.shape[1]
    full = pl.BlockSpec((rows, d), lambda i: (0, 0))
    return pl.pallas_call(
        _o_proj_kernel,
        grid=(1,),
        in_specs=[full, pl.BlockSpec((1, mr, 6 * d), lambda i: (0, 0, 0)),
                  pl.BlockSpec((rows, O_WIDTH), lambda i: (0, 0)),
                  pl.BlockSpec((O_WIDTH, d), lambda i: (0, 0))],
        out_specs=full,
        out_shape=jax.ShapeDtypeStruct((rows, d), F32),
        compiler_params=_cparams("arbitrary"),
        name="o_proj_sample",
    )(x, mod, o, w_o)


def _trunk(x, mods, mod_kv, pos, weights, tm, conv_prev, kv_prev):
    (g_norm_mix, g_norm_ffn, w_in, w_conv, w_out_conv, g_norm_kv, w_kv, g_k, w_q, g_q, w_o,
     w_gu_dense, w_down_dense, w_router, b_router, w_gu_moe, w_down_moe) = weights
    batch, seq, d = x.shape
    rows = batch * seq
    depth = g_norm_mix.shape[0]
    n_conv = w_in.shape[0]
    sample = kv_prev is not None
    tiles_per_seq = 1 if sample else seq // tm
    x = x.reshape(rows, d)
    cos, sin = _rope_tables(pos)
    if sample:
        cos, sin = jnp.tile(cos, (batch, 1)), jnp.tile(sin, (batch, 1))
    conv_state = []
    kv_state = None
    for layer in range(depth):
        mod = mods[layer]
        if layer == n_conv:
            kvs = _kv_proj(x, mod_kv, tiles_per_seq, g_norm_kv, w_kv, g_k, cos, sin, tm, not sample)
            k_new, v_new = kvs[:N_GROUPS], kvs[N_GROUPS:2 * N_GROUPS]
            k_att, v_att = list(k_new), list(v_new)
            for n, g_idx in enumerate(SPLIT_GROUPS if not sample else ()):
                k_att[g_idx], v_att[g_idx] = kvs[2 * N_GROUPS + 2 * n], kvs[2 * N_GROUPS + 2 * n + 1]
        if layer < n_conv:
            if sample:
                st = conv_prev[layer]
                zero = jnp.zeros((batch, seq - 2, d), F32)
                pa = jnp.concatenate([st[:, 1:2], jnp.zeros((batch, seq - 1, d), F32)], axis=1)
                pb = jnp.concatenate([st, zero], axis=1)
                prev = (seq, pa.reshape(rows, d), pb.reshape(rows, d))
            else:
                prev = None
            x, u_tail = _conv_layer(x, mod, tiles_per_seq, g_norm_mix[layer], w_in[layer], w_conv[layer],
                                    w_out_conv[layer], tm, prev)
            if sample:
                conv_state.append(u_tail.reshape(batch, seq, d)[:, seq - (CONV_W - 1):])
            else:
                conv_state.append(u_tail[:, 8 - (CONV_W - 1):])
        else:
            lb = layer - n_conv
            qs = _q_proj(x, mod, tiles_per_seq, g_norm_mix[layer], w_q[lb], g_q[lb], cos, sin, tm, not sample)
            if sample:
                o, new_ck, new_cv = _attn_sample(jnp.concatenate(qs, axis=1), k_new, v_new, kv_prev[0],
                                                 kv_prev[1], batch, seq)
                kv_state = (new_ck, new_cv)
                x = _o_proj(x, mod, o, w_o[lb])
            else:
                res = [_attn_prompt_group(qs[g], k_att[g], v_att[g], g, batch, seq) for g in range(N_GROUPS)]
                x = _merge_o(x, mod, tiles_per_seq, [r[0] for r in res], [r[1] for r in res], w_o[lb], tm)
        if layer % 2 == 0:
            x = _ffn_layer(x, mod, tiles_per_seq, g_norm_ffn[layer], w_gu_dense[layer // 2],
                           w_down_dense[layer // 2], tm)
        elif sample:
            x = _moe_layer(x, mod, 1, g_norm_ffn[layer], w_router[layer // 2], b_router[layer // 2],
                           w_gu_moe[layer // 2], w_down_moe[layer // 2], tm)
        else:
            tm_moe = min(SPARSE_TILE, seq)
            x = _moe_sparse_layer(x, mod, seq // tm_moe, g_norm_ffn[layer], w_router[layer // 2],
                                  b_router[layer // 2], w_gu_moe[layer // 2], w_down_moe[layer // 2], tm_moe)
    if not sample:
        shape4 = lambda a: a.reshape(batch, seq, HKV_G, HEAD_DIM)
        kv_state = ([shape4(k)[:, -min(w, seq):] for k, (w, _) in zip(k_new, DIL_GROUPS)],
                    [shape4(v)[:, -min(w, seq):] for v, (w, _) in zip(v_new, DIL_GROUPS)])
    return x.reshape(batch, seq, d), jnp.stack(conv_state, axis=0), kv_state


def kernel(x_prompt, x_sample, state_conv, cache_k_g0, cache_v_g0, cache_k_g1, cache_v_g1, cache_k_g2,
           cache_v_g2, c_prompt, c_sample, g_norm_mix, g_norm_ffn, w_ada, b_ada, w_in, w_conv, w_out_conv,
           g_norm_kv, w_ada_kv, b_ada_kv, w_kv, g_k, w_q, g_q, w_o, w_gu_dense, w_down_dense, w_router,
           b_router, w_gu_moe, w_down_moe):
    b_p, t_p, d = x_prompt.shape
    b_s, t_s, _ = x_sample.shape
    bf = lambda w: w.astype(BF16)
    weights = (g_norm_mix, g_norm_ffn, bf(w_in), w_conv, bf(w_out_conv), g_norm_kv, bf(w_kv), g_k, bf(w_q),
               g_q, bf(w_o), bf(w_gu_dense), bf(w_down_dense), w_router, b_router, bf(w_gu_moe),
               bf(w_down_moe))

    c_all = jnp.concatenate([c_prompt, c_sample], axis=0)
    mods_all = _ada(c_all, w_ada, b_ada)
    mod_kv_all = _ada(c_all, w_ada_kv[None], b_ada_kv[None])[0]
    depth = w_ada.shape[0]
    mods_p = [mods_all[l, :b_p, None, :] for l in range(depth)]
    mods_s = [jnp.repeat(mods_all[l, b_p:], t_s, axis=0)[None] for l in range(depth)]
    mod_kv_p = mod_kv_all[:b_p, None, :]
    mod_kv_s = jnp.repeat(mod_kv_all[b_p:], t_s, axis=0)[None]

    pos_p = jnp.arange(t_p, dtype=jnp.int32)
    y_prompt, conv_p, (kp, vp) = _trunk(x_prompt, mods_p, mod_kv_p, pos_p, weights,
                                        min(PROMPT_TILE, t_p), None, None)

    pos_s = PAST_LEN + jnp.arange(t_s, dtype=jnp.int32)
    kv_prev = ([cache_k_g0, cache_k_g1, cache_k_g2], [cache_v_g0, cache_v_g1, cache_v_g2])
    y_sample, conv_s, (ksn, vsn) = _trunk(x_sample, mods_s, mod_kv_s, pos_s, weights, b_s * t_s,
                                          state_conv, kv_prev)
    return (y_prompt, y_sample, conv_p, conv_s,
            kp[0], vp[0], kp[1], vp[1], kp[2], vp[2],
            ksn[0], vsn[0], ksn[1], vsn[1], ksn[2], vsn[2])
```

```python
import functools

import jax
import jax.numpy as jnp
from jax import lax
from jax.experimental import pallas as pl
from jax.experimental.pallas import tpu as pltpu

D_MODEL = 1024
PAST_LEN = 8192
CONV_W = 3
HEAD_DIM = 128
ROT_DIM = HEAD_DIM // 4
ROPE_THETA = 500000.0
DIL_GROUPS = ((128, 1), (512, 4), (2048, 16))
N_GROUPS = len(DIL_GROUPS)
HQ_G = 8
HKV_G = 2
REP = HQ_G // HKV_G
N_Q_HEADS = N_GROUPS * HQ_G
N_KV_HEADS = N_GROUPS * HKV_G
Q_WIDTH = N_Q_HEADS * HEAD_DIM
KV_WIDTH = N_KV_HEADS * HEAD_DIM
O_WIDTH = HQ_G * HEAD_DIM
GROUP_KV_WIDTH = HKV_G * HEAD_DIM
N_EXPERTS = 8
EPS = 1e-6
ATTN_SCALE = HEAD_DIM ** -0.5
NEG = -1e30
Q_BLOCK = 128

F32 = jnp.float32
BF16 = jnp.bfloat16
HIGHEST = lax.Precision.HIGHEST

V7X_VMEM_LIMIT_BYTES = 56 * 1024 * 1024
PROMPT_TILE = 512
MOE_FF_CHUNK = 512


def _cparams(*sem):
    return pltpu.CompilerParams(dimension_semantics=sem, vmem_limit_bytes=V7X_VMEM_LIMIT_BYTES)


def _resident(shape, index_map):
    return pl.BlockSpec(shape, index_map, pipeline_mode=pl.Buffered(1))


def _stacked(w, index):
    shape = w.shape[1:]
    return pl.BlockSpec((None,) + shape, lambda *_: (index,) + (0,) * len(shape), pipeline_mode=pl.Buffered(1))


def _silu(x):
    return x * jax.nn.sigmoid(x)


def _norm_mod(x, g, shift, scale):
    y = x * lax.rsqrt(jnp.mean(x * x, axis=-1, keepdims=True) + EPS) * g
    return y * (1.0 + scale) + shift


def _mod_part(mod_ref, k):
    return mod_ref[0, :, k * D_MODEL:(k + 1) * D_MODEL]


def _ada_kernel(c_ref, w_ref, b_ref, o_ref):
    c = c_ref[...]
    o_ref[0] = jnp.dot(_silu(c), w_ref[0], preferred_element_type=F32, precision=HIGHEST) + b_ref[0]


def _ada(c, w, b):
    n_layers, d, n = w.shape
    m = c.shape[0]
    tn = 1024
    return pl.pallas_call(
        _ada_kernel,
        grid=(n_layers, n // tn),
        in_specs=[pl.BlockSpec((m, d), lambda l, j: (0, 0)),
                  pl.BlockSpec((1, d, tn), lambda l, j: (l, 0, j)),
                  pl.BlockSpec((1, 1, tn), lambda l, j: (l, 0, j))],
        out_specs=pl.BlockSpec((1, m, tn), lambda l, j: (l, 0, j)),
        out_shape=jax.ShapeDtypeStruct((n_layers, m, n), F32),
        compiler_params=_cparams("parallel", "parallel"),
        name="ada",
    )(c, w, b.reshape(n_layers, 1, n))


def _conv_kernel(*refs, tm, tiles_per_seq, seq_rows):
    if seq_rows is None:
        x_ref, mod_ref, g_ref, win_ref, wc_ref, wout_ref, xo_ref, ust_ref, carry_ref = refs
    else:
        x_ref, mod_ref, g_ref, win_ref, wc_ref, wout_ref, pa_ref, pb_ref, xo_ref, ust_ref = refs
    x = x_ref[...]
    h = _norm_mod(x, g_ref[...], _mod_part(mod_ref, 0), _mod_part(mod_ref, 1)).astype(BF16)
    b_gate = jnp.dot(h, win_ref[:, 0:D_MODEL], preferred_element_type=F32)
    c_gate = jnp.dot(h, win_ref[:, D_MODEL:2 * D_MODEL], preferred_element_type=F32)
    v = jnp.dot(h, win_ref[:, 2 * D_MODEL:3 * D_MODEL], preferred_element_type=F32)
    u = c_gate * v
    row = lax.broadcasted_iota(jnp.int32, (tm, 1), 0)
    r1 = pltpu.roll(u, 1, 0)
    r2 = pltpu.roll(u, 2, 0)
    if seq_rows is None:
        @pl.when(pl.program_id(0) % tiles_per_seq == 0)
        def _():
            carry_ref[...] = jnp.zeros_like(carry_ref)
        um1 = jnp.where(row == 0, carry_ref[7:8, :], r1)
        um2 = jnp.where(row == 0, carry_ref[6:7, :], jnp.where(row == 1, carry_ref[7:8, :], r2))
        carry_ref[...] = u[tm - 8:tm, :]
        ust_ref[0] = u[tm - 8:tm, :]
    else:
        rowm = row & (seq_rows - 1)
        um1 = jnp.where(rowm == 0, pa_ref[...], r1)
        um2 = jnp.where(rowm < 2, pb_ref[...], r2)
        ust_ref[...] = u
    y = um2 * wc_ref[0:1, :] + um1 * wc_ref[1:2, :] + u * wc_ref[2:3, :]
    z = (b_gate * y).astype(BF16)
    out = jnp.dot(z, wout_ref[...], preferred_element_type=F32)
    xo_ref[...] = x + _mod_part(mod_ref, 2) * out


def _conv_layer(x, mod, tiles_per_seq, g, w_in, w_conv, w_out, layer, tm, prev=None):
    rows, d = x.shape
    n_tiles = rows // tm
    mr = mod.shape[1]
    row_spec = pl.BlockSpec((tm, d), lambda i: (i, 0))
    in_specs = [row_spec,
                pl.BlockSpec((1, mr, 6 * d), lambda i: (i // tiles_per_seq, 0, 0)),
                pl.BlockSpec((1, d), lambda i: (0, 0)),
                _stacked(w_in, layer), _stacked(w_conv, layer), _stacked(w_out, layer)]
    args = [x, mod, g.reshape(1, d), w_in, w_conv, w_out]
    if prev is None:
        seq_rows = None
        n_seq = n_tiles // tiles_per_seq
        ust_shape = jax.ShapeDtypeStruct((n_seq, 8, d), F32)
        ust_spec = pl.BlockSpec((1, 8, d), lambda i: (i // tiles_per_seq, 0, 0))
        scratch = [pltpu.VMEM((8, d), F32)]
    else:
        seq_rows, pa, pb = prev
        in_specs += [row_spec, row_spec]
        args += [pa, pb]
        ust_shape = jax.ShapeDtypeStruct((rows, d), F32)
        ust_spec = row_spec
        scratch = []
    return pl.pallas_call(
        functools.partial(_conv_kernel, tm=tm, tiles_per_seq=tiles_per_seq, seq_rows=seq_rows),
        grid=(n_tiles,),
        in_specs=in_specs,
        out_specs=[row_spec, ust_spec],
        out_shape=[jax.ShapeDtypeStruct((rows, d), F32), ust_shape],
        scratch_shapes=scratch,
        compiler_params=_cparams("arbitrary"),
        name="conv_layer",
    )(*args)


def _ffn_kernel(x_ref, mod_ref, g_ref, wgu_ref, wd_ref, xo_ref, *, d_ff, n_chunks):
    x = x_ref[...]
    h = _norm_mod(x, g_ref[...], _mod_part(mod_ref, 3), _mod_part(mod_ref, 4)).astype(BF16)
    fc = d_ff // n_chunks
    acc = None
    for c in range(n_chunks):
        gate = jnp.dot(h, wgu_ref[:, c * fc:(c + 1) * fc], preferred_element_type=F32)
        up = jnp.dot(h, wgu_ref[:, d_ff + c * fc:d_ff + (c + 1) * fc], preferred_element_type=F32)
        a = (_silu(gate) * up).astype(BF16)
        part = jnp.dot(a, wd_ref[c * fc:(c + 1) * fc, :], preferred_element_type=F32)
        acc = part if acc is None else acc + part
    xo_ref[...] = x + _mod_part(mod_ref, 5) * acc


def _ffn_layer(x, mod, tiles_per_seq, g, w_gu, w_down, index, tm):
    rows, d = x.shape
    d_ff = w_down.shape[1]
    mr = mod.shape[1]
    row_spec = pl.BlockSpec((tm, d), lambda i: (i, 0))
    return pl.pallas_call(
        functools.partial(_ffn_kernel, d_ff=d_ff, n_chunks=2),
        grid=(rows // tm,),
        in_specs=[row_spec,
                  pl.BlockSpec((1, mr, 6 * d), lambda i: (i // tiles_per_seq, 0, 0)),
                  pl.BlockSpec((1, d), lambda i: (0, 0)),
                  _stacked(w_gu, index), _stacked(w_down, index)],
        out_specs=row_spec,
        out_shape=jax.ShapeDtypeStruct((rows, d), F32),
        compiler_params=_cparams("parallel"),
        name="ffn_dense",
    )(x, mod, g.reshape(1, d), w_gu, w_down)


SEG_ALIGN = 16
SEG_SIZES = (512, 256, 128, 64, 32, 16)
SPARSE_TILE = 512
EXPERT_ROW_TILE = 256


def _route_kernel(x_ref, mod_ref, g_ref, wrt_ref, br_ref, h_ref, slots_ref, gw_ref, seg_ref, *, tm):
    x = x_ref[...]
    h = _norm_mod(x, g_ref[...], _mod_part(mod_ref, 3), _mod_part(mod_ref, 4))
    h_ref[...] = h.astype(BF16)
    logits = lax.dot_general(wrt_ref[...], h, (((1,), (1,)), ((), ())), preferred_element_type=F32,
                             precision=HIGHEST) + br_ref[...]
    eidx = lax.broadcasted_iota(jnp.int32, logits.shape, 0).astype(F32)
    m1 = jnp.max(logits, axis=0, keepdims=True)
    i1 = jnp.min(jnp.where(logits == m1, eidx, float(N_EXPERTS)), axis=0, keepdims=True)
    sel1 = eidx == i1
    rest = jnp.where(sel1, -jnp.inf, logits)
    m2 = jnp.max(rest, axis=0, keepdims=True)
    i2 = jnp.min(jnp.where(rest == m2, eidx, float(N_EXPERTS)), axis=0, keepdims=True)
    sel2 = eidx == i2
    e2 = jnp.exp(m2 - m1)
    den = 1.0 + e2
    gw_ref[0] = jnp.concatenate([1.0 / den, e2 / den], axis=0)
    mask = jnp.where(sel1 | sel2, 1.0, 0.0)
    before = (lax.broadcasted_iota(jnp.int32, (tm, tm), 0) < lax.broadcasted_iota(jnp.int32, (tm, tm), 1))
    rank = jnp.dot(mask.astype(BF16), jnp.where(before, 1.0, 0.0).astype(BF16), preferred_element_type=F32)
    count = jnp.sum(mask, axis=1, keepdims=True)
    seg = jnp.floor((count + (SEG_ALIGN - 1)) * (1.0 / SEG_ALIGN)) * SEG_ALIGN
    start = jnp.zeros_like(seg)
    for e in range(N_EXPERTS - 1):
        start = start + jnp.where(eidx[:, 0:1] > e, seg[e:e + 1, :], 0.0)
    slot = start + rank
    slot1 = jnp.sum(jnp.where(sel1, slot, 0.0), axis=0, keepdims=True)
    slot2 = jnp.sum(jnp.where(sel2, slot, 0.0), axis=0, keepdims=True)
    slots_ref[0] = jnp.concatenate([slot1, slot2], axis=0).astype(jnp.int32)
    seg_ref[0] = jnp.broadcast_to(seg, (N_EXPERTS, HEAD_DIM)).astype(jnp.int32)


def _route(x, mod, tiles_per_seq, g, w_router, b_router, tm):
    rows, d = x.shape
    mr = mod.shape[1]
    n_tiles = rows // tm
    row_spec = pl.BlockSpec((tm, d), lambda i: (i, 0))
    pair_spec = pl.BlockSpec((1, 2, tm), lambda i: (i, 0, 0))
    return pl.pallas_call(
        functools.partial(_route_kernel, tm=tm),
        grid=(n_tiles,),
        in_specs=[row_spec,
                  pl.BlockSpec((1, mr, 6 * d), lambda i: (i // tiles_per_seq, 0, 0)),
                  pl.BlockSpec((1, d), lambda i: (0, 0)),
                  pl.BlockSpec((N_EXPERTS, d), lambda i: (0, 0)),
                  pl.BlockSpec((N_EXPERTS, 1), lambda i: (0, 0))],
        out_specs=[row_spec, pair_spec, pair_spec,
                   pl.BlockSpec((1, N_EXPERTS, HEAD_DIM), lambda i: (i, 0, 0))],
        out_shape=[jax.ShapeDtypeStruct((rows, d), BF16),
                   jax.ShapeDtypeStruct((n_tiles, 2, tm), jnp.int32),
                   jax.ShapeDtypeStruct((n_tiles, 2, tm), F32),
                   jax.ShapeDtypeStruct((n_tiles, N_EXPERTS, HEAD_DIM), jnp.int32)],
        compiler_params=_cparams("parallel"),
        name="moe_route",
    )(x, mod, g.reshape(1, d), w_router.T, b_router.reshape(N_EXPERTS, 1))


def _segment_copies(tile, seg_ref, loc_ref, goff_ref, packed, sorted_hbm, sem, to_sorted):
    copies = []
    tokens = (packed.shape[0] - N_EXPERTS * SEG_ALIGN) // 2
    max_seg = -(-tokens // SEG_ALIGN) * SEG_ALIGN
    for e in range(N_EXPERTS):
        seg = seg_ref[tile * N_EXPERTS + e]
        loc = loc_ref[tile * N_EXPERTS + e]
        goff = goff_ref[tile * N_EXPERTS + e]
        done = jnp.int32(0)
        for size in (s for s in SEG_SIZES if s <= max_seg):
            vm = packed.at[pl.ds(pl.multiple_of(loc + done, SEG_ALIGN), size), :]
            hb = sorted_hbm.at[pl.ds(pl.multiple_of(goff + done, SEG_ALIGN), size), :]
            src, dst = (vm, hb) if to_sorted else (hb, vm)
            copies.append(((seg & size) != 0, pltpu.make_async_copy(src, dst, sem)))
            done = done + (seg & size)
    return copies


def _start_all(copies):
    for cond, cp in copies:
        pl.when(cond)(cp.start)


def _wait_all(copies):
    for cond, cp in copies:
        pl.when(cond)(cp.wait)


def _dispatch_kernel(seg_ref, loc_ref, goff_ref, h_ref, slots_ref, sorted_in, sorted_out, stage_ref, sems,
                     *, n_slots, n_tiles):
    del sorted_in
    i = pl.program_id(0)
    par = i % 2

    def copies(tile, parity):
        return _segment_copies(tile, seg_ref, loc_ref, goff_ref, stage_ref.at[parity], sorted_out,
                               sems.at[parity], True)

    @pl.when(i >= 2)
    def _():
        _wait_all(copies(i - 2, par))

    tm = h_ref.shape[0]
    srow = lax.broadcasted_iota(jnp.int32, (n_slots, tm), 0)
    onehot = (srow == slots_ref[0, 0:1, :]) | (srow == slots_ref[0, 1:2, :])
    packed = jnp.dot(jnp.where(onehot, 1.0, 0.0).astype(BF16), h_ref[...], preferred_element_type=F32)
    stage_ref[par] = packed.astype(BF16)
    _start_all(copies(i, par))

    @pl.when(i == n_tiles - 1)
    def _():
        if n_tiles >= 2:
            _wait_all(copies(i - 1, 1 - par))
        _wait_all(copies(i, par))


def _dispatch(h, slots, seg, loc, goff, h_sorted, tm):
    rows, d = h.shape
    n_tiles = rows // tm
    n_slots = 2 * tm + N_EXPERTS * SEG_ALIGN
    return pl.pallas_call(
        functools.partial(_dispatch_kernel, n_slots=n_slots, n_tiles=n_tiles),
        grid_spec=pltpu.PrefetchScalarGridSpec(
            num_scalar_prefetch=3,
            grid=(n_tiles,),
            in_specs=[pl.BlockSpec((tm, d), lambda i, *_: (i, 0)),
                      pl.BlockSpec((1, 2, tm), lambda i, *_: (i, 0, 0)),
                      pl.BlockSpec(memory_space=pl.ANY)],
            out_specs=pl.BlockSpec(memory_space=pl.ANY),
            scratch_shapes=[pltpu.VMEM((2, n_slots, d), BF16), pltpu.SemaphoreType.DMA((2,))]),
        out_shape=jax.ShapeDtypeStruct(h_sorted.shape, BF16),
        input_output_aliases={5: 0},
        compiler_params=_cparams("arbitrary"),
        name="moe_dispatch",
    )(seg, loc, goff, h, slots, h_sorted)


def _experts_kernel(te_ref, jb_ref, nact_ref, h_ref, wgu_ref, wd_ref, y_ref, *, d_ff, fc):
    del te_ref, jb_ref

    @pl.when(pl.program_id(0) < nact_ref[0])
    def _():
        h = h_ref[...]
        acc = None
        for c in range(d_ff // fc):
            gate = jnp.dot(h, wgu_ref[0, :, c * fc:(c + 1) * fc], preferred_element_type=F32)
            up = jnp.dot(h, wgu_ref[0, :, d_ff + c * fc:d_ff + (c + 1) * fc], preferred_element_type=F32)
            a = (_silu(gate) * up).astype(BF16)
            part = jnp.dot(a, wd_ref[0, c * fc:(c + 1) * fc, :], preferred_element_type=F32)
            acc = part if acc is None else acc + part
        y_ref[...] = acc.astype(BF16)

    @pl.when(pl.program_id(0) >= nact_ref[0])
    def _():
        y_ref[...] = jnp.zeros_like(y_ref)


def _experts(h_sorted, tile_expert, tile_block, n_active, w_gu, w_down, index, tmf):
    n_rows, d = h_sorted.shape
    d_ff = w_down.shape[2]
    first = index * N_EXPERTS
    w_gu = w_gu.reshape(-1, d, 2 * d_ff)
    w_down = w_down.reshape(-1, d_ff, d)
    return pl.pallas_call(
        functools.partial(_experts_kernel, d_ff=d_ff, fc=MOE_FF_CHUNK),
        grid_spec=pltpu.PrefetchScalarGridSpec(
            num_scalar_prefetch=3,
            grid=(n_rows // tmf,),
            in_specs=[pl.BlockSpec((tmf, d), lambda j, te, jb, na: (jb[j], 0)),
                      pl.BlockSpec((1, d, 2 * d_ff), lambda j, te, jb, na: (first + te[j], 0, 0)),
                      pl.BlockSpec((1, d_ff, d), lambda j, te, jb, na: (first + te[j], 0, 0))],
            out_specs=pl.BlockSpec((tmf, d), lambda j, te, jb, na: (j, 0))),
        out_shape=jax.ShapeDtypeStruct((n_rows, d), BF16),
        compiler_params=pltpu.CompilerParams(dimension_semantics=("arbitrary",),
                                             vmem_limit_bytes=60 * 1024 * 1024),
        name="moe_experts_sorted",
    )(tile_expert, tile_block, n_active, h_sorted, w_gu, w_down)


def _combine_kernel(seg_ref, loc_ref, goff_ref, x_ref, mod_ref, slots_ref, gw_ref, sorted_hbm, xo_ref,
                    stage_ref, sems, *, n_slots, n_tiles):
    i = pl.program_id(0)
    par = i % 2

    def copies(tile, parity):
        return _segment_copies(tile, seg_ref, loc_ref, goff_ref, stage_ref.at[parity], sorted_hbm,
                               sems.at[parity], False)

    @pl.when(i == 0)
    def _():
        stage_ref[...] = jnp.zeros_like(stage_ref)
        _start_all(copies(i, par))

    @pl.when(i + 1 < n_tiles)
    def _():
        _start_all(copies(i + 1, 1 - par))

    _wait_all(copies(i, par))
    y = stage_ref[par]
    tm = x_ref.shape[0]
    scol = lax.broadcasted_iota(jnp.int32, (tm, n_slots), 1)
    moe = None
    for k in range(2):
        onehot = jnp.where(scol == slots_ref[:, k:k + 1], 1.0, 0.0).astype(BF16)
        part = gw_ref[:, k:k + 1] * jnp.dot(onehot, y, preferred_element_type=F32)
        moe = part if moe is None else moe + part
    xo_ref[...] = x_ref[...] + _mod_part(mod_ref, 5) * moe


def _combine(x, mod, tiles_per_seq, slots_col, gw_col, y_sorted, seg, loc, goff, tm):
    rows, d = x.shape
    mr = mod.shape[1]
    n_tiles = rows // tm
    n_slots = 2 * tm + N_EXPERTS * SEG_ALIGN
    row_spec = pl.BlockSpec((tm, d), lambda i, *_: (i, 0))
    pair_spec = pl.BlockSpec((tm, 2), lambda i, *_: (i, 0))
    return pl.pallas_call(
        functools.partial(_combine_kernel, n_slots=n_slots, n_tiles=n_tiles),
        grid_spec=pltpu.PrefetchScalarGridSpec(
            num_scalar_prefetch=3,
            grid=(n_tiles,),
            in_specs=[row_spec,
                      pl.BlockSpec((1, mr, 6 * d), lambda i, *_: (i // tiles_per_seq, 0, 0)),
                      pair_spec, pair_spec,
                      pl.BlockSpec(memory_space=pl.ANY)],
            out_specs=row_spec,
            scratch_shapes=[pltpu.VMEM((2, n_slots, d), BF16), pltpu.SemaphoreType.DMA((2,))]),
        out_shape=jax.ShapeDtypeStruct((rows, d), F32),
        compiler_params=_cparams("arbitrary"),
        name="moe_combine",
    )(seg, loc, goff, x, mod, slots_col, gw_col, y_sorted)


def _moe_sparse_layer(parts, g, w_router, b_router, w_gu, w_down, index):
    tmf = EXPERT_ROW_TILE
    d = parts[0][0].shape[1]
    routed = [_route(x, mod, tps, g, w_router, b_router, tm) for x, mod, tps, tm in parts]
    tiles = [x.shape[0] // tm for x, _, _, tm in parts]
    seg = jnp.concatenate([r[3][:, :, 0] for r in routed], axis=0)
    loc = jnp.cumsum(seg, axis=1) - seg
    region = ((jnp.sum(seg, axis=0) + tmf - 1) // tmf) * tmf
    region_end = jnp.cumsum(region)
    goff = (region_end - region)[None, :] + jnp.cumsum(seg, axis=0) - seg
    n_sorted = sum(2 * x.shape[0] for x, _, _, _ in parts) + sum(tiles) * N_EXPERTS * (SEG_ALIGN - 1)
    n_sorted += N_EXPERTS * (tmf - SEG_ALIGN)
    n_sorted = ((n_sorted + tmf - 1) // tmf) * tmf
    n_row_tiles = n_sorted // tmf
    n_active = region_end[-1] // tmf
    tile_ids = jnp.arange(n_row_tiles, dtype=jnp.int32)
    tile_block = jnp.maximum(jnp.minimum(tile_ids, n_active - 1), 0)
    tile_expert = jnp.sum((region_end // tmf)[None, :] <= tile_block[:, None], axis=1)
    tile_expert = jnp.minimum(tile_expert, N_EXPERTS - 1).astype(jnp.int32)
    tables, first = [], 0
    for n in tiles:
        cut = lambda a: a[first:first + n].reshape(-1).astype(jnp.int32)
        tables.append((cut(seg), cut(loc), cut(goff)))
        first += n
    h_sorted = jnp.zeros((n_sorted, d), BF16)
    for (_, _, _, tm), (h, slots, _, _), tab in zip(parts, routed, tables):
        h_sorted = _dispatch(h, slots, *tab, h_sorted, tm)
    y_sorted = _experts(h_sorted, tile_expert, tile_block.astype(jnp.int32),
                        n_active.reshape(1).astype(jnp.int32), w_gu, w_down, index, tmf)
    outs = []
    for (x, mod, tps, tm), (_, slots, gw, _), tab in zip(parts, routed, tables):
        rows = x.shape[0]
        slots_col = jnp.transpose(slots, (0, 2, 1)).reshape(rows, 2)
        gw_col = jnp.transpose(gw, (0, 2, 1)).reshape(rows, 2)
        outs.append(_combine(x, mod, tps, slots_col, gw_col, y_sorted, *tab, tm))
    return outs


def _rope_tables(pos):
    half = ROT_DIM // 2
    inv = jnp.float32(ROPE_THETA) ** (-jnp.arange(half, dtype=jnp.float32) / half)
    ang = pos.astype(jnp.float32)[:, None] * inv[None, :]
    cos, sin = jnp.cos(ang), jnp.sin(ang)
    t = pos.shape[0]
    pad1 = jnp.ones((t, HEAD_DIM - ROT_DIM), F32)
    pad0 = jnp.zeros((t, HEAD_DIM - ROT_DIM), F32)
    return (jnp.concatenate([cos, cos, pad1], axis=1), jnp.concatenate([-sin, sin, pad0], axis=1))


def _swap_rot_halves(v):
    half = ROT_DIM // 2
    return jnp.concatenate([v[half:ROT_DIM], v[:half], v[ROT_DIM:]])


def _rope_gain_tables(cos_ref, sin_ref, g_ref, gs_ref, scale):
    return cos_ref[...] * (g_ref[...] * scale), sin_ref[...] * (gs_ref[...] * scale)


def _head_norm_rope(xh, a, b):
    half = ROT_DIM // 2
    rs = lax.rsqrt(jnp.mean(xh * xh, axis=-1, keepdims=True) + EPS)
    lane = lax.broadcasted_iota(jnp.int32, xh.shape, 1)
    swapped = jnp.where(lane < half, pltpu.roll(xh, HEAD_DIM - half, 1), pltpu.roll(xh, half, 1))
    return (xh * a + swapped * b) * rs


SPLIT = 4
SPLIT_TILE = SPLIT * Q_BLOCK
SPLIT_GROUPS = (1, 2)


def _kv_kernel(x_ref, mod_ref, g_ref, wkv_ref, gk_ref, cos_ref, sin_ref, *out_refs, split):
    x = x_ref[...]
    h = _norm_mod(x, g_ref[...], mod_ref[0, :, 0:D_MODEL], mod_ref[0, :, D_MODEL:2 * D_MODEL]).astype(BF16)
    rope_a, rope_b = _rope_gain_tables(cos_ref, sin_ref, gk_ref.at[0:1], gk_ref.at[1:2], 1.0)
    k_refs, v_refs = out_refs[:N_GROUPS], out_refs[N_GROUPS:2 * N_GROUPS]
    kv_all = jnp.dot(h, wkv_ref[...], preferred_element_type=F32)
    for hd in range(N_KV_HEADS):
        kh = kv_all[:, hd * HEAD_DIM:(hd + 1) * HEAD_DIM]
        vh = kv_all[:, KV_WIDTH + hd * HEAD_DIM:KV_WIDTH + (hd + 1) * HEAD_DIM]
        g_idx, k_idx = divmod(hd, HKV_G)
        sl = slice(k_idx * HEAD_DIM, (k_idx + 1) * HEAD_DIM)
        kh = _head_norm_rope(kh, rope_a, rope_b)
        k_refs[g_idx][:, sl] = kh
        v_refs[g_idx][:, sl] = vh
        if split and g_idx in SPLIT_GROUPS:
            n = SPLIT_GROUPS.index(g_idx)
            scr = out_refs[-1]
            for j, (val, dst) in enumerate(((kh, out_refs[2 * N_GROUPS + 2 * n]),
                                            (vh, out_refs[2 * N_GROUPS + 2 * n + 1]))):
                slot = (n * HKV_G + k_idx) * 2 + j
                scr[slot] = val
                for c in range(SPLIT):
                    dst[c * Q_BLOCK:(c + 1) * Q_BLOCK, sl] = (
                        scr[slot, pl.ds(c, Q_BLOCK, stride=SPLIT), :].astype(BF16))


def _kv_proj(x, mod_kv, tiles_per_seq, g, w_kv, g_k, cos, sin, tm, split):
    rows, d = x.shape
    mr = mod_kv.shape[1]
    assert not split or tm == SPLIT_TILE
    row_spec = pl.BlockSpec((tm, d), lambda i: (i, 0))
    tab_spec = pl.BlockSpec((tm, HEAD_DIM), lambda i: (i % tiles_per_seq, 0))
    out_spec = pl.BlockSpec((tm, GROUP_KV_WIDTH), lambda i: (i, 0))
    n_split = 2 * len(SPLIT_GROUPS) if split else 0
    return pl.pallas_call(
        functools.partial(_kv_kernel, split=split),
        grid=(rows // tm,),
        in_specs=[row_spec,
                  pl.BlockSpec((1, mr, 2 * d), lambda i: (i // tiles_per_seq, 0, 0)),
                  pl.BlockSpec((1, d), lambda i: (0, 0)),
                  _resident((d, 2 * KV_WIDTH), lambda i: (0, 0)),
                  pl.BlockSpec((2, HEAD_DIM), lambda i: (0, 0)),
                  tab_spec, tab_spec],
        out_specs=[out_spec] * (2 * N_GROUPS + n_split),
        out_shape=[jax.ShapeDtypeStruct((rows, GROUP_KV_WIDTH), F32)] * (2 * N_GROUPS)
                  + [jax.ShapeDtypeStruct((rows, GROUP_KV_WIDTH), BF16)] * n_split,
        scratch_shapes=[pltpu.VMEM((n_split * HKV_G, tm, HEAD_DIM), F32)] if split else [],
        compiler_params=_cparams("parallel"),
        name="kv_proj",
    )(x, mod_kv, g.reshape(1, d), w_kv, jnp.stack([g_k, _swap_rot_halves(g_k)]), cos, sin)


def _q_kernel(x_ref, mod_ref, g_ref, wq_ref, gq_ref, cos_ref, sin_ref, *refs, split):
    q_refs = refs[:N_GROUPS]
    x = x_ref[...]
    h = _norm_mod(x, g_ref[...], _mod_part(mod_ref, 0), _mod_part(mod_ref, 1)).astype(BF16)
    rope_a, rope_b = _rope_gain_tables(cos_ref, sin_ref, gq_ref.at[0:1], gq_ref.at[1:2], ATTN_SCALE)
    q_all = jnp.dot(h, wq_ref[...], preferred_element_type=F32)
    for hd in range(N_Q_HEADS):
        g_idx, h_idx = divmod(hd, HQ_G)
        sl = slice(h_idx * HEAD_DIM, (h_idx + 1) * HEAD_DIM)
        qh = _head_norm_rope(q_all[:, hd * HEAD_DIM:(hd + 1) * HEAD_DIM], rope_a, rope_b)
        if split and g_idx in SPLIT_GROUPS:
            scr = refs[N_GROUPS]
            n = SPLIT_GROUPS.index(g_idx) * HQ_G + h_idx
            scr[n] = qh
            for c in range(SPLIT):
                q_refs[g_idx][c * Q_BLOCK:(c + 1) * Q_BLOCK, sl] = (
                    scr[n, pl.ds(c, Q_BLOCK, stride=SPLIT), :].astype(BF16))
        else:
            q_refs[g_idx][:, sl] = qh.astype(BF16)


def _q_proj(x, mod, tiles_per_seq, g, w_q, index, g_q, cos, sin, tm, split):
    rows, d = x.shape
    mr = mod.shape[1]
    assert not split or tm == SPLIT_TILE
    row_spec = pl.BlockSpec((tm, d), lambda i: (i, 0))
    tab_spec = pl.BlockSpec((tm, HEAD_DIM), lambda i: (i % tiles_per_seq, 0))
    return pl.pallas_call(
        functools.partial(_q_kernel, split=split),
        grid=(rows // tm,),
        in_specs=[row_spec,
                  pl.BlockSpec((1, mr, 6 * d), lambda i: (i // tiles_per_seq, 0, 0)),
                  pl.BlockSpec((1, d), lambda i: (0, 0)),
                  _stacked(w_q, index),
                  pl.BlockSpec((2, HEAD_DIM), lambda i: (0, 0)),
                  tab_spec, tab_spec],
        out_specs=[pl.BlockSpec((tm, O_WIDTH), lambda i: (i, 0))] * N_GROUPS,
        out_shape=[jax.ShapeDtypeStruct((rows, O_WIDTH), BF16)] * N_GROUPS,
        scratch_shapes=[pltpu.VMEM((len(SPLIT_GROUPS) * HQ_G, tm, HEAD_DIM), F32)] if split else [],
        compiler_params=_cparams("parallel"),
        name="q_proj",
    )(x, mod, g.reshape(1, d), w_q, jnp.stack([g_q, _swap_rot_halves(g_q)]), cos, sin)


def _attn_prompt_kernel(*refs, step, has_prev, n_blk):
    if has_prev:
        q_ref, k_ref, v_ref, kp_ref, vp_ref, o_ref, st_ref = refs
        not_first = pl.program_id(1) > 0
    else:
        q_ref, k_ref, v_ref, o_ref, st_ref = refs
    nq = REP * Q_BLOCK
    lane = lax.broadcasted_iota(jnp.int32, (Q_BLOCK, HEAD_DIM), 1)
    for i in range(n_blk):
        first_kb = i - 1 if step == 1 else 0
        if first_kb < 0 and not has_prev:
            first_kb = 0
        kbs = list(range(first_kb, i + 1))
        nk = len(kbs) * Q_BLOCK
        qpos = i * Q_BLOCK + (lax.broadcasted_iota(jnp.int32, (nq, nk), 0) & (Q_BLOCK - 1))
        kpos = first_kb * Q_BLOCK + lax.broadcasted_iota(jnp.int32, (nq, nk), 1)
        dist = qpos - kpos
        valid = (dist >= 0) & (dist <= step * Q_BLOCK) & ((dist & (step - 1)) == 0)
        if first_kb < 0:
            valid = valid & (not_first | (kpos >= 0))
        for k in range(HKV_G):
            ksl = slice(k * HEAD_DIM, (k + 1) * HEAD_DIM)
            q = jnp.concatenate(
                [q_ref[i, :, (k * REP + r) * HEAD_DIM:(k * REP + r + 1) * HEAD_DIM] for r in range(REP)], axis=0)
            keys = jnp.concatenate([kp_ref[:, ksl] if j < 0 else k_ref[j, :, ksl] for j in kbs], axis=0)
            vals = jnp.concatenate([vp_ref[:, ksl] if j < 0 else v_ref[j, :, ksl] for j in kbs], axis=0)
            s = lax.dot_general(q, keys.astype(BF16), (((1,), (1,)), ((), ())), preferred_element_type=F32)
            s = jnp.where(valid, s, NEG)
            m = jnp.max(s, axis=1, keepdims=True)
            p = jnp.exp(s - m)
            den = jnp.sum(p, axis=1, keepdims=True)
            out = jnp.dot(p.astype(BF16), vals.astype(BF16), preferred_element_type=F32) / den
            lse = m + jnp.log(den)
            stats = jnp.zeros((Q_BLOCK, HEAD_DIM), F32)
            for r in range(REP):
                rs = slice(r * Q_BLOCK, (r + 1) * Q_BLOCK)
                o_ref[i, :, (k * REP + r) * HEAD_DIM:(k * REP + r + 1) * HEAD_DIM] = out[rs].astype(BF16)
                stats = jnp.where(lane == r, lse[rs], stats)
            st_ref[i, :, ksl] = stats


def _attn_prompt_group(q, k_g, v_g, group, batch, seq):
    _, dil = DIL_GROUPS[group]
    rows = batch * seq
    n_tiles = seq // SPLIT_TILE
    if group in SPLIT_GROUPS:
        step = dil // SPLIT
        lead = (batch, n_tiles, SPLIT, Q_BLOCK)
        view = lambda a: a.reshape(*lead, a.shape[-1])
        spec = lambda w: pl.BlockSpec((None, n_tiles, None, Q_BLOCK, w), lambda b, c: (b, 0, c, 0, 0))
        out_shape = [jax.ShapeDtypeStruct((*lead, O_WIDTH), BF16),
                     jax.ShapeDtypeStruct((*lead, GROUP_KV_WIDTH), F32)]
        out, stats = pl.pallas_call(
            functools.partial(_attn_prompt_kernel, step=step, has_prev=False, n_blk=n_tiles),
            grid=(batch, SPLIT),
            in_specs=[spec(O_WIDTH), spec(GROUP_KV_WIDTH), spec(GROUP_KV_WIDTH)],
            out_specs=[spec(O_WIDTH), spec(GROUP_KV_WIDTH)],
            out_shape=out_shape,
            compiler_params=_cparams("parallel", "parallel"),
            name=f"attn_prompt_g{group}",
        )(view(q), view(k_g), view(v_g))
    else:
        lead = (batch * n_tiles, SPLIT, Q_BLOCK)
        view = lambda a: a.reshape(*lead, a.shape[-1])
        spec = lambda w: pl.BlockSpec((None, SPLIT, Q_BLOCK, w), lambda b, i: (b * n_tiles + i, 0, 0, 0))
        prev = pl.BlockSpec((None, None, Q_BLOCK, GROUP_KV_WIDTH),
                            lambda b, i: (jnp.maximum(b * n_tiles + i - 1, 0), SPLIT - 1, 0, 0))
        out_shape = [jax.ShapeDtypeStruct((*lead, O_WIDTH), BF16),
                     jax.ShapeDtypeStruct((*lead, GROUP_KV_WIDTH), F32)]
        out, stats = pl.pallas_call(
            functools.partial(_attn_prompt_kernel, step=dil, has_prev=True, n_blk=SPLIT),
            grid=(batch, n_tiles),
            in_specs=[spec(O_WIDTH), spec(GROUP_KV_WIDTH), spec(GROUP_KV_WIDTH), prev, prev],
            out_specs=[spec(O_WIDTH), spec(GROUP_KV_WIDTH)],
            out_shape=out_shape,
            compiler_params=_cparams("parallel", "arbitrary"),
            name=f"attn_prompt_g{group}",
        )(view(q), view(k_g), view(v_g), view(k_g), view(v_g))
    return out.reshape(rows, O_WIDTH), stats.reshape(rows, GROUP_KV_WIDTH)


def _merge_o_kernel(x_ref, mod_ref, o0_ref, o1_ref, o2_ref, s0_ref, s1_ref, s2_ref, wo_ref, xo_ref,
                    on_ref, sn_ref):
    for n, (o_ref, s_ref) in enumerate(((o1_ref, s1_ref), (o2_ref, s2_ref))):
        for c in range(SPLIT):
            rows = slice(c * Q_BLOCK, (c + 1) * Q_BLOCK)
            dst = pl.ds(c, Q_BLOCK, stride=SPLIT)
            for hd in range(HQ_G):
                on_ref[n, hd, dst, :] = o_ref[rows, hd * HEAD_DIM:(hd + 1) * HEAD_DIM].astype(F32)
            for k in range(HKV_G):
                sn_ref[n, k, dst, :] = s_ref[rows, k * HEAD_DIM:(k + 1) * HEAD_DIM]
    heads = []
    for k in range(HKV_G):
        for r in range(REP):
            hd = k * REP + r
            lses = [s0_ref[:, k * HEAD_DIM + r:k * HEAD_DIM + r + 1],
                    sn_ref[0, k, :, r:r + 1], sn_ref[1, k, :, r:r + 1]]
            m = jnp.maximum(jnp.maximum(lses[0], lses[1]), lses[2])
            es = [jnp.exp(l - m) for l in lses]
            den = es[0] + es[1] + es[2]
            sl = slice(hd * HEAD_DIM, (hd + 1) * HEAD_DIM)
            outs = [o0_ref[:, sl].astype(F32), on_ref[0, hd], on_ref[1, hd]]
            o = sum((e / den) * o_g for e, o_g in zip(es, outs))
            heads.append(o.astype(BF16))
    o = jnp.concatenate(heads, axis=1)
    y = jnp.dot(o, wo_ref[...], preferred_element_type=F32)
    xo_ref[...] = x_ref[...] + _mod_part(mod_ref, 2) * y


def _merge_o(x, mod, tiles_per_seq, outs, stats, w_o, index, tm):
    rows, d = x.shape
    mr = mod.shape[1]
    assert tm == SPLIT_TILE
    row_spec = pl.BlockSpec((tm, d), lambda i: (i, 0))
    o_spec = pl.BlockSpec((tm, O_WIDTH), lambda i: (i, 0))
    st_spec = pl.BlockSpec((tm, GROUP_KV_WIDTH), lambda i: (i, 0))
    return pl.pallas_call(
        _merge_o_kernel,
        grid=(rows // tm,),
        in_specs=[row_spec,
                  pl.BlockSpec((1, mr, 6 * d), lambda i: (i // tiles_per_seq, 0, 0))]
                 + [o_spec] * 3 + [st_spec] * 3 + [_stacked(w_o, index)],
        out_specs=row_spec,
        out_shape=jax.ShapeDtypeStruct((rows, d), F32),
        scratch_shapes=[pltpu.VMEM((len(SPLIT_GROUPS), HQ_G, tm, HEAD_DIM), F32),
                        pltpu.VMEM((len(SPLIT_GROUPS), HKV_G, tm, HEAD_DIM), F32)],
        compiler_params=_cparams("parallel"),
        name="merge_o_proj",
    )(x, mod, *outs, *stats, w_o)


def _attn_sample_kernel(*refs, n_new, write_buffers):
    q_ref = refs[0]
    nk_refs = refs[1:1 + N_GROUPS]
    nv_refs = refs[1 + N_GROUPS:1 + 2 * N_GROUPS]
    ck_refs = refs[1 + 2 * N_GROUPS:1 + 3 * N_GROUPS]
    cv_refs = refs[1 + 3 * N_GROUPS:1 + 4 * N_GROUPS]
    o_ref = refs[1 + 4 * N_GROUPS]
    nq = REP * n_new
    n_new_rows = HKV_G * n_new
    qi = lax.broadcasted_iota(jnp.int32, (nq, 1), 0) & (n_new - 1)

    if write_buffers:
        ok_refs = refs[2 + 4 * N_GROUPS:2 + 5 * N_GROUPS]
        ov_refs = refs[2 + 5 * N_GROUPS:2 + 6 * N_GROUPS]
        for g in range(N_GROUPS):
            for c_ref, n_ref, o_buf in ((ck_refs[g], nk_refs[g], ok_refs[g]), (cv_refs[g], nv_refs[g], ov_refs[g])):
                n_rows = c_ref.shape[1]
                o_buf[0, 0:n_rows - n_new_rows, :] = c_ref[0, n_new_rows:n_rows, :]
                o_buf[0, n_rows - n_new_rows:n_rows, :] = n_ref[0]

    for k in range(HKV_G):
        outs, lses = [], []
        for g, (window, dil) in enumerate(DIL_GROUPS):
            length = ck_refs[g].shape[1] // HKV_G
            q = q_ref[0, g, k]
            qf = q.astype(F32)
            keys = ck_refs[g][0, pl.ds(k, length, stride=HKV_G), :].astype(BF16)
            vals = cv_refs[g][0, pl.ds(k, length, stride=HKV_G), :].astype(BF16)
            s = lax.dot_general(q, keys, (((1,), (1,)), ((), ())), preferred_element_type=F32)
            idx = lax.broadcasted_iota(jnp.int32, (nq, length), 1)
            diff = length + qi - idx
            s = jnp.where(((diff & (dil - 1)) == 0) & (diff <= window), s, NEG)
            new_k = nk_refs[g][0]
            new_v = nv_refs[g][0]
            s_new = []
            for j in range(n_new):
                sj = jnp.sum(qf * new_k[HKV_G * j + k:HKV_G * j + k + 1, :], axis=1, keepdims=True)
                ok = (qi >= j) & (((qi - j) & (dil - 1)) == 0)
                s_new.append(jnp.where(ok, sj, NEG))
            m = jnp.max(s, axis=1, keepdims=True)
            for sj in s_new:
                m = jnp.maximum(m, sj)
            p = jnp.exp(s - m)
            den = jnp.sum(p, axis=1, keepdims=True)
            acc = jnp.dot(p.astype(BF16), vals, preferred_element_type=F32)
            for j, sj in enumerate(s_new):
                pj = jnp.exp(sj - m)
                den = den + pj
                acc = acc + pj * new_v[HKV_G * j + k:HKV_G * j + k + 1, :]
            outs.append(acc / den)
            lses.append(m + jnp.log(den))
        m = jnp.maximum(jnp.maximum(lses[0], lses[1]), lses[2])
        es = [jnp.exp(l - m) for l in lses]
        den = es[0] + es[1] + es[2]
        o_ref[0, k] = sum((e / den) * o for e, o in zip(es, outs))


def _attn_sample(q, new_k, new_v, cache_k, cache_v, n_seq, n_new, write_buffers):
    assert HKV_G * n_new == 8
    q5 = q.reshape(n_seq, n_new, N_GROUPS, HKV_G, REP, HEAD_DIM)
    q5 = jnp.transpose(q5, (0, 2, 3, 4, 1, 5)).reshape(n_seq, N_GROUPS, HKV_G, REP * n_new, HEAD_DIM)
    rows2 = lambda a, n: a.reshape(n_seq, n * HKV_G, HEAD_DIM)
    nk = [rows2(a, n_new) for a in new_k]
    nv = [rows2(a, n_new) for a in new_v]
    ck = [rows2(c, c.shape[1]) for c in cache_k]
    cv = [rows2(c, c.shape[1]) for c in cache_v]
    new_spec = pl.BlockSpec((1, HKV_G * n_new, HEAD_DIM), lambda b: (b, 0, 0))
    cache_specs = [pl.BlockSpec((1, c.shape[1], HEAD_DIM), lambda b: (b, 0, 0)) for c in ck]
    cache_shapes = [jax.ShapeDtypeStruct(c.shape, F32) for c in ck]
    n_buf = 2 if write_buffers else 0
    res = pl.pallas_call(
        functools.partial(_attn_sample_kernel, n_new=n_new, write_buffers=write_buffers),
        grid=(n_seq,),
        in_specs=[pl.BlockSpec((1, N_GROUPS, HKV_G, REP * n_new, HEAD_DIM), lambda b: (b, 0, 0, 0, 0))]
                 + [new_spec] * (2 * N_GROUPS) + cache_specs + cache_specs,
        out_specs=[pl.BlockSpec((1, HKV_G, REP * n_new, HEAD_DIM), lambda b: (b, 0, 0, 0))]
                  + cache_specs * n_buf,
        out_shape=[jax.ShapeDtypeStruct((n_seq, HKV_G, REP * n_new, HEAD_DIM), F32)] + cache_shapes * n_buf,
        compiler_params=_cparams("parallel"),
        name="attn_sample",
    )(q5, *nk, *nv, *ck, *cv)
    o = res[0].reshape(n_seq, HKV_G, REP, n_new, HEAD_DIM)
    o = jnp.transpose(o, (0, 3, 1, 2, 4)).reshape(n_seq * n_new, O_WIDTH)
    if not write_buffers:
        return o, None, None
    shape4 = lambda a: a.reshape(n_seq, a.shape[1] // HKV_G, HKV_G, HEAD_DIM)
    new_ck = [shape4(a) for a in res[1:1 + N_GROUPS]]
    new_cv = [shape4(a) for a in res[1 + N_GROUPS:1 + 2 * N_GROUPS]]
    return o, new_ck, new_cv


def _o_proj_kernel(x_ref, mod_ref, o_ref, wo_ref, xo_ref):
    y = jnp.dot(o_ref[...].astype(BF16), wo_ref[...], preferred_element_type=F32)
    xo_ref[...] = x_ref[...] + _mod_part(mod_ref, 2) * y


def _o_proj(x, mod, o, w_o, index):
    rows, d = x.shape
    mr = mod.shape[1]
    full = pl.BlockSpec((rows, d), lambda i: (0, 0))
    return pl.pallas_call(
        _o_proj_kernel,
        grid=(1,),
        in_specs=[full, pl.BlockSpec((1, mr, 6 * d), lambda i: (0, 0, 0)),
                  pl.BlockSpec((rows, O_WIDTH), lambda i: (0, 0)),
                  _stacked(w_o, index)],
        out_specs=full,
        out_shape=jax.ShapeDtypeStruct((rows, d), F32),
        compiler_params=_cparams("arbitrary"),
        name="o_proj_sample",
    )(x, mod, o, w_o)


def _trunk(x, mods, mod_kv, pos, weights, tm, conv_prev, kv_prev):
    (g_norm_mix, g_norm_ffn, w_in, w_conv, w_out_conv, g_norm_kv, w_kv, g_k, w_q, g_q, w_o,
     w_gu_dense, w_down_dense) = weights
    batch, seq, d = x.shape
    rows = batch * seq
    depth = g_norm_mix.shape[0]
    n_conv = w_in.shape[0]
    sample = kv_prev is not None
    tiles_per_seq = 1 if sample else seq // tm
    x = x.reshape(rows, d)
    cos, sin = _rope_tables(pos)
    if sample:
        cos, sin = jnp.tile(cos, (batch, 1)), jnp.tile(sin, (batch, 1))
    conv_state = []
    kv_state = None
    for layer in range(depth):
        mod = mods[layer]
        if layer == n_conv:
            kvs = _kv_proj(x, mod_kv, tiles_per_seq, g_norm_kv, w_kv, g_k, cos, sin, tm, not sample)
            k_new, v_new = kvs[:N_GROUPS], kvs[N_GROUPS:2 * N_GROUPS]
            k_att, v_att = list(k_new), list(v_new)
            for n, g_idx in enumerate(SPLIT_GROUPS if not sample else ()):
                k_att[g_idx], v_att[g_idx] = kvs[2 * N_GROUPS + 2 * n], kvs[2 * N_GROUPS + 2 * n + 1]
        if layer < n_conv:
            if sample:
                st = conv_prev[layer]
                zero = jnp.zeros((batch, seq - 2, d), F32)
                pa = jnp.concatenate([st[:, 1:2], jnp.zeros((batch, seq - 1, d), F32)], axis=1)
                pb = jnp.concatenate([st, zero], axis=1)
                prev = (seq, pa.reshape(rows, d), pb.reshape(rows, d))
            else:
                prev = None
            x, u_tail = _conv_layer(x, mod, tiles_per_seq, g_norm_mix[layer], w_in, w_conv, w_out_conv, layer,
                                    tm, prev)
            if sample:
                conv_state.append(u_tail.reshape(batch, seq, d)[:, seq - (CONV_W - 1):])
            else:
                conv_state.append(u_tail[:, 8 - (CONV_W - 1):])
        else:
            lb = layer - n_conv
            qs = _q_proj(x, mod, tiles_per_seq, g_norm_mix[layer], w_q, lb, g_q[lb], cos, sin, tm, not sample)
            if sample:
                o, new_ck, new_cv = _attn_sample(jnp.concatenate(qs, axis=1), k_new, v_new, kv_prev[0],
                                                 kv_prev[1], batch, seq, kv_state is None)
                if kv_state is None:
                    kv_state = (new_ck, new_cv)
                x = _o_proj(x, mod, o, w_o, lb)
            else:
                res = [_attn_prompt_group(qs[g], k_att[g], v_att[g], g, batch, seq) for g in range(N_GROUPS)]
                x = _merge_o(x, mod, tiles_per_seq, [r[0] for r in res], [r[1] for r in res], w_o, lb, tm)
        if layer % 2 == 0:
            x = _ffn_layer(x, mod, tiles_per_seq, g_norm_ffn[layer], w_gu_dense, w_down_dense, layer // 2, tm)
        else:
            tm_moe = tm if sample else min(SPARSE_TILE, seq)
            x = yield (x, mod, 1 if sample else seq // tm_moe, tm_moe), layer
    if not sample:
        shape4 = lambda a: a.reshape(batch, seq, HKV_G, HEAD_DIM)
        kv_state = ([shape4(k)[:, -min(w, seq):] for k, (w, _) in zip(k_new, DIL_GROUPS)],
                    [shape4(v)[:, -min(w, seq):] for v, (w, _) in zip(v_new, DIL_GROUPS)])
    return x.reshape(batch, seq, d), jnp.stack(conv_state, axis=0), kv_state


def _resume(trunk, x):
    try:
        return False, trunk.send(x)
    except StopIteration as done:
        return True, done.value


def kernel(x_prompt, x_sample, state_conv, cache_k_g0, cache_v_g0, cache_k_g1, cache_v_g1, cache_k_g2,
           cache_v_g2, c_prompt, c_sample, g_norm_mix, g_norm_ffn, w_ada, b_ada, w_in, w_conv, w_out_conv,
           g_norm_kv, w_ada_kv, b_ada_kv, w_kv, g_k, w_q, g_q, w_o, w_gu_dense, w_down_dense, w_router,
           b_router, w_gu_moe, w_down_moe):
    b_p, t_p, d = x_prompt.shape
    b_s, t_s, _ = x_sample.shape
    bf = lambda w: w.astype(BF16)
    weights = (g_norm_mix, g_norm_ffn, bf(w_in), w_conv, bf(w_out_conv), g_norm_kv, bf(w_kv), g_k, bf(w_q),
               g_q, bf(w_o), bf(w_gu_dense), bf(w_down_dense))
    w_gu_moe, w_down_moe = bf(w_gu_moe), bf(w_down_moe)

    c_all = jnp.concatenate([c_prompt, c_sample], axis=0)
    mods_all = _ada(c_all, w_ada, b_ada)
    mod_kv_all = _ada(c_all, w_ada_kv[None], b_ada_kv[None])[0]
    depth = w_ada.shape[0]
    mods_p = [mods_all[l, :b_p, None, :] for l in range(depth)]
    mods_s = [jnp.repeat(mods_all[l, b_p:], t_s, axis=0)[None] for l in range(depth)]
    mod_kv_p = mod_kv_all[:b_p, None, :]
    mod_kv_s = jnp.repeat(mod_kv_all[b_p:], t_s, axis=0)[None]

    pos_p = jnp.arange(t_p, dtype=jnp.int32)
    pos_s = PAST_LEN + jnp.arange(t_s, dtype=jnp.int32)
    kv_prev = ([cache_k_g0, cache_k_g1, cache_k_g2], [cache_v_g0, cache_v_g1, cache_v_g2])
    trunks = [_trunk(x_prompt, mods_p, mod_kv_p, pos_p, weights, min(PROMPT_TILE, t_p), None, None),
              _trunk(x_sample, mods_s, mod_kv_s, pos_s, weights, b_s * t_s, state_conv, kv_prev)]
    requests = [next(t) for t in trunks]
    results = None
    while results is None:
        layer = requests[0][1]
        xs = _moe_sparse_layer([r[0] for r in requests], g_norm_ffn[layer], w_router[layer // 2],
                               b_router[layer // 2], w_gu_moe, w_down_moe, layer // 2)
        steps = [_resume(t, x) for t, x in zip(trunks, xs)]
        if all(done for done, _ in steps):
            results = [value for _, value in steps]
        else:
            requests = [value for _, value in steps]
    (y_prompt, conv_p, (kp, vp)), (y_sample, conv_s, (ksn, vsn)) = results
    return (y_prompt, y_sample, conv_p, conv_s,
            kp[0], vp[0], kp[1], vp[1], kp[2], vp[2],
            ksn[0], vsn[0], ksn[1], vsn[1], ksn[2], vsn[2])
```

```python
import functools

import jax
import jax.numpy as jnp
from jax import lax
from jax.experimental import pallas as pl
from jax.experimental.pallas import tpu as pltpu

D_MODEL = 1024
PAST_LEN = 8192
CONV_W = 3
HEAD_DIM = 128
ROT_DIM = HEAD_DIM // 4
ROPE_THETA = 500000.0
DIL_GROUPS = ((128, 1), (512, 4), (2048, 16))
N_GROUPS = len(DIL_GROUPS)
HQ_G = 8
HKV_G = 2
REP = HQ_G // HKV_G
N_Q_HEADS = N_GROUPS * HQ_G
N_KV_HEADS = N_GROUPS * HKV_G
Q_WIDTH = N_Q_HEADS * HEAD_DIM
KV_WIDTH = N_KV_HEADS * HEAD_DIM
O_WIDTH = HQ_G * HEAD_DIM
GROUP_KV_WIDTH = HKV_G * HEAD_DIM
N_EXPERTS = 8
EPS = 1e-6
ATTN_SCALE = HEAD_DIM ** -0.5
NEG = -1e30
Q_BLOCK = 128

F32 = jnp.float32
BF16 = jnp.bfloat16
HIGHEST = lax.Precision.HIGHEST

V7X_VMEM_LIMIT_BYTES = 56 * 1024 * 1024
PROMPT_TILE = 512
MOE_FF_CHUNK = 512


def _cparams(*sem):
    return pltpu.CompilerParams(dimension_semantics=sem, vmem_limit_bytes=V7X_VMEM_LIMIT_BYTES)


def _resident(shape, index_map):
    return pl.BlockSpec(shape, index_map, pipeline_mode=pl.Buffered(1))


def _stacked(w, index):
    shape = w.shape[1:]
    return pl.BlockSpec((None,) + shape, lambda *_: (index,) + (0,) * len(shape), pipeline_mode=pl.Buffered(1))


def _silu(x):
    return x * jax.nn.sigmoid(x)


def _norm_mod(x, g, shift, scale):
    y = x * lax.rsqrt(jnp.mean(x * x, axis=-1, keepdims=True) + EPS) * g
    return y * (1.0 + scale) + shift


def _mod_part(mod_ref, k):
    return mod_ref[0, :, k * D_MODEL:(k + 1) * D_MODEL]


def _ada_kernel(c_ref, w_ref, b_ref, o_ref):
    c = c_ref[...]
    o_ref[0] = jnp.dot(_silu(c), w_ref[0], preferred_element_type=F32, precision=HIGHEST) + b_ref[0]


def _ada(c, w, b):
    n_layers, d, n = w.shape
    m = c.shape[0]
    tn = 1024
    return pl.pallas_call(
        _ada_kernel,
        grid=(n_layers, n // tn),
        in_specs=[pl.BlockSpec((m, d), lambda l, j: (0, 0)),
                  pl.BlockSpec((1, d, tn), lambda l, j: (l, 0, j)),
                  pl.BlockSpec((1, 1, tn), lambda l, j: (l, 0, j))],
        out_specs=pl.BlockSpec((1, m, tn), lambda l, j: (l, 0, j)),
        out_shape=jax.ShapeDtypeStruct((n_layers, m, n), F32),
        compiler_params=_cparams("parallel", "parallel"),
        name="ada",
    )(c, w, b.reshape(n_layers, 1, n))


def _conv_kernel(*refs, tm, tiles_per_seq, seq_rows):
    if seq_rows is None:
        x_ref, mod_ref, g_ref, win_ref, wc_ref, wout_ref, xo_ref, ust_ref, carry_ref = refs
    else:
        x_ref, mod_ref, g_ref, win_ref, wc_ref, wout_ref, pa_ref, pb_ref, xo_ref, ust_ref = refs
    x = x_ref[...]
    h = _norm_mod(x, g_ref[...], _mod_part(mod_ref, 0), _mod_part(mod_ref, 1)).astype(BF16)
    b_gate = jnp.dot(h, win_ref[:, 0:D_MODEL], preferred_element_type=F32)
    c_gate = jnp.dot(h, win_ref[:, D_MODEL:2 * D_MODEL], preferred_element_type=F32)
    v = jnp.dot(h, win_ref[:, 2 * D_MODEL:3 * D_MODEL], preferred_element_type=F32)
    u = c_gate * v
    row = lax.broadcasted_iota(jnp.int32, (tm, 1), 0)
    r1 = pltpu.roll(u, 1, 0)
    r2 = pltpu.roll(u, 2, 0)
    if seq_rows is None:
        @pl.when(pl.program_id(0) % tiles_per_seq == 0)
        def _():
            carry_ref[...] = jnp.zeros_like(carry_ref)
        um1 = jnp.where(row == 0, carry_ref[7:8, :], r1)
        um2 = jnp.where(row == 0, carry_ref[6:7, :], jnp.where(row == 1, carry_ref[7:8, :], r2))
        carry_ref[...] = u[tm - 8:tm, :]
        ust_ref[0] = u[tm - 8:tm, :]
    else:
        rowm = row & (seq_rows - 1)
        um1 = jnp.where(rowm == 0, pa_ref[...], r1)
        um2 = jnp.where(rowm < 2, pb_ref[...], r2)
        ust_ref[...] = u
    y = um2 * wc_ref[0:1, :] + um1 * wc_ref[1:2, :] + u * wc_ref[2:3, :]
    z = (b_gate * y).astype(BF16)
    out = jnp.dot(z, wout_ref[...], preferred_element_type=F32)
    xo_ref[...] = x + _mod_part(mod_ref, 2) * out


def _conv_layer(x, mod, tiles_per_seq, g, w_in, w_conv, w_out, layer, tm, prev=None):
    rows, d = x.shape
    n_tiles = rows // tm
    mr = mod.shape[1]
    row_spec = pl.BlockSpec((tm, d), lambda i: (i, 0))
    in_specs = [row_spec,
                pl.BlockSpec((1, mr, 6 * d), lambda i: (i // tiles_per_seq, 0, 0)),
                pl.BlockSpec((1, d), lambda i: (0, 0)),
                _stacked(w_in, layer), _stacked(w_conv, layer), _stacked(w_out, layer)]
    args = [x, mod, g.reshape(1, d), w_in, w_conv, w_out]
    if prev is None:
        seq_rows = None
        n_seq = n_tiles // tiles_per_seq
        ust_shape = jax.ShapeDtypeStruct((n_seq, 8, d), F32)
        ust_spec = pl.BlockSpec((1, 8, d), lambda i: (i // tiles_per_seq, 0, 0))
        scratch = [pltpu.VMEM((8, d), F32)]
    else:
        seq_rows, pa, pb = prev
        in_specs += [row_spec, row_spec]
        args += [pa, pb]
        ust_shape = jax.ShapeDtypeStruct((rows, d), F32)
        ust_spec = row_spec
        scratch = []
    return pl.pallas_call(
        functools.partial(_conv_kernel, tm=tm, tiles_per_seq=tiles_per_seq, seq_rows=seq_rows),
        grid=(n_tiles,),
        in_specs=in_specs,
        out_specs=[row_spec, ust_spec],
        out_shape=[jax.ShapeDtypeStruct((rows, d), F32), ust_shape],
        scratch_shapes=scratch,
        compiler_params=_cparams("arbitrary"),
        name="conv_layer",
    )(*args)


def _ffn_kernel(x_ref, mod_ref, g_ref, wgu_ref, wd_ref, xo_ref, *, d_ff, n_chunks):
    x = x_ref[...]
    h = _norm_mod(x, g_ref[...], _mod_part(mod_ref, 3), _mod_part(mod_ref, 4)).astype(BF16)
    fc = d_ff // n_chunks
    acc = None
    for c in range(n_chunks):
        gate = jnp.dot(h, wgu_ref[:, c * fc:(c + 1) * fc], preferred_element_type=F32)
        up = jnp.dot(h, wgu_ref[:, d_ff + c * fc:d_ff + (c + 1) * fc], preferred_element_type=F32)
        a = (_silu(gate) * up).astype(BF16)
        part = jnp.dot(a, wd_ref[c * fc:(c + 1) * fc, :], preferred_element_type=F32)
        acc = part if acc is None else acc + part
    xo_ref[...] = x + _mod_part(mod_ref, 5) * acc


def _ffn_layer(x, mod, tiles_per_seq, g, w_gu, w_down, index, tm):
    rows, d = x.shape
    d_ff = w_down.shape[1]
    mr = mod.shape[1]
    row_spec = pl.BlockSpec((tm, d), lambda i: (i, 0))
    return pl.pallas_call(
        functools.partial(_ffn_kernel, d_ff=d_ff, n_chunks=2),
        grid=(rows // tm,),
        in_specs=[row_spec,
                  pl.BlockSpec((1, mr, 6 * d), lambda i: (i // tiles_per_seq, 0, 0)),
                  pl.BlockSpec((1, d), lambda i: (0, 0)),
                  _stacked(w_gu, index), _stacked(w_down, index)],
        out_specs=row_spec,
        out_shape=jax.ShapeDtypeStruct((rows, d), F32),
        compiler_params=_cparams("parallel"),
        name="ffn_dense",
    )(x, mod, g.reshape(1, d), w_gu, w_down)


SEG_ALIGN = 16
SEG_SIZES = (512, 256, 128, 64, 32, 16)
SPARSE_TILE = 512
EXPERT_ROW_TILE = 256


def _route_kernel(x_ref, mod_ref, g_ref, wrt_ref, br_ref, h_ref, slots_ref, gw_ref, seg_ref, *, tm):
    x = x_ref[...]
    h = _norm_mod(x, g_ref[...], _mod_part(mod_ref, 3), _mod_part(mod_ref, 4))
    h_ref[...] = h.astype(BF16)
    logits = lax.dot_general(wrt_ref[...], h, (((1,), (1,)), ((), ())), preferred_element_type=F32,
                             precision=HIGHEST) + br_ref[...]
    eidx = lax.broadcasted_iota(jnp.int32, logits.shape, 0).astype(F32)
    m1 = jnp.max(logits, axis=0, keepdims=True)
    i1 = jnp.min(jnp.where(logits == m1, eidx, float(N_EXPERTS)), axis=0, keepdims=True)
    sel1 = eidx == i1
    rest = jnp.where(sel1, -jnp.inf, logits)
    m2 = jnp.max(rest, axis=0, keepdims=True)
    i2 = jnp.min(jnp.where(rest == m2, eidx, float(N_EXPERTS)), axis=0, keepdims=True)
    sel2 = eidx == i2
    e2 = jnp.exp(m2 - m1)
    den = 1.0 + e2
    gw_ref[0] = jnp.concatenate([1.0 / den, e2 / den], axis=0)
    mask = jnp.where(sel1 | sel2, 1.0, 0.0)
    before = (lax.broadcasted_iota(jnp.int32, (tm, tm), 0) < lax.broadcasted_iota(jnp.int32, (tm, tm), 1))
    rank = jnp.dot(mask.astype(BF16), jnp.where(before, 1.0, 0.0).astype(BF16), preferred_element_type=F32)
    count = jnp.sum(mask, axis=1, keepdims=True)
    seg = jnp.floor((count + (SEG_ALIGN - 1)) * (1.0 / SEG_ALIGN)) * SEG_ALIGN
    start = jnp.zeros_like(seg)
    for e in range(N_EXPERTS - 1):
        start = start + jnp.where(eidx[:, 0:1] > e, seg[e:e + 1, :], 0.0)
    slot = start + rank
    slot1 = jnp.sum(jnp.where(sel1, slot, 0.0), axis=0, keepdims=True)
    slot2 = jnp.sum(jnp.where(sel2, slot, 0.0), axis=0, keepdims=True)
    slots_ref[0] = jnp.concatenate([slot1, slot2], axis=0).astype(jnp.int32)
    seg_ref[0] = jnp.broadcast_to(seg, (N_EXPERTS, HEAD_DIM)).astype(jnp.int32)


def _route(x, mod, tiles_per_seq, g, w_router, b_router, tm):
    rows, d = x.shape
    mr = mod.shape[1]
    n_tiles = rows // tm
    row_spec = pl.BlockSpec((tm, d), lambda i: (i, 0))
    pair_spec = pl.BlockSpec((1, 2, tm), lambda i: (i, 0, 0))
    return pl.pallas_call(
        functools.partial(_route_kernel, tm=tm),
        grid=(n_tiles,),
        in_specs=[row_spec,
                  pl.BlockSpec((1, mr, 6 * d), lambda i: (i // tiles_per_seq, 0, 0)),
                  pl.BlockSpec((1, d), lambda i: (0, 0)),
                  pl.BlockSpec((N_EXPERTS, d), lambda i: (0, 0)),
                  pl.BlockSpec((N_EXPERTS, 1), lambda i: (0, 0))],
        out_specs=[row_spec, pair_spec, pair_spec,
                   pl.BlockSpec((1, N_EXPERTS, HEAD_DIM), lambda i: (i, 0, 0))],
        out_shape=[jax.ShapeDtypeStruct((rows, d), BF16),
                   jax.ShapeDtypeStruct((n_tiles, 2, tm), jnp.int32),
                   jax.ShapeDtypeStruct((n_tiles, 2, tm), F32),
                   jax.ShapeDtypeStruct((n_tiles, N_EXPERTS, HEAD_DIM), jnp.int32)],
        compiler_params=_cparams("parallel"),
        name="moe_route",
    )(x, mod, g.reshape(1, d), w_router.T, b_router.reshape(N_EXPERTS, 1))


def _segment_copies(tile, seg_ref, loc_ref, goff_ref, packed, sorted_hbm, sem, to_sorted):
    copies = []
    tokens = (packed.shape[0] - N_EXPERTS * SEG_ALIGN) // 2
    max_seg = -(-tokens // SEG_ALIGN) * SEG_ALIGN
    for e in range(N_EXPERTS):
        seg = seg_ref[tile * N_EXPERTS + e]
        loc = loc_ref[tile * N_EXPERTS + e]
        goff = goff_ref[tile * N_EXPERTS + e]
        done = jnp.int32(0)
        for size in (s for s in SEG_SIZES if s <= max_seg):
            vm = packed.at[pl.ds(pl.multiple_of(loc + done, SEG_ALIGN), size), :]
            hb = sorted_hbm.at[pl.ds(pl.multiple_of(goff + done, SEG_ALIGN), size), :]
            src, dst = (vm, hb) if to_sorted else (hb, vm)
            copies.append(((seg & size) != 0, pltpu.make_async_copy(src, dst, sem)))
            done = done + (seg & size)
    return copies


def _start_all(copies):
    for cond, cp in copies:
        pl.when(cond)(cp.start)


def _wait_all(copies):
    for cond, cp in copies:
        pl.when(cond)(cp.wait)


def _dispatch_kernel(seg_ref, loc_ref, goff_ref, h_ref, slots_ref, sorted_in, sorted_out, stage_ref, sems,
                     *, n_slots, n_tiles):
    del sorted_in
    i = pl.program_id(0)
    par = i % 2

    def copies(tile, parity):
        return _segment_copies(tile, seg_ref, loc_ref, goff_ref, stage_ref.at[parity], sorted_out,
                               sems.at[parity], True)

    @pl.when(i >= 2)
    def _():
        _wait_all(copies(i - 2, par))

    tm = h_ref.shape[0]
    srow = lax.broadcasted_iota(jnp.int32, (n_slots, tm), 0)
    onehot = (srow == slots_ref[0, 0:1, :]) | (srow == slots_ref[0, 1:2, :])
    packed = jnp.dot(jnp.where(onehot, 1.0, 0.0).astype(BF16), h_ref[...], preferred_element_type=F32)
    stage_ref[par] = packed.astype(BF16)
    _start_all(copies(i, par))

    @pl.when(i == n_tiles - 1)
    def _():
        if n_tiles >= 2:
            _wait_all(copies(i - 1, 1 - par))
        _wait_all(copies(i, par))


def _dispatch(h, slots, seg, loc, goff, h_sorted, tm):
    rows, d = h.shape
    n_tiles = rows // tm
    n_slots = 2 * tm + N_EXPERTS * SEG_ALIGN
    return pl.pallas_call(
        functools.partial(_dispatch_kernel, n_slots=n_slots, n_tiles=n_tiles),
        grid_spec=pltpu.PrefetchScalarGridSpec(
            num_scalar_prefetch=3,
            grid=(n_tiles,),
            in_specs=[pl.BlockSpec((tm, d), lambda i, *_: (i, 0)),
                      pl.BlockSpec((1, 2, tm), lambda i, *_: (i, 0, 0)),
                      pl.BlockSpec(memory_space=pl.ANY)],
            out_specs=pl.BlockSpec(memory_space=pl.ANY),
            scratch_shapes=[pltpu.VMEM((2, n_slots, d), BF16), pltpu.SemaphoreType.DMA((2,))]),
        out_shape=jax.ShapeDtypeStruct(h_sorted.shape, BF16),
        input_output_aliases={5: 0},
        compiler_params=_cparams("arbitrary"),
        name="moe_dispatch",
    )(seg, loc, goff, h, slots, h_sorted)


def _experts_kernel(te_ref, jb_ref, nact_ref, h_ref, wgu_ref, wd_ref, y_ref, *, d_ff, fc):
    del te_ref, jb_ref

    @pl.when(pl.program_id(0) < nact_ref[0])
    def _():
        h = h_ref[...]
        acc = None
        for c in range(d_ff // fc):
            gate = jnp.dot(h, wgu_ref[0, :, c * fc:(c + 1) * fc], preferred_element_type=F32)
            up = jnp.dot(h, wgu_ref[0, :, d_ff + c * fc:d_ff + (c + 1) * fc], preferred_element_type=F32)
            a = (_silu(gate) * up).astype(BF16)
            part = jnp.dot(a, wd_ref[0, c * fc:(c + 1) * fc, :], preferred_element_type=F32)
            acc = part if acc is None else acc + part
        y_ref[...] = acc.astype(BF16)

    @pl.when(pl.program_id(0) >= nact_ref[0])
    def _():
        y_ref[...] = jnp.zeros_like(y_ref)


def _experts(h_sorted, tile_expert, tile_block, n_active, w_gu, w_down, index, tmf):
    n_rows, d = h_sorted.shape
    d_ff = w_down.shape[2]
    first = index * N_EXPERTS
    w_gu = w_gu.reshape(-1, d, 2 * d_ff)
    w_down = w_down.reshape(-1, d_ff, d)
    return pl.pallas_call(
        functools.partial(_experts_kernel, d_ff=d_ff, fc=MOE_FF_CHUNK),
        grid_spec=pltpu.PrefetchScalarGridSpec(
            num_scalar_prefetch=3,
            grid=(n_rows // tmf,),
            in_specs=[pl.BlockSpec((tmf, d), lambda j, te, jb, na: (jb[j], 0)),
                      pl.BlockSpec((1, d, 2 * d_ff), lambda j, te, jb, na: (first + te[j], 0, 0)),
                      pl.BlockSpec((1, d_ff, d), lambda j, te, jb, na: (first + te[j], 0, 0))],
            out_specs=pl.BlockSpec((tmf, d), lambda j, te, jb, na: (j, 0))),
        out_shape=jax.ShapeDtypeStruct((n_rows, d), BF16),
        compiler_params=pltpu.CompilerParams(dimension_semantics=("arbitrary",),
                                             vmem_limit_bytes=60 * 1024 * 1024),
        name="moe_experts_sorted",
    )(tile_expert, tile_block, n_active, h_sorted, w_gu, w_down)


def _combine_kernel(seg_ref, loc_ref, goff_ref, x_ref, mod_ref, slots_ref, gw_ref, sorted_hbm, xo_ref,
                    stage_ref, sems, *, n_slots, n_tiles):
    i = pl.program_id(0)
    par = i % 2

    def copies(tile, parity):
        return _segment_copies(tile, seg_ref, loc_ref, goff_ref, stage_ref.at[parity], sorted_hbm,
                               sems.at[parity], False)

    @pl.when(i == 0)
    def _():
        stage_ref[...] = jnp.zeros_like(stage_ref)
        _start_all(copies(i, par))

    @pl.when(i + 1 < n_tiles)
    def _():
        _start_all(copies(i + 1, 1 - par))

    _wait_all(copies(i, par))
    y = stage_ref[par]
    tm = x_ref.shape[0]
    scol = lax.broadcasted_iota(jnp.int32, (tm, n_slots), 1)
    moe = None
    for k in range(2):
        onehot = jnp.where(scol == slots_ref[:, k:k + 1], 1.0, 0.0).astype(BF16)
        part = gw_ref[:, k:k + 1] * jnp.dot(onehot, y, preferred_element_type=F32)
        moe = part if moe is None else moe + part
    xo_ref[...] = x_ref[...] + _mod_part(mod_ref, 5) * moe


def _combine(x, mod, tiles_per_seq, slots_col, gw_col, y_sorted, seg, loc, goff, tm):
    rows, d = x.shape
    mr = mod.shape[1]
    n_tiles = rows // tm
    n_slots = 2 * tm + N_EXPERTS * SEG_ALIGN
    row_spec = pl.BlockSpec((tm, d), lambda i, *_: (i, 0))
    pair_spec = pl.BlockSpec((tm, 2), lambda i, *_: (i, 0))
    return pl.pallas_call(
        functools.partial(_combine_kernel, n_slots=n_slots, n_tiles=n_tiles),
        grid_spec=pltpu.PrefetchScalarGridSpec(
            num_scalar_prefetch=3,
            grid=(n_tiles,),
            in_specs=[row_spec,
                      pl.BlockSpec((1, mr, 6 * d), lambda i, *_: (i // tiles_per_seq, 0, 0)),
                      pair_spec, pair_spec,
                      pl.BlockSpec(memory_space=pl.ANY)],
            out_specs=row_spec,
            scratch_shapes=[pltpu.VMEM((2, n_slots, d), BF16), pltpu.SemaphoreType.DMA((2,))]),
        out_shape=jax.ShapeDtypeStruct((rows, d), F32),
        compiler_params=_cparams("arbitrary"),
        name="moe_combine",
    )(seg, loc, goff, x, mod, slots_col, gw_col, y_sorted)


def _moe_sparse_layer(parts, g, w_router, b_router, w_gu, w_down, index):
    tmf = EXPERT_ROW_TILE
    d = parts[0][0].shape[1]
    routed = [_route(x, mod, tps, g, w_router, b_router, tm) for x, mod, tps, tm in parts]
    tiles = [x.shape[0] // tm for x, _, _, tm in parts]
    seg = jnp.concatenate([r[3][:, :, 0] for r in routed], axis=0)
    loc = jnp.cumsum(seg, axis=1) - seg
    region = ((jnp.sum(seg, axis=0) + tmf - 1) // tmf) * tmf
    region_end = jnp.cumsum(region)
    goff = (region_end - region)[None, :] + jnp.cumsum(seg, axis=0) - seg
    n_sorted = sum(2 * x.shape[0] for x, _, _, _ in parts) + sum(tiles) * N_EXPERTS * (SEG_ALIGN - 1)
    n_sorted += N_EXPERTS * (tmf - SEG_ALIGN)
    n_sorted = ((n_sorted + tmf - 1) // tmf) * tmf
    n_row_tiles = n_sorted // tmf
    n_active = region_end[-1] // tmf
    tile_ids = jnp.arange(n_row_tiles, dtype=jnp.int32)
    tile_block = jnp.maximum(jnp.minimum(tile_ids, n_active - 1), 0)
    tile_expert = jnp.sum((region_end // tmf)[None, :] <= tile_block[:, None], axis=1)
    tile_expert = jnp.minimum(tile_expert, N_EXPERTS - 1).astype(jnp.int32)
    tables, first = [], 0
    for n in tiles:
        cut = lambda a: a[first:first + n].reshape(-1).astype(jnp.int32)
        tables.append((cut(seg), cut(loc), cut(goff)))
        first += n
    h_sorted = jnp.zeros((n_sorted, d), BF16)
    for (_, _, _, tm), (h, slots, _, _), tab in zip(parts, routed, tables):
        h_sorted = _dispatch(h, slots, *tab, h_sorted, tm)
    y_sorted = _experts(h_sorted, tile_expert, tile_block.astype(jnp.int32),
                        n_active.reshape(1).astype(jnp.int32), w_gu, w_down, index, tmf)
    outs = []
    for (x, mod, tps, tm), (_, slots, gw, _), tab in zip(parts, routed, tables):
        rows = x.shape[0]
        slots_col = jnp.transpose(slots, (0, 2, 1)).reshape(rows, 2)
        gw_col = jnp.transpose(gw, (0, 2, 1)).reshape(rows, 2)
        outs.append(_combine(x, mod, tps, slots_col, gw_col, y_sorted, *tab, tm))
    return outs


def _rope_tables(pos):
    half = ROT_DIM // 2
    inv = jnp.float32(ROPE_THETA) ** (-jnp.arange(half, dtype=jnp.float32) / half)
    ang = pos.astype(jnp.float32)[:, None] * inv[None, :]
    cos, sin = jnp.cos(ang), jnp.sin(ang)
    t = pos.shape[0]
    pad1 = jnp.ones((t, HEAD_DIM - ROT_DIM), F32)
    pad0 = jnp.zeros((t, HEAD_DIM - ROT_DIM), F32)
    return (jnp.concatenate([cos, cos, pad1], axis=1), jnp.concatenate([-sin, sin, pad0], axis=1))


def _swap_rot_halves(v):
    half = ROT_DIM // 2
    return jnp.concatenate([v[half:ROT_DIM], v[:half], v[ROT_DIM:]])


def _rope_gain_tables(cos_ref, sin_ref, g_ref, gs_ref, scale):
    return cos_ref[...] * (g_ref[...] * scale), sin_ref[...] * (gs_ref[...] * scale)


def _head_norm_rope(xh, a, b):
    half = ROT_DIM // 2
    ones = jnp.ones((HEAD_DIM, HEAD_DIM), BF16)
    ss = jnp.dot((xh * xh).astype(BF16), ones, preferred_element_type=F32)
    rs = lax.rsqrt(ss * (1.0 / HEAD_DIM) + EPS)
    lane = lax.broadcasted_iota(jnp.int32, xh.shape, 1)
    swapped = jnp.where(lane < half, pltpu.roll(xh, HEAD_DIM - half, 1), pltpu.roll(xh, half, 1))
    return (xh * a + swapped * b) * rs


SPLIT = 4
SPLIT_TILE = SPLIT * Q_BLOCK
SPLIT_GROUPS = (1, 2)


def _kv_kernel(x_ref, mod_ref, g_ref, wkv_ref, gk_ref, cos_ref, sin_ref, *out_refs, split):
    x = x_ref[...]
    h = _norm_mod(x, g_ref[...], mod_ref[0, :, 0:D_MODEL], mod_ref[0, :, D_MODEL:2 * D_MODEL]).astype(BF16)
    rope_a, rope_b = _rope_gain_tables(cos_ref, sin_ref, gk_ref.at[0:1], gk_ref.at[1:2], 1.0)
    k_refs, v_refs = out_refs[:N_GROUPS], out_refs[N_GROUPS:2 * N_GROUPS]
    kv_all = jnp.dot(h, wkv_ref[...], preferred_element_type=F32)
    for hd in range(N_KV_HEADS):
        kh = kv_all[:, hd * HEAD_DIM:(hd + 1) * HEAD_DIM]
        vh = kv_all[:, KV_WIDTH + hd * HEAD_DIM:KV_WIDTH + (hd + 1) * HEAD_DIM]
        g_idx, k_idx = divmod(hd, HKV_G)
        sl = slice(k_idx * HEAD_DIM, (k_idx + 1) * HEAD_DIM)
        kh = _head_norm_rope(kh, rope_a, rope_b)
        rows_of_head = pl.ds(k_idx, kh.shape[0], stride=HKV_G)
        k_refs[g_idx][rows_of_head, :] = kh
        v_refs[g_idx][rows_of_head, :] = vh
        if split and g_idx in SPLIT_GROUPS:
            n = SPLIT_GROUPS.index(g_idx)
            scr = out_refs[-1]
            for j, (val, dst) in enumerate(((kh, out_refs[2 * N_GROUPS + 2 * n]),
                                            (vh, out_refs[2 * N_GROUPS + 2 * n + 1]))):
                slot = (n * HKV_G + k_idx) * 2 + j
                scr[slot] = val
                for c in range(SPLIT):
                    dst[c * Q_BLOCK:(c + 1) * Q_BLOCK, sl] = (
                        scr[slot, pl.ds(c, Q_BLOCK, stride=SPLIT), :].astype(BF16))


def _kv_proj(x, mod_kv, tiles_per_seq, g, w_kv, g_k, cos, sin, tm, split):
    rows, d = x.shape
    mr = mod_kv.shape[1]
    assert not split or tm == SPLIT_TILE
    row_spec = pl.BlockSpec((tm, d), lambda i: (i, 0))
    tab_spec = pl.BlockSpec((tm, HEAD_DIM), lambda i: (i % tiles_per_seq, 0))
    out_spec = pl.BlockSpec((tm, GROUP_KV_WIDTH), lambda i: (i, 0))
    nat_spec = pl.BlockSpec((HKV_G * tm, HEAD_DIM), lambda i: (i, 0))
    n_split = 2 * len(SPLIT_GROUPS) if split else 0
    return pl.pallas_call(
        functools.partial(_kv_kernel, split=split),
        grid=(rows // tm,),
        in_specs=[row_spec,
                  pl.BlockSpec((1, mr, 2 * d), lambda i: (i // tiles_per_seq, 0, 0)),
                  pl.BlockSpec((1, d), lambda i: (0, 0)),
                  _resident((d, 2 * KV_WIDTH), lambda i: (0, 0)),
                  pl.BlockSpec((2, HEAD_DIM), lambda i: (0, 0)),
                  tab_spec, tab_spec],
        out_specs=[nat_spec] * (2 * N_GROUPS) + [out_spec] * n_split,
        out_shape=[jax.ShapeDtypeStruct((HKV_G * rows, HEAD_DIM), F32)] * (2 * N_GROUPS)
                  + [jax.ShapeDtypeStruct((rows, GROUP_KV_WIDTH), BF16)] * n_split,
        scratch_shapes=[pltpu.VMEM((n_split * HKV_G, tm, HEAD_DIM), F32)] if split else [],
        compiler_params=_cparams("parallel"),
        name="kv_proj",
    )(x, mod_kv, g.reshape(1, d), w_kv, jnp.stack([g_k, _swap_rot_halves(g_k)]), cos, sin)


def _q_kernel(x_ref, mod_ref, g_ref, wq_ref, gq_ref, cos_ref, sin_ref, *refs, split):
    q_refs = refs[:N_GROUPS]
    x = x_ref[...]
    h = _norm_mod(x, g_ref[...], _mod_part(mod_ref, 0), _mod_part(mod_ref, 1)).astype(BF16)
    rope_a, rope_b = _rope_gain_tables(cos_ref, sin_ref, gq_ref.at[0:1], gq_ref.at[1:2], ATTN_SCALE)
    q_all = jnp.dot(h, wq_ref[...], preferred_element_type=F32)
    for hd in range(N_Q_HEADS):
        g_idx, h_idx = divmod(hd, HQ_G)
        sl = slice(h_idx * HEAD_DIM, (h_idx + 1) * HEAD_DIM)
        qh = _head_norm_rope(q_all[:, hd * HEAD_DIM:(hd + 1) * HEAD_DIM], rope_a, rope_b)
        if split and g_idx in SPLIT_GROUPS:
            scr = refs[N_GROUPS]
            n = SPLIT_GROUPS.index(g_idx) * HQ_G + h_idx
            scr[n] = qh
            for c in range(SPLIT):
                q_refs[g_idx][c * Q_BLOCK:(c + 1) * Q_BLOCK, sl] = (
                    scr[n, pl.ds(c, Q_BLOCK, stride=SPLIT), :].astype(BF16))
        else:
            q_refs[g_idx][:, sl] = qh.astype(BF16)


def _q_proj(x, mod, tiles_per_seq, g, w_q, index, g_q, cos, sin, tm, split):
    rows, d = x.shape
    mr = mod.shape[1]
    assert not split or tm == SPLIT_TILE
    row_spec = pl.BlockSpec((tm, d), lambda i: (i, 0))
    tab_spec = pl.BlockSpec((tm, HEAD_DIM), lambda i: (i % tiles_per_seq, 0))
    return pl.pallas_call(
        functools.partial(_q_kernel, split=split),
        grid=(rows // tm,),
        in_specs=[row_spec,
                  pl.BlockSpec((1, mr, 6 * d), lambda i: (i // tiles_per_seq, 0, 0)),
                  pl.BlockSpec((1, d), lambda i: (0, 0)),
                  _stacked(w_q, index),
                  pl.BlockSpec((2, HEAD_DIM), lambda i: (0, 0)),
                  tab_spec, tab_spec],
        out_specs=[pl.BlockSpec((tm, O_WIDTH), lambda i: (i, 0))] * N_GROUPS,
        out_shape=[jax.ShapeDtypeStruct((rows, O_WIDTH), BF16)] * N_GROUPS,
        scratch_shapes=[pltpu.VMEM((len(SPLIT_GROUPS) * HQ_G, tm, HEAD_DIM), F32)] if split else [],
        compiler_params=_cparams("parallel"),
        name="q_proj",
    )(x, mod, g.reshape(1, d), w_q, jnp.stack([g_q, _swap_rot_halves(g_q)]), cos, sin)


def _attn_prompt_kernel(*refs, step, has_prev, n_blk):
    if has_prev:
        q_ref, k_ref, v_ref, kp_ref, vp_ref, o_ref, st_ref = refs
        not_first = pl.program_id(1) > 0
    else:
        q_ref, k_ref, v_ref, o_ref, st_ref = refs
    nq = REP * Q_BLOCK
    lane = lax.broadcasted_iota(jnp.int32, (Q_BLOCK, HEAD_DIM), 1)
    max_kb = 2 if step == 1 else n_blk
    col = lax.broadcasted_iota(jnp.int32, (nq, max_kb * Q_BLOCK), 1)
    dist = (max_kb - 1) * Q_BLOCK + (lax.broadcasted_iota(jnp.int32, col.shape, 0) & (Q_BLOCK - 1)) - col
    reach = (dist >= 0) & (dist <= step * Q_BLOCK) & ((dist & (step - 1)) == 0)
    for i in range(n_blk):
        first_kb = i - 1 if step == 1 else 0
        if first_kb < 0 and not has_prev:
            first_kb = 0
        kbs = list(range(first_kb, i + 1))
        valid = reach[:, (max_kb - len(kbs)) * Q_BLOCK:]
        if first_kb < 0:
            valid = valid & (not_first | (col >= Q_BLOCK))
        for k in range(HKV_G):
            ksl = slice(k * HEAD_DIM, (k + 1) * HEAD_DIM)
            q = jnp.concatenate(
                [q_ref[i, :, (k * REP + r) * HEAD_DIM:(k * REP + r + 1) * HEAD_DIM] for r in range(REP)], axis=0)
            if has_prev:
                hrows = pl.ds(k, Q_BLOCK, stride=HKV_G)
                keys = jnp.concatenate([kp_ref[hrows, :] if j < 0 else k_ref[j, hrows, :] for j in kbs], axis=0)
                vals = jnp.concatenate([vp_ref[hrows, :] if j < 0 else v_ref[j, hrows, :] for j in kbs], axis=0)
            else:
                keys = jnp.concatenate([k_ref[j, :, ksl] for j in kbs], axis=0)
                vals = jnp.concatenate([v_ref[j, :, ksl] for j in kbs], axis=0)
            s = lax.dot_general(q, keys.astype(BF16), (((1,), (1,)), ((), ())), preferred_element_type=F32)
            s = jnp.where(valid, s, NEG)
            m = jnp.max(s, axis=1, keepdims=True)
            p = jnp.exp(s - m)
            den = jnp.sum(p, axis=1, keepdims=True)
            out = jnp.dot(p.astype(BF16), vals.astype(BF16), preferred_element_type=F32) / den
            lse = m + jnp.log(den)
            stats = jnp.zeros((Q_BLOCK, HEAD_DIM), F32)
            for r in range(REP):
                rs = slice(r * Q_BLOCK, (r + 1) * Q_BLOCK)
                o_ref[i, :, (k * REP + r) * HEAD_DIM:(k * REP + r + 1) * HEAD_DIM] = out[rs].astype(BF16)
                stats = jnp.where(lane == r, lse[rs], stats)
            st_ref[i, :, ksl] = stats


def _attn_prompt_group(q, k_g, v_g, group, batch, seq):
    _, dil = DIL_GROUPS[group]
    rows = batch * seq
    n_tiles = seq // SPLIT_TILE
    if group in SPLIT_GROUPS:
        step = dil // SPLIT
        lead = (batch, n_tiles, SPLIT, Q_BLOCK)
        view = lambda a: a.reshape(*lead, a.shape[-1])
        spec = lambda w: pl.BlockSpec((None, n_tiles, None, Q_BLOCK, w), lambda b, c: (b, 0, c, 0, 0))
        out_shape = [jax.ShapeDtypeStruct((*lead, O_WIDTH), BF16),
                     jax.ShapeDtypeStruct((*lead, GROUP_KV_WIDTH), F32)]
        out, stats = pl.pallas_call(
            functools.partial(_attn_prompt_kernel, step=step, has_prev=False, n_blk=n_tiles),
            grid=(batch, SPLIT),
            in_specs=[spec(O_WIDTH), spec(GROUP_KV_WIDTH), spec(GROUP_KV_WIDTH)],
            out_specs=[spec(O_WIDTH), spec(GROUP_KV_WIDTH)],
            out_shape=out_shape,
            compiler_params=_cparams("parallel", "parallel"),
            name=f"attn_prompt_g{group}",
        )(view(q), view(k_g), view(v_g))
    else:
        lead = (batch * n_tiles, SPLIT)
        view = lambda a, r: a.reshape(*lead, r, a.shape[-1])
        spec = lambda r, w: pl.BlockSpec((None, SPLIT, r, w), lambda b, i: (b * n_tiles + i, 0, 0, 0))
        kv_rows = HKV_G * Q_BLOCK
        kv_spec = spec(kv_rows, HEAD_DIM)
        prev = pl.BlockSpec((None, None, kv_rows, HEAD_DIM),
                            lambda b, i: (jnp.maximum(b * n_tiles + i - 1, 0), SPLIT - 1, 0, 0))
        out_shape = [jax.ShapeDtypeStruct((*lead, Q_BLOCK, O_WIDTH), BF16),
                     jax.ShapeDtypeStruct((*lead, Q_BLOCK, GROUP_KV_WIDTH), F32)]
        k4, v4 = view(k_g, kv_rows), view(v_g, kv_rows)
        out, stats = pl.pallas_call(
            functools.partial(_attn_prompt_kernel, step=dil, has_prev=True, n_blk=SPLIT),
            grid=(batch, n_tiles),
            in_specs=[spec(Q_BLOCK, O_WIDTH), kv_spec, kv_spec, prev, prev],
            out_specs=[spec(Q_BLOCK, O_WIDTH), spec(Q_BLOCK, GROUP_KV_WIDTH)],
            out_shape=out_shape,
            compiler_params=_cparams("parallel", "arbitrary"),
            name=f"attn_prompt_g{group}",
        )(view(q, Q_BLOCK), k4, v4, k4, v4)
    return out.reshape(rows, O_WIDTH), stats.reshape(rows, GROUP_KV_WIDTH)


def _merge_o_kernel(x_ref, mod_ref, o0_ref, o1_ref, o2_ref, s0_ref, s1_ref, s2_ref, wo_ref, xo_ref,
                    on_ref, sn_ref):
    for n, (o_ref, s_ref) in enumerate(((o1_ref, s1_ref), (o2_ref, s2_ref))):
        for c in range(SPLIT):
            rows = slice(c * Q_BLOCK, (c + 1) * Q_BLOCK)
            dst = pl.ds(c, Q_BLOCK, stride=SPLIT)
            for hd in range(HQ_G):
                on_ref[n, hd, dst, :] = o_ref[rows, hd * HEAD_DIM:(hd + 1) * HEAD_DIM].astype(F32)
            for k in range(HKV_G):
                sn_ref[n, k, dst, :] = s_ref[rows, k * HEAD_DIM:(k + 1) * HEAD_DIM]
    heads = []
    for k in range(HKV_G):
        for r in range(REP):
            hd = k * REP + r
            lses = [s0_ref[:, k * HEAD_DIM + r:k * HEAD_DIM + r + 1],
                    sn_ref[0, k, :, r:r + 1], sn_ref[1, k, :, r:r + 1]]
            m = jnp.maximum(jnp.maximum(lses[0], lses[1]), lses[2])
            es = [jnp.exp(l - m) for l in lses]
            den = es[0] + es[1] + es[2]
            sl = slice(hd * HEAD_DIM, (hd + 1) * HEAD_DIM)
            outs = [o0_ref[:, sl].astype(F32), on_ref[0, hd], on_ref[1, hd]]
            o = sum((e / den) * o_g for e, o_g in zip(es, outs))
            heads.append(o.astype(BF16))
    o = jnp.concatenate(heads, axis=1)
    y = jnp.dot(o, wo_ref[...], preferred_element_type=F32)
    xo_ref[...] = x_ref[...] + _mod_part(mod_ref, 2) * y


def _merge_o(x, mod, tiles_per_seq, outs, stats, w_o, index, tm):
    rows, d = x.shape
    mr = mod.shape[1]
    assert tm == SPLIT_TILE
    row_spec = pl.BlockSpec((tm, d), lambda i: (i, 0))
    o_spec = pl.BlockSpec((tm, O_WIDTH), lambda i: (i, 0))
    st_spec = pl.BlockSpec((tm, GROUP_KV_WIDTH), lambda i: (i, 0))
    return pl.pallas_call(
        _merge_o_kernel,
        grid=(rows // tm,),
        in_specs=[row_spec,
                  pl.BlockSpec((1, mr, 6 * d), lambda i: (i // tiles_per_seq, 0, 0))]
                 + [o_spec] * 3 + [st_spec] * 3 + [_stacked(w_o, index)],
        out_specs=row_spec,
        out_shape=jax.ShapeDtypeStruct((rows, d), F32),
        scratch_shapes=[pltpu.VMEM((len(SPLIT_GROUPS), HQ_G, tm, HEAD_DIM), F32),
                        pltpu.VMEM((len(SPLIT_GROUPS), HKV_G, tm, HEAD_DIM), F32)],
        compiler_params=_cparams("parallel"),
        name="merge_o_proj",
    )(x, mod, *outs, *stats, w_o)


def _attn_sample_kernel(*refs, n_new, write_buffers):
    q_ref = refs[0]
    nk_refs = refs[1:1 + N_GROUPS]
    nv_refs = refs[1 + N_GROUPS:1 + 2 * N_GROUPS]
    ck_refs = refs[1 + 2 * N_GROUPS:1 + 3 * N_GROUPS]
    cv_refs = refs[1 + 3 * N_GROUPS:1 + 4 * N_GROUPS]
    o_ref = refs[1 + 4 * N_GROUPS]
    nq = REP * n_new
    n_new_rows = HKV_G * n_new
    qi = lax.broadcasted_iota(jnp.int32, (nq, 1), 0) & (n_new - 1)

    if write_buffers:
        ok_refs = refs[2 + 4 * N_GROUPS:2 + 5 * N_GROUPS]
        ov_refs = refs[2 + 5 * N_GROUPS:2 + 6 * N_GROUPS]
        for g in range(N_GROUPS):
            for c_ref, n_ref, o_buf in ((ck_refs[g], nk_refs[g], ok_refs[g]), (cv_refs[g], nv_refs[g], ov_refs[g])):
                n_rows = c_ref.shape[1]
                o_buf[0, 0:n_rows - n_new_rows, :] = c_ref[0, n_new_rows:n_rows, :]
                o_buf[0, n_rows - n_new_rows:n_rows, :] = n_ref[0]

    for k in range(HKV_G):
        outs, lses = [], []
        for g, (window, dil) in enumerate(DIL_GROUPS):
            length = ck_refs[g].shape[1] // HKV_G
            q = q_ref[0, g, k]
            qf = q.astype(F32)
            keys = ck_refs[g][0, pl.ds(k, length, stride=HKV_G), :].astype(BF16)
            vals = cv_refs[g][0, pl.ds(k, length, stride=HKV_G), :].astype(BF16)
            s = lax.dot_general(q, keys, (((1,), (1,)), ((), ())), preferred_element_type=F32)
            idx = lax.broadcasted_iota(jnp.int32, (nq, length), 1)
            diff = length + qi - idx
            s = jnp.where(((diff & (dil - 1)) == 0) & (diff <= window), s, NEG)
            new_k = nk_refs[g][0]
            new_v = nv_refs[g][0]
            s_new = []
            for j in range(n_new):
                sj = jnp.sum(qf * new_k[HKV_G * j + k:HKV_G * j + k + 1, :], axis=1, keepdims=True)
                ok = (qi >= j) & (((qi - j) & (dil - 1)) == 0)
                s_new.append(jnp.where(ok, sj, NEG))
            m = jnp.max(s, axis=1, keepdims=True)
            for sj in s_new:
                m = jnp.maximum(m, sj)
            p = jnp.exp(s - m)
            den = jnp.sum(p, axis=1, keepdims=True)
            acc = jnp.dot(p.astype(BF16), vals, preferred_element_type=F32)
            for j, sj in enumerate(s_new):
                pj = jnp.exp(sj - m)
                den = den + pj
                acc = acc + pj * new_v[HKV_G * j + k:HKV_G * j + k + 1, :]
            outs.append(acc / den)
            lses.append(m + jnp.log(den))
        m = jnp.maximum(jnp.maximum(lses[0], lses[1]), lses[2])
        es = [jnp.exp(l - m) for l in lses]
        den = es[0] + es[1] + es[2]
        o_ref[0, k] = sum((e / den) * o for e, o in zip(es, outs))


def _attn_sample(q, new_k, new_v, cache_k, cache_v, n_seq, n_new, write_buffers):
    assert HKV_G * n_new == 8
    q5 = q.reshape(n_seq, n_new, N_GROUPS, HKV_G, REP, HEAD_DIM)
    q5 = jnp.transpose(q5, (0, 2, 3, 4, 1, 5)).reshape(n_seq, N_GROUPS, HKV_G, REP * n_new, HEAD_DIM)
    rows2 = lambda a, n: a.reshape(n_seq, n * HKV_G, HEAD_DIM)
    nk = [rows2(a, n_new) for a in new_k]
    nv = [rows2(a, n_new) for a in new_v]
    ck = [rows2(c, c.shape[1]) for c in cache_k]
    cv = [rows2(c, c.shape[1]) for c in cache_v]
    new_spec = pl.BlockSpec((1, HKV_G * n_new, HEAD_DIM), lambda b: (b, 0, 0))
    cache_specs = [pl.BlockSpec((1, c.shape[1], HEAD_DIM), lambda b: (b, 0, 0)) for c in ck]
    cache_shapes = [jax.ShapeDtypeStruct(c.shape, F32) for c in ck]
    n_buf = 2 if write_buffers else 0
    res = pl.pallas_call(
        functools.partial(_attn_sample_kernel, n_new=n_new, write_buffers=write_buffers),
        grid=(n_seq,),
        in_specs=[pl.BlockSpec((1, N_GROUPS, HKV_G, REP * n_new, HEAD_DIM), lambda b: (b, 0, 0, 0, 0))]
                 + [new_spec] * (2 * N_GROUPS) + cache_specs + cache_specs,
        out_specs=[pl.BlockSpec((1, HKV_G, REP * n_new, HEAD_DIM), lambda b: (b, 0, 0, 0))]
                  + cache_specs * n_buf,
        out_shape=[jax.ShapeDtypeStruct((n_seq, HKV_G, REP * n_new, HEAD_DIM), F32)] + cache_shapes * n_buf,
        compiler_params=_cparams("parallel"),
        name="attn_sample",
    )(q5, *nk, *nv, *ck, *cv)
    o = res[0].reshape(n_seq, HKV_G, REP, n_new, HEAD_DIM)
    o = jnp.transpose(o, (0, 3, 1, 2, 4)).reshape(n_seq * n_new, O_WIDTH)
    if not write_buffers:
        return o, None, None
    shape4 = lambda a: a.reshape(n_seq, a.shape[1] // HKV_G, HKV_G, HEAD_DIM)
    new_ck = [shape4(a) for a in res[1:1 + N_GROUPS]]
    new_cv = [shape4(a) for a in res[1 + N_GROUPS:1 + 2 * N_GROUPS]]
    return o, new_ck, new_cv


def _o_proj_kernel(x_ref, mod_ref, o_ref, wo_ref, xo_ref):
    y = jnp.dot(o_ref[...].astype(BF16), wo_ref[...], preferred_element_type=F32)
    xo_ref[...] = x_ref[...] + _mod_part(mod_ref, 2) * y


def _o_proj(x, mod, o, w_o, index):
    rows, d = x.shape
    mr = mod.shape[1]
    full = pl.BlockSpec((rows, d), lambda i: (0, 0))
    return pl.pallas_call(
        _o_proj_kernel,
        grid=(1,),
        in_specs=[full, pl.BlockSpec((1, mr, 6 * d), lambda i: (0, 0, 0)),
                  pl.BlockSpec((rows, O_WIDTH), lambda i: (0, 0)),
                  _stacked(w_o, index)],
        out_specs=full,
        out_shape=jax.ShapeDtypeStruct((rows, d), F32),
        compiler_params=_cparams("arbitrary"),
        name="o_proj_sample",
    )(x, mod, o, w_o)


def _trunk(x, mods, mod_kv, pos, weights, tm, conv_prev, kv_prev):
    (g_norm_mix, g_norm_ffn, w_in, w_conv, w_out_conv, g_norm_kv, w_kv, g_k, w_q, g_q, w_o,
     w_gu_dense, w_down_dense) = weights
    batch, seq, d = x.shape
    rows = batch * seq
    depth = g_norm_mix.shape[0]
    n_conv = w_in.shape[0]
    sample = kv_prev is not None
    tiles_per_seq = 1 if sample else seq // tm
    x = x.reshape(rows, d)
    cos, sin = _rope_tables(pos)
    if sample:
        cos, sin = jnp.tile(cos, (batch, 1)), jnp.tile(sin, (batch, 1))
    conv_state = []
    kv_state = None
    for layer in range(depth):
        mod = mods[layer]
        if layer == n_conv:
            kvs = _kv_proj(x, mod_kv, tiles_per_seq, g_norm_kv, w_kv, g_k, cos, sin, tm, not sample)
            k_new, v_new = kvs[:N_GROUPS], kvs[N_GROUPS:2 * N_GROUPS]
            k_att, v_att = list(k_new), list(v_new)
            for n, g_idx in enumerate(SPLIT_GROUPS if not sample else ()):
                k_att[g_idx], v_att[g_idx] = kvs[2 * N_GROUPS + 2 * n], kvs[2 * N_GROUPS + 2 * n + 1]
        if layer < n_conv:
            if sample:
                st = conv_prev[layer]
                zero = jnp.zeros((batch, seq - 2, d), F32)
                pa = jnp.concatenate([st[:, 1:2], jnp.zeros((batch, seq - 1, d), F32)], axis=1)
                pb = jnp.concatenate([st, zero], axis=1)
                prev = (seq, pa.reshape(rows, d), pb.reshape(rows, d))
            else:
                prev = None
            x, u_tail = _conv_layer(x, mod, tiles_per_seq, g_norm_mix[layer], w_in, w_conv, w_out_conv, layer,
                                    tm, prev)
            if sample:
                conv_state.append(u_tail.reshape(batch, seq, d)[:, seq - (CONV_W - 1):])
            else:
                conv_state.append(u_tail[:, 8 - (CONV_W - 1):])
        else:
            lb = layer - n_conv
            qs = _q_proj(x, mod, tiles_per_seq, g_norm_mix[layer], w_q, lb, g_q[lb], cos, sin, tm, not sample)
            if sample:
                o, new_ck, new_cv = _attn_sample(jnp.concatenate(qs, axis=1), k_new, v_new, kv_prev[0],
                                                 kv_prev[1], batch, seq, kv_state is None)
                if kv_state is None:
                    kv_state = (new_ck, new_cv)
                x = _o_proj(x, mod, o, w_o, lb)
            else:
                res = [_attn_prompt_group(qs[g], k_att[g], v_att[g], g, batch, seq) for g in range(N_GROUPS)]
                x = _merge_o(x, mod, tiles_per_seq, [r[0] for r in res], [r[1] for r in res], w_o, lb, tm)
        if layer % 2 == 0:
            x = _ffn_layer(x, mod, tiles_per_seq, g_norm_ffn[layer], w_gu_dense, w_down_dense, layer // 2, tm)
        else:
            tm_moe = tm if sample else min(SPARSE_TILE, seq)
            x = yield (x, mod, 1 if sample else seq // tm_moe, tm_moe), layer
    if not sample:
        shape4 = lambda a: a.reshape(batch, seq, HKV_G, HEAD_DIM)
        kv_state = ([shape4(k)[:, -min(w, seq):] for k, (w, _) in zip(k_new, DIL_GROUPS)],
                    [shape4(v)[:, -min(w, seq):] for v, (w, _) in zip(v_new, DIL_GROUPS)])
    return x.reshape(batch, seq, d), jnp.stack(conv_state, axis=0), kv_state


def _resume(trunk, x):
    try:
        return False, trunk.send(x)
    except StopIteration as done:
        return True, done.value


def kernel(x_prompt, x_sample, state_conv, cache_k_g0, cache_v_g0, cache_k_g1, cache_v_g1, cache_k_g2,
           cache_v_g2, c_prompt, c_sample, g_norm_mix, g_norm_ffn, w_ada, b_ada, w_in, w_conv, w_out_conv,
           g_norm_kv, w_ada_kv, b_ada_kv, w_kv, g_k, w_q, g_q, w_o, w_gu_dense, w_down_dense, w_router,
           b_router, w_gu_moe, w_down_moe):
    b_p, t_p, d = x_prompt.shape
    b_s, t_s, _ = x_sample.shape
    bf = lambda w: w.astype(BF16)
    weights = (g_norm_mix, g_norm_ffn, bf(w_in), w_conv, bf(w_out_conv), g_norm_kv, bf(w_kv), g_k, bf(w_q),
               g_q, bf(w_o), bf(w_gu_dense), bf(w_down_dense))
    w_gu_moe, w_down_moe = bf(w_gu_moe), bf(w_down_moe)

    c_all = jnp.concatenate([c_prompt, c_sample], axis=0)
    mods_all = _ada(c_all, w_ada, b_ada)
    mod_kv_all = _ada(c_all, w_ada_kv[None], b_ada_kv[None])[0]
    depth = w_ada.shape[0]
    mods_p = [mods_all[l, :b_p, None, :] for l in range(depth)]
    mods_s = [jnp.repeat(mods_all[l, b_p:], t_s, axis=0)[None] for l in range(depth)]
    mod_kv_p = mod_kv_all[:b_p, None, :]
    mod_kv_s = jnp.repeat(mod_kv_all[b_p:], t_s, axis=0)[None]

    pos_p = jnp.arange(t_p, dtype=jnp.int32)
    pos_s = PAST_LEN + jnp.arange(t_s, dtype=jnp.int32)
    kv_prev = ([cache_k_g0, cache_k_g1, cache_k_g2], [cache_v_g0, cache_v_g1, cache_v_g2])
    trunks = [_trunk(x_prompt, mods_p, mod_kv_p, pos_p, weights, min(PROMPT_TILE, t_p), None, None),
              _trunk(x_sample, mods_s, mod_kv_s, pos_s, weights, b_s * t_s, state_conv, kv_prev)]
    requests = [next(t) for t in trunks]
    results = None
    while results is None:
        layer = requests[0][1]
        xs = _moe_sparse_layer([r[0] for r in requests], g_norm_ffn[layer], w_router[layer // 2],
                               b_router[layer // 2], w_gu_moe, w_down_moe, layer // 2)
        steps = [_resume(t, x) for t, x in zip(trunks, xs)]
        if all(done for done, _ in steps):
            results = [value for _, value in steps]
        else:
            requests = [value for _, value in steps]
    (y_prompt, conv_p, (kp, vp)), (y_sample, conv_s, (ksn, vsn)) = results
    return (y_prompt, y_sample, conv_p, conv_s,
            kp[0], vp[0], kp[1], vp[1], kp[2], vp[2],
            ksn[0], vsn[0], ksn[1], vsn[1], ksn[2], vsn[2])
```

```python
import functools

import jax
import jax.numpy as jnp
from jax import lax
from jax.experimental import pallas as pl
from jax.experimental.pallas import tpu as pltpu

D_MODEL = 1024
PAST_LEN = 8192
CONV_W = 3
HEAD_DIM = 128
ROT_DIM = HEAD_DIM // 4
ROPE_THETA = 500000.0
DIL_GROUPS = ((128, 1), (512, 4), (2048, 16))
N_GROUPS = len(DIL_GROUPS)
HQ_G = 8
HKV_G = 2
REP = HQ_G // HKV_G
N_Q_HEADS = N_GROUPS * HQ_G
N_KV_HEADS = N_GROUPS * HKV_G
Q_WIDTH = N_Q_HEADS * HEAD_DIM
KV_WIDTH = N_KV_HEADS * HEAD_DIM
O_WIDTH = HQ_G * HEAD_DIM
GROUP_KV_WIDTH = HKV_G * HEAD_DIM
N_EXPERTS = 8
EPS = 1e-6
ATTN_SCALE = HEAD_DIM ** -0.5
NEG = -1e30
Q_BLOCK = 128

F32 = jnp.float32
BF16 = jnp.bfloat16
HIGHEST = lax.Precision.HIGHEST

V7X_VMEM_LIMIT_BYTES = 56 * 1024 * 1024
V7X_MXU_WIDTH = 256
PROMPT_TILE = 512
MOE_FF_CHUNK = 512


def _cparams(*sem):
    return pltpu.CompilerParams(dimension_semantics=sem, vmem_limit_bytes=V7X_VMEM_LIMIT_BYTES)


def _resident(shape, index_map):
    return pl.BlockSpec(shape, index_map, pipeline_mode=pl.Buffered(1))


def _stacked(w, index):
    shape = w.shape[1:]
    return pl.BlockSpec((None,) + shape, lambda *_: (index,) + (0,) * len(shape), pipeline_mode=pl.Buffered(1))


def _silu(x):
    return x * jax.nn.sigmoid(x)


def _norm_mod(x, g, shift, scale):
    y = x * lax.rsqrt(jnp.mean(x * x, axis=-1, keepdims=True) + EPS) * g
    return y * (1.0 + scale) + shift


def _mod_part(mod_ref, k):
    return mod_ref[0, :, k * D_MODEL:(k + 1) * D_MODEL]


def _ada_kernel(c_ref, w_ref, b_ref, o_ref):
    c = c_ref[...]
    o_ref[0] = jnp.dot(_silu(c), w_ref[0], preferred_element_type=F32, precision=HIGHEST) + b_ref[0]


def _ada(c, w, b):
    n_layers, d, n = w.shape
    m = c.shape[0]
    tn = 2048
    return pl.pallas_call(
        _ada_kernel,
        grid=(n_layers, n // tn),
        in_specs=[pl.BlockSpec((m, d), lambda l, j: (0, 0)),
                  pl.BlockSpec((1, d, tn), lambda l, j: (l, 0, j)),
                  pl.BlockSpec((1, 1, tn), lambda l, j: (l, 0, j))],
        out_specs=pl.BlockSpec((1, m, tn), lambda l, j: (l, 0, j)),
        out_shape=jax.ShapeDtypeStruct((n_layers, m, n), F32),
        compiler_params=_cparams("parallel", "parallel"),
        name="ada",
    )(c, w, b.reshape(n_layers, 1, n))


def _conv_kernel(*refs, tm, tiles_per_seq, seq_rows):
    if seq_rows is None:
        x_ref, mod_ref, g_ref, win_ref, wc_ref, wout_ref, xo_ref, ust_ref, carry_ref = refs
    else:
        x_ref, mod_ref, g_ref, win_ref, wc_ref, wout_ref, pa_ref, pb_ref, xo_ref, ust_ref = refs
    x = x_ref[...]
    h = _norm_mod(x, g_ref[...], _mod_part(mod_ref, 0), _mod_part(mod_ref, 1)).astype(BF16)
    b_gate = jnp.dot(h, win_ref[:, 0:D_MODEL], preferred_element_type=F32)
    c_gate = jnp.dot(h, win_ref[:, D_MODEL:2 * D_MODEL], preferred_element_type=F32)
    v = jnp.dot(h, win_ref[:, 2 * D_MODEL:3 * D_MODEL], preferred_element_type=F32)
    u = c_gate * v
    row = lax.broadcasted_iota(jnp.int32, (tm, 1), 0)
    r1 = pltpu.roll(u, 1, 0)
    r2 = pltpu.roll(u, 2, 0)
    if seq_rows is None:
        @pl.when(pl.program_id(0) % tiles_per_seq == 0)
        def _():
            carry_ref[...] = jnp.zeros_like(carry_ref)
        um1 = jnp.where(row == 0, carry_ref[7:8, :], r1)
        um2 = jnp.where(row == 0, carry_ref[6:7, :], jnp.where(row == 1, carry_ref[7:8, :], r2))
        carry_ref[...] = u[tm - 8:tm, :]
        ust_ref[0] = u[tm - 8:tm, :]
    else:
        rowm = row & (seq_rows - 1)
        um1 = jnp.where(rowm == 0, pa_ref[...], r1)
        um2 = jnp.where(rowm < 2, pb_ref[...], r2)
        ust_ref[...] = u
    y = um2 * wc_ref[0:1, :] + um1 * wc_ref[1:2, :] + u * wc_ref[2:3, :]
    z = (b_gate * y).astype(BF16)
    out = jnp.dot(z, wout_ref[...], preferred_element_type=F32)
    xo_ref[...] = x + _mod_part(mod_ref, 2) * out


def _conv_layer(x, mod, tiles_per_seq, g, w_in, w_conv, w_out, layer, tm, prev=None):
    rows, d = x.shape
    n_tiles = rows // tm
    mr = mod.shape[1]
    row_spec = pl.BlockSpec((tm, d), lambda i: (i, 0))
    in_specs = [row_spec,
                pl.BlockSpec((1, mr, 6 * d), lambda i: (i // tiles_per_seq, 0, 0)),
                pl.BlockSpec((1, d), lambda i: (0, 0)),
                _stacked(w_in, layer), _stacked(w_conv, layer), _stacked(w_out, layer)]
    args = [x, mod, g.reshape(1, d), w_in, w_conv, w_out]
    if prev is None:
        seq_rows = None
        n_seq = n_tiles // tiles_per_seq
        ust_shape = jax.ShapeDtypeStruct((n_seq, 8, d), F32)
        ust_spec = pl.BlockSpec((1, 8, d), lambda i: (i // tiles_per_seq, 0, 0))
        scratch = [pltpu.VMEM((8, d), F32)]
    else:
        seq_rows, pa, pb = prev
        in_specs += [row_spec, row_spec]
        args += [pa, pb]
        ust_shape = jax.ShapeDtypeStruct((rows, d), F32)
        ust_spec = row_spec
        scratch = []
    return pl.pallas_call(
        functools.partial(_conv_kernel, tm=tm, tiles_per_seq=tiles_per_seq, seq_rows=seq_rows),
        grid=(n_tiles,),
        in_specs=in_specs,
        out_specs=[row_spec, ust_spec],
        out_shape=[jax.ShapeDtypeStruct((rows, d), F32), ust_shape],
        scratch_shapes=scratch,
        compiler_params=_cparams("arbitrary"),
        name="conv_layer",
    )(*args)


def _ffn_kernel(x_ref, mod_ref, g_ref, wgu_ref, wd_ref, xo_ref, *, d_ff, n_chunks):
    x = x_ref[...]
    h = _norm_mod(x, g_ref[...], _mod_part(mod_ref, 3), _mod_part(mod_ref, 4)).astype(BF16)
    n_mxu = d_ff // V7X_MXU_WIDTH
    edges = [(c * n_mxu // n_chunks) * V7X_MXU_WIDTH for c in range(n_chunks)] + [d_ff]
    acc = None
    for lo, hi in zip(edges[:-1], edges[1:]):
        gate = jnp.dot(h, wgu_ref[:, lo:hi], preferred_element_type=F32)
        up = jnp.dot(h, wgu_ref[:, d_ff + lo:d_ff + hi], preferred_element_type=F32)
        a = (_silu(gate) * up).astype(BF16)
        part = jnp.dot(a, wd_ref[lo:hi, :], preferred_element_type=F32)
        acc = part if acc is None else acc + part
    xo_ref[...] = x + _mod_part(mod_ref, 5) * acc


def _ffn_layer(x, mod, tiles_per_seq, g, w_gu, w_down, index, tm):
    rows, d = x.shape
    d_ff = w_down.shape[1]
    mr = mod.shape[1]
    row_spec = pl.BlockSpec((tm, d), lambda i: (i, 0))
    return pl.pallas_call(
        functools.partial(_ffn_kernel, d_ff=d_ff, n_chunks=2),
        grid=(rows // tm,),
        in_specs=[row_spec,
                  pl.BlockSpec((1, mr, 6 * d), lambda i: (i // tiles_per_seq, 0, 0)),
                  pl.BlockSpec((1, d), lambda i: (0, 0)),
                  _stacked(w_gu, index), _stacked(w_down, index)],
        out_specs=row_spec,
        out_shape=jax.ShapeDtypeStruct((rows, d), F32),
        compiler_params=_cparams("parallel"),
        name="ffn_dense",
    )(x, mod, g.reshape(1, d), w_gu, w_down)


SEG_ALIGN = 16
SEG_SIZES = (512, 256, 128, 64, 32, 16)
SPARSE_TILE = 512
EXPERT_ROW_TILE = 256


def _route_kernel(x_ref, mod_ref, g_ref, wrt_ref, br_ref, h_ref, slots_ref, gw_ref, seg_ref, *, tm):
    x = x_ref[...]
    h = _norm_mod(x, g_ref[...], _mod_part(mod_ref, 3), _mod_part(mod_ref, 4))
    h_ref[...] = h.astype(BF16)
    logits = lax.dot_general(wrt_ref[...], h, (((1,), (1,)), ((), ())), preferred_element_type=F32,
                             precision=HIGHEST) + br_ref[...]
    eidx = lax.broadcasted_iota(jnp.int32, logits.shape, 0).astype(F32)
    m1 = jnp.max(logits, axis=0, keepdims=True)
    i1 = jnp.min(jnp.where(logits == m1, eidx, float(N_EXPERTS)), axis=0, keepdims=True)
    sel1 = eidx == i1
    rest = jnp.where(sel1, -jnp.inf, logits)
    m2 = jnp.max(rest, axis=0, keepdims=True)
    i2 = jnp.min(jnp.where(rest == m2, eidx, float(N_EXPERTS)), axis=0, keepdims=True)
    sel2 = eidx == i2
    e2 = jnp.exp(m2 - m1)
    den = 1.0 + e2
    gw_ref[0] = jnp.concatenate([1.0 / den, e2 / den], axis=0)
    mask = jnp.where(sel1 | sel2, 1.0, 0.0)
    before = (lax.broadcasted_iota(jnp.int32, (tm, tm), 0) < lax.broadcasted_iota(jnp.int32, (tm, tm), 1))
    rank = jnp.dot(mask.astype(BF16), jnp.where(before, 1.0, 0.0).astype(BF16), preferred_element_type=F32)
    count = jnp.sum(mask, axis=1, keepdims=True)
    seg = jnp.floor((count + (SEG_ALIGN - 1)) * (1.0 / SEG_ALIGN)) * SEG_ALIGN
    start = jnp.zeros_like(seg)
    for e in range(N_EXPERTS - 1):
        start = start + jnp.where(eidx[:, 0:1] > e, seg[e:e + 1, :], 0.0)
    slot = start + rank
    slot1 = jnp.sum(jnp.where(sel1, slot, 0.0), axis=0, keepdims=True)
    slot2 = jnp.sum(jnp.where(sel2, slot, 0.0), axis=0, keepdims=True)
    slots_ref[0] = jnp.concatenate([slot1, slot2], axis=0).astype(jnp.int32)
    seg_ref[0] = jnp.broadcast_to(seg, (N_EXPERTS, HEAD_DIM)).astype(jnp.int32)


def _route(x, mod, tiles_per_seq, g, w_router, b_router, tm):
    rows, d = x.shape
    mr = mod.shape[1]
    n_tiles = rows // tm
    row_spec = pl.BlockSpec((tm, d), lambda i: (i, 0))
    pair_spec = pl.BlockSpec((1, 2, tm), lambda i: (i, 0, 0))
    return pl.pallas_call(
        functools.partial(_route_kernel, tm=tm),
        grid=(n_tiles,),
        in_specs=[row_spec,
                  pl.BlockSpec((1, mr, 6 * d), lambda i: (i // tiles_per_seq, 0, 0)),
                  pl.BlockSpec((1, d), lambda i: (0, 0)),
                  pl.BlockSpec((N_EXPERTS, d), lambda i: (0, 0)),
                  pl.BlockSpec((N_EXPERTS, 1), lambda i: (0, 0))],
        out_specs=[row_spec, pair_spec, pair_spec,
                   pl.BlockSpec((1, N_EXPERTS, HEAD_DIM), lambda i: (i, 0, 0))],
        out_shape=[jax.ShapeDtypeStruct((rows, d), BF16),
                   jax.ShapeDtypeStruct((n_tiles, 2, tm), jnp.int32),
                   jax.ShapeDtypeStruct((n_tiles, 2, tm), F32),
                   jax.ShapeDtypeStruct((n_tiles, N_EXPERTS, HEAD_DIM), jnp.int32)],
        compiler_params=_cparams("parallel"),
        name="moe_route",
    )(x, mod, g.reshape(1, d), w_router.T, b_router.reshape(N_EXPERTS, 1))


def _segment_copies(tile, seg_ref, loc_ref, goff_ref, packed, sorted_hbm, sem, to_sorted):
    copies = []
    tokens = (packed.shape[0] - N_EXPERTS * SEG_ALIGN) // 2
    max_seg = -(-tokens // SEG_ALIGN) * SEG_ALIGN
    for e in range(N_EXPERTS):
        seg = seg_ref[tile * N_EXPERTS + e]
        loc = loc_ref[tile * N_EXPERTS + e]
        goff = goff_ref[tile * N_EXPERTS + e]
        done = jnp.int32(0)
        for size in (s for s in SEG_SIZES if s <= max_seg):
            vm = packed.at[pl.ds(pl.multiple_of(loc + done, SEG_ALIGN), size), :]
            hb = sorted_hbm.at[pl.ds(pl.multiple_of(goff + done, SEG_ALIGN), size), :]
            src, dst = (vm, hb) if to_sorted else (hb, vm)
            copies.append(((seg & size) != 0, pltpu.make_async_copy(src, dst, sem)))
            done = done + (seg & size)
    return copies


def _start_all(copies):
    for cond, cp in copies:
        pl.when(cond)(cp.start)


def _wait_all(copies):
    for cond, cp in copies:
        pl.when(cond)(cp.wait)


def _dispatch_kernel(seg_ref, loc_ref, goff_ref, h_ref, slots_ref, sorted_in, sorted_out, stage_ref, sems,
                     *, n_slots, n_tiles):
    del sorted_in
    i = pl.program_id(0)
    par = i % 2

    def copies(tile, parity):
        return _segment_copies(tile, seg_ref, loc_ref, goff_ref, stage_ref.at[parity], sorted_out,
                               sems.at[parity], True)

    @pl.when(i >= 2)
    def _():
        _wait_all(copies(i - 2, par))

    tm = h_ref.shape[0]
    srow = lax.broadcasted_iota(jnp.int32, (n_slots, tm), 0)
    onehot = (srow == slots_ref[0, 0:1, :]) | (srow == slots_ref[0, 1:2, :])
    packed = jnp.dot(jnp.where(onehot, 1.0, 0.0).astype(BF16), h_ref[...], preferred_element_type=F32)
    stage_ref[par] = packed.astype(BF16)
    _start_all(copies(i, par))

    @pl.when(i == n_tiles - 1)
    def _():
        if n_tiles >= 2:
            _wait_all(copies(i - 1, 1 - par))
        _wait_all(copies(i, par))


def _dispatch(h, slots, seg, loc, goff, h_sorted, tm):
    rows, d = h.shape
    n_tiles = rows // tm
    n_slots = 2 * tm + N_EXPERTS * SEG_ALIGN
    return pl.pallas_call(
        functools.partial(_dispatch_kernel, n_slots=n_slots, n_tiles=n_tiles),
        grid_spec=pltpu.PrefetchScalarGridSpec(
            num_scalar_prefetch=3,
            grid=(n_tiles,),
            in_specs=[pl.BlockSpec((tm, d), lambda i, *_: (i, 0)),
                      pl.BlockSpec((1, 2, tm), lambda i, *_: (i, 0, 0)),
                      pl.BlockSpec(memory_space=pl.ANY)],
            out_specs=pl.BlockSpec(memory_space=pl.ANY),
            scratch_shapes=[pltpu.VMEM((2, n_slots, d), BF16), pltpu.SemaphoreType.DMA((2,))]),
        out_shape=jax.ShapeDtypeStruct(h_sorted.shape, BF16),
        input_output_aliases={5: 0},
        compiler_params=_cparams("arbitrary"),
        name="moe_dispatch",
    )(seg, loc, goff, h, slots, h_sorted)


def _experts_kernel(te_ref, jb_ref, nact_ref, h_ref, wgu_ref, wd_ref, y_ref, *, d_ff, fc):
    del te_ref, jb_ref

    @pl.when(pl.program_id(0) < nact_ref[0])
    def _():
        h = h_ref[...]
        acc = None
        for c in range(d_ff // fc):
            gate = jnp.dot(h, wgu_ref[0, :, c * fc:(c + 1) * fc], preferred_element_type=F32)
            up = jnp.dot(h, wgu_ref[0, :, d_ff + c * fc:d_ff + (c + 1) * fc], preferred_element_type=F32)
            a = (_silu(gate) * up).astype(BF16)
            part = jnp.dot(a, wd_ref[0, c * fc:(c + 1) * fc, :], preferred_element_type=F32)
            acc = part if acc is None else acc + part
        y_ref[...] = acc.astype(BF16)

    @pl.when(pl.program_id(0) >= nact_ref[0])
    def _():
        y_ref[...] = jnp.zeros_like(y_ref)


def _experts(h_sorted, tile_expert, tile_block, n_active, w_gu, w_down, index, tmf):
    n_rows, d = h_sorted.shape
    d_ff = w_down.shape[2]
    first = index * N_EXPERTS
    w_gu = w_gu.reshape(-1, d, 2 * d_ff)
    w_down = w_down.reshape(-1, d_ff, d)
    return pl.pallas_call(
        functools.partial(_experts_kernel, d_ff=d_ff, fc=MOE_FF_CHUNK),
        grid_spec=pltpu.PrefetchScalarGridSpec(
            num_scalar_prefetch=3,
            grid=(n_rows // tmf,),
            in_specs=[pl.BlockSpec((tmf, d), lambda j, te, jb, na: (jb[j], 0)),
                      pl.BlockSpec((1, d, 2 * d_ff), lambda j, te, jb, na: (first + te[j], 0, 0)),
                      pl.BlockSpec((1, d_ff, d), lambda j, te, jb, na: (first + te[j], 0, 0))],
            out_specs=pl.BlockSpec((tmf, d), lambda j, te, jb, na: (j, 0))),
        out_shape=jax.ShapeDtypeStruct((n_rows, d), BF16),
        compiler_params=pltpu.CompilerParams(dimension_semantics=("arbitrary",),
                                             vmem_limit_bytes=60 * 1024 * 1024),
        name="moe_experts_sorted",
    )(tile_expert, tile_block, n_active, h_sorted, w_gu, w_down)


def _combine_kernel(seg_ref, loc_ref, goff_ref, x_ref, mod_ref, slots_ref, gw_ref, sorted_hbm, xo_ref,
                    stage_ref, sems, *, n_slots, n_tiles):
    i = pl.program_id(0)
    par = i % 2

    def copies(tile, parity):
        return _segment_copies(tile, seg_ref, loc_ref, goff_ref, stage_ref.at[parity], sorted_hbm,
                               sems.at[parity], False)

    @pl.when(i == 0)
    def _():
        stage_ref[...] = jnp.zeros_like(stage_ref)
        _start_all(copies(i, par))

    @pl.when(i + 1 < n_tiles)
    def _():
        _start_all(copies(i + 1, 1 - par))

    _wait_all(copies(i, par))
    y = stage_ref[par]
    tm = x_ref.shape[0]
    scol = lax.broadcasted_iota(jnp.int32, (tm, n_slots), 1)
    gates = (jnp.where(scol == slots_ref[:, 0:1], gw_ref[:, 0:1], 0.0)
             + jnp.where(scol == slots_ref[:, 1:2], gw_ref[:, 1:2], 0.0))
    moe = jnp.dot(gates.astype(BF16), y, preferred_element_type=F32)
    xo_ref[...] = x_ref[...] + _mod_part(mod_ref, 5) * moe


def _combine(x, mod, tiles_per_seq, slots_col, gw_col, y_sorted, seg, loc, goff, tm):
    rows, d = x.shape
    mr = mod.shape[1]
    n_tiles = rows // tm
    n_slots = 2 * tm + N_EXPERTS * SEG_ALIGN
    row_spec = pl.BlockSpec((tm, d), lambda i, *_: (i, 0))
    pair_spec = pl.BlockSpec((tm, 2), lambda i, *_: (i, 0))
    return pl.pallas_call(
        functools.partial(_combine_kernel, n_slots=n_slots, n_tiles=n_tiles),
        grid_spec=pltpu.PrefetchScalarGridSpec(
            num_scalar_prefetch=3,
            grid=(n_tiles,),
            in_specs=[row_spec,
                      pl.BlockSpec((1, mr, 6 * d), lambda i, *_: (i // tiles_per_seq, 0, 0)),
                      pair_spec, pair_spec,
                      pl.BlockSpec(memory_space=pl.ANY)],
            out_specs=row_spec,
            scratch_shapes=[pltpu.VMEM((2, n_slots, d), BF16), pltpu.SemaphoreType.DMA((2,))]),
        out_shape=jax.ShapeDtypeStruct((rows, d), F32),
        compiler_params=_cparams("arbitrary"),
        name="moe_combine",
    )(seg, loc, goff, x, mod, slots_col, gw_col, y_sorted)


def _moe_sparse_layer(parts, g, w_router, b_router, w_gu, w_down, index):
    tmf = EXPERT_ROW_TILE
    d = parts[0][0].shape[1]
    routed = [_route(x, mod, tps, g, w_router, b_router, tm) for x, mod, tps, tm in parts]
    tiles = [x.shape[0] // tm for x, _, _, tm in parts]
    seg = jnp.concatenate([r[3][:, :, 0] for r in routed], axis=0)
    loc = jnp.cumsum(seg, axis=1) - seg
    region = ((jnp.sum(seg, axis=0) + tmf - 1) // tmf) * tmf
    region_end = jnp.cumsum(region)
    goff = (region_end - region)[None, :] + jnp.cumsum(seg, axis=0) - seg
    n_sorted = sum(2 * x.shape[0] for x, _, _, _ in parts) + sum(tiles) * N_EXPERTS * (SEG_ALIGN - 1)
    n_sorted += N_EXPERTS * (tmf - SEG_ALIGN)
    n_sorted = ((n_sorted + tmf - 1) // tmf) * tmf
    n_row_tiles = n_sorted // tmf
    n_active = region_end[-1] // tmf
    tile_ids = jnp.arange(n_row_tiles, dtype=jnp.int32)
    tile_block = jnp.maximum(jnp.minimum(tile_ids, n_active - 1), 0)
    tile_expert = jnp.sum((region_end // tmf)[None, :] <= tile_block[:, None], axis=1)
    tile_expert = jnp.minimum(tile_expert, N_EXPERTS - 1).astype(jnp.int32)
    tables, first = [], 0
    for n in tiles:
        cut = lambda a: a[first:first + n].reshape(-1).astype(jnp.int32)
        tables.append((cut(seg), cut(loc), cut(goff)))
        first += n
    h_sorted = jnp.zeros((n_sorted, d), BF16)
    for (_, _, _, tm), (h, slots, _, _), tab in zip(parts, routed, tables):
        h_sorted = _dispatch(h, slots, *tab, h_sorted, tm)
    y_sorted = _experts(h_sorted, tile_expert, tile_block.astype(jnp.int32),
                        n_active.reshape(1).astype(jnp.int32), w_gu, w_down, index, tmf)
    outs = []
    for (x, mod, tps, tm), (_, slots, gw, _), tab in zip(parts, routed, tables):
        rows = x.shape[0]
        slots_col = jnp.transpose(slots, (0, 2, 1)).reshape(rows, 2)
        gw_col = jnp.transpose(gw, (0, 2, 1)).reshape(rows, 2)
        outs.append(_combine(x, mod, tps, slots_col, gw_col, y_sorted, *tab, tm))
    return outs


def _rope_tables(pos):
    half = ROT_DIM // 2
    inv = jnp.float32(ROPE_THETA) ** (-jnp.arange(half, dtype=jnp.float32) / half)
    ang = pos.astype(jnp.float32)[:, None] * inv[None, :]
    cos, sin = jnp.cos(ang), jnp.sin(ang)
    t = pos.shape[0]
    pad1 = jnp.ones((t, HEAD_DIM - ROT_DIM), F32)
    pad0 = jnp.zeros((t, HEAD_DIM - ROT_DIM), F32)
    return (jnp.concatenate([cos, cos, pad1], axis=1), jnp.concatenate([-sin, sin, pad0], axis=1))


def _swap_rot_halves(v):
    half = ROT_DIM // 2
    return jnp.concatenate([v[half:ROT_DIM], v[:half], v[ROT_DIM:]])


def _rope_gain_tables(cos_ref, sin_ref, g_ref, gs_ref, scale):
    return cos_ref[...] * (g_ref[...] * scale), sin_ref[...] * (gs_ref[...] * scale)


def _head_norm_rope(xh, a, b):
    half = ROT_DIM // 2
    ones = jnp.ones((HEAD_DIM, HEAD_DIM), BF16)
    ss = jnp.dot((xh * xh).astype(BF16), ones, preferred_element_type=F32)
    rs = lax.rsqrt(ss * (1.0 / HEAD_DIM) + EPS)
    lane = lax.broadcasted_iota(jnp.int32, xh.shape, 1)
    swapped = jnp.where(lane < half, pltpu.roll(xh, HEAD_DIM - half, 1), pltpu.roll(xh, half, 1))
    return (xh * a + swapped * b) * rs


SPLIT = 4
SPLIT_TILE = SPLIT * Q_BLOCK
SPLIT_GROUPS = (1, 2)


def _kv_kernel(x_ref, mod_ref, g_ref, wkv_ref, gk_ref, cos_ref, sin_ref, *out_refs, split):
    x = x_ref[...]
    h = _norm_mod(x, g_ref[...], mod_ref[0, :, 0:D_MODEL], mod_ref[0, :, D_MODEL:2 * D_MODEL]).astype(BF16)
    rope_a, rope_b = _rope_gain_tables(cos_ref, sin_ref, gk_ref.at[0:1], gk_ref.at[1:2], 1.0)
    k_refs, v_refs = out_refs[:N_GROUPS], out_refs[N_GROUPS:2 * N_GROUPS]
    kv_all = jnp.dot(h, wkv_ref[...], preferred_element_type=F32)
    for hd in range(N_KV_HEADS):
        kh = kv_all[:, hd * HEAD_DIM:(hd + 1) * HEAD_DIM]
        vh = kv_all[:, KV_WIDTH + hd * HEAD_DIM:KV_WIDTH + (hd + 1) * HEAD_DIM]
        g_idx, k_idx = divmod(hd, HKV_G)
        sl = slice(k_idx * HEAD_DIM, (k_idx + 1) * HEAD_DIM)
        kh = _head_norm_rope(kh, rope_a, rope_b)
        rows_of_head = pl.ds(k_idx, kh.shape[0], stride=HKV_G)
        k_refs[g_idx][rows_of_head, :] = kh
        v_refs[g_idx][rows_of_head, :] = vh
        if split and g_idx in SPLIT_GROUPS:
            n = SPLIT_GROUPS.index(g_idx)
            scr = out_refs[-1]
            for j, (val, dst) in enumerate(((kh, out_refs[2 * N_GROUPS + 2 * n]),
                                            (vh, out_refs[2 * N_GROUPS + 2 * n + 1]))):
                slot = (n * HKV_G + k_idx) * 2 + j
                scr[slot] = val
                for c in range(SPLIT):
                    dst[c * Q_BLOCK:(c + 1) * Q_BLOCK, sl] = (
                        scr[slot, pl.ds(c, Q_BLOCK, stride=SPLIT), :].astype(BF16))


def _kv_proj(x, mod_kv, tiles_per_seq, g, w_kv, g_k, cos, sin, tm, split):
    rows, d = x.shape
    mr = mod_kv.shape[1]
    assert not split or tm == SPLIT_TILE
    row_spec = pl.BlockSpec((tm, d), lambda i: (i, 0))
    tab_spec = pl.BlockSpec((tm, HEAD_DIM), lambda i: (i % tiles_per_seq, 0))
    out_spec = pl.BlockSpec((tm, GROUP_KV_WIDTH), lambda i: (i, 0))
    nat_spec = pl.BlockSpec((HKV_G * tm, HEAD_DIM), lambda i: (i, 0))
    n_split = 2 * len(SPLIT_GROUPS) if split else 0
    return pl.pallas_call(
        functools.partial(_kv_kernel, split=split),
        grid=(rows // tm,),
        in_specs=[row_spec,
                  pl.BlockSpec((1, mr, 2 * d), lambda i: (i // tiles_per_seq, 0, 0)),
                  pl.BlockSpec((1, d), lambda i: (0, 0)),
                  _resident((d, 2 * KV_WIDTH), lambda i: (0, 0)),
                  pl.BlockSpec((2, HEAD_DIM), lambda i: (0, 0)),
                  tab_spec, tab_spec],
        out_specs=[nat_spec] * (2 * N_GROUPS) + [out_spec] * n_split,
        out_shape=[jax.ShapeDtypeStruct((HKV_G * rows, HEAD_DIM), F32)] * (2 * N_GROUPS)
                  + [jax.ShapeDtypeStruct((rows, GROUP_KV_WIDTH), BF16)] * n_split,
        scratch_shapes=[pltpu.VMEM((n_split * HKV_G, tm, HEAD_DIM), F32)] if split else [],
        compiler_params=_cparams("parallel"),
        name="kv_proj",
    )(x, mod_kv, g.reshape(1, d), w_kv, jnp.stack([g_k, _swap_rot_halves(g_k)]), cos, sin)


def _q_kernel(x_ref, mod_ref, g_ref, wq_ref, gq_ref, cos_ref, sin_ref, *refs, split):
    q_refs = refs[:N_GROUPS]
    x = x_ref[...]
    h = _norm_mod(x, g_ref[...], _mod_part(mod_ref, 0), _mod_part(mod_ref, 1)).astype(BF16)
    rope_a, rope_b = _rope_gain_tables(cos_ref, sin_ref, gq_ref.at[0:1], gq_ref.at[1:2], ATTN_SCALE)
    q_all = jnp.dot(h, wq_ref[...], preferred_element_type=F32)
    for hd in range(N_Q_HEADS):
        g_idx, h_idx = divmod(hd, HQ_G)
        sl = slice(h_idx * HEAD_DIM, (h_idx + 1) * HEAD_DIM)
        qh = _head_norm_rope(q_all[:, hd * HEAD_DIM:(hd + 1) * HEAD_DIM], rope_a, rope_b)
        if split and g_idx in SPLIT_GROUPS:
            scr = refs[N_GROUPS]
            n = SPLIT_GROUPS.index(g_idx) * HQ_G + h_idx
            scr[n] = qh
            for c in range(SPLIT):
                q_refs[g_idx][c * Q_BLOCK:(c + 1) * Q_BLOCK, sl] = (
                    scr[n, pl.ds(c, Q_BLOCK, stride=SPLIT), :].astype(BF16))
        else:
            q_refs[g_idx][:, sl] = qh.astype(BF16)


def _q_proj(x, mod, tiles_per_seq, g, w_q, index, g_q, cos, sin, tm, split):
    rows, d = x.shape
    mr = mod.shape[1]
    assert not split or tm == SPLIT_TILE
    row_spec = pl.BlockSpec((tm, d), lambda i: (i, 0))
    tab_spec = pl.BlockSpec((tm, HEAD_DIM), lambda i: (i % tiles_per_seq, 0))
    return pl.pallas_call(
        functools.partial(_q_kernel, split=split),
        grid=(rows // tm,),
        in_specs=[row_spec,
                  pl.BlockSpec((1, mr, 6 * d), lambda i: (i // tiles_per_seq, 0, 0)),
                  pl.BlockSpec((1, d), lambda i: (0, 0)),
                  _stacked(w_q, index),
                  pl.BlockSpec((2, HEAD_DIM), lambda i: (0, 0)),
                  tab_spec, tab_spec],
        out_specs=[pl.BlockSpec((tm, O_WIDTH), lambda i: (i, 0))] * N_GROUPS,
        out_shape=[jax.ShapeDtypeStruct((rows, O_WIDTH), BF16)] * N_GROUPS,
        scratch_shapes=[pltpu.VMEM((len(SPLIT_GROUPS) * HQ_G, tm, HEAD_DIM), F32)] if split else [],
        compiler_params=_cparams("parallel"),
        name="q_proj",
    )(x, mod, g.reshape(1, d), w_q, jnp.stack([g_q, _swap_rot_halves(g_q)]), cos, sin)


def _attn_prompt_kernel(*refs, step, has_prev, n_blk):
    if has_prev:
        q_ref, k_ref, v_ref, kp_ref, vp_ref, o_ref, st_ref = refs
        not_first = pl.program_id(1) > 0
    else:
        q_ref, k_ref, v_ref, o_ref, st_ref = refs
    nq = REP * Q_BLOCK
    lane = lax.broadcasted_iota(jnp.int32, (Q_BLOCK, HEAD_DIM), 1)
    max_kb = 2 if step == 1 else n_blk
    col = lax.broadcasted_iota(jnp.int32, (nq, max_kb * Q_BLOCK), 1)
    dist = (max_kb - 1) * Q_BLOCK + (lax.broadcasted_iota(jnp.int32, col.shape, 0) & (Q_BLOCK - 1)) - col
    reach = (dist >= 0) & (dist <= step * Q_BLOCK) & ((dist & (step - 1)) == 0)
    for i in range(n_blk):
        first_kb = i - 1 if step == 1 else 0
        if first_kb < 0 and not has_prev:
            first_kb = 0
        kbs = list(range(first_kb, i + 1))
        valid = reach[:, (max_kb - len(kbs)) * Q_BLOCK:]
        if first_kb < 0:
            valid = valid & (not_first | (col >= Q_BLOCK))
        for k in range(HKV_G):
            ksl = slice(k * HEAD_DIM, (k + 1) * HEAD_DIM)
            q = jnp.concatenate(
                [q_ref[i, :, (k * REP + r) * HEAD_DIM:(k * REP + r + 1) * HEAD_DIM] for r in range(REP)], axis=0)
            if has_prev:
                hrows = pl.ds(k, Q_BLOCK, stride=HKV_G)
                keys = jnp.concatenate([kp_ref[hrows, :] if j < 0 else k_ref[j, hrows, :] for j in kbs], axis=0)
                vals = jnp.concatenate([vp_ref[hrows, :] if j < 0 else v_ref[j, hrows, :] for j in kbs], axis=0)
            else:
                keys = jnp.concatenate([k_ref[j, :, ksl] for j in kbs], axis=0)
                vals = jnp.concatenate([v_ref[j, :, ksl] for j in kbs], axis=0)
            s = lax.dot_general(q, keys.astype(BF16), (((1,), (1,)), ((), ())), preferred_element_type=F32)
            s = jnp.where(valid, s, NEG)
            m = jnp.max(s, axis=1, keepdims=True)
            p = jnp.exp(s - m)
            den = jnp.sum(p, axis=1, keepdims=True)
            out = jnp.dot(p.astype(BF16), vals.astype(BF16), preferred_element_type=F32) / den
            lse = m + jnp.log(den)
            stats = jnp.zeros((Q_BLOCK, HEAD_DIM), F32)
            for r in range(REP):
                rs = slice(r * Q_BLOCK, (r + 1) * Q_BLOCK)
                o_ref[i, :, (k * REP + r) * HEAD_DIM:(k * REP + r + 1) * HEAD_DIM] = out[rs].astype(BF16)
                stats = jnp.where(lane == r, lse[rs], stats)
            st_ref[i, :, ksl] = stats


def _attn_prompt_group(q, k_g, v_g, group, batch, seq):
    _, dil = DIL_GROUPS[group]
    rows = batch * seq
    n_tiles = seq // SPLIT_TILE
    if group in SPLIT_GROUPS:
        step = dil // SPLIT
        lead = (batch, n_tiles, SPLIT, Q_BLOCK)
        view = lambda a: a.reshape(*lead, a.shape[-1])
        spec = lambda w: pl.BlockSpec((None, n_tiles, None, Q_BLOCK, w), lambda b, c: (b, 0, c, 0, 0))
        out_shape = [jax.ShapeDtypeStruct((*lead, O_WIDTH), BF16),
                     jax.ShapeDtypeStruct((*lead, GROUP_KV_WIDTH), F32)]
        out, stats = pl.pallas_call(
            functools.partial(_attn_prompt_kernel, step=step, has_prev=False, n_blk=n_tiles),
            grid=(batch, SPLIT),
            in_specs=[spec(O_WIDTH), spec(GROUP_KV_WIDTH), spec(GROUP_KV_WIDTH)],
            out_specs=[spec(O_WIDTH), spec(GROUP_KV_WIDTH)],
            out_shape=out_shape,
            compiler_params=_cparams("parallel", "parallel"),
            name=f"attn_prompt_g{group}",
        )(view(q), view(k_g), view(v_g))
    else:
        lead = (batch * n_tiles, SPLIT)
        view = lambda a, r: a.reshape(*lead, r, a.shape[-1])
        spec = lambda r, w: pl.BlockSpec((None, SPLIT, r, w), lambda b, i: (b * n_tiles + i, 0, 0, 0))
        kv_rows = HKV_G * Q_BLOCK
        kv_spec = spec(kv_rows, HEAD_DIM)
        prev = pl.BlockSpec((None, None, kv_rows, HEAD_DIM),
                            lambda b, i: (jnp.maximum(b * n_tiles + i - 1, 0), SPLIT - 1, 0, 0))
        out_shape = [jax.ShapeDtypeStruct((*lead, Q_BLOCK, O_WIDTH), BF16),
                     jax.ShapeDtypeStruct((*lead, Q_BLOCK, GROUP_KV_WIDTH), F32)]
        k4, v4 = view(k_g, kv_rows), view(v_g, kv_rows)
        out, stats = pl.pallas_call(
            functools.partial(_attn_prompt_kernel, step=dil, has_prev=True, n_blk=SPLIT),
            grid=(batch, n_tiles),
            in_specs=[spec(Q_BLOCK, O_WIDTH), kv_spec, kv_spec, prev, prev],
            out_specs=[spec(Q_BLOCK, O_WIDTH), spec(Q_BLOCK, GROUP_KV_WIDTH)],
            out_shape=out_shape,
            compiler_params=_cparams("parallel", "arbitrary"),
            name=f"attn_prompt_g{group}",
        )(view(q, Q_BLOCK), k4, v4, k4, v4)
    return out.reshape(rows, O_WIDTH), stats.reshape(rows, GROUP_KV_WIDTH)


def _merge_o_kernel(x_ref, mod_ref, o0_ref, o1_ref, o2_ref, s0_ref, s1_ref, s2_ref, wo_ref, xo_ref,
                    on_ref, sn_ref):
    for n, (o_ref, s_ref) in enumerate(((o1_ref, s1_ref), (o2_ref, s2_ref))):
        for c in range(SPLIT):
            rows = slice(c * Q_BLOCK, (c + 1) * Q_BLOCK)
            dst = pl.ds(c, Q_BLOCK, stride=SPLIT)
            for hd in range(HQ_G):
                on_ref[n, hd, dst, :] = o_ref[rows, hd * HEAD_DIM:(hd + 1) * HEAD_DIM].astype(F32)
            for k in range(HKV_G):
                sn_ref[n, k, dst, :] = s_ref[rows, k * HEAD_DIM:(k + 1) * HEAD_DIM]
    heads = []
    for k in range(HKV_G):
        for r in range(REP):
            hd = k * REP + r
            lses = [s0_ref[:, k * HEAD_DIM + r:k * HEAD_DIM + r + 1],
                    sn_ref[0, k, :, r:r + 1], sn_ref[1, k, :, r:r + 1]]
            m = jnp.maximum(jnp.maximum(lses[0], lses[1]), lses[2])
            es = [jnp.exp(l - m) for l in lses]
            den = es[0] + es[1] + es[2]
            sl = slice(hd * HEAD_DIM, (hd + 1) * HEAD_DIM)
            outs = [o0_ref[:, sl].astype(F32), on_ref[0, hd], on_ref[1, hd]]
            o = sum((e / den) * o_g for e, o_g in zip(es, outs))
            heads.append(o.astype(BF16))
    o = jnp.concatenate(heads, axis=1)
    y = jnp.dot(o, wo_ref[...], preferred_element_type=F32)
    xo_ref[...] = x_ref[...] + _mod_part(mod_ref, 2) * y


def _merge_o(x, mod, tiles_per_seq, outs, stats, w_o, index, tm):
    rows, d = x.shape
    mr = mod.shape[1]
    assert tm == SPLIT_TILE
    row_spec = pl.BlockSpec((tm, d), lambda i: (i, 0))
    o_spec = pl.BlockSpec((tm, O_WIDTH), lambda i: (i, 0))
    st_spec = pl.BlockSpec((tm, GROUP_KV_WIDTH), lambda i: (i, 0))
    return pl.pallas_call(
        _merge_o_kernel,
        grid=(rows // tm,),
        in_specs=[row_spec,
                  pl.BlockSpec((1, mr, 6 * d), lambda i: (i // tiles_per_seq, 0, 0))]
                 + [o_spec] * 3 + [st_spec] * 3 + [_stacked(w_o, index)],
        out_specs=row_spec,
        out_shape=jax.ShapeDtypeStruct((rows, d), F32),
        scratch_shapes=[pltpu.VMEM((len(SPLIT_GROUPS), HQ_G, tm, HEAD_DIM), F32),
                        pltpu.VMEM((len(SPLIT_GROUPS), HKV_G, tm, HEAD_DIM), F32)],
        compiler_params=_cparams("parallel"),
        name="merge_o_proj",
    )(x, mod, *outs, *stats, w_o)


def _attn_sample_kernel(*refs, n_new, write_buffers):
    q_ref = refs[0]
    nk_refs = refs[1:1 + N_GROUPS]
    nv_refs = refs[1 + N_GROUPS:1 + 2 * N_GROUPS]
    ck_refs = refs[1 + 2 * N_GROUPS:1 + 3 * N_GROUPS]
    cv_refs = refs[1 + 3 * N_GROUPS:1 + 4 * N_GROUPS]
    o_ref = refs[1 + 4 * N_GROUPS]
    nq = REP * n_new
    n_new_rows = HKV_G * n_new
    qi = lax.broadcasted_iota(jnp.int32, (nq, 1), 0) & (n_new - 1)

    if write_buffers:
        ok_refs = refs[2 + 4 * N_GROUPS:2 + 5 * N_GROUPS]
        ov_refs = refs[2 + 5 * N_GROUPS:2 + 6 * N_GROUPS]
        for g in range(N_GROUPS):
            for c_ref, n_ref, o_buf in ((ck_refs[g], nk_refs[g], ok_refs[g]), (cv_refs[g], nv_refs[g], ov_refs[g])):
                n_rows = c_ref.shape[1]
                o_buf[0, 0:n_rows - n_new_rows, :] = c_ref[0, n_new_rows:n_rows, :]
                o_buf[0, n_rows - n_new_rows:n_rows, :] = n_ref[0]

    for k in range(HKV_G):
        outs, lses = [], []
        for g, (window, dil) in enumerate(DIL_GROUPS):
            length = ck_refs[g].shape[1] // HKV_G
            q = q_ref[0, g, k]
            qf = q.astype(F32)
            keys = ck_refs[g][0, pl.ds(k, length, stride=HKV_G), :].astype(BF16)
            vals = cv_refs[g][0, pl.ds(k, length, stride=HKV_G), :].astype(BF16)
            s = lax.dot_general(q, keys, (((1,), (1,)), ((), ())), preferred_element_type=F32)
            idx = lax.broadcasted_iota(jnp.int32, (nq, length), 1)
            diff = length + qi - idx
            s = jnp.where(((diff & (dil - 1)) == 0) & (diff <= window), s, NEG)
            new_k = nk_refs[g][0]
            new_v = nv_refs[g][0]
            s_new = []
            for j in range(n_new):
                sj = jnp.sum(qf * new_k[HKV_G * j + k:HKV_G * j + k + 1, :], axis=1, keepdims=True)
                ok = (qi >= j) & (((qi - j) & (dil - 1)) == 0)
                s_new.append(jnp.where(ok, sj, NEG))
            m = jnp.max(s, axis=1, keepdims=True)
            for sj in s_new:
                m = jnp.maximum(m, sj)
            p = jnp.exp(s - m)
            den = jnp.sum(p, axis=1, keepdims=True)
            acc = jnp.dot(p.astype(BF16), vals, preferred_element_type=F32)
            for j, sj in enumerate(s_new):
                pj = jnp.exp(sj - m)
                den = den + pj
                acc = acc + pj * new_v[HKV_G * j + k:HKV_G * j + k + 1, :]
            outs.append(acc / den)
            lses.append(m + jnp.log(den))
        m = jnp.maximum(jnp.maximum(lses[0], lses[1]), lses[2])
        es = [jnp.exp(l - m) for l in lses]
        den = es[0] + es[1] + es[2]
        o_ref[0, k] = sum((e / den) * o for e, o in zip(es, outs))


def _attn_sample(q, new_k, new_v, cache_k, cache_v, n_seq, n_new, write_buffers):
    assert HKV_G * n_new == 8
    q5 = q.reshape(n_seq, n_new, N_GROUPS, HKV_G, REP, HEAD_DIM)
    q5 = jnp.transpose(q5, (0, 2, 3, 4, 1, 5)).reshape(n_seq, N_GROUPS, HKV_G, REP * n_new, HEAD_DIM)
    rows2 = lambda a, n: a.reshape(n_seq, n * HKV_G, HEAD_DIM)
    nk = [rows2(a, n_new) for a in new_k]
    nv = [rows2(a, n_new) for a in new_v]
    ck = [rows2(c, c.shape[1]) for c in cache_k]
    cv = [rows2(c, c.shape[1]) for c in cache_v]
    new_spec = pl.BlockSpec((1, HKV_G * n_new, HEAD_DIM), lambda b: (b, 0, 0))
    cache_specs = [pl.BlockSpec((1, c.shape[1], HEAD_DIM), lambda b: (b, 0, 0)) for c in ck]
    cache_shapes = [jax.ShapeDtypeStruct(c.shape, F32) for c in ck]
    n_buf = 2 if write_buffers else 0
    res = pl.pallas_call(
        functools.partial(_attn_sample_kernel, n_new=n_new, write_buffers=write_buffers),
        grid=(n_seq,),
        in_specs=[pl.BlockSpec((1, N_GROUPS, HKV_G, REP * n_new, HEAD_DIM), lambda b: (b, 0, 0, 0, 0))]
                 + [new_spec] * (2 * N_GROUPS) + cache_specs + cache_specs,
        out_specs=[pl.BlockSpec((1, HKV_G, REP * n_new, HEAD_DIM), lambda b: (b, 0, 0, 0))]
                  + cache_specs * n_buf,
        out_shape=[jax.ShapeDtypeStruct((n_seq, HKV_G, REP * n_new, HEAD_DIM), F32)] + cache_shapes * n_buf,
        compiler_params=_cparams("parallel"),
        name="attn_sample",
    )(q5, *nk, *nv, *ck, *cv)
    o = res[0].reshape(n_seq, HKV_G, REP, n_new, HEAD_DIM)
    o = jnp.transpose(o, (0, 3, 1, 2, 4)).reshape(n_seq * n_new, O_WIDTH)
    if not write_buffers:
        return o, None, None
    shape4 = lambda a: a.reshape(n_seq, a.shape[1] // HKV_G, HKV_G, HEAD_DIM)
    new_ck = [shape4(a) for a in res[1:1 + N_GROUPS]]
    new_cv = [shape4(a) for a in res[1 + N_GROUPS:1 + 2 * N_GROUPS]]
    return o, new_ck, new_cv


def _o_proj_kernel(x_ref, mod_ref, o_ref, wo_ref, xo_ref):
    y = jnp.dot(o_ref[...].astype(BF16), wo_ref[...], preferred_element_type=F32)
    xo_ref[...] = x_ref[...] + _mod_part(mod_ref, 2) * y


def _o_proj(x, mod, o, w_o, index):
    rows, d = x.shape
    mr = mod.shape[1]
    full = pl.BlockSpec((rows, d), lambda i: (0, 0))
    return pl.pallas_call(
        _o_proj_kernel,
        grid=(1,),
        in_specs=[full, pl.BlockSpec((1, mr, 6 * d), lambda i: (0, 0, 0)),
                  pl.BlockSpec((rows, O_WIDTH), lambda i: (0, 0)),
                  _stacked(w_o, index)],
        out_specs=full,
        out_shape=jax.ShapeDtypeStruct((rows, d), F32),
        compiler_params=_cparams("arbitrary"),
        name="o_proj_sample",
    )(x, mod, o, w_o)


def _trunk(x, mods, mod_kv, pos, weights, tm, conv_prev, kv_prev):
    (g_norm_mix, g_norm_ffn, w_in, w_conv, w_out_conv, g_norm_kv, w_kv, g_k, w_q, g_q, w_o,
     w_gu_dense, w_down_dense) = weights
    batch, seq, d = x.shape
    rows = batch * seq
    depth = g_norm_mix.shape[0]
    n_conv = w_in.shape[0]
    sample = kv_prev is not None
    tiles_per_seq = 1 if sample else seq // tm
    x = x.reshape(rows, d)
    cos, sin = _rope_tables(pos)
    if sample:
        cos, sin = jnp.tile(cos, (batch, 1)), jnp.tile(sin, (batch, 1))
    conv_state = []
    kv_state = None
    for layer in range(depth):
        mod = mods[layer]
        if layer == n_conv:
            kvs = _kv_proj(x, mod_kv, tiles_per_seq, g_norm_kv, w_kv, g_k, cos, sin, tm, not sample)
            k_new, v_new = kvs[:N_GROUPS], kvs[N_GROUPS:2 * N_GROUPS]
            k_att, v_att = list(k_new), list(v_new)
            for n, g_idx in enumerate(SPLIT_GROUPS if not sample else ()):
                k_att[g_idx], v_att[g_idx] = kvs[2 * N_GROUPS + 2 * n], kvs[2 * N_GROUPS + 2 * n + 1]
        if layer < n_conv:
            if sample:
                st = conv_prev[layer]
                zero = jnp.zeros((batch, seq - 2, d), F32)
                pa = jnp.concatenate([st[:, 1:2], jnp.zeros((batch, seq - 1, d), F32)], axis=1)
                pb = jnp.concatenate([st, zero], axis=1)
                prev = (seq, pa.reshape(rows, d), pb.reshape(rows, d))
            else:
                prev = None
            x, u_tail = _conv_layer(x, mod, tiles_per_seq, g_norm_mix[layer], w_in, w_conv, w_out_conv, layer,
                                    tm, prev)
            if sample:
                conv_state.append(u_tail.reshape(batch, seq, d)[:, seq - (CONV_W - 1):])
            else:
                conv_state.append(u_tail[:, 8 - (CONV_W - 1):])
        else:
            lb = layer - n_conv
            qs = _q_proj(x, mod, tiles_per_seq, g_norm_mix[layer], w_q, lb, g_q[lb], cos, sin, tm, not sample)
            if sample:
                o, new_ck, new_cv = _attn_sample(jnp.concatenate(qs, axis=1), k_new, v_new, kv_prev[0],
                                                 kv_prev[1], batch, seq, kv_state is None)
                if kv_state is None:
                    kv_state = (new_ck, new_cv)
                x = _o_proj(x, mod, o, w_o, lb)
            else:
                res = [_attn_prompt_group(qs[g], k_att[g], v_att[g], g, batch, seq) for g in range(N_GROUPS)]
                x = _merge_o(x, mod, tiles_per_seq, [r[0] for r in res], [r[1] for r in res], w_o, lb, tm)
        if layer % 2 == 0:
            x = _ffn_layer(x, mod, tiles_per_seq, g_norm_ffn[layer], w_gu_dense, w_down_dense, layer // 2, tm)
        else:
            tm_moe = tm if sample else min(SPARSE_TILE, seq)
            x = yield (x, mod, 1 if sample else seq // tm_moe, tm_moe), layer
    if not sample:
        shape4 = lambda a: a.reshape(batch, seq, HKV_G, HEAD_DIM)
        kv_state = ([shape4(k)[:, -min(w, seq):] for k, (w, _) in zip(k_new, DIL_GROUPS)],
                    [shape4(v)[:, -min(w, seq):] for v, (w, _) in zip(v_new, DIL_GROUPS)])
    return x.reshape(batch, seq, d), jnp.stack(conv_state, axis=0), kv_state


def _resume(trunk, x):
    try:
        return False, trunk.send(x)
    except StopIteration as done:
        return True, done.value


def kernel(x_prompt, x_sample, state_conv, cache_k_g0, cache_v_g0, cache_k_g1, cache_v_g1, cache_k_g2,
           cache_v_g2, c_prompt, c_sample, g_norm_mix, g_norm_ffn, w_ada, b_ada, w_in, w_conv, w_out_conv,
           g_norm_kv, w_ada_kv, b_ada_kv, w_kv, g_k, w_q, g_q, w_o, w_gu_dense, w_down_dense, w_router,
           b_router, w_gu_moe, w_down_moe):
    b_p, t_p, d = x_prompt.shape
    b_s, t_s, _ = x_sample.shape
    bf = lambda w: w.astype(BF16)
    weights = (g_norm_mix, g_norm_ffn, bf(w_in), w_conv, bf(w_out_conv), g_norm_kv, bf(w_kv), g_k, bf(w_q),
               g_q, bf(w_o), bf(w_gu_dense), bf(w_down_dense))
    w_gu_moe, w_down_moe = bf(w_gu_moe), bf(w_down_moe)

    c_all = jnp.concatenate([c_prompt, c_sample], axis=0)
    mods_all = _ada(c_all, w_ada, b_ada)
    mod_kv_all = _ada(c_all, w_ada_kv[None], b_ada_kv[None])[0]
    depth = w_ada.shape[0]
    mods_p = [mods_all[l, :b_p, None, :] for l in range(depth)]
    mods_s = [jnp.repeat(mods_all[l, b_p:], t_s, axis=0)[None] for l in range(depth)]
    mod_kv_p = mod_kv_all[:b_p, None, :]
    mod_kv_s = jnp.repeat(mod_kv_all[b_p:], t_s, axis=0)[None]

    pos_p = jnp.arange(t_p, dtype=jnp.int32)
    pos_s = PAST_LEN + jnp.arange(t_s, dtype=jnp.int32)
    kv_prev = ([cache_k_g0, cache_k_g1, cache_k_g2], [cache_v_g0, cache_v_g1, cache_v_g2])
    trunks = [_trunk(x_prompt, mods_p, mod_kv_p, pos_p, weights, min(PROMPT_TILE, t_p), None, None),
              _trunk(x_sample, mods_s, mod_kv_s, pos_s, weights, b_s * t_s, state_conv, kv_prev)]
    requests = [next(t) for t in trunks]
    results = None
    while results is None:
        layer = requests[0][1]
        xs = _moe_sparse_layer([r[0] for r in requests], g_norm_ffn[layer], w_router[layer // 2],
                               b_router[layer // 2], w_gu_moe, w_down_moe, layer // 2)
        steps = [_resume(t, x) for t, x in zip(trunks, xs)]
        if all(done for done, _ in steps):
            results = [value for _, value in steps]
        else:
            requests = [value for _, value in steps]
    (y_prompt, conv_p, (kp, vp)), (y_sample, conv_s, (ksn, vsn)) = results
    return (y_prompt, y_sample, conv_p, conv_s,
            kp[0], vp[0], kp[1], vp[1], kp[2], vp[2],
            ksn[0], vsn[0], ksn[1], vsn[1], ksn[2], vsn[2])
```

```python
import functools

import jax
import jax.numpy as jnp
from jax import lax
from jax.experimental import pallas as pl
from jax.experimental.pallas import tpu as pltpu

D_MODEL = 1024
PAST_LEN = 8192
CONV_W = 3
HEAD_DIM = 128
ROT_DIM = HEAD_DIM // 4
ROPE_THETA = 500000.0
DIL_GROUPS = ((128, 1), (512, 4), (2048, 16))
N_GROUPS = len(DIL_GROUPS)
HQ_G = 8
HKV_G = 2
REP = HQ_G // HKV_G
N_Q_HEADS = N_GROUPS * HQ_G
N_KV_HEADS = N_GROUPS * HKV_G
Q_WIDTH = N_Q_HEADS * HEAD_DIM
KV_WIDTH = N_KV_HEADS * HEAD_DIM
O_WIDTH = HQ_G * HEAD_DIM
GROUP_KV_WIDTH = HKV_G * HEAD_DIM
N_EXPERTS = 8
EPS = 1e-6
ATTN_SCALE = HEAD_DIM ** -0.5
NEG = -1e30
Q_BLOCK = 128

F32 = jnp.float32
BF16 = jnp.bfloat16
HIGHEST = lax.Precision.HIGHEST

V7X_VMEM_LIMIT_BYTES = 56 * 1024 * 1024
V7X_MXU_WIDTH = 256
PROMPT_TILE = 512


def _cparams(*sem):
    return pltpu.CompilerParams(dimension_semantics=sem, vmem_limit_bytes=V7X_VMEM_LIMIT_BYTES)


def _resident(shape, index_map):
    return pl.BlockSpec(shape, index_map, pipeline_mode=pl.Buffered(1))


def _stacked(w, index):
    shape = w.shape[1:]
    return pl.BlockSpec((None,) + shape, lambda *_: (index,) + (0,) * len(shape), pipeline_mode=pl.Buffered(1))


def _silu(x):
    return x * jax.nn.sigmoid(x)


def _norm_mod(x, g, shift, scale):
    y = x * lax.rsqrt(jnp.mean(x * x, axis=-1, keepdims=True) + EPS) * g
    return y * (1.0 + scale) + shift


def _mod_part(mod_ref, k):
    return mod_ref[0, :, k * D_MODEL:(k + 1) * D_MODEL]


def _ada_kernel(c_ref, w_ref, b_ref, o_ref):
    c = c_ref[...]
    o_ref[0] = jnp.dot(_silu(c), w_ref[0], preferred_element_type=F32, precision=HIGHEST) + b_ref[0]


def _ada(c, w, b):
    n_layers, d, n = w.shape
    m = c.shape[0]
    tn = 2048
    return pl.pallas_call(
        _ada_kernel,
        grid=(n_layers, n // tn),
        in_specs=[pl.BlockSpec((m, d), lambda l, j: (0, 0)),
                  pl.BlockSpec((1, d, tn), lambda l, j: (l, 0, j)),
                  pl.BlockSpec((1, 1, tn), lambda l, j: (l, 0, j))],
        out_specs=pl.BlockSpec((1, m, tn), lambda l, j: (l, 0, j)),
        out_shape=jax.ShapeDtypeStruct((n_layers, m, n), F32),
        compiler_params=_cparams("parallel", "parallel"),
        name="ada",
    )(c, w, b.reshape(n_layers, 1, n))


def _conv_kernel(*refs, tm, tiles_per_seq, seq_rows):
    if seq_rows is None:
        x_ref, mod_ref, g_ref, win_ref, wc_ref, wout_ref, xo_ref, ust_ref, carry_ref = refs
    else:
        x_ref, mod_ref, g_ref, win_ref, wc_ref, wout_ref, pa_ref, pb_ref, xo_ref, ust_ref = refs
    x = x_ref[...]
    h = _norm_mod(x, g_ref[...], _mod_part(mod_ref, 0), _mod_part(mod_ref, 1)).astype(BF16)
    b_gate = jnp.dot(h, win_ref[:, 0:D_MODEL], preferred_element_type=F32)
    c_gate = jnp.dot(h, win_ref[:, D_MODEL:2 * D_MODEL], preferred_element_type=F32)
    v = jnp.dot(h, win_ref[:, 2 * D_MODEL:3 * D_MODEL], preferred_element_type=F32)
    u = c_gate * v
    row = lax.broadcasted_iota(jnp.int32, (tm, 1), 0)
    r1 = pltpu.roll(u, 1, 0)
    r2 = pltpu.roll(u, 2, 0)
    if seq_rows is None:
        @pl.when(pl.program_id(0) % tiles_per_seq == 0)
        def _():
            carry_ref[...] = jnp.zeros_like(carry_ref)
        um1 = jnp.where(row == 0, carry_ref[7:8, :], r1)
        um2 = jnp.where(row == 0, carry_ref[6:7, :], jnp.where(row == 1, carry_ref[7:8, :], r2))
        carry_ref[...] = u[tm - 8:tm, :]
        ust_ref[0] = u[tm - 8:tm, :]
    else:
        rowm = row & (seq_rows - 1)
        um1 = jnp.where(rowm == 0, pa_ref[...], r1)
        um2 = jnp.where(rowm < 2, pb_ref[...], r2)
        ust_ref[...] = u
    y = um2 * wc_ref[0:1, :] + um1 * wc_ref[1:2, :] + u * wc_ref[2:3, :]
    z = (b_gate * y).astype(BF16)
    out = jnp.dot(z, wout_ref[...], preferred_element_type=F32)
    xo_ref[...] = x + _mod_part(mod_ref, 2) * out


def _conv_layer(x, mod, tiles_per_seq, g, w_in, w_conv, w_out, layer, tm, prev=None):
    rows, d = x.shape
    n_tiles = rows // tm
    mr = mod.shape[1]
    row_spec = pl.BlockSpec((tm, d), lambda i: (i, 0))
    in_specs = [row_spec,
                pl.BlockSpec((1, mr, 6 * d), lambda i: (i // tiles_per_seq, 0, 0)),
                pl.BlockSpec((1, d), lambda i: (0, 0)),
                _stacked(w_in, layer), _stacked(w_conv, layer), _stacked(w_out, layer)]
    args = [x, mod, g.reshape(1, d), w_in, w_conv, w_out]
    if prev is None:
        seq_rows = None
        n_seq = n_tiles // tiles_per_seq
        ust_shape = jax.ShapeDtypeStruct((n_seq, 8, d), F32)
        ust_spec = pl.BlockSpec((1, 8, d), lambda i: (i // tiles_per_seq, 0, 0))
        scratch = [pltpu.VMEM((8, d), F32)]
    else:
        seq_rows, pa, pb = prev
        in_specs += [row_spec, row_spec]
        args += [pa, pb]
        ust_shape = jax.ShapeDtypeStruct((rows, d), F32)
        ust_spec = row_spec
        scratch = []
    return pl.pallas_call(
        functools.partial(_conv_kernel, tm=tm, tiles_per_seq=tiles_per_seq, seq_rows=seq_rows),
        grid=(n_tiles,),
        in_specs=in_specs,
        out_specs=[row_spec, ust_spec],
        out_shape=[jax.ShapeDtypeStruct((rows, d), F32), ust_shape],
        scratch_shapes=scratch,
        compiler_params=_cparams("arbitrary"),
        name="conv_layer",
    )(*args)


def _ffn_kernel(x_ref, mod_ref, g_ref, wgu_ref, wd_ref, xo_ref, *, d_ff, n_chunks):
    x = x_ref[...]
    h = _norm_mod(x, g_ref[...], _mod_part(mod_ref, 3), _mod_part(mod_ref, 4)).astype(BF16)
    n_mxu = d_ff // V7X_MXU_WIDTH
    edges = [(c * n_mxu // n_chunks) * V7X_MXU_WIDTH for c in range(n_chunks)] + [d_ff]
    acc = None
    for lo, hi in zip(edges[:-1], edges[1:]):
        gate = jnp.dot(h, wgu_ref[:, lo:hi], preferred_element_type=F32)
        up = jnp.dot(h, wgu_ref[:, d_ff + lo:d_ff + hi], preferred_element_type=F32)
        a = (_silu(gate) * up).astype(BF16)
        part = jnp.dot(a, wd_ref[lo:hi, :], preferred_element_type=F32)
        acc = part if acc is None else acc + part
    xo_ref[...] = x + _mod_part(mod_ref, 5) * acc


def _ffn_layer(x, mod, tiles_per_seq, g, w_gu, w_down, index, tm):
    rows, d = x.shape
    d_ff = w_down.shape[1]
    mr = mod.shape[1]
    row_spec = pl.BlockSpec((tm, d), lambda i: (i, 0))
    return pl.pallas_call(
        functools.partial(_ffn_kernel, d_ff=d_ff, n_chunks=2),
        grid=(rows // tm,),
        in_specs=[row_spec,
                  pl.BlockSpec((1, mr, 6 * d), lambda i: (i // tiles_per_seq, 0, 0)),
                  pl.BlockSpec((1, d), lambda i: (0, 0)),
                  _stacked(w_gu, index), _stacked(w_down, index)],
        out_specs=row_spec,
        out_shape=jax.ShapeDtypeStruct((rows, d), F32),
        compiler_params=_cparams("parallel"),
        name="ffn_dense",
    )(x, mod, g.reshape(1, d), w_gu, w_down)


SEG_ALIGN = 16
SEG_SIZES = (512, 256, 128, 64, 32, 16)
SPARSE_TILE = 512
EXPERT_ROW_TILE = 256
EXPERT_FF_UNIT = 2 * V7X_MXU_WIDTH


def _route_kernel(x_ref, mod_ref, g_ref, wrt_ref, br_ref, h_ref, slots_ref, gw_ref, seg_ref, *, tm):
    x = x_ref[...]
    h = _norm_mod(x, g_ref[...], _mod_part(mod_ref, 3), _mod_part(mod_ref, 4))
    h_ref[...] = h.astype(BF16)
    logits = lax.dot_general(wrt_ref[...], h, (((1,), (1,)), ((), ())), preferred_element_type=F32,
                             precision=HIGHEST) + br_ref[...]
    eidx = lax.broadcasted_iota(jnp.int32, logits.shape, 0).astype(F32)
    m1 = jnp.max(logits, axis=0, keepdims=True)
    i1 = jnp.min(jnp.where(logits == m1, eidx, float(N_EXPERTS)), axis=0, keepdims=True)
    sel1 = eidx == i1
    rest = jnp.where(sel1, -jnp.inf, logits)
    m2 = jnp.max(rest, axis=0, keepdims=True)
    i2 = jnp.min(jnp.where(rest == m2, eidx, float(N_EXPERTS)), axis=0, keepdims=True)
    sel2 = eidx == i2
    e2 = jnp.exp(m2 - m1)
    den = 1.0 + e2
    gw_ref[0] = jnp.concatenate([1.0 / den, e2 / den], axis=0)
    mask = jnp.where(sel1 | sel2, 1.0, 0.0)
    before = (lax.broadcasted_iota(jnp.int32, (tm, tm), 0) < lax.broadcasted_iota(jnp.int32, (tm, tm), 1))
    rank = jnp.dot(mask.astype(BF16), jnp.where(before, 1.0, 0.0).astype(BF16), preferred_element_type=F32)
    count = jnp.sum(mask, axis=1, keepdims=True)
    seg = jnp.floor((count + (SEG_ALIGN - 1)) * (1.0 / SEG_ALIGN)) * SEG_ALIGN
    start = jnp.zeros_like(seg)
    for e in range(N_EXPERTS - 1):
        start = start + jnp.where(eidx[:, 0:1] > e, seg[e:e + 1, :], 0.0)
    slot = start + rank
    slot1 = jnp.sum(jnp.where(sel1, slot, 0.0), axis=0, keepdims=True)
    slot2 = jnp.sum(jnp.where(sel2, slot, 0.0), axis=0, keepdims=True)
    slots_ref[0] = jnp.concatenate([slot1, slot2], axis=0).astype(jnp.int32)
    seg_ref[0] = jnp.broadcast_to(seg, (N_EXPERTS, HEAD_DIM)).astype(jnp.int32)


def _route(x, mod, tiles_per_seq, g, w_router, b_router, tm):
    rows, d = x.shape
    mr = mod.shape[1]
    n_tiles = rows // tm
    row_spec = pl.BlockSpec((tm, d), lambda i: (i, 0))
    pair_spec = pl.BlockSpec((1, 2, tm), lambda i: (i, 0, 0))
    return pl.pallas_call(
        functools.partial(_route_kernel, tm=tm),
        grid=(n_tiles,),
        in_specs=[row_spec,
                  pl.BlockSpec((1, mr, 6 * d), lambda i: (i // tiles_per_seq, 0, 0)),
                  pl.BlockSpec((1, d), lambda i: (0, 0)),
                  pl.BlockSpec((N_EXPERTS, d), lambda i: (0, 0)),
                  pl.BlockSpec((N_EXPERTS, 1), lambda i: (0, 0))],
        out_specs=[row_spec, pair_spec, pair_spec,
                   pl.BlockSpec((1, N_EXPERTS, HEAD_DIM), lambda i: (i, 0, 0))],
        out_shape=[jax.ShapeDtypeStruct((rows, d), BF16),
                   jax.ShapeDtypeStruct((n_tiles, 2, tm), jnp.int32),
                   jax.ShapeDtypeStruct((n_tiles, 2, tm), F32),
                   jax.ShapeDtypeStruct((n_tiles, N_EXPERTS, HEAD_DIM), jnp.int32)],
        compiler_params=_cparams("parallel"),
        name="moe_route",
    )(x, mod, g.reshape(1, d), w_router.T, b_router.reshape(N_EXPERTS, 1))


def _segment_copies(tile, seg_ref, loc_ref, goff_ref, packed, sorted_hbm, sem, to_sorted):
    copies = []
    tokens = (packed.shape[0] - N_EXPERTS * SEG_ALIGN) // 2
    max_seg = -(-tokens // SEG_ALIGN) * SEG_ALIGN
    for e in range(N_EXPERTS):
        seg = seg_ref[tile * N_EXPERTS + e]
        loc = loc_ref[tile * N_EXPERTS + e]
        goff = goff_ref[tile * N_EXPERTS + e]
        done = jnp.int32(0)
        for size in (s for s in SEG_SIZES if s <= max_seg):
            vm = packed.at[pl.ds(pl.multiple_of(loc + done, SEG_ALIGN), size), :]
            hb = sorted_hbm.at[pl.ds(pl.multiple_of(goff + done, SEG_ALIGN), size), :]
            src, dst = (vm, hb) if to_sorted else (hb, vm)
            copies.append(((seg & size) != 0, pltpu.make_async_copy(src, dst, sem)))
            done = done + (seg & size)
    return copies


def _start_all(copies):
    for cond, cp in copies:
        pl.when(cond)(cp.start)


def _wait_all(copies):
    for cond, cp in copies:
        pl.when(cond)(cp.wait)


def _dispatch_kernel(seg_ref, loc_ref, goff_ref, h_ref, slots_ref, sorted_in, sorted_out, stage_ref, sems,
                     *, n_slots, n_tiles):
    del sorted_in
    i = pl.program_id(0)
    par = i % 2

    def copies(tile, parity):
        return _segment_copies(tile, seg_ref, loc_ref, goff_ref, stage_ref.at[parity], sorted_out,
                               sems.at[parity], True)

    @pl.when(i >= 2)
    def _():
        _wait_all(copies(i - 2, par))

    tm = h_ref.shape[0]
    srow = lax.broadcasted_iota(jnp.int32, (n_slots, tm), 0)
    onehot = (srow == slots_ref[0, 0:1, :]) | (srow == slots_ref[0, 1:2, :])
    packed = jnp.dot(jnp.where(onehot, 1.0, 0.0).astype(BF16), h_ref[...], preferred_element_type=F32)
    stage_ref[par] = packed.astype(BF16)
    _start_all(copies(i, par))

    @pl.when(i == n_tiles - 1)
    def _():
        if n_tiles >= 2:
            _wait_all(copies(i - 1, 1 - par))
        _wait_all(copies(i, par))


def _dispatch(h, slots, seg, loc, goff, h_sorted, tm):
    rows, d = h.shape
    n_tiles = rows // tm
    n_slots = 2 * tm + N_EXPERTS * SEG_ALIGN
    return pl.pallas_call(
        functools.partial(_dispatch_kernel, n_slots=n_slots, n_tiles=n_tiles),
        grid_spec=pltpu.PrefetchScalarGridSpec(
            num_scalar_prefetch=3,
            grid=(n_tiles,),
            in_specs=[pl.BlockSpec((tm, d), lambda i, *_: (i, 0)),
                      pl.BlockSpec((1, 2, tm), lambda i, *_: (i, 0, 0)),
                      pl.BlockSpec(memory_space=pl.ANY)],
            out_specs=pl.BlockSpec(memory_space=pl.ANY),
            scratch_shapes=[pltpu.VMEM((2, n_slots, d), BF16), pltpu.SemaphoreType.DMA((2,))]),
        out_shape=jax.ShapeDtypeStruct(h_sorted.shape, BF16),
        input_output_aliases={5: 0},
        compiler_params=_cparams("arbitrary"),
        name="moe_dispatch",
    )(seg, loc, goff, h, slots, h_sorted)


def _experts_kernel(te_ref, jb_ref, nact_ref, last_ref, nxt_ref, h_ref, wgu_hbm, wd_hbm, y_ref,
                    wg_ref, wd_ref, sg_ref, su_ref, sd_ref, sems, *, first_expert, d_ff, unit):
    del jb_ref
    j = pl.program_id(0)
    n_units = d_ff // unit
    active = j < nact_ref[0]

    def fetch(expert, u, buf):
        return (pltpu.make_async_copy(wgu_hbm.at[expert, :, pl.ds(u * unit, unit)], sg_ref.at[buf], sems.at[buf]),
                pltpu.make_async_copy(wgu_hbm.at[expert, :, pl.ds(d_ff + u * unit, unit)], su_ref.at[buf],
                                      sems.at[buf]),
                pltpu.make_async_copy(wd_hbm.at[expert, pl.ds(u * unit, unit), :], sd_ref.at[buf], sems.at[buf]))

    def start(expert, u):
        for cp in fetch(expert, u, u % 2):
            cp.start()

    def finish(expert, u):
        for cp in fetch(expert, u, u % 2):
            cp.wait()
        wg_ref[u, :, 0:unit] = sg_ref[u % 2].astype(BF16)
        wg_ref[u, :, unit:2 * unit] = su_ref[u % 2].astype(BF16)
        wd_ref[u] = sd_ref[u % 2].astype(BF16)
        if u + 2 < n_units:
            start(expert, u + 2)

    def unit_out(h, u):
        gate = jnp.dot(h, wg_ref[u, :, 0:unit], preferred_element_type=F32)
        up = jnp.dot(h, wg_ref[u, :, unit:2 * unit], preferred_element_type=F32)
        a = (_silu(gate) * up).astype(BF16)
        return jnp.dot(a, wd_ref[u], preferred_element_type=F32)

    @pl.when(active & (j == 0))
    def _():
        expert = first_expert + te_ref[0]
        start(expert, 0)
        start(expert, 1)
        for u in range(n_units):
            finish(expert, u)

    swap = active & (last_ref[j] == 1) & (nxt_ref[j] >= 0)

    @pl.when(active & jnp.logical_not(swap))
    def _():
        h = h_ref[...]
        acc = unit_out(h, 0)
        for u in range(1, n_units):
            acc = acc + unit_out(h, u)
        y_ref[...] = acc.astype(BF16)

    @pl.when(swap)
    def _():
        nxt = first_expert + nxt_ref[j]
        start(nxt, 0)
        start(nxt, 1)
        h = h_ref[...]
        acc = None
        for u in range(n_units):
            part = unit_out(h, u)
            acc = part if acc is None else acc + part
            finish(nxt, u)
        y_ref[...] = acc.astype(BF16)

    @pl.when(jnp.logical_not(active))
    def _():
        y_ref[...] = jnp.zeros_like(y_ref)


def _experts(h_sorted, tile_expert, tile_block, n_active, tile_last, tile_next, w_gu, w_down, index, tmf):
    n_rows, d = h_sorted.shape
    d_ff = w_down.shape[2]
    unit = EXPERT_FF_UNIT
    n_units = d_ff // unit
    w_gu = w_gu.reshape(-1, d, 2 * d_ff)
    w_down = w_down.reshape(-1, d_ff, d)
    return pl.pallas_call(
        functools.partial(_experts_kernel, first_expert=index * N_EXPERTS, d_ff=d_ff, unit=unit),
        grid_spec=pltpu.PrefetchScalarGridSpec(
            num_scalar_prefetch=5,
            grid=(n_rows // tmf,),
            in_specs=[pl.BlockSpec((tmf, d), lambda j, te, jb, *_: (jb[j], 0)),
                      pl.BlockSpec(memory_space=pl.ANY),
                      pl.BlockSpec(memory_space=pl.ANY)],
            out_specs=pl.BlockSpec((tmf, d), lambda j, *_: (j, 0)),
            scratch_shapes=[pltpu.VMEM((n_units, d, 2 * unit), BF16),
                            pltpu.VMEM((n_units, unit, d), BF16),
                            pltpu.VMEM((2, d, unit), F32),
                            pltpu.VMEM((2, d, unit), F32),
                            pltpu.VMEM((2, unit, d), F32),
                            pltpu.SemaphoreType.DMA((2,))]),
        out_shape=jax.ShapeDtypeStruct((n_rows, d), BF16),
        compiler_params=_cparams("arbitrary"),
        name="moe_experts_sorted",
    )(tile_expert, tile_block, n_active, tile_last, tile_next, h_sorted, w_gu, w_down)


def _combine_kernel(seg_ref, loc_ref, goff_ref, x_ref, mod_ref, slots_ref, gw_ref, sorted_hbm, xo_ref,
                    stage_ref, sems, *, n_slots, n_tiles):
    i = pl.program_id(0)
    par = i % 2

    def copies(tile, parity):
        return _segment_copies(tile, seg_ref, loc_ref, goff_ref, stage_ref.at[parity], sorted_hbm,
                               sems.at[parity], False)

    @pl.when(i == 0)
    def _():
        stage_ref[...] = jnp.zeros_like(stage_ref)
        _start_all(copies(i, par))

    @pl.when(i + 1 < n_tiles)
    def _():
        _start_all(copies(i + 1, 1 - par))

    _wait_all(copies(i, par))
    y = stage_ref[par]
    tm = x_ref.shape[0]
    scol = lax.broadcasted_iota(jnp.int32, (tm, n_slots), 1)
    gates = (jnp.where(scol == slots_ref[:, 0:1], gw_ref[:, 0:1], 0.0)
             + jnp.where(scol == slots_ref[:, 1:2], gw_ref[:, 1:2], 0.0))
    moe = jnp.dot(gates.astype(BF16), y, preferred_element_type=F32)
    xo_ref[...] = x_ref[...] + _mod_part(mod_ref, 5) * moe


def _combine(x, mod, tiles_per_seq, slots_col, gw_col, y_sorted, seg, loc, goff, tm):
    rows, d = x.shape
    mr = mod.shape[1]
    n_tiles = rows // tm
    n_slots = 2 * tm + N_EXPERTS * SEG_ALIGN
    row_spec = pl.BlockSpec((tm, d), lambda i, *_: (i, 0))
    pair_spec = pl.BlockSpec((tm, 2), lambda i, *_: (i, 0))
    return pl.pallas_call(
        functools.partial(_combine_kernel, n_slots=n_slots, n_tiles=n_tiles),
        grid_spec=pltpu.PrefetchScalarGridSpec(
            num_scalar_prefetch=3,
            grid=(n_tiles,),
            in_specs=[row_spec,
                      pl.BlockSpec((1, mr, 6 * d), lambda i, *_: (i // tiles_per_seq, 0, 0)),
                      pair_spec, pair_spec,
                      pl.BlockSpec(memory_space=pl.ANY)],
            out_specs=row_spec,
            scratch_shapes=[pltpu.VMEM((2, n_slots, d), BF16), pltpu.SemaphoreType.DMA((2,))]),
        out_shape=jax.ShapeDtypeStruct((rows, d), F32),
        compiler_params=_cparams("arbitrary"),
        name="moe_combine",
    )(seg, loc, goff, x, mod, slots_col, gw_col, y_sorted)


def _moe_sparse_layer(parts, g, w_router, b_router, w_gu, w_down, index):
    tmf = EXPERT_ROW_TILE
    d = parts[0][0].shape[1]
    routed = [_route(x, mod, tps, g, w_router, b_router, tm) for x, mod, tps, tm in parts]
    tiles = [x.shape[0] // tm for x, _, _, tm in parts]
    seg = jnp.concatenate([r[3][:, :, 0] for r in routed], axis=0)
    loc = jnp.cumsum(seg, axis=1) - seg
    region = ((jnp.sum(seg, axis=0) + tmf - 1) // tmf) * tmf
    region_end = jnp.cumsum(region)
    goff = (region_end - region)[None, :] + jnp.cumsum(seg, axis=0) - seg
    n_sorted = sum(2 * x.shape[0] for x, _, _, _ in parts) + sum(tiles) * N_EXPERTS * (SEG_ALIGN - 1)
    n_sorted += N_EXPERTS * (tmf - SEG_ALIGN)
    n_sorted = ((n_sorted + tmf - 1) // tmf) * tmf
    n_row_tiles = n_sorted // tmf
    n_active = region_end[-1] // tmf
    tile_ids = jnp.arange(n_row_tiles, dtype=jnp.int32)
    tile_block = jnp.maximum(jnp.minimum(tile_ids, n_active - 1), 0)
    tile_expert = jnp.sum((region_end // tmf)[None, :] <= tile_block[:, None], axis=1)
    tile_expert = jnp.minimum(tile_expert, N_EXPERTS - 1).astype(jnp.int32)
    tile_last = (tile_block + 1 == (region_end // tmf)[tile_expert]).astype(jnp.int32)
    ids = jnp.arange(N_EXPERTS, dtype=jnp.int32)
    later = jnp.where((ids[None, :] > ids[:, None]) & (region[None, :] > 0), ids[None, :], N_EXPERTS)
    next_expert = jnp.min(later, axis=1)
    tile_next = jnp.where(next_expert < N_EXPERTS, next_expert, -1)[tile_expert].astype(jnp.int32)
    tables, first = [], 0
    for n in tiles:
        cut = lambda a: a[first:first + n].reshape(-1).astype(jnp.int32)
        tables.append((cut(seg), cut(loc), cut(goff)))
        first += n
    h_sorted = jnp.zeros((n_sorted, d), BF16)
    for (_, _, _, tm), (h, slots, _, _), tab in zip(parts, routed, tables):
        h_sorted = _dispatch(h, slots, *tab, h_sorted, tm)
    y_sorted = _experts(h_sorted, tile_expert, tile_block.astype(jnp.int32),
                        n_active.reshape(1).astype(jnp.int32), tile_last, tile_next, w_gu, w_down, index, tmf)
    outs = []
    for (x, mod, tps, tm), (_, slots, gw, _), tab in zip(parts, routed, tables):
        rows = x.shape[0]
        slots_col = jnp.transpose(slots, (0, 2, 1)).reshape(rows, 2)
        gw_col = jnp.transpose(gw, (0, 2, 1)).reshape(rows, 2)
        outs.append(_combine(x, mod, tps, slots_col, gw_col, y_sorted, *tab, tm))
    return outs


def _rope_tables(pos):
    half = ROT_DIM // 2
    inv = jnp.float32(ROPE_THETA) ** (-jnp.arange(half, dtype=jnp.float32) / half)
    ang = pos.astype(jnp.float32)[:, None] * inv[None, :]
    cos, sin = jnp.cos(ang), jnp.sin(ang)
    t = pos.shape[0]
    pad1 = jnp.ones((t, HEAD_DIM - ROT_DIM), F32)
    pad0 = jnp.zeros((t, HEAD_DIM - ROT_DIM), F32)
    return (jnp.concatenate([cos, cos, pad1], axis=1), jnp.concatenate([-sin, sin, pad0], axis=1))


def _swap_rot_halves(v):
    half = ROT_DIM // 2
    return jnp.concatenate([v[half:ROT_DIM], v[:half], v[ROT_DIM:]])


def _rope_gain_tables(cos_ref, sin_ref, g_ref, gs_ref, scale):
    return cos_ref[...] * (g_ref[...] * scale), sin_ref[...] * (gs_ref[...] * scale)


def _head_norm_rope(xh, a, b):
    half = ROT_DIM // 2
    ones = jnp.ones((HEAD_DIM, HEAD_DIM), BF16)
    ss = jnp.dot((xh * xh).astype(BF16), ones, preferred_element_type=F32)
    rs = lax.rsqrt(ss * (1.0 / HEAD_DIM) + EPS)
    lane = lax.broadcasted_iota(jnp.int32, xh.shape, 1)
    swapped = jnp.where(lane < half, pltpu.roll(xh, HEAD_DIM - half, 1), pltpu.roll(xh, half, 1))
    return (xh * a + swapped * b) * rs


SPLIT = 4
SPLIT_TILE = SPLIT * Q_BLOCK
SPLIT_GROUPS = (1, 2)


def _kv_kernel(x_ref, mod_ref, g_ref, wkv_ref, gk_ref, cos_ref, sin_ref, *out_refs, split):
    x = x_ref[...]
    h = _norm_mod(x, g_ref[...], mod_ref[0, :, 0:D_MODEL], mod_ref[0, :, D_MODEL:2 * D_MODEL]).astype(BF16)
    rope_a, rope_b = _rope_gain_tables(cos_ref, sin_ref, gk_ref.at[0:1], gk_ref.at[1:2], 1.0)
    k_refs, v_refs = out_refs[:N_GROUPS], out_refs[N_GROUPS:2 * N_GROUPS]
    kv_all = jnp.dot(h, wkv_ref[...], preferred_element_type=F32)
    for hd in range(N_KV_HEADS):
        kh = kv_all[:, hd * HEAD_DIM:(hd + 1) * HEAD_DIM]
        vh = kv_all[:, KV_WIDTH + hd * HEAD_DIM:KV_WIDTH + (hd + 1) * HEAD_DIM]
        g_idx, k_idx = divmod(hd, HKV_G)
        sl = slice(k_idx * HEAD_DIM, (k_idx + 1) * HEAD_DIM)
        kh = _head_norm_rope(kh, rope_a, rope_b)
        rows_of_head = pl.ds(k_idx, kh.shape[0], stride=HKV_G)
        k_refs[g_idx][rows_of_head, :] = kh
        v_refs[g_idx][rows_of_head, :] = vh
        if split and g_idx in SPLIT_GROUPS:
            n = SPLIT_GROUPS.index(g_idx)
            scr = out_refs[-1]
            for j, (val, dst) in enumerate(((kh, out_refs[2 * N_GROUPS + 2 * n]),
                                            (vh, out_refs[2 * N_GROUPS + 2 * n + 1]))):
                slot = (n * HKV_G + k_idx) * 2 + j
                scr[slot] = val
                for c in range(SPLIT):
                    dst[c * Q_BLOCK:(c + 1) * Q_BLOCK, sl] = (
                        scr[slot, pl.ds(c, Q_BLOCK, stride=SPLIT), :].astype(BF16))


def _kv_proj(x, mod_kv, tiles_per_seq, g, w_kv, g_k, cos, sin, tm, split):
    rows, d = x.shape
    mr = mod_kv.shape[1]
    assert not split or tm == SPLIT_TILE
    row_spec = pl.BlockSpec((tm, d), lambda i: (i, 0))
    tab_spec = pl.BlockSpec((tm, HEAD_DIM), lambda i: (i % tiles_per_seq, 0))
    out_spec = pl.BlockSpec((tm, GROUP_KV_WIDTH), lambda i: (i, 0))
    nat_spec = pl.BlockSpec((HKV_G * tm, HEAD_DIM), lambda i: (i, 0))
    n_split = 2 * len(SPLIT_GROUPS) if split else 0
    return pl.pallas_call(
        functools.partial(_kv_kernel, split=split),
        grid=(rows // tm,),
        in_specs=[row_spec,
                  pl.BlockSpec((1, mr, 2 * d), lambda i: (i // tiles_per_seq, 0, 0)),
                  pl.BlockSpec((1, d), lambda i: (0, 0)),
                  _resident((d, 2 * KV_WIDTH), lambda i: (0, 0)),
                  pl.BlockSpec((2, HEAD_DIM), lambda i: (0, 0)),
                  tab_spec, tab_spec],
        out_specs=[nat_spec] * (2 * N_GROUPS) + [out_spec] * n_split,
        out_shape=[jax.ShapeDtypeStruct((HKV_G * rows, HEAD_DIM), F32)] * (2 * N_GROUPS)
                  + [jax.ShapeDtypeStruct((rows, GROUP_KV_WIDTH), BF16)] * n_split,
        scratch_shapes=[pltpu.VMEM((n_split * HKV_G, tm, HEAD_DIM), F32)] if split else [],
        compiler_params=_cparams("parallel"),
        name="kv_proj",
    )(x, mod_kv, g.reshape(1, d), w_kv, jnp.stack([g_k, _swap_rot_halves(g_k)]), cos, sin)


def _q_kernel(x_ref, mod_ref, g_ref, wq_ref, gq_ref, cos_ref, sin_ref, *refs, split):
    q_refs = refs[:N_GROUPS]
    x = x_ref[...]
    h = _norm_mod(x, g_ref[...], _mod_part(mod_ref, 0), _mod_part(mod_ref, 1)).astype(BF16)
    rope_a, rope_b = _rope_gain_tables(cos_ref, sin_ref, gq_ref.at[0:1], gq_ref.at[1:2], ATTN_SCALE)
    q_all = jnp.dot(h, wq_ref[...], preferred_element_type=F32)
    for hd in range(N_Q_HEADS):
        g_idx, h_idx = divmod(hd, HQ_G)
        sl = slice(h_idx * HEAD_DIM, (h_idx + 1) * HEAD_DIM)
        qh = _head_norm_rope(q_all[:, hd * HEAD_DIM:(hd + 1) * HEAD_DIM], rope_a, rope_b)
        if split and g_idx in SPLIT_GROUPS:
            scr = refs[N_GROUPS]
            n = SPLIT_GROUPS.index(g_idx) * HQ_G + h_idx
            scr[n] = qh
            for c in range(SPLIT):
                q_refs[g_idx][c * Q_BLOCK:(c + 1) * Q_BLOCK, sl] = (
                    scr[n, pl.ds(c, Q_BLOCK, stride=SPLIT), :].astype(BF16))
        else:
            q_refs[g_idx][:, sl] = qh.astype(BF16)


def _q_proj(x, mod, tiles_per_seq, g, w_q, index, g_q, cos, sin, tm, split):
    rows, d = x.shape
    mr = mod.shape[1]
    assert not split or tm == SPLIT_TILE
    row_spec = pl.BlockSpec((tm, d), lambda i: (i, 0))
    tab_spec = pl.BlockSpec((tm, HEAD_DIM), lambda i: (i % tiles_per_seq, 0))
    return pl.pallas_call(
        functools.partial(_q_kernel, split=split),
        grid=(rows // tm,),
        in_specs=[row_spec,
                  pl.BlockSpec((1, mr, 6 * d), lambda i: (i // tiles_per_seq, 0, 0)),
                  pl.BlockSpec((1, d), lambda i: (0, 0)),
                  _stacked(w_q, index),
                  pl.BlockSpec((2, HEAD_DIM), lambda i: (0, 0)),
                  tab_spec, tab_spec],
        out_specs=[pl.BlockSpec((tm, O_WIDTH), lambda i: (i, 0))] * N_GROUPS,
        out_shape=[jax.ShapeDtypeStruct((rows, O_WIDTH), BF16)] * N_GROUPS,
        scratch_shapes=[pltpu.VMEM((len(SPLIT_GROUPS) * HQ_G, tm, HEAD_DIM), F32)] if split else [],
        compiler_params=_cparams("parallel"),
        name="q_proj",
    )(x, mod, g.reshape(1, d), w_q, jnp.stack([g_q, _swap_rot_halves(g_q)]), cos, sin)


def _attn_prompt_kernel(*refs, step, has_prev, n_blk):
    if has_prev:
        q_ref, k_ref, v_ref, kp_ref, vp_ref, o_ref, st_ref = refs
        not_first = pl.program_id(1) > 0
    else:
        q_ref, k_ref, v_ref, o_ref, st_ref = refs
    nq = REP * Q_BLOCK
    lane = lax.broadcasted_iota(jnp.int32, (Q_BLOCK, HEAD_DIM), 1)
    max_kb = 2 if step == 1 else n_blk
    col = lax.broadcasted_iota(jnp.int32, (nq, max_kb * Q_BLOCK), 1)
    dist = (max_kb - 1) * Q_BLOCK + (lax.broadcasted_iota(jnp.int32, col.shape, 0) & (Q_BLOCK - 1)) - col
    reach = (dist >= 0) & (dist <= step * Q_BLOCK) & ((dist & (step - 1)) == 0)
    for i in range(n_blk):
        first_kb = i - 1 if step == 1 else 0
        if first_kb < 0 and not has_prev:
            first_kb = 0
        kbs = list(range(first_kb, i + 1))
        valid = reach[:, (max_kb - len(kbs)) * Q_BLOCK:]
        if first_kb < 0:
            valid = valid & (not_first | (col >= Q_BLOCK))
        for k in range(HKV_G):
            ksl = slice(k * HEAD_DIM, (k + 1) * HEAD_DIM)
            q = jnp.concatenate(
                [q_ref[i, :, (k * REP + r) * HEAD_DIM:(k * REP + r + 1) * HEAD_DIM] for r in range(REP)], axis=0)
            if has_prev:
                hrows = pl.ds(k, Q_BLOCK, stride=HKV_G)
                keys = jnp.concatenate([kp_ref[hrows, :] if j < 0 else k_ref[j, hrows, :] for j in kbs], axis=0)
                vals = jnp.concatenate([vp_ref[hrows, :] if j < 0 else v_ref[j, hrows, :] for j in kbs], axis=0)
            else:
                keys = jnp.concatenate([k_ref[j, :, ksl] for j in kbs], axis=0)
                vals = jnp.concatenate([v_ref[j, :, ksl] for j in kbs], axis=0)
            s = lax.dot_general(q, keys.astype(BF16), (((1,), (1,)), ((), ())), preferred_element_type=F32)
            s = jnp.where(valid, s, NEG)
            m = jnp.max(s, axis=1, keepdims=True)
            p = jnp.exp(s - m)
            den = jnp.sum(p, axis=1, keepdims=True)
            out = jnp.dot(p.astype(BF16), vals.astype(BF16), preferred_element_type=F32) / den
            lse = m + jnp.log(den)
            stats = jnp.zeros((Q_BLOCK, HEAD_DIM), F32)
            for r in range(REP):
                rs = slice(r * Q_BLOCK, (r + 1) * Q_BLOCK)
                o_ref[i, :, (k * REP + r) * HEAD_DIM:(k * REP + r + 1) * HEAD_DIM] = out[rs].astype(BF16)
                stats = jnp.where(lane == r, lse[rs], stats)
            st_ref[i, :, ksl] = stats


def _attn_prompt_group(q, k_g, v_g, group, batch, seq):
    _, dil = DIL_GROUPS[group]
    rows = batch * seq
    n_tiles = seq // SPLIT_TILE
    if group in SPLIT_GROUPS:
        step = dil // SPLIT
        lead = (batch, n_tiles, SPLIT, Q_BLOCK)
        view = lambda a: a.reshape(*lead, a.shape[-1])
        spec = lambda w: pl.BlockSpec((None, n_tiles, None, Q_BLOCK, w), lambda b, c: (b, 0, c, 0, 0))
        out_shape = [jax.ShapeDtypeStruct((*lead, O_WIDTH), BF16),
                     jax.ShapeDtypeStruct((*lead, GROUP_KV_WIDTH), F32)]
        out, stats = pl.pallas_call(
            functools.partial(_attn_prompt_kernel, step=step, has_prev=False, n_blk=n_tiles),
            grid=(batch, SPLIT),
            in_specs=[spec(O_WIDTH), spec(GROUP_KV_WIDTH), spec(GROUP_KV_WIDTH)],
            out_specs=[spec(O_WIDTH), spec(GROUP_KV_WIDTH)],
            out_shape=out_shape,
            compiler_params=_cparams("parallel", "parallel"),
            name=f"attn_prompt_g{group}",
        )(view(q), view(k_g), view(v_g))
    else:
        lead = (batch * n_tiles, SPLIT)
        view = lambda a, r: a.reshape(*lead, r, a.shape[-1])
        spec = lambda r, w: pl.BlockSpec((None, SPLIT, r, w), lambda b, i: (b * n_tiles + i, 0, 0, 0))
        kv_rows = HKV_G * Q_BLOCK
        kv_spec = spec(kv_rows, HEAD_DIM)
        prev = pl.BlockSpec((None, None, kv_rows, HEAD_DIM),
                            lambda b, i: (jnp.maximum(b * n_tiles + i - 1, 0), SPLIT - 1, 0, 0))
        out_shape = [jax.ShapeDtypeStruct((*lead, Q_BLOCK, O_WIDTH), BF16),
                     jax.ShapeDtypeStruct((*lead, Q_BLOCK, GROUP_KV_WIDTH), F32)]
        k4, v4 = view(k_g, kv_rows), view(v_g, kv_rows)
        out, stats = pl.pallas_call(
            functools.partial(_attn_prompt_kernel, step=dil, has_prev=True, n_blk=SPLIT),
            grid=(batch, n_tiles),
            in_specs=[spec(Q_BLOCK, O_WIDTH), kv_spec, kv_spec, prev, prev],
            out_specs=[spec(Q_BLOCK, O_WIDTH), spec(Q_BLOCK, GROUP_KV_WIDTH)],
            out_shape=out_shape,
            compiler_params=_cparams("parallel", "arbitrary"),
            name=f"attn_prompt_g{group}",
        )(view(q, Q_BLOCK), k4, v4, k4, v4)
    return out.reshape(rows, O_WIDTH), stats.reshape(rows, GROUP_KV_WIDTH)


def _merge_o_kernel(x_ref, mod_ref, o0_ref, o1_ref, o2_ref, s0_ref, s1_ref, s2_ref, wo_ref, xo_ref,
                    on_ref, sn_ref):
    for n, (o_ref, s_ref) in enumerate(((o1_ref, s1_ref), (o2_ref, s2_ref))):
        for c in range(SPLIT):
            rows = slice(c * Q_BLOCK, (c + 1) * Q_BLOCK)
            dst = pl.ds(c, Q_BLOCK, stride=SPLIT)
            for hd in range(HQ_G):
                on_ref[n, hd, dst, :] = o_ref[rows, hd * HEAD_DIM:(hd + 1) * HEAD_DIM].astype(F32)
            for k in range(HKV_G):
                sn_ref[n, k, dst, :] = s_ref[rows, k * HEAD_DIM:(k + 1) * HEAD_DIM]
    heads = []
    for k in range(HKV_G):
        for r in range(REP):
            hd = k * REP + r
            lses = [s0_ref[:, k * HEAD_DIM + r:k * HEAD_DIM + r + 1],
                    sn_ref[0, k, :, r:r + 1], sn_ref[1, k, :, r:r + 1]]
            m = jnp.maximum(jnp.maximum(lses[0], lses[1]), lses[2])
            es = [jnp.exp(l - m) for l in lses]
            den = es[0] + es[1] + es[2]
            sl = slice(hd * HEAD_DIM, (hd + 1) * HEAD_DIM)
            outs = [o0_ref[:, sl].astype(F32), on_ref[0, hd], on_ref[1, hd]]
            o = sum((e / den) * o_g for e, o_g in zip(es, outs))
            heads.append(o.astype(BF16))
    o = jnp.concatenate(heads, axis=1)
    y = jnp.dot(o, wo_ref[...], preferred_element_type=F32)
    xo_ref[...] = x_ref[...] + _mod_part(mod_ref, 2) * y


def _merge_o(x, mod, tiles_per_seq, outs, stats, w_o, index, tm):
    rows, d = x.shape
    mr = mod.shape[1]
    assert tm == SPLIT_TILE
    row_spec = pl.BlockSpec((tm, d), lambda i: (i, 0))
    o_spec = pl.BlockSpec((tm, O_WIDTH), lambda i: (i, 0))
    st_spec = pl.BlockSpec((tm, GROUP_KV_WIDTH), lambda i: (i, 0))
    return pl.pallas_call(
        _merge_o_kernel,
        grid=(rows // tm,),
        in_specs=[row_spec,
                  pl.BlockSpec((1, mr, 6 * d), lambda i: (i // tiles_per_seq, 0, 0))]
                 + [o_spec] * 3 + [st_spec] * 3 + [_stacked(w_o, index)],
        out_specs=row_spec,
        out_shape=jax.ShapeDtypeStruct((rows, d), F32),
        scratch_shapes=[pltpu.VMEM((len(SPLIT_GROUPS), HQ_G, tm, HEAD_DIM), F32),
                        pltpu.VMEM((len(SPLIT_GROUPS), HKV_G, tm, HEAD_DIM), F32)],
        compiler_params=_cparams("parallel"),
        name="merge_o_proj",
    )(x, mod, *outs, *stats, w_o)


def _attn_sample_kernel(*refs, n_new, write_buffers):
    q_ref = refs[0]
    nk_refs = refs[1:1 + N_GROUPS]
    nv_refs = refs[1 + N_GROUPS:1 + 2 * N_GROUPS]
    ck_refs = refs[1 + 2 * N_GROUPS:1 + 3 * N_GROUPS]
    cv_refs = refs[1 + 3 * N_GROUPS:1 + 4 * N_GROUPS]
    o_ref = refs[1 + 4 * N_GROUPS]
    nq = REP * n_new
    n_new_rows = HKV_G * n_new
    qi = lax.broadcasted_iota(jnp.int32, (nq, 1), 0) & (n_new - 1)

    if write_buffers:
        ok_refs = refs[2 + 4 * N_GROUPS:2 + 5 * N_GROUPS]
        ov_refs = refs[2 + 5 * N_GROUPS:2 + 6 * N_GROUPS]
        for g in range(N_GROUPS):
            for c_ref, n_ref, o_buf in ((ck_refs[g], nk_refs[g], ok_refs[g]), (cv_refs[g], nv_refs[g], ov_refs[g])):
                n_rows = c_ref.shape[1]
                o_buf[0, 0:n_rows - n_new_rows, :] = c_ref[0, n_new_rows:n_rows, :]
                o_buf[0, n_rows - n_new_rows:n_rows, :] = n_ref[0]

    for k in range(HKV_G):
        outs, lses = [], []
        for g, (window, dil) in enumerate(DIL_GROUPS):
            length = ck_refs[g].shape[1] // HKV_G
            q = q_ref[0, g, k]
            qf = q.astype(F32)
            keys = ck_refs[g][0, pl.ds(k, length, stride=HKV_G), :].astype(BF16)
            vals = cv_refs[g][0, pl.ds(k, length, stride=HKV_G), :].astype(BF16)
            s = lax.dot_general(q, keys, (((1,), (1,)), ((), ())), preferred_element_type=F32)
            idx = lax.broadcasted_iota(jnp.int32, (nq, length), 1)
            diff = length + qi - idx
            s = jnp.where(((diff & (dil - 1)) == 0) & (diff <= window), s, NEG)
            new_k = nk_refs[g][0]
            new_v = nv_refs[g][0]
            s_new = []
            for j in range(n_new):
                sj = jnp.sum(qf * new_k[HKV_G * j + k:HKV_G * j + k + 1, :], axis=1, keepdims=True)
                ok = (qi >= j) & (((qi - j) & (dil - 1)) == 0)
                s_new.append(jnp.where(ok, sj, NEG))
            m = jnp.max(s, axis=1, keepdims=True)
            for sj in s_new:
                m = jnp.maximum(m, sj)
            p = jnp.exp(s - m)
            den = jnp.sum(p, axis=1, keepdims=True)
            acc = jnp.dot(p.astype(BF16), vals, preferred_element_type=F32)
            for j, sj in enumerate(s_new):
                pj = jnp.exp(sj - m)
                den = den + pj
                acc = acc + pj * new_v[HKV_G * j + k:HKV_G * j + k + 1, :]
            outs.append(acc / den)
            lses.append(m + jnp.log(den))
        m = jnp.maximum(jnp.maximum(lses[0], lses[1]), lses[2])
        es = [jnp.exp(l - m) for l in lses]
        den = es[0] + es[1] + es[2]
        o_ref[0, k] = sum((e / den) * o for e, o in zip(es, outs))


def _attn_sample(q, new_k, new_v, cache_k, cache_v, n_seq, n_new, write_buffers):
    assert HKV_G * n_new == 8
    q5 = q.reshape(n_seq, n_new, N_GROUPS, HKV_G, REP, HEAD_DIM)
    q5 = jnp.transpose(q5, (0, 2, 3, 4, 1, 5)).reshape(n_seq, N_GROUPS, HKV_G, REP * n_new, HEAD_DIM)
    rows2 = lambda a, n: a.reshape(n_seq, n * HKV_G, HEAD_DIM)
    nk = [rows2(a, n_new) for a in new_k]
    nv = [rows2(a, n_new) for a in new_v]
    ck = [rows2(c, c.shape[1]) for c in cache_k]
    cv = [rows2(c, c.shape[1]) for c in cache_v]
    new_spec = pl.BlockSpec((1, HKV_G * n_new, HEAD_DIM), lambda b: (b, 0, 0))
    cache_specs = [pl.BlockSpec((1, c.shape[1], HEAD_DIM), lambda b: (b, 0, 0)) for c in ck]
    cache_shapes = [jax.ShapeDtypeStruct(c.shape, F32) for c in ck]
    n_buf = 2 if write_buffers else 0
    res = pl.pallas_call(
        functools.partial(_attn_sample_kernel, n_new=n_new, write_buffers=write_buffers),
        grid=(n_seq,),
        in_specs=[pl.BlockSpec((1, N_GROUPS, HKV_G, REP * n_new, HEAD_DIM), lambda b: (b, 0, 0, 0, 0))]
                 + [new_spec] * (2 * N_GROUPS) + cache_specs + cache_specs,
        out_specs=[pl.BlockSpec((1, HKV_G, REP * n_new, HEAD_DIM), lambda b: (b, 0, 0, 0))]
                  + cache_specs * n_buf,
        out_shape=[jax.ShapeDtypeStruct((n_seq, HKV_G, REP * n_new, HEAD_DIM), F32)] + cache_shapes * n_buf,
        compiler_params=_cparams("parallel"),
        name="attn_sample",
    )(q5, *nk, *nv, *ck, *cv)
    o = res[0].reshape(n_seq, HKV_G, REP, n_new, HEAD_DIM)
    o = jnp.transpose(o, (0, 3, 1, 2, 4)).reshape(n_seq * n_new, O_WIDTH)
    if not write_buffers:
        return o, None, None
    shape4 = lambda a: a.reshape(n_seq, a.shape[1] // HKV_G, HKV_G, HEAD_DIM)
    new_ck = [shape4(a) for a in res[1:1 + N_GROUPS]]
    new_cv = [shape4(a) for a in res[1 + N_GROUPS:1 + 2 * N_GROUPS]]
    return o, new_ck, new_cv


def _o_proj_kernel(x_ref, mod_ref, o_ref, wo_ref, xo_ref):
    y = jnp.dot(o_ref[...].astype(BF16), wo_ref[...], preferred_element_type=F32)
    xo_ref[...] = x_ref[...] + _mod_part(mod_ref, 2) * y


def _o_proj(x, mod, o, w_o, index):
    rows, d = x.shape
    mr = mod.shape[1]
    full = pl.BlockSpec((rows, d), lambda i: (0, 0))
    return pl.pallas_call(
        _o_proj_kernel,
        grid=(1,),
        in_specs=[full, pl.BlockSpec((1, mr, 6 * d), lambda i: (0, 0, 0)),
                  pl.BlockSpec((rows, O_WIDTH), lambda i: (0, 0)),
                  _stacked(w_o, index)],
        out_specs=full,
        out_shape=jax.ShapeDtypeStruct((rows, d), F32),
        compiler_params=_cparams("arbitrary"),
        name="o_proj_sample",
    )(x, mod, o, w_o)


def _trunk(x, mods, mod_kv, pos, weights, tm, conv_prev, kv_prev):
    (g_norm_mix, g_norm_ffn, w_in, w_conv, w_out_conv, g_norm_kv, w_kv, g_k, w_q, g_q, w_o,
     w_gu_dense, w_down_dense) = weights
    batch, seq, d = x.shape
    rows = batch * seq
    depth = g_norm_mix.shape[0]
    n_conv = w_in.shape[0]
    sample = kv_prev is not None
    tiles_per_seq = 1 if sample else seq // tm
    x = x.reshape(rows, d)
    cos, sin = _rope_tables(pos)
    if sample:
        cos, sin = jnp.tile(cos, (batch, 1)), jnp.tile(sin, (batch, 1))
    conv_state = []
    kv_state = None
    for layer in range(depth):
        mod = mods[layer]
        if layer == n_conv:
            kvs = _kv_proj(x, mod_kv, tiles_per_seq, g_norm_kv, w_kv, g_k, cos, sin, tm, not sample)
            k_new, v_new = kvs[:N_GROUPS], kvs[N_GROUPS:2 * N_GROUPS]
            k_att, v_att = list(k_new), list(v_new)
            for n, g_idx in enumerate(SPLIT_GROUPS if not sample else ()):
                k_att[g_idx], v_att[g_idx] = kvs[2 * N_GROUPS + 2 * n], kvs[2 * N_GROUPS + 2 * n + 1]
        if layer < n_conv:
            if sample:
                st = conv_prev[layer]
                zero = jnp.zeros((batch, seq - 2, d), F32)
                pa = jnp.concatenate([st[:, 1:2], jnp.zeros((batch, seq - 1, d), F32)], axis=1)
                pb = jnp.concatenate([st, zero], axis=1)
                prev = (seq, pa.reshape(rows, d), pb.reshape(rows, d))
            else:
                prev = None
            x, u_tail = _conv_layer(x, mod, tiles_per_seq, g_norm_mix[layer], w_in, w_conv, w_out_conv, layer,
                                    tm, prev)
            if sample:
                conv_state.append(u_tail.reshape(batch, seq, d)[:, seq - (CONV_W - 1):])
            else:
                conv_state.append(u_tail[:, 8 - (CONV_W - 1):])
        else:
            lb = layer - n_conv
            qs = _q_proj(x, mod, tiles_per_seq, g_norm_mix[layer], w_q, lb, g_q[lb], cos, sin, tm, not sample)
            if sample:
                o, new_ck, new_cv = _attn_sample(jnp.concatenate(qs, axis=1), k_new, v_new, kv_prev[0],
                                                 kv_prev[1], batch, seq, kv_state is None)
                if kv_state is None:
                    kv_state = (new_ck, new_cv)
                x = _o_proj(x, mod, o, w_o, lb)
            else:
                res = [_attn_prompt_group(qs[g], k_att[g], v_att[g], g, batch, seq) for g in range(N_GROUPS)]
                x = _merge_o(x, mod, tiles_per_seq, [r[0] for r in res], [r[1] for r in res], w_o, lb, tm)
        if layer % 2 == 0:
            x = _ffn_layer(x, mod, tiles_per_seq, g_norm_ffn[layer], w_gu_dense, w_down_dense, layer // 2, tm)
        else:
            tm_moe = tm if sample else min(SPARSE_TILE, seq)
            x = yield (x, mod, 1 if sample else seq // tm_moe, tm_moe), layer
    if not sample:
        shape4 = lambda a: a.reshape(batch, seq, HKV_G, HEAD_DIM)
        kv_state = ([shape4(k)[:, -min(w, seq):] for k, (w, _) in zip(k_new, DIL_GROUPS)],
                    [shape4(v)[:, -min(w, seq):] for v, (w, _) in zip(v_new, DIL_GROUPS)])
    return x.reshape(batch, seq, d), jnp.stack(conv_state, axis=0), kv_state


def _resume(trunk, x):
    try:
        return False, trunk.send(x)
    except StopIteration as done:
        return True, done.value


def kernel(x_prompt, x_sample, state_conv, cache_k_g0, cache_v_g0, cache_k_g1, cache_v_g1, cache_k_g2,
           cache_v_g2, c_prompt, c_sample, g_norm_mix, g_norm_ffn, w_ada, b_ada, w_in, w_conv, w_out_conv,
           g_norm_kv, w_ada_kv, b_ada_kv, w_kv, g_k, w_q, g_q, w_o, w_gu_dense, w_down_dense, w_router,
           b_router, w_gu_moe, w_down_moe):
    b_p, t_p, d = x_prompt.shape
    b_s, t_s, _ = x_sample.shape
    bf = lambda w: w.astype(BF16)
    weights = (g_norm_mix, g_norm_ffn, bf(w_in), w_conv, bf(w_out_conv), g_norm_kv, bf(w_kv), g_k, bf(w_q),
               g_q, bf(w_o), bf(w_gu_dense), bf(w_down_dense))

    c_all = jnp.concatenate([c_prompt, c_sample], axis=0)
    mods_all = _ada(c_all, w_ada, b_ada)
    mod_kv_all = _ada(c_all, w_ada_kv[None], b_ada_kv[None])[0]
    depth = w_ada.shape[0]
    mods_p = [mods_all[l, :b_p, None, :] for l in range(depth)]
    mods_s = [jnp.repeat(mods_all[l, b_p:], t_s, axis=0)[None] for l in range(depth)]
    mod_kv_p = mod_kv_all[:b_p, None, :]
    mod_kv_s = jnp.repeat(mod_kv_all[b_p:], t_s, axis=0)[None]

    pos_p = jnp.arange(t_p, dtype=jnp.int32)
    pos_s = PAST_LEN + jnp.arange(t_s, dtype=jnp.int32)
    kv_prev = ([cache_k_g0, cache_k_g1, cache_k_g2], [cache_v_g0, cache_v_g1, cache_v_g2])
    trunks = [_trunk(x_prompt, mods_p, mod_kv_p, pos_p, weights, min(PROMPT_TILE, t_p), None, None),
              _trunk(x_sample, mods_s, mod_kv_s, pos_s, weights, b_s * t_s, state_conv, kv_prev)]
    requests = [next(t) for t in trunks]
    results = None
    while results is None:
        layer = requests[0][1]
        xs = _moe_sparse_layer([r[0] for r in requests], g_norm_ffn[layer], w_router[layer // 2],
                               b_router[layer // 2], w_gu_moe, w_down_moe, layer // 2)
        steps = [_resume(t, x) for t, x in zip(trunks, xs)]
        if all(done for done, _ in steps):
            results = [value for _, value in steps]
        else:
            requests = [value for _, value in steps]
    (y_prompt, conv_p, (kp, vp)), (y_sample, conv_s, (ksn, vsn)) = results
    return (y_prompt, y_sample, conv_p, conv_s,
            kp[0], vp[0], kp[1], vp[1], kp[2], vp[2],
            ksn[0], vsn[0], ksn[1], vsn[1], ksn[2], vsn[2])
```

```python
import functools

import jax
import jax.numpy as jnp
from jax import lax
from jax.experimental import pallas as pl
from jax.experimental.pallas import tpu as pltpu

D_MODEL = 1024
PAST_LEN = 8192
CONV_W = 3
HEAD_DIM = 128
ROT_DIM = HEAD_DIM // 4
ROPE_THETA = 500000.0
DIL_GROUPS = ((128, 1), (512, 4), (2048, 16))
N_GROUPS = len(DIL_GROUPS)
HQ_G = 8
HKV_G = 2
REP = HQ_G // HKV_G
N_Q_HEADS = N_GROUPS * HQ_G
N_KV_HEADS = N_GROUPS * HKV_G
Q_WIDTH = N_Q_HEADS * HEAD_DIM
KV_WIDTH = N_KV_HEADS * HEAD_DIM
O_WIDTH = HQ_G * HEAD_DIM
GROUP_KV_WIDTH = HKV_G * HEAD_DIM
N_EXPERTS = 8
EPS = 1e-6
ATTN_SCALE = HEAD_DIM ** -0.5
NEG = -1e30
Q_BLOCK = 128

F32 = jnp.float32
BF16 = jnp.bfloat16
HIGHEST = lax.Precision.HIGHEST

V7X_VMEM_LIMIT_BYTES = 56 * 1024 * 1024
V7X_MXU_WIDTH = 256
PROMPT_TILE = 512


def _cparams(*sem):
    return pltpu.CompilerParams(dimension_semantics=sem, vmem_limit_bytes=V7X_VMEM_LIMIT_BYTES)


def _resident(shape, index_map):
    return pl.BlockSpec(shape, index_map, pipeline_mode=pl.Buffered(1))


def _stacked(w, index):
    shape = w.shape[1:]
    return pl.BlockSpec((None,) + shape, lambda *_: (index,) + (0,) * len(shape), pipeline_mode=pl.Buffered(1))


def _silu(x):
    return x * jax.nn.sigmoid(x)


def _norm_mod(x, g, shift, scale):
    y = x * lax.rsqrt(jnp.mean(x * x, axis=-1, keepdims=True) + EPS) * g
    return y * (1.0 + scale) + shift


def _mod_part(mod_ref, k):
    return mod_ref[0, :, k * D_MODEL:(k + 1) * D_MODEL]


def _ada_kernel(c_ref, w_ref, b_ref, o_ref):
    c = c_ref[...]
    o_ref[0] = jnp.dot(_silu(c), w_ref[0], preferred_element_type=F32, precision=HIGHEST) + b_ref[0]


def _ada(c, w, b):
    n_layers, d, n = w.shape
    m = c.shape[0]
    tn = 2048
    return pl.pallas_call(
        _ada_kernel,
        grid=(n_layers, n // tn),
        in_specs=[pl.BlockSpec((m, d), lambda l, j: (0, 0)),
                  pl.BlockSpec((1, d, tn), lambda l, j: (l, 0, j)),
                  pl.BlockSpec((1, 1, tn), lambda l, j: (l, 0, j))],
        out_specs=pl.BlockSpec((1, m, tn), lambda l, j: (l, 0, j)),
        out_shape=jax.ShapeDtypeStruct((n_layers, m, n), F32),
        compiler_params=_cparams("parallel", "parallel"),
        name="ada",
    )(c, w, b.reshape(n_layers, 1, n))


def _conv_kernel(*refs, tm, tiles_per_seq, seq_rows):
    if seq_rows is None:
        x_ref, mod_ref, g_ref, win_ref, wc_ref, wout_ref, xo_ref, ust_ref, carry_ref = refs
    else:
        x_ref, mod_ref, g_ref, win_ref, wc_ref, wout_ref, pa_ref, pb_ref, xo_ref, ust_ref = refs
    x = x_ref[...]
    h = _norm_mod(x, g_ref[...], _mod_part(mod_ref, 0), _mod_part(mod_ref, 1)).astype(BF16)
    b_gate = jnp.dot(h, win_ref[:, 0:D_MODEL], preferred_element_type=F32)
    c_gate = jnp.dot(h, win_ref[:, D_MODEL:2 * D_MODEL], preferred_element_type=F32)
    v = jnp.dot(h, win_ref[:, 2 * D_MODEL:3 * D_MODEL], preferred_element_type=F32)
    u = c_gate * v
    row = lax.broadcasted_iota(jnp.int32, (tm, 1), 0)
    r1 = pltpu.roll(u, 1, 0)
    r2 = pltpu.roll(u, 2, 0)
    if seq_rows is None:
        @pl.when(pl.program_id(0) % tiles_per_seq == 0)
        def _():
            carry_ref[...] = jnp.zeros_like(carry_ref)
        um1 = jnp.where(row == 0, carry_ref[7:8, :], r1)
        um2 = jnp.where(row == 0, carry_ref[6:7, :], jnp.where(row == 1, carry_ref[7:8, :], r2))
        carry_ref[...] = u[tm - 8:tm, :]
        ust_ref[0] = u[tm - 8:tm, :]
    else:
        rowm = row & (seq_rows - 1)
        um1 = jnp.where(rowm == 0, pa_ref[...], r1)
        um2 = jnp.where(rowm < 2, pb_ref[...], r2)
        ust_ref[...] = u
    y = um2 * wc_ref[0:1, :] + um1 * wc_ref[1:2, :] + u * wc_ref[2:3, :]
    z = (b_gate * y).astype(BF16)
    out = jnp.dot(z, wout_ref[...], preferred_element_type=F32)
    xo_ref[...] = x + _mod_part(mod_ref, 2) * out


def _conv_layer(x, mod, tiles_per_seq, g, w_in, w_conv, w_out, layer, tm, prev=None):
    rows, d = x.shape
    n_tiles = rows // tm
    mr = mod.shape[1]
    row_spec = pl.BlockSpec((tm, d), lambda i: (i, 0))
    in_specs = [row_spec,
                pl.BlockSpec((1, mr, 6 * d), lambda i: (i // tiles_per_seq, 0, 0)),
                pl.BlockSpec((1, d), lambda i: (0, 0)),
                _stacked(w_in, layer), _stacked(w_conv, layer), _stacked(w_out, layer)]
    args = [x, mod, g.reshape(1, d), w_in, w_conv, w_out]
    if prev is None:
        seq_rows = None
        n_seq = n_tiles // tiles_per_seq
        ust_shape = jax.ShapeDtypeStruct((n_seq, 8, d), F32)
        ust_spec = pl.BlockSpec((1, 8, d), lambda i: (i // tiles_per_seq, 0, 0))
        scratch = [pltpu.VMEM((8, d), F32)]
    else:
        seq_rows, pa, pb = prev
        in_specs += [row_spec, row_spec]
        args += [pa, pb]
        ust_shape = jax.ShapeDtypeStruct((rows, d), F32)
        ust_spec = row_spec
        scratch = []
    return pl.pallas_call(
        functools.partial(_conv_kernel, tm=tm, tiles_per_seq=tiles_per_seq, seq_rows=seq_rows),
        grid=(n_tiles,),
        in_specs=in_specs,
        out_specs=[row_spec, ust_spec],
        out_shape=[jax.ShapeDtypeStruct((rows, d), F32), ust_shape],
        scratch_shapes=scratch,
        compiler_params=_cparams("arbitrary"),
        name="conv_layer",
    )(*args)


def _ffn_kernel(x_ref, mod_ref, g_ref, wgu_ref, wd_ref, xo_ref, *, d_ff, n_chunks):
    x = x_ref[...]
    h = _norm_mod(x, g_ref[...], _mod_part(mod_ref, 3), _mod_part(mod_ref, 4)).astype(BF16)
    n_mxu = d_ff // V7X_MXU_WIDTH
    edges = [(c * n_mxu // n_chunks) * V7X_MXU_WIDTH for c in range(n_chunks)] + [d_ff]
    acc = None
    for lo, hi in zip(edges[:-1], edges[1:]):
        gate = jnp.dot(h, wgu_ref[:, lo:hi], preferred_element_type=F32)
        up = jnp.dot(h, wgu_ref[:, d_ff + lo:d_ff + hi], preferred_element_type=F32)
        a = (_silu(gate) * up).astype(BF16)
        part = jnp.dot(a, wd_ref[lo:hi, :], preferred_element_type=F32)
        acc = part if acc is None else acc + part
    xo_ref[...] = x + _mod_part(mod_ref, 5) * acc


def _ffn_layer(x, mod, tiles_per_seq, g, w_gu, w_down, index, tm):
    rows, d = x.shape
    d_ff = w_down.shape[1]
    mr = mod.shape[1]
    row_spec = pl.BlockSpec((tm, d), lambda i: (i, 0))
    return pl.pallas_call(
        functools.partial(_ffn_kernel, d_ff=d_ff, n_chunks=2),
        grid=(rows // tm,),
        in_specs=[row_spec,
                  pl.BlockSpec((1, mr, 6 * d), lambda i: (i // tiles_per_seq, 0, 0)),
                  pl.BlockSpec((1, d), lambda i: (0, 0)),
                  _stacked(w_gu, index), _stacked(w_down, index)],
        out_specs=row_spec,
        out_shape=jax.ShapeDtypeStruct((rows, d), F32),
        compiler_params=_cparams("parallel"),
        name="ffn_dense",
    )(x, mod, g.reshape(1, d), w_gu, w_down)


SEG_ALIGN = 16
SEG_SIZES = (512, 256, 128, 64, 32, 16)
SPARSE_TILE = 512
EXPERT_ROW_TILE = 256
EXPERT_FF_UNIT = 2 * V7X_MXU_WIDTH
EXPERT_STAGE_BUFFERS = 4


def _route_kernel(x_ref, mod_ref, g_ref, wrt_ref, br_ref, h_ref, slots_ref, gw_ref, seg_ref, *, tm):
    x = x_ref[...]
    h = _norm_mod(x, g_ref[...], _mod_part(mod_ref, 3), _mod_part(mod_ref, 4))
    h_ref[...] = h.astype(BF16)
    logits = lax.dot_general(wrt_ref[...], h, (((1,), (1,)), ((), ())), preferred_element_type=F32,
                             precision=HIGHEST) + br_ref[...]
    eidx = lax.broadcasted_iota(jnp.int32, logits.shape, 0).astype(F32)
    m1 = jnp.max(logits, axis=0, keepdims=True)
    i1 = jnp.min(jnp.where(logits == m1, eidx, float(N_EXPERTS)), axis=0, keepdims=True)
    sel1 = eidx == i1
    rest = jnp.where(sel1, -jnp.inf, logits)
    m2 = jnp.max(rest, axis=0, keepdims=True)
    i2 = jnp.min(jnp.where(rest == m2, eidx, float(N_EXPERTS)), axis=0, keepdims=True)
    sel2 = eidx == i2
    e2 = jnp.exp(m2 - m1)
    den = 1.0 + e2
    gw_ref[0] = jnp.concatenate([1.0 / den, e2 / den], axis=0)
    mask = jnp.where(sel1 | sel2, 1.0, 0.0)
    before = (lax.broadcasted_iota(jnp.int32, (tm, tm), 0) < lax.broadcasted_iota(jnp.int32, (tm, tm), 1))
    rank = jnp.dot(mask.astype(BF16), jnp.where(before, 1.0, 0.0).astype(BF16), preferred_element_type=F32)
    count = jnp.sum(mask, axis=1, keepdims=True)
    seg = jnp.floor((count + (SEG_ALIGN - 1)) * (1.0 / SEG_ALIGN)) * SEG_ALIGN
    start = jnp.zeros_like(seg)
    for e in range(N_EXPERTS - 1):
        start = start + jnp.where(eidx[:, 0:1] > e, seg[e:e + 1, :], 0.0)
    slot = start + rank
    slot1 = jnp.sum(jnp.where(sel1, slot, 0.0), axis=0, keepdims=True)
    slot2 = jnp.sum(jnp.where(sel2, slot, 0.0), axis=0, keepdims=True)
    slots_ref[0] = jnp.concatenate([slot1, slot2], axis=0).astype(jnp.int32)
    seg_ref[0] = jnp.broadcast_to(seg, (N_EXPERTS, HEAD_DIM)).astype(jnp.int32)


def _route(x, mod, tiles_per_seq, g, w_router, b_router, tm):
    rows, d = x.shape
    mr = mod.shape[1]
    n_tiles = rows // tm
    row_spec = pl.BlockSpec((tm, d), lambda i: (i, 0))
    pair_spec = pl.BlockSpec((1, 2, tm), lambda i: (i, 0, 0))
    return pl.pallas_call(
        functools.partial(_route_kernel, tm=tm),
        grid=(n_tiles,),
        in_specs=[row_spec,
                  pl.BlockSpec((1, mr, 6 * d), lambda i: (i // tiles_per_seq, 0, 0)),
                  pl.BlockSpec((1, d), lambda i: (0, 0)),
                  pl.BlockSpec((N_EXPERTS, d), lambda i: (0, 0)),
                  pl.BlockSpec((N_EXPERTS, 1), lambda i: (0, 0))],
        out_specs=[row_spec, pair_spec, pair_spec,
                   pl.BlockSpec((1, N_EXPERTS, HEAD_DIM), lambda i: (i, 0, 0))],
        out_shape=[jax.ShapeDtypeStruct((rows, d), BF16),
                   jax.ShapeDtypeStruct((n_tiles, 2, tm), jnp.int32),
                   jax.ShapeDtypeStruct((n_tiles, 2, tm), F32),
                   jax.ShapeDtypeStruct((n_tiles, N_EXPERTS, HEAD_DIM), jnp.int32)],
        compiler_params=_cparams("parallel"),
        name="moe_route",
    )(x, mod, g.reshape(1, d), w_router.T, b_router.reshape(N_EXPERTS, 1))


def _segment_copies(tile, seg_ref, loc_ref, goff_ref, packed, sorted_hbm, sem, to_sorted):
    copies = []
    tokens = (packed.shape[0] - N_EXPERTS * SEG_ALIGN) // 2
    max_seg = -(-tokens // SEG_ALIGN) * SEG_ALIGN
    for e in range(N_EXPERTS):
        seg = seg_ref[tile * N_EXPERTS + e]
        loc = loc_ref[tile * N_EXPERTS + e]
        goff = goff_ref[tile * N_EXPERTS + e]
        done = jnp.int32(0)
        for size in (s for s in SEG_SIZES if s <= max_seg):
            vm = packed.at[pl.ds(pl.multiple_of(loc + done, SEG_ALIGN), size), :]
            hb = sorted_hbm.at[pl.ds(pl.multiple_of(goff + done, SEG_ALIGN), size), :]
            src, dst = (vm, hb) if to_sorted else (hb, vm)
            copies.append(((seg & size) != 0, pltpu.make_async_copy(src, dst, sem)))
            done = done + (seg & size)
    return copies


def _start_all(copies):
    for cond, cp in copies:
        pl.when(cond)(cp.start)


def _wait_all(copies):
    for cond, cp in copies:
        pl.when(cond)(cp.wait)


def _dispatch_kernel(seg_ref, loc_ref, goff_ref, h_ref, slots_ref, sorted_in, sorted_out, stage_ref, sems,
                     *, n_slots, n_tiles):
    del sorted_in
    i = pl.program_id(0)
    par = i % 2

    def copies(tile, parity):
        return _segment_copies(tile, seg_ref, loc_ref, goff_ref, stage_ref.at[parity], sorted_out,
                               sems.at[parity], True)

    @pl.when(i >= 2)
    def _():
        _wait_all(copies(i - 2, par))

    tm = h_ref.shape[0]
    srow = lax.broadcasted_iota(jnp.int32, (n_slots, tm), 0)
    onehot = (srow == slots_ref[0, 0:1, :]) | (srow == slots_ref[0, 1:2, :])
    packed = jnp.dot(jnp.where(onehot, 1.0, 0.0).astype(BF16), h_ref[...], preferred_element_type=F32)
    stage_ref[par] = packed.astype(BF16)
    _start_all(copies(i, par))

    @pl.when(i == n_tiles - 1)
    def _():
        if n_tiles >= 2:
            _wait_all(copies(i - 1, 1 - par))
        _wait_all(copies(i, par))


def _dispatch(h, slots, seg, loc, goff, h_sorted, tm):
    rows, d = h.shape
    n_tiles = rows // tm
    n_slots = 2 * tm + N_EXPERTS * SEG_ALIGN
    return pl.pallas_call(
        functools.partial(_dispatch_kernel, n_slots=n_slots, n_tiles=n_tiles),
        grid_spec=pltpu.PrefetchScalarGridSpec(
            num_scalar_prefetch=3,
            grid=(n_tiles,),
            in_specs=[pl.BlockSpec((tm, d), lambda i, *_: (i, 0)),
                      pl.BlockSpec((1, 2, tm), lambda i, *_: (i, 0, 0)),
                      pl.BlockSpec(memory_space=pl.ANY)],
            out_specs=pl.BlockSpec(memory_space=pl.ANY),
            scratch_shapes=[pltpu.VMEM((2, n_slots, d), BF16), pltpu.SemaphoreType.DMA((2,))]),
        out_shape=jax.ShapeDtypeStruct(h_sorted.shape, BF16),
        input_output_aliases={5: 0},
        compiler_params=_cparams("arbitrary"),
        name="moe_dispatch",
    )(seg, loc, goff, h, slots, h_sorted)


def _experts_kernel(te_ref, jb_ref, nact_ref, phase_ref, nxt_ref, h_ref, wgu_hbm, wd_hbm, y_ref,
                    wg_ref, wd_ref, sg_ref, su_ref, sd_ref, sems, *, first_expert, d_ff, unit):
    del jb_ref
    j = pl.program_id(0)
    n_units = d_ff // unit
    depth = sg_ref.shape[0]
    active = j < nact_ref[0]

    def fetch(expert, u, buf):
        return (pltpu.make_async_copy(wgu_hbm.at[expert, :, pl.ds(u * unit, unit)], sg_ref.at[buf], sems.at[buf]),
                pltpu.make_async_copy(wgu_hbm.at[expert, :, pl.ds(d_ff + u * unit, unit)], su_ref.at[buf],
                                      sems.at[buf]),
                pltpu.make_async_copy(wd_hbm.at[expert, pl.ds(u * unit, unit), :], sd_ref.at[buf], sems.at[buf]))

    def start(expert, u):
        for cp in fetch(expert, u, u % depth):
            cp.start()

    def start_first(expert):
        for u in range(min(depth, n_units)):
            start(expert, u)

    def finish(expert, u):
        for cp in fetch(expert, u, u % depth):
            cp.wait()
        wg_ref[u, :, 0:unit] = sg_ref[u % depth].astype(BF16)
        wg_ref[u, :, unit:2 * unit] = su_ref[u % depth].astype(BF16)
        wd_ref[u] = sd_ref[u % depth].astype(BF16)
        if u + depth < n_units:
            start(expert, u + depth)

    def unit_out(h, u):
        gate = jnp.dot(h, wg_ref[u, :, 0:unit], preferred_element_type=F32)
        up = jnp.dot(h, wg_ref[u, :, unit:2 * unit], preferred_element_type=F32)
        a = (_silu(gate) * up).astype(BF16)
        return jnp.dot(a, wd_ref[u], preferred_element_type=F32)

    @pl.when(active & (j == 0))
    def _():
        expert = first_expert + te_ref[0]
        start_first(expert)
        for u in range(n_units):
            finish(expert, u)

    phase = phase_ref[j]
    swap = active & (phase >= 2)

    @pl.when(active & (phase == 1))
    def _():
        start_first(first_expert + nxt_ref[j])

    @pl.when(active & jnp.logical_not(swap))
    def _():
        h = h_ref[...]
        acc = unit_out(h, 0)
        for u in range(1, n_units):
            acc = acc + unit_out(h, u)
        y_ref[...] = acc.astype(BF16)

    @pl.when(swap)
    def _():
        nxt = first_expert + nxt_ref[j]
        pl.when(phase == 3)(functools.partial(start_first, nxt))
        h = h_ref[...]
        acc = None
        for u in range(n_units):
            part = unit_out(h, u)
            acc = part if acc is None else acc + part
            finish(nxt, u)
        y_ref[...] = acc.astype(BF16)

    @pl.when(jnp.logical_not(active))
    def _():
        y_ref[...] = jnp.zeros_like(y_ref)


def _experts(h_sorted, tile_expert, tile_block, n_active, tile_phase, tile_next, w_gu, w_down, index, tmf):
    n_rows, d = h_sorted.shape
    d_ff = w_down.shape[2]
    unit = EXPERT_FF_UNIT
    n_units = d_ff // unit
    depth = EXPERT_STAGE_BUFFERS
    w_gu = w_gu.reshape(-1, d, 2 * d_ff)
    w_down = w_down.reshape(-1, d_ff, d)
    return pl.pallas_call(
        functools.partial(_experts_kernel, first_expert=index * N_EXPERTS, d_ff=d_ff, unit=unit),
        grid_spec=pltpu.PrefetchScalarGridSpec(
            num_scalar_prefetch=5,
            grid=(n_rows // tmf,),
            in_specs=[pl.BlockSpec((tmf, d), lambda j, te, jb, *_: (jb[j], 0)),
                      pl.BlockSpec(memory_space=pl.ANY),
                      pl.BlockSpec(memory_space=pl.ANY)],
            out_specs=pl.BlockSpec((tmf, d), lambda j, *_: (j, 0)),
            scratch_shapes=[pltpu.VMEM((n_units, d, 2 * unit), BF16),
                            pltpu.VMEM((n_units, unit, d), BF16),
                            pltpu.VMEM((depth, d, unit), F32),
                            pltpu.VMEM((depth, d, unit), F32),
                            pltpu.VMEM((depth, unit, d), F32),
                            pltpu.SemaphoreType.DMA((depth,))]),
        out_shape=jax.ShapeDtypeStruct((n_rows, d), BF16),
        compiler_params=_cparams("arbitrary"),
        name="moe_experts_sorted",
    )(tile_expert, tile_block, n_active, tile_phase, tile_next, h_sorted, w_gu, w_down)


def _combine_kernel(seg_ref, loc_ref, goff_ref, x_ref, mod_ref, slots_ref, gw_ref, sorted_hbm, xo_ref,
                    stage_ref, sems, *, n_slots, n_tiles):
    i = pl.program_id(0)
    par = i % 2

    def copies(tile, parity):
        return _segment_copies(tile, seg_ref, loc_ref, goff_ref, stage_ref.at[parity], sorted_hbm,
                               sems.at[parity], False)

    @pl.when(i == 0)
    def _():
        stage_ref[...] = jnp.zeros_like(stage_ref)
        _start_all(copies(i, par))

    @pl.when(i + 1 < n_tiles)
    def _():
        _start_all(copies(i + 1, 1 - par))

    _wait_all(copies(i, par))
    y = stage_ref[par]
    tm = x_ref.shape[0]
    scol = lax.broadcasted_iota(jnp.int32, (tm, n_slots), 1)
    gates = (jnp.where(scol == slots_ref[:, 0:1], gw_ref[:, 0:1], 0.0)
             + jnp.where(scol == slots_ref[:, 1:2], gw_ref[:, 1:2], 0.0))
    moe = jnp.dot(gates.astype(BF16), y, preferred_element_type=F32)
    xo_ref[...] = x_ref[...] + _mod_part(mod_ref, 5) * moe


def _combine(x, mod, tiles_per_seq, slots_col, gw_col, y_sorted, seg, loc, goff, tm):
    rows, d = x.shape
    mr = mod.shape[1]
    n_tiles = rows // tm
    n_slots = 2 * tm + N_EXPERTS * SEG_ALIGN
    row_spec = pl.BlockSpec((tm, d), lambda i, *_: (i, 0))
    pair_spec = pl.BlockSpec((tm, 2), lambda i, *_: (i, 0))
    return pl.pallas_call(
        functools.partial(_combine_kernel, n_slots=n_slots, n_tiles=n_tiles),
        grid_spec=pltpu.PrefetchScalarGridSpec(
            num_scalar_prefetch=3,
            grid=(n_tiles,),
            in_specs=[row_spec,
                      pl.BlockSpec((1, mr, 6 * d), lambda i, *_: (i // tiles_per_seq, 0, 0)),
                      pair_spec, pair_spec,
                      pl.BlockSpec(memory_space=pl.ANY)],
            out_specs=row_spec,
            scratch_shapes=[pltpu.VMEM((2, n_slots, d), BF16), pltpu.SemaphoreType.DMA((2,))]),
        out_shape=jax.ShapeDtypeStruct((rows, d), F32),
        compiler_params=_cparams("arbitrary"),
        name="moe_combine",
    )(seg, loc, goff, x, mod, slots_col, gw_col, y_sorted)


def _moe_sparse_layer(parts, g, w_router, b_router, w_gu, w_down, index):
    tmf = EXPERT_ROW_TILE
    d = parts[0][0].shape[1]
    routed = [_route(x, mod, tps, g, w_router, b_router, tm) for x, mod, tps, tm in parts]
    tiles = [x.shape[0] // tm for x, _, _, tm in parts]
    seg = jnp.concatenate([r[3][:, :, 0] for r in routed], axis=0)
    loc = jnp.cumsum(seg, axis=1) - seg
    region = ((jnp.sum(seg, axis=0) + tmf - 1) // tmf) * tmf
    region_end = jnp.cumsum(region)
    goff = (region_end - region)[None, :] + jnp.cumsum(seg, axis=0) - seg
    n_sorted = sum(2 * x.shape[0] for x, _, _, _ in parts) + sum(tiles) * N_EXPERTS * (SEG_ALIGN - 1)
    n_sorted += N_EXPERTS * (tmf - SEG_ALIGN)
    n_sorted = ((n_sorted + tmf - 1) // tmf) * tmf
    n_row_tiles = n_sorted // tmf
    n_active = region_end[-1] // tmf
    tile_ids = jnp.arange(n_row_tiles, dtype=jnp.int32)
    tile_block = jnp.maximum(jnp.minimum(tile_ids, n_active - 1), 0)
    tile_expert = jnp.sum((region_end // tmf)[None, :] <= tile_block[:, None], axis=1)
    tile_expert = jnp.minimum(tile_expert, N_EXPERTS - 1).astype(jnp.int32)
    ids = jnp.arange(N_EXPERTS, dtype=jnp.int32)
    later = jnp.where((ids[None, :] > ids[:, None]) & (region[None, :] > 0), ids[None, :], N_EXPERTS)
    next_expert = jnp.min(later, axis=1)
    tile_next = jnp.where(next_expert < N_EXPERTS, next_expert, -1)[tile_expert].astype(jnp.int32)
    from_end = (region_end // tmf)[tile_expert] - 1 - tile_block
    single = (region // tmf)[tile_expert] == 1
    tile_phase = jnp.where(from_end == 1, 1, jnp.where(from_end == 0, jnp.where(single, 3, 2), 0))
    tile_phase = jnp.where(tile_next >= 0, tile_phase, 0).astype(jnp.int32)
    tables, first = [], 0
    for n in tiles:
        cut = lambda a: a[first:first + n].reshape(-1).astype(jnp.int32)
        tables.append((cut(seg), cut(loc), cut(goff)))
        first += n
    h_sorted = jnp.zeros((n_sorted, d), BF16)
    for (_, _, _, tm), (h, slots, _, _), tab in zip(parts, routed, tables):
        h_sorted = _dispatch(h, slots, *tab, h_sorted, tm)
    y_sorted = _experts(h_sorted, tile_expert, tile_block.astype(jnp.int32),
                        n_active.reshape(1).astype(jnp.int32), tile_phase, tile_next, w_gu, w_down, index, tmf)
    outs = []
    for (x, mod, tps, tm), (_, slots, gw, _), tab in zip(parts, routed, tables):
        rows = x.shape[0]
        slots_col = jnp.transpose(slots, (0, 2, 1)).reshape(rows, 2)
        gw_col = jnp.transpose(gw, (0, 2, 1)).reshape(rows, 2)
        outs.append(_combine(x, mod, tps, slots_col, gw_col, y_sorted, *tab, tm))
    return outs


def _rope_tables(pos):
    half = ROT_DIM // 2
    inv = jnp.float32(ROPE_THETA) ** (-jnp.arange(half, dtype=jnp.float32) / half)
    ang = pos.astype(jnp.float32)[:, None] * inv[None, :]
    cos, sin = jnp.cos(ang), jnp.sin(ang)
    t = pos.shape[0]
    pad1 = jnp.ones((t, HEAD_DIM - ROT_DIM), F32)
    pad0 = jnp.zeros((t, HEAD_DIM - ROT_DIM), F32)
    return (jnp.concatenate([cos, cos, pad1], axis=1), jnp.concatenate([-sin, sin, pad0], axis=1))


def _swap_rot_halves(v):
    half = ROT_DIM // 2
    return jnp.concatenate([v[half:ROT_DIM], v[:half], v[ROT_DIM:]])


def _rope_gain_tables(cos_ref, sin_ref, g_ref, gs_ref, scale):
    return cos_ref[...] * (g_ref[...] * scale), sin_ref[...] * (gs_ref[...] * scale)


def _head_norm_rope(xh, a, b):
    half = ROT_DIM // 2
    ones = jnp.ones((HEAD_DIM, HEAD_DIM), BF16)
    ss = jnp.dot((xh * xh).astype(BF16), ones, preferred_element_type=F32)
    rs = lax.rsqrt(ss * (1.0 / HEAD_DIM) + EPS)
    lane = lax.broadcasted_iota(jnp.int32, xh.shape, 1)
    swapped = jnp.where(lane < half, pltpu.roll(xh, HEAD_DIM - half, 1), pltpu.roll(xh, half, 1))
    return (xh * a + swapped * b) * rs


SPLIT = 4
SPLIT_TILE = SPLIT * Q_BLOCK
SPLIT_GROUPS = (1, 2)


def _kv_kernel(x_ref, mod_ref, g_ref, wkv_ref, gk_ref, cos_ref, sin_ref, *out_refs, split):
    x = x_ref[...]
    h = _norm_mod(x, g_ref[...], mod_ref[0, :, 0:D_MODEL], mod_ref[0, :, D_MODEL:2 * D_MODEL]).astype(BF16)
    rope_a, rope_b = _rope_gain_tables(cos_ref, sin_ref, gk_ref.at[0:1], gk_ref.at[1:2], 1.0)
    k_refs, v_refs = out_refs[:N_GROUPS], out_refs[N_GROUPS:2 * N_GROUPS]
    kv_all = jnp.dot(h, wkv_ref[...], preferred_element_type=F32)
    for hd in range(N_KV_HEADS):
        kh = kv_all[:, hd * HEAD_DIM:(hd + 1) * HEAD_DIM]
        vh = kv_all[:, KV_WIDTH + hd * HEAD_DIM:KV_WIDTH + (hd + 1) * HEAD_DIM]
        g_idx, k_idx = divmod(hd, HKV_G)
        sl = slice(k_idx * HEAD_DIM, (k_idx + 1) * HEAD_DIM)
        kh = _head_norm_rope(kh, rope_a, rope_b)
        rows_of_head = pl.ds(k_idx, kh.shape[0], stride=HKV_G)
        k_refs[g_idx][rows_of_head, :] = kh
        v_refs[g_idx][rows_of_head, :] = vh
        if split and g_idx in SPLIT_GROUPS:
            n = SPLIT_GROUPS.index(g_idx)
            scr = out_refs[-1]
            for j, (val, dst) in enumerate(((kh, out_refs[2 * N_GROUPS + 2 * n]),
                                            (vh, out_refs[2 * N_GROUPS + 2 * n + 1]))):
                slot = (n * HKV_G + k_idx) * 2 + j
                scr[slot] = val
                for c in range(SPLIT):
                    dst[c * Q_BLOCK:(c + 1) * Q_BLOCK, sl] = (
                        scr[slot, pl.ds(c, Q_BLOCK, stride=SPLIT), :].astype(BF16))


def _kv_proj(x, mod_kv, tiles_per_seq, g, w_kv, g_k, cos, sin, tm, split):
    rows, d = x.shape
    mr = mod_kv.shape[1]
    assert not split or tm == SPLIT_TILE
    row_spec = pl.BlockSpec((tm, d), lambda i: (i, 0))
    tab_spec = pl.BlockSpec((tm, HEAD_DIM), lambda i: (i % tiles_per_seq, 0))
    out_spec = pl.BlockSpec((tm, GROUP_KV_WIDTH), lambda i: (i, 0))
    nat_spec = pl.BlockSpec((HKV_G * tm, HEAD_DIM), lambda i: (i, 0))
    n_split = 2 * len(SPLIT_GROUPS) if split else 0
    return pl.pallas_call(
        functools.partial(_kv_kernel, split=split),
        grid=(rows // tm,),
        in_specs=[row_spec,
                  pl.BlockSpec((1, mr, 2 * d), lambda i: (i // tiles_per_seq, 0, 0)),
                  pl.BlockSpec((1, d), lambda i: (0, 0)),
                  _resident((d, 2 * KV_WIDTH), lambda i: (0, 0)),
                  pl.BlockSpec((2, HEAD_DIM), lambda i: (0, 0)),
                  tab_spec, tab_spec],
        out_specs=[nat_spec] * (2 * N_GROUPS) + [out_spec] * n_split,
        out_shape=[jax.ShapeDtypeStruct((HKV_G * rows, HEAD_DIM), F32)] * (2 * N_GROUPS)
                  + [jax.ShapeDtypeStruct((rows, GROUP_KV_WIDTH), BF16)] * n_split,
        scratch_shapes=[pltpu.VMEM((n_split * HKV_G, tm, HEAD_DIM), F32)] if split else [],
        compiler_params=_cparams("parallel"),
        name="kv_proj",
    )(x, mod_kv, g.reshape(1, d), w_kv, jnp.stack([g_k, _swap_rot_halves(g_k)]), cos, sin)


def _q_kernel(x_ref, mod_ref, g_ref, wq_ref, gq_ref, cos_ref, sin_ref, *refs, split):
    q_refs = refs[:N_GROUPS]
    x = x_ref[...]
    h = _norm_mod(x, g_ref[...], _mod_part(mod_ref, 0), _mod_part(mod_ref, 1)).astype(BF16)
    rope_a, rope_b = _rope_gain_tables(cos_ref, sin_ref, gq_ref.at[0:1], gq_ref.at[1:2], ATTN_SCALE)
    q_all = jnp.dot(h, wq_ref[...], preferred_element_type=F32)
    for hd in range(N_Q_HEADS):
        g_idx, h_idx = divmod(hd, HQ_G)
        sl = slice(h_idx * HEAD_DIM, (h_idx + 1) * HEAD_DIM)
        qh = _head_norm_rope(q_all[:, hd * HEAD_DIM:(hd + 1) * HEAD_DIM], rope_a, rope_b)
        if split and g_idx in SPLIT_GROUPS:
            scr = refs[N_GROUPS]
            n = SPLIT_GROUPS.index(g_idx) * HQ_G + h_idx
            scr[n] = qh
            for c in range(SPLIT):
                q_refs[g_idx][c * Q_BLOCK:(c + 1) * Q_BLOCK, sl] = (
                    scr[n, pl.ds(c, Q_BLOCK, stride=SPLIT), :].astype(BF16))
        else:
            q_refs[g_idx][:, sl] = qh.astype(BF16)


def _q_proj(x, mod, tiles_per_seq, g, w_q, index, g_q, cos, sin, tm, split):
    rows, d = x.shape
    mr = mod.shape[1]
    assert not split or tm == SPLIT_TILE
    row_spec = pl.BlockSpec((tm, d), lambda i: (i, 0))
    tab_spec = pl.BlockSpec((tm, HEAD_DIM), lambda i: (i % tiles_per_seq, 0))
    return pl.pallas_call(
        functools.partial(_q_kernel, split=split),
        grid=(rows // tm,),
        in_specs=[row_spec,
                  pl.BlockSpec((1, mr, 6 * d), lambda i: (i // tiles_per_seq, 0, 0)),
                  pl.BlockSpec((1, d), lambda i: (0, 0)),
                  _stacked(w_q, index),
                  pl.BlockSpec((2, HEAD_DIM), lambda i: (0, 0)),
                  tab_spec, tab_spec],
        out_specs=[pl.BlockSpec((tm, O_WIDTH), lambda i: (i, 0))] * N_GROUPS,
        out_shape=[jax.ShapeDtypeStruct((rows, O_WIDTH), BF16)] * N_GROUPS,
        scratch_shapes=[pltpu.VMEM((len(SPLIT_GROUPS) * HQ_G, tm, HEAD_DIM), F32)] if split else [],
        compiler_params=_cparams("parallel"),
        name="q_proj",
    )(x, mod, g.reshape(1, d), w_q, jnp.stack([g_q, _swap_rot_halves(g_q)]), cos, sin)


def _attn_prompt_kernel(*refs, step, has_prev, n_blk):
    if has_prev:
        q_ref, k_ref, v_ref, kp_ref, vp_ref, o_ref, st_ref = refs
        not_first = pl.program_id(1) > 0
    else:
        q_ref, k_ref, v_ref, o_ref, st_ref = refs
    nq = REP * Q_BLOCK
    lane = lax.broadcasted_iota(jnp.int32, (Q_BLOCK, HEAD_DIM), 1)
    max_kb = 2 if step == 1 else n_blk
    col = lax.broadcasted_iota(jnp.int32, (nq, max_kb * Q_BLOCK), 1)
    dist = (max_kb - 1) * Q_BLOCK + (lax.broadcasted_iota(jnp.int32, col.shape, 0) & (Q_BLOCK - 1)) - col
    reach = (dist >= 0) & (dist <= step * Q_BLOCK) & ((dist & (step - 1)) == 0)
    for i in range(n_blk):
        first_kb = i - 1 if step == 1 else 0
        if first_kb < 0 and not has_prev:
            first_kb = 0
        kbs = list(range(first_kb, i + 1))
        valid = reach[:, (max_kb - len(kbs)) * Q_BLOCK:]
        if first_kb < 0:
            valid = valid & (not_first | (col >= Q_BLOCK))
        for k in range(HKV_G):
            ksl = slice(k * HEAD_DIM, (k + 1) * HEAD_DIM)
            q = jnp.concatenate(
                [q_ref[i, :, (k * REP + r) * HEAD_DIM:(k * REP + r + 1) * HEAD_DIM] for r in range(REP)], axis=0)
            if has_prev:
                hrows = pl.ds(k, Q_BLOCK, stride=HKV_G)
                keys = jnp.concatenate([kp_ref[hrows, :] if j < 0 else k_ref[j, hrows, :] for j in kbs], axis=0)
                vals = jnp.concatenate([vp_ref[hrows, :] if j < 0 else v_ref[j, hrows, :] for j in kbs], axis=0)
            else:
                keys = jnp.concatenate([k_ref[j, :, ksl] for j in kbs], axis=0)
                vals = jnp.concatenate([v_ref[j, :, ksl] for j in kbs], axis=0)
            s = lax.dot_general(q, keys.astype(BF16), (((1,), (1,)), ((), ())), preferred_element_type=F32)
            s = jnp.where(valid, s, NEG)
            m = jnp.max(s, axis=1, keepdims=True)
            p = jnp.exp(s - m)
            den = jnp.sum(p, axis=1, keepdims=True)
            out = jnp.dot(p.astype(BF16), vals.astype(BF16), preferred_element_type=F32) / den
            lse = m + jnp.log(den)
            stats = jnp.zeros((Q_BLOCK, HEAD_DIM), F32)
            for r in range(REP):
                rs = slice(r * Q_BLOCK, (r + 1) * Q_BLOCK)
                o_ref[i, :, (k * REP + r) * HEAD_DIM:(k * REP + r + 1) * HEAD_DIM] = out[rs].astype(BF16)
                stats = jnp.where(lane == r, lse[rs], stats)
            st_ref[i, :, ksl] = stats


def _attn_prompt_group(q, k_g, v_g, group, batch, seq):
    _, dil = DIL_GROUPS[group]
    rows = batch * seq
    n_tiles = seq // SPLIT_TILE
    if group in SPLIT_GROUPS:
        step = dil // SPLIT
        lead = (batch, n_tiles, SPLIT, Q_BLOCK)
        view = lambda a: a.reshape(*lead, a.shape[-1])
        spec = lambda w: pl.BlockSpec((None, n_tiles, None, Q_BLOCK, w), lambda b, c: (b, 0, c, 0, 0))
        out_shape = [jax.ShapeDtypeStruct((*lead, O_WIDTH), BF16),
                     jax.ShapeDtypeStruct((*lead, GROUP_KV_WIDTH), F32)]
        out, stats = pl.pallas_call(
            functools.partial(_attn_prompt_kernel, step=step, has_prev=False, n_blk=n_tiles),
            grid=(batch, SPLIT),
            in_specs=[spec(O_WIDTH), spec(GROUP_KV_WIDTH), spec(GROUP_KV_WIDTH)],
            out_specs=[spec(O_WIDTH), spec(GROUP_KV_WIDTH)],
            out_shape=out_shape,
            compiler_params=_cparams("parallel", "parallel"),
            name=f"attn_prompt_g{group}",
        )(view(q), view(k_g), view(v_g))
    else:
        lead = (batch * n_tiles, SPLIT)
        view = lambda a, r: a.reshape(*lead, r, a.shape[-1])
        spec = lambda r, w: pl.BlockSpec((None, SPLIT, r, w), lambda b, i: (b * n_tiles + i, 0, 0, 0))
        kv_rows = HKV_G * Q_BLOCK
        kv_spec = spec(kv_rows, HEAD_DIM)
        prev = pl.BlockSpec((None, None, kv_rows, HEAD_DIM),
                            lambda b, i: (jnp.maximum(b * n_tiles + i - 1, 0), SPLIT - 1, 0, 0))
        out_shape = [jax.ShapeDtypeStruct((*lead, Q_BLOCK, O_WIDTH), BF16),
                     jax.ShapeDtypeStruct((*lead, Q_BLOCK, GROUP_KV_WIDTH), F32)]
        k4, v4 = view(k_g, kv_rows), view(v_g, kv_rows)
        out, stats = pl.pallas_call(
            functools.partial(_attn_prompt_kernel, step=dil, has_prev=True, n_blk=SPLIT),
            grid=(batch, n_tiles),
            in_specs=[spec(Q_BLOCK, O_WIDTH), kv_spec, kv_spec, prev, prev],
            out_specs=[spec(Q_BLOCK, O_WIDTH), spec(Q_BLOCK, GROUP_KV_WIDTH)],
            out_shape=out_shape,
            compiler_params=_cparams("parallel", "arbitrary"),
            name=f"attn_prompt_g{group}",
        )(view(q, Q_BLOCK), k4, v4, k4, v4)
    return out.reshape(rows, O_WIDTH), stats.reshape(rows, GROUP_KV_WIDTH)


def _merge_o_kernel(x_ref, mod_ref, o0_ref, o1_ref, o2_ref, s0_ref, s1_ref, s2_ref, wo_ref, xo_ref,
                    on_ref, sn_ref):
    for n, (o_ref, s_ref) in enumerate(((o1_ref, s1_ref), (o2_ref, s2_ref))):
        for c in range(SPLIT):
            rows = slice(c * Q_BLOCK, (c + 1) * Q_BLOCK)
            dst = pl.ds(c, Q_BLOCK, stride=SPLIT)
            for hd in range(HQ_G):
                on_ref[n, hd, dst, :] = o_ref[rows, hd * HEAD_DIM:(hd + 1) * HEAD_DIM].astype(F32)
            for k in range(HKV_G):
                sn_ref[n, k, dst, :] = s_ref[rows, k * HEAD_DIM:(k + 1) * HEAD_DIM]
    heads = []
    for k in range(HKV_G):
        for r in range(REP):
            hd = k * REP + r
            lses = [s0_ref[:, k * HEAD_DIM + r:k * HEAD_DIM + r + 1],
                    sn_ref[0, k, :, r:r + 1], sn_ref[1, k, :, r:r + 1]]
            m = jnp.maximum(jnp.maximum(lses[0], lses[1]), lses[2])
            es = [jnp.exp(l - m) for l in lses]
            den = es[0] + es[1] + es[2]
            sl = slice(hd * HEAD_DIM, (hd + 1) * HEAD_DIM)
            outs = [o0_ref[:, sl].astype(F32), on_ref[0, hd], on_ref[1, hd]]
            o = sum((e / den) * o_g for e, o_g in zip(es, outs))
            heads.append(o.astype(BF16))
    o = jnp.concatenate(heads, axis=1)
    y = jnp.dot(o, wo_ref[...], preferred_element_type=F32)
    xo_ref[...] = x_ref[...] + _mod_part(mod_ref, 2) * y


def _merge_o(x, mod, tiles_per_seq, outs, stats, w_o, index, tm):
    rows, d = x.shape
    mr = mod.shape[1]
    assert tm == SPLIT_TILE
    row_spec = pl.BlockSpec((tm, d), lambda i: (i, 0))
    o_spec = pl.BlockSpec((tm, O_WIDTH), lambda i: (i, 0))
    st_spec = pl.BlockSpec((tm, GROUP_KV_WIDTH), lambda i: (i, 0))
    return pl.pallas_call(
        _merge_o_kernel,
        grid=(rows // tm,),
        in_specs=[row_spec,
                  pl.BlockSpec((1, mr, 6 * d), lambda i: (i // tiles_per_seq, 0, 0))]
                 + [o_spec] * 3 + [st_spec] * 3 + [_stacked(w_o, index)],
        out_specs=row_spec,
        out_shape=jax.ShapeDtypeStruct((rows, d), F32),
        scratch_shapes=[pltpu.VMEM((len(SPLIT_GROUPS), HQ_G, tm, HEAD_DIM), F32),
                        pltpu.VMEM((len(SPLIT_GROUPS), HKV_G, tm, HEAD_DIM), F32)],
        compiler_params=_cparams("parallel"),
        name="merge_o_proj",
    )(x, mod, *outs, *stats, w_o)


def _attn_sample_kernel(*refs, n_new, write_buffers):
    q_ref = refs[0]
    nk_refs = refs[1:1 + N_GROUPS]
    nv_refs = refs[1 + N_GROUPS:1 + 2 * N_GROUPS]
    ck_refs = refs[1 + 2 * N_GROUPS:1 + 3 * N_GROUPS]
    cv_refs = refs[1 + 3 * N_GROUPS:1 + 4 * N_GROUPS]
    o_ref = refs[1 + 4 * N_GROUPS]
    nq = REP * n_new
    n_new_rows = HKV_G * n_new
    qi = lax.broadcasted_iota(jnp.int32, (nq, 1), 0) & (n_new - 1)

    if write_buffers:
        ok_refs = refs[2 + 4 * N_GROUPS:2 + 5 * N_GROUPS]
        ov_refs = refs[2 + 5 * N_GROUPS:2 + 6 * N_GROUPS]
        for g in range(N_GROUPS):
            for c_ref, n_ref, o_buf in ((ck_refs[g], nk_refs[g], ok_refs[g]), (cv_refs[g], nv_refs[g], ov_refs[g])):
                n_rows = c_ref.shape[1]
                o_buf[0, 0:n_rows - n_new_rows, :] = c_ref[0, n_new_rows:n_rows, :]
                o_buf[0, n_rows - n_new_rows:n_rows, :] = n_ref[0]

    for k in range(HKV_G):
        outs, lses = [], []
        for g, (window, dil) in enumerate(DIL_GROUPS):
            length = ck_refs[g].shape[1] // HKV_G
            q = q_ref[0, g, k]
            qf = q.astype(F32)
            keys = ck_refs[g][0, pl.ds(k, length, stride=HKV_G), :].astype(BF16)
            vals = cv_refs[g][0, pl.ds(k, length, stride=HKV_G), :].astype(BF16)
            s = lax.dot_general(q, keys, (((1,), (1,)), ((), ())), preferred_element_type=F32)
            idx = lax.broadcasted_iota(jnp.int32, (nq, length), 1)
            diff = length + qi - idx
            s = jnp.where(((diff & (dil - 1)) == 0) & (diff <= window), s, NEG)
            new_k = nk_refs[g][0]
            new_v = nv_refs[g][0]
            s_new = []
            for j in range(n_new):
                sj = jnp.sum(qf * new_k[HKV_G * j + k:HKV_G * j + k + 1, :], axis=1, keepdims=True)
                ok = (qi >= j) & (((qi - j) & (dil - 1)) == 0)
                s_new.append(jnp.where(ok, sj, NEG))
            m = jnp.max(s, axis=1, keepdims=True)
            for sj in s_new:
                m = jnp.maximum(m, sj)
            p = jnp.exp(s - m)
            den = jnp.sum(p, axis=1, keepdims=True)
            acc = jnp.dot(p.astype(BF16), vals, preferred_element_type=F32)
            for j, sj in enumerate(s_new):
                pj = jnp.exp(sj - m)
                den = den + pj
                acc = acc + pj * new_v[HKV_G * j + k:HKV_G * j + k + 1, :]
            outs.append(acc / den)
            lses.append(m + jnp.log(den))
        m = jnp.maximum(jnp.maximum(lses[0], lses[1]), lses[2])
        es = [jnp.exp(l - m) for l in lses]
        den = es[0] + es[1] + es[2]
        o_ref[0, k] = sum((e / den) * o for e, o in zip(es, outs))


def _attn_sample(q, new_k, new_v, cache_k, cache_v, n_seq, n_new, write_buffers):
    assert HKV_G * n_new == 8
    q5 = q.reshape(n_seq, n_new, N_GROUPS, HKV_G, REP, HEAD_DIM)
    q5 = jnp.transpose(q5, (0, 2, 3, 4, 1, 5)).reshape(n_seq, N_GROUPS, HKV_G, REP * n_new, HEAD_DIM)
    rows2 = lambda a, n: a.reshape(n_seq, n * HKV_G, HEAD_DIM)
    nk = [rows2(a, n_new) for a in new_k]
    nv = [rows2(a, n_new) for a in new_v]
    ck = [rows2(c, c.shape[1]) for c in cache_k]
    cv = [rows2(c, c.shape[1]) for c in cache_v]
    new_spec = pl.BlockSpec((1, HKV_G * n_new, HEAD_DIM), lambda b: (b, 0, 0))
    cache_specs = [pl.BlockSpec((1, c.shape[1], HEAD_DIM), lambda b: (b, 0, 0)) for c in ck]
    cache_shapes = [jax.ShapeDtypeStruct(c.shape, F32) for c in ck]
    n_buf = 2 if write_buffers else 0
    res = pl.pallas_call(
        functools.partial(_attn_sample_kernel, n_new=n_new, write_buffers=write_buffers),
        grid=(n_seq,),
        in_specs=[pl.BlockSpec((1, N_GROUPS, HKV_G, REP * n_new, HEAD_DIM), lambda b: (b, 0, 0, 0, 0))]
                 + [new_spec] * (2 * N_GROUPS) + cache_specs + cache_specs,
        out_specs=[pl.BlockSpec((1, HKV_G, REP * n_new, HEAD_DIM), lambda b: (b, 0, 0, 0))]
                  + cache_specs * n_buf,
        out_shape=[jax.ShapeDtypeStruct((n_seq, HKV_G, REP * n_new, HEAD_DIM), F32)] + cache_shapes * n_buf,
        compiler_params=_cparams("parallel"),
        name="attn_sample",
    )(q5, *nk, *nv, *ck, *cv)
    o = res[0].reshape(n_seq, HKV_G, REP, n_new, HEAD_DIM)
    o = jnp.transpose(o, (0, 3, 1, 2, 4)).reshape(n_seq * n_new, O_WIDTH)
    if not write_buffers:
        return o, None, None
    shape4 = lambda a: a.reshape(n_seq, a.shape[1] // HKV_G, HKV_G, HEAD_DIM)
    new_ck = [shape4(a) for a in res[1:1 + N_GROUPS]]
    new_cv = [shape4(a) for a in res[1 + N_GROUPS:1 + 2 * N_GROUPS]]
    return o, new_ck, new_cv


def _o_proj_kernel(x_ref, mod_ref, o_ref, wo_ref, xo_ref):
    y = jnp.dot(o_ref[...].astype(BF16), wo_ref[...], preferred_element_type=F32)
    xo_ref[...] = x_ref[...] + _mod_part(mod_ref, 2) * y


def _o_proj(x, mod, o, w_o, index):
    rows, d = x.shape
    mr = mod.shape[1]
    full = pl.BlockSpec((rows, d), lambda i: (0, 0))
    return pl.pallas_call(
        _o_proj_kernel,
        grid=(1,),
        in_specs=[full, pl.BlockSpec((1, mr, 6 * d), lambda i: (0, 0, 0)),
                  pl.BlockSpec((rows, O_WIDTH), lambda i: (0, 0)),
                  _stacked(w_o, index)],
        out_specs=full,
        out_shape=jax.ShapeDtypeStruct((rows, d), F32),
        compiler_params=_cparams("arbitrary"),
        name="o_proj_sample",
    )(x, mod, o, w_o)


def _trunk(x, mods, mod_kv, pos, weights, tm, conv_prev, kv_prev):
    (g_norm_mix, g_norm_ffn, w_in, w_conv, w_out_conv, g_norm_kv, w_kv, g_k, w_q, g_q, w_o,
     w_gu_dense, w_down_dense) = weights
    batch, seq, d = x.shape
    rows = batch * seq
    depth = g_norm_mix.shape[0]
    n_conv = w_in.shape[0]
    sample = kv_prev is not None
    tiles_per_seq = 1 if sample else seq // tm
    x = x.reshape(rows, d)
    cos, sin = _rope_tables(pos)
    if sample:
        cos, sin = jnp.tile(cos, (batch, 1)), jnp.tile(sin, (batch, 1))
    conv_state = []
    kv_state = None
    for layer in range(depth):
        mod = mods[layer]
        if layer == n_conv:
            kvs = _kv_proj(x, mod_kv, tiles_per_seq, g_norm_kv, w_kv, g_k, cos, sin, tm, not sample)
            k_new, v_new = kvs[:N_GROUPS], kvs[N_GROUPS:2 * N_GROUPS]
            k_att, v_att = list(k_new), list(v_new)
            for n, g_idx in enumerate(SPLIT_GROUPS if not sample else ()):
                k_att[g_idx], v_att[g_idx] = kvs[2 * N_GROUPS + 2 * n], kvs[2 * N_GROUPS + 2 * n + 1]
        if layer < n_conv:
            if sample:
                st = conv_prev[layer]
                zero = jnp.zeros((batch, seq - 2, d), F32)
                pa = jnp.concatenate([st[:, 1:2], jnp.zeros((batch, seq - 1, d), F32)], axis=1)
                pb = jnp.concatenate([st, zero], axis=1)
                prev = (seq, pa.reshape(rows, d), pb.reshape(rows, d))
            else:
                prev = None
            x, u_tail = _conv_layer(x, mod, tiles_per_seq, g_norm_mix[layer], w_in, w_conv, w_out_conv, layer,
                                    tm, prev)
            if sample:
                conv_state.append(u_tail.reshape(batch, seq, d)[:, seq - (CONV_W - 1):])
            else:
                conv_state.append(u_tail[:, 8 - (CONV_W - 1):])
        else:
            lb = layer - n_conv
            qs = _q_proj(x, mod, tiles_per_seq, g_norm_mix[layer], w_q, lb, g_q[lb], cos, sin, tm, not sample)
            if sample:
                o, new_ck, new_cv = _attn_sample(jnp.concatenate(qs, axis=1), k_new, v_new, kv_prev[0],
                                                 kv_prev[1], batch, seq, kv_state is None)
                if kv_state is None:
                    kv_state = (new_ck, new_cv)
                x = _o_proj(x, mod, o, w_o, lb)
            else:
                res = [_attn_prompt_group(qs[g], k_att[g], v_att[g], g, batch, seq) for g in range(N_GROUPS)]
                x = _merge_o(x, mod, tiles_per_seq, [r[0] for r in res], [r[1] for r in res], w_o, lb, tm)
        if layer % 2 == 0:
            x = _ffn_layer(x, mod, tiles_per_seq, g_norm_ffn[layer], w_gu_dense, w_down_dense, layer // 2, tm)
        else:
            tm_moe = tm if sample else min(SPARSE_TILE, seq)
            x = yield (x, mod, 1 if sample else seq // tm_moe, tm_moe), layer
    if not sample:
        shape4 = lambda a: a.reshape(batch, seq, HKV_G, HEAD_DIM)
        kv_state = ([shape4(k)[:, -min(w, seq):] for k, (w, _) in zip(k_new, DIL_GROUPS)],
                    [shape4(v)[:, -min(w, seq):] for v, (w, _) in zip(v_new, DIL_GROUPS)])
    return x.reshape(batch, seq, d), jnp.stack(conv_state, axis=0), kv_state


def _resume(trunk, x):
    try:
        return False, trunk.send(x)
    except StopIteration as done:
        return True, done.value


def kernel(x_prompt, x_sample, state_conv, cache_k_g0, cache_v_g0, cache_k_g1, cache_v_g1, cache_k_g2,
           cache_v_g2, c_prompt, c_sample, g_norm_mix, g_norm_ffn, w_ada, b_ada, w_in, w_conv, w_out_conv,
           g_norm_kv, w_ada_kv, b_ada_kv, w_kv, g_k, w_q, g_q, w_o, w_gu_dense, w_down_dense, w_router,
           b_router, w_gu_moe, w_down_moe):
    b_p, t_p, d = x_prompt.shape
    b_s, t_s, _ = x_sample.shape
    bf = lambda w: w.astype(BF16)
    weights = (g_norm_mix, g_norm_ffn, bf(w_in), w_conv, bf(w_out_conv), g_norm_kv, bf(w_kv), g_k, bf(w_q),
               g_q, bf(w_o), bf(w_gu_dense), bf(w_down_dense))

    c_all = jnp.concatenate([c_prompt, c_sample], axis=0)
    mods_all = _ada(c_all, w_ada, b_ada)
    mod_kv_all = _ada(c_all, w_ada_kv[None], b_ada_kv[None])[0]
    depth = w_ada.shape[0]
    mods_p = [mods_all[l, :b_p, None, :] for l in range(depth)]
    mods_s = [jnp.repeat(mods_all[l, b_p:], t_s, axis=0)[None] for l in range(depth)]
    mod_kv_p = mod_kv_all[:b_p, None, :]
    mod_kv_s = jnp.repeat(mod_kv_all[b_p:], t_s, axis=0)[None]

    pos_p = jnp.arange(t_p, dtype=jnp.int32)
    pos_s = PAST_LEN + jnp.arange(t_s, dtype=jnp.int32)
    kv_prev = ([cache_k_g0, cache_k_g1, cache_k_g2], [cache_v_g0, cache_v_g1, cache_v_g2])
    trunks = [_trunk(x_prompt, mods_p, mod_kv_p, pos_p, weights, min(PROMPT_TILE, t_p), None, None),
              _trunk(x_sample, mods_s, mod_kv_s, pos_s, weights, b_s * t_s, state_conv, kv_prev)]
    requests = [next(t) for t in trunks]
    results = None
    while results is None:
        layer = requests[0][1]
        xs = _moe_sparse_layer([r[0] for r in requests], g_norm_ffn[layer], w_router[layer // 2],
                               b_router[layer // 2], w_gu_moe, w_down_moe, layer // 2)
        steps = [_resume(t, x) for t, x in zip(trunks, xs)]
        if all(done for done, _ in steps):
            results = [value for _, value in steps]
        else:
            requests = [value for _, value in steps]
    (y_prompt, conv_p, (kp, vp)), (y_sample, conv_s, (ksn, vsn)) = results
    return (y_prompt, y_sample, conv_p, conv_s,
            kp[0], vp[0], kp[1], vp[1], kp[2], vp[2],
            ksn[0], vsn[0], ksn[1], vsn[1], ksn[2], vsn[2])
```

```python
import functools

import jax
import jax.numpy as jnp
from jax import lax
from jax.experimental import pallas as pl
from jax.experimental.pallas import tpu as pltpu

D_MODEL = 1024
PAST_LEN = 8192
CONV_W = 3
HEAD_DIM = 128
ROT_DIM = HEAD_DIM // 4
ROPE_THETA = 500000.0
DIL_GROUPS = ((128, 1), (512, 4), (2048, 16))
N_GROUPS = len(DIL_GROUPS)
HQ_G = 8
HKV_G = 2
REP = HQ_G // HKV_G
N_Q_HEADS = N_GROUPS * HQ_G
N_KV_HEADS = N_GROUPS * HKV_G
Q_WIDTH = N_Q_HEADS * HEAD_DIM
KV_WIDTH = N_KV_HEADS * HEAD_DIM
O_WIDTH = HQ_G * HEAD_DIM
GROUP_KV_WIDTH = HKV_G * HEAD_DIM
N_EXPERTS = 8
EPS = 1e-6
ATTN_SCALE = HEAD_DIM ** -0.5
NEG = -1e30
Q_BLOCK = 128

F32 = jnp.float32
BF16 = jnp.bfloat16
HIGHEST = lax.Precision.HIGHEST

V7X_VMEM_LIMIT_BYTES = 56 * 1024 * 1024
V7X_MXU_WIDTH = 256
F32_SUBLANES = 8
PROMPT_TILE = 512


def _cparams(*sem):
    return pltpu.CompilerParams(dimension_semantics=sem, vmem_limit_bytes=V7X_VMEM_LIMIT_BYTES)


def _resident(shape, index_map):
    return pl.BlockSpec(shape, index_map, pipeline_mode=pl.Buffered(1))


def _stacked(w, index):
    shape = w.shape[1:]
    return pl.BlockSpec((None,) + shape, lambda *_: (index,) + (0,) * len(shape), pipeline_mode=pl.Buffered(1))


def _silu(x):
    return x * jax.nn.sigmoid(x)


def _norm_mod(x, g, shift, scale):
    y = x * lax.rsqrt(jnp.mean(x * x, axis=-1, keepdims=True) + EPS) * g
    return y * (1.0 + scale) + shift


def _mod_part(mod_ref, k):
    return mod_ref[0, :, k * D_MODEL:(k + 1) * D_MODEL]


def _split_bf16(v):
    hi = v.astype(BF16)
    return hi, (v - hi.astype(F32)).astype(BF16)


def _ada_kernel(c_ref, w_ref, b_ref, o_ref):
    s_hi, s_lo = _split_bf16(_silu(c_ref[...]))
    w_hi, w_lo = _split_bf16(w_ref[0])
    acc = jnp.dot(s_hi, w_hi, preferred_element_type=F32)
    acc = acc + jnp.dot(s_lo, w_hi, preferred_element_type=F32)
    acc = acc + jnp.dot(s_hi, w_lo, preferred_element_type=F32)
    o_ref[0] = acc + b_ref[0]


def _ada(c, w, b):
    n_layers, d, n = w.shape
    m = c.shape[0]
    tn = 2048
    return pl.pallas_call(
        _ada_kernel,
        grid=(n_layers, n // tn),
        in_specs=[pl.BlockSpec((m, d), lambda l, j: (0, 0)),
                  pl.BlockSpec((1, d, tn), lambda l, j: (l, 0, j)),
                  pl.BlockSpec((1, 1, tn), lambda l, j: (l, 0, j))],
        out_specs=pl.BlockSpec((1, m, tn), lambda l, j: (l, 0, j)),
        out_shape=jax.ShapeDtypeStruct((n_layers, m, n), F32),
        compiler_params=_cparams("parallel", "parallel"),
        name="ada",
    )(c, w, b.reshape(n_layers, 1, n))


def _conv_kernel(*refs, tm, tiles_per_seq, seq_rows):
    if seq_rows is None:
        x_ref, mod_ref, g_ref, win_ref, wc_ref, wout_ref, xo_ref, ust_ref, carry_ref = refs
    else:
        x_ref, mod_ref, g_ref, win_ref, wc_ref, wout_ref, pa_ref, pb_ref, xo_ref, ust_ref = refs
    x = x_ref[...]
    h = _norm_mod(x, g_ref[...], _mod_part(mod_ref, 0), _mod_part(mod_ref, 1)).astype(BF16)
    b_gate = jnp.dot(h, win_ref[:, 0:D_MODEL], preferred_element_type=F32)
    c_gate = jnp.dot(h, win_ref[:, D_MODEL:2 * D_MODEL], preferred_element_type=F32)
    v = jnp.dot(h, win_ref[:, 2 * D_MODEL:3 * D_MODEL], preferred_element_type=F32)
    u = c_gate * v
    row = lax.broadcasted_iota(jnp.int32, (tm, 1), 0)
    r1 = pltpu.roll(u, 1, 0)
    r2 = pltpu.roll(u, 2, 0)
    if seq_rows is None:
        @pl.when(pl.program_id(0) % tiles_per_seq == 0)
        def _():
            carry_ref[...] = jnp.zeros_like(carry_ref)
        last = F32_SUBLANES - 1
        um1 = jnp.where(row == 0, carry_ref[last:last + 1, :], r1)
        um2 = jnp.where(row == 0, carry_ref[last - 1:last, :],
                        jnp.where(row == 1, carry_ref[last:last + 1, :], r2))
        carry_ref[...] = u[tm - F32_SUBLANES:tm, :]
        ust_ref[0] = u[tm - F32_SUBLANES:tm, :]
    else:
        rowm = row & (seq_rows - 1)
        um1 = jnp.where(rowm == 0, pa_ref[...], r1)
        um2 = jnp.where(rowm < 2, pb_ref[...], r2)
        ust_ref[...] = u
    y = um2 * wc_ref[0:1, :] + um1 * wc_ref[1:2, :] + u * wc_ref[2:3, :]
    z = (b_gate * y).astype(BF16)
    out = jnp.dot(z, wout_ref[...], preferred_element_type=F32)
    xo_ref[...] = x + _mod_part(mod_ref, 2) * out


def _conv_layer(x, mod, tiles_per_seq, g, w_in, w_conv, w_out, layer, tm, prev=None):
    rows, d = x.shape
    n_tiles = rows // tm
    mr = mod.shape[1]
    row_spec = pl.BlockSpec((tm, d), lambda i: (i, 0))
    in_specs = [row_spec,
                pl.BlockSpec((1, mr, 6 * d), lambda i: (i // tiles_per_seq, 0, 0)),
                pl.BlockSpec((1, d), lambda i: (0, 0)),
                _stacked(w_in, layer), _stacked(w_conv, layer), _stacked(w_out, layer)]
    args = [x, mod, g.reshape(1, d), w_in, w_conv, w_out]
    if prev is None:
        seq_rows = None
        n_seq = n_tiles // tiles_per_seq
        ust_shape = jax.ShapeDtypeStruct((n_seq, F32_SUBLANES, d), F32)
        ust_spec = pl.BlockSpec((1, F32_SUBLANES, d), lambda i: (i // tiles_per_seq, 0, 0))
        scratch = [pltpu.VMEM((F32_SUBLANES, d), F32)]
    else:
        seq_rows, pa, pb = prev
        in_specs += [row_spec, row_spec]
        args += [pa, pb]
        ust_shape = jax.ShapeDtypeStruct((rows, d), F32)
        ust_spec = row_spec
        scratch = []
    return pl.pallas_call(
        functools.partial(_conv_kernel, tm=tm, tiles_per_seq=tiles_per_seq, seq_rows=seq_rows),
        grid=(n_tiles,),
        in_specs=in_specs,
        out_specs=[row_spec, ust_spec],
        out_shape=[jax.ShapeDtypeStruct((rows, d), F32), ust_shape],
        scratch_shapes=scratch,
        compiler_params=_cparams("arbitrary"),
        name="conv_layer",
    )(*args)


def _ffn_kernel(x_ref, mod_ref, g_ref, wgu_ref, wd_ref, xo_ref, *, d_ff, n_chunks):
    x = x_ref[...]
    h = _norm_mod(x, g_ref[...], _mod_part(mod_ref, 3), _mod_part(mod_ref, 4)).astype(BF16)
    n_mxu = d_ff // V7X_MXU_WIDTH
    edges = [(c * n_mxu // n_chunks) * V7X_MXU_WIDTH for c in range(n_chunks)] + [d_ff]
    acc = None
    for lo, hi in zip(edges[:-1], edges[1:]):
        gate = jnp.dot(h, wgu_ref[:, lo:hi], preferred_element_type=F32)
        up = jnp.dot(h, wgu_ref[:, d_ff + lo:d_ff + hi], preferred_element_type=F32)
        a = (_silu(gate) * up).astype(BF16)
        part = jnp.dot(a, wd_ref[lo:hi, :], preferred_element_type=F32)
        acc = part if acc is None else acc + part
    xo_ref[...] = x + _mod_part(mod_ref, 5) * acc


def _ffn_layer(x, mod, tiles_per_seq, g, w_gu, w_down, index, tm):
    rows, d = x.shape
    d_ff = w_down.shape[1]
    mr = mod.shape[1]
    row_spec = pl.BlockSpec((tm, d), lambda i: (i, 0))
    return pl.pallas_call(
        functools.partial(_ffn_kernel, d_ff=d_ff, n_chunks=2),
        grid=(rows // tm,),
        in_specs=[row_spec,
                  pl.BlockSpec((1, mr, 6 * d), lambda i: (i // tiles_per_seq, 0, 0)),
                  pl.BlockSpec((1, d), lambda i: (0, 0)),
                  _stacked(w_gu, index), _stacked(w_down, index)],
        out_specs=row_spec,
        out_shape=jax.ShapeDtypeStruct((rows, d), F32),
        compiler_params=_cparams("parallel"),
        name="ffn_dense",
    )(x, mod, g.reshape(1, d), w_gu, w_down)


SEG_ALIGN = 16
SEG_SIZES = (512, 256, 128, 64, 32, 16)
SPARSE_TILE = 512
EXPERT_ROW_TILE = 256
EXPERT_FF_UNIT = 2 * V7X_MXU_WIDTH
EXPERT_STAGE_BUFFERS = 4


def _route_kernel(x_ref, mod_ref, g_ref, wrt_ref, br_ref, h_ref, slots_ref, gw_ref, seg_ref, *, tm):
    x = x_ref[...]
    h = _norm_mod(x, g_ref[...], _mod_part(mod_ref, 3), _mod_part(mod_ref, 4))
    h_ref[...] = h.astype(BF16)
    logits = lax.dot_general(wrt_ref[...], h, (((1,), (1,)), ((), ())), preferred_element_type=F32,
                             precision=HIGHEST) + br_ref[...]
    eidx = lax.broadcasted_iota(jnp.int32, logits.shape, 0).astype(F32)
    m1 = jnp.max(logits, axis=0, keepdims=True)
    i1 = jnp.min(jnp.where(logits == m1, eidx, float(N_EXPERTS)), axis=0, keepdims=True)
    sel1 = eidx == i1
    rest = jnp.where(sel1, -jnp.inf, logits)
    m2 = jnp.max(rest, axis=0, keepdims=True)
    i2 = jnp.min(jnp.where(rest == m2, eidx, float(N_EXPERTS)), axis=0, keepdims=True)
    sel2 = eidx == i2
    e2 = jnp.exp(m2 - m1)
    den = 1.0 + e2
    gw_ref[0] = jnp.concatenate([1.0 / den, e2 / den], axis=0)
    mask = jnp.where(sel1 | sel2, 1.0, 0.0)
    before = (lax.broadcasted_iota(jnp.int32, (tm, tm), 0) < lax.broadcasted_iota(jnp.int32, (tm, tm), 1))
    rank = jnp.dot(mask.astype(BF16), jnp.where(before, 1.0, 0.0).astype(BF16), preferred_element_type=F32)
    count = jnp.sum(mask, axis=1, keepdims=True)
    seg = jnp.floor((count + (SEG_ALIGN - 1)) * (1.0 / SEG_ALIGN)) * SEG_ALIGN
    start = jnp.zeros_like(seg)
    for e in range(N_EXPERTS - 1):
        start = start + jnp.where(eidx[:, 0:1] > e, seg[e:e + 1, :], 0.0)
    slot = start + rank
    slot1 = jnp.sum(jnp.where(sel1, slot, 0.0), axis=0, keepdims=True)
    slot2 = jnp.sum(jnp.where(sel2, slot, 0.0), axis=0, keepdims=True)
    slots_ref[0] = jnp.concatenate([slot1, slot2], axis=0).astype(jnp.int32)
    seg_ref[0] = jnp.broadcast_to(seg, (N_EXPERTS, HEAD_DIM)).astype(jnp.int32)


def _route(x, mod, tiles_per_seq, g, w_router, b_router, tm):
    rows, d = x.shape
    mr = mod.shape[1]
    n_tiles = rows // tm
    row_spec = pl.BlockSpec((tm, d), lambda i: (i, 0))
    pair_spec = pl.BlockSpec((1, 2, tm), lambda i: (i, 0, 0))
    return pl.pallas_call(
        functools.partial(_route_kernel, tm=tm),
        grid=(n_tiles,),
        in_specs=[row_spec,
                  pl.BlockSpec((1, mr, 6 * d), lambda i: (i // tiles_per_seq, 0, 0)),
                  pl.BlockSpec((1, d), lambda i: (0, 0)),
                  pl.BlockSpec((N_EXPERTS, d), lambda i: (0, 0)),
                  pl.BlockSpec((N_EXPERTS, 1), lambda i: (0, 0))],
        out_specs=[row_spec, pair_spec, pair_spec,
                   pl.BlockSpec((1, N_EXPERTS, HEAD_DIM), lambda i: (i, 0, 0))],
        out_shape=[jax.ShapeDtypeStruct((rows, d), BF16),
                   jax.ShapeDtypeStruct((n_tiles, 2, tm), jnp.int32),
                   jax.ShapeDtypeStruct((n_tiles, 2, tm), F32),
                   jax.ShapeDtypeStruct((n_tiles, N_EXPERTS, HEAD_DIM), jnp.int32)],
        compiler_params=_cparams("parallel"),
        name="moe_route",
    )(x, mod, g.reshape(1, d), w_router.T, b_router.reshape(N_EXPERTS, 1))


def _segment_copies(tile, seg_ref, loc_ref, goff_ref, packed, sorted_hbm, sem, to_sorted):
    copies = []
    tokens = (packed.shape[0] - N_EXPERTS * SEG_ALIGN) // 2
    max_seg = -(-tokens // SEG_ALIGN) * SEG_ALIGN
    for e in range(N_EXPERTS):
        seg = seg_ref[tile * N_EXPERTS + e]
        loc = loc_ref[tile * N_EXPERTS + e]
        goff = goff_ref[tile * N_EXPERTS + e]
        done = jnp.int32(0)
        for size in (s for s in SEG_SIZES if s <= max_seg):
            vm = packed.at[pl.ds(pl.multiple_of(loc + done, SEG_ALIGN), size), :]
            hb = sorted_hbm.at[pl.ds(pl.multiple_of(goff + done, SEG_ALIGN), size), :]
            src, dst = (vm, hb) if to_sorted else (hb, vm)
            copies.append(((seg & size) != 0, pltpu.make_async_copy(src, dst, sem)))
            done = done + (seg & size)
    return copies


def _start_all(copies):
    for cond, cp in copies:
        pl.when(cond)(cp.start)


def _wait_all(copies):
    for cond, cp in copies:
        pl.when(cond)(cp.wait)


def _dispatch_kernel(seg_ref, loc_ref, goff_ref, h_ref, slots_ref, sorted_in, sorted_out, stage_ref, sems,
                     *, n_slots, n_tiles):
    del sorted_in
    i = pl.program_id(0)
    par = i % 2

    def copies(tile, parity):
        return _segment_copies(tile, seg_ref, loc_ref, goff_ref, stage_ref.at[parity], sorted_out,
                               sems.at[parity], True)

    @pl.when(i >= 2)
    def _():
        _wait_all(copies(i - 2, par))

    tm = h_ref.shape[0]
    srow = lax.broadcasted_iota(jnp.int32, (n_slots, tm), 0)
    onehot = (srow == slots_ref[0, 0:1, :]) | (srow == slots_ref[0, 1:2, :])
    packed = jnp.dot(jnp.where(onehot, 1.0, 0.0).astype(BF16), h_ref[...], preferred_element_type=F32)
    stage_ref[par] = packed.astype(BF16)
    _start_all(copies(i, par))

    @pl.when(i == n_tiles - 1)
    def _():
        if n_tiles >= 2:
            _wait_all(copies(i - 1, 1 - par))
        _wait_all(copies(i, par))


def _dispatch(h, slots, seg, loc, goff, h_sorted, tm):
    rows, d = h.shape
    n_tiles = rows // tm
    n_slots = 2 * tm + N_EXPERTS * SEG_ALIGN
    return pl.pallas_call(
        functools.partial(_dispatch_kernel, n_slots=n_slots, n_tiles=n_tiles),
        grid_spec=pltpu.PrefetchScalarGridSpec(
            num_scalar_prefetch=3,
            grid=(n_tiles,),
            in_specs=[pl.BlockSpec((tm, d), lambda i, *_: (i, 0)),
                      pl.BlockSpec((1, 2, tm), lambda i, *_: (i, 0, 0)),
                      pl.BlockSpec(memory_space=pl.ANY)],
            out_specs=pl.BlockSpec(memory_space=pl.ANY),
            scratch_shapes=[pltpu.VMEM((2, n_slots, d), BF16), pltpu.SemaphoreType.DMA((2,))]),
        out_shape=jax.ShapeDtypeStruct(h_sorted.shape, BF16),
        input_output_aliases={5: 0},
        compiler_params=_cparams("arbitrary"),
        name="moe_dispatch",
    )(seg, loc, goff, h, slots, h_sorted)


def _experts_kernel(te_ref, jb_ref, nact_ref, phase_ref, nxt_ref, h_ref, wgu_hbm, wd_hbm, y_ref,
                    wg_ref, wd_ref, sg_ref, su_ref, sd_ref, sems, *, first_expert, d_ff, unit):
    del jb_ref
    j = pl.program_id(0)
    n_units = d_ff // unit
    depth = sg_ref.shape[0]
    active = j < nact_ref[0]

    def fetch(expert, u, buf):
        return (pltpu.make_async_copy(wgu_hbm.at[expert, :, pl.ds(u * unit, unit)], sg_ref.at[buf], sems.at[buf]),
                pltpu.make_async_copy(wgu_hbm.at[expert, :, pl.ds(d_ff + u * unit, unit)], su_ref.at[buf],
                                      sems.at[buf]),
                pltpu.make_async_copy(wd_hbm.at[expert, pl.ds(u * unit, unit), :], sd_ref.at[buf], sems.at[buf]))

    def start(expert, u):
        for cp in fetch(expert, u, u % depth):
            cp.start()

    def start_first(expert):
        for u in range(min(depth, n_units)):
            start(expert, u)

    def finish(expert, u):
        for cp in fetch(expert, u, u % depth):
            cp.wait()
        wg_ref[u, :, 0:unit] = sg_ref[u % depth].astype(BF16)
        wg_ref[u, :, unit:2 * unit] = su_ref[u % depth].astype(BF16)
        wd_ref[u] = sd_ref[u % depth].astype(BF16)
        if u + depth < n_units:
            start(expert, u + depth)

    def unit_out(h, u):
        gate = jnp.dot(h, wg_ref[u, :, 0:unit], preferred_element_type=F32)
        up = jnp.dot(h, wg_ref[u, :, unit:2 * unit], preferred_element_type=F32)
        a = (_silu(gate) * up).astype(BF16)
        return jnp.dot(a, wd_ref[u], preferred_element_type=F32)

    @pl.when(active & (j == 0))
    def _():
        expert = first_expert + te_ref[0]
        start_first(expert)
        for u in range(n_units):
            finish(expert, u)

    phase = phase_ref[j]
    swap = active & (phase >= 2)

    @pl.when(active & (phase == 1))
    def _():
        start_first(first_expert + nxt_ref[j])

    @pl.when(active & jnp.logical_not(swap))
    def _():
        h = h_ref[...]
        acc = unit_out(h, 0)
        for u in range(1, n_units):
            acc = acc + unit_out(h, u)
        y_ref[...] = acc.astype(BF16)

    @pl.when(swap)
    def _():
        nxt = first_expert + nxt_ref[j]
        pl.when(phase == 3)(functools.partial(start_first, nxt))
        h = h_ref[...]
        acc = None
        for u in range(n_units):
            part = unit_out(h, u)
            acc = part if acc is None else acc + part
            finish(nxt, u)
        y_ref[...] = acc.astype(BF16)

    @pl.when(jnp.logical_not(active))
    def _():
        y_ref[...] = jnp.zeros_like(y_ref)


def _experts(h_sorted, tile_expert, tile_block, n_active, tile_phase, tile_next, w_gu, w_down, index, tmf):
    n_rows, d = h_sorted.shape
    d_ff = w_down.shape[2]
    unit = EXPERT_FF_UNIT
    n_units = d_ff // unit
    depth = EXPERT_STAGE_BUFFERS
    w_gu = w_gu.reshape(-1, d, 2 * d_ff)
    w_down = w_down.reshape(-1, d_ff, d)
    return pl.pallas_call(
        functools.partial(_experts_kernel, first_expert=index * N_EXPERTS, d_ff=d_ff, unit=unit),
        grid_spec=pltpu.PrefetchScalarGridSpec(
            num_scalar_prefetch=5,
            grid=(n_rows // tmf,),
            in_specs=[pl.BlockSpec((tmf, d), lambda j, te, jb, *_: (jb[j], 0)),
                      pl.BlockSpec(memory_space=pl.ANY),
                      pl.BlockSpec(memory_space=pl.ANY)],
            out_specs=pl.BlockSpec((tmf, d), lambda j, *_: (j, 0)),
            scratch_shapes=[pltpu.VMEM((n_units, d, 2 * unit), BF16),
                            pltpu.VMEM((n_units, unit, d), BF16),
                            pltpu.VMEM((depth, d, unit), F32),
                            pltpu.VMEM((depth, d, unit), F32),
                            pltpu.VMEM((depth, unit, d), F32),
                            pltpu.SemaphoreType.DMA((depth,))]),
        out_shape=jax.ShapeDtypeStruct((n_rows, d), BF16),
        compiler_params=_cparams("arbitrary"),
        name="moe_experts_sorted",
    )(tile_expert, tile_block, n_active, tile_phase, tile_next, h_sorted, w_gu, w_down)


def _combine_kernel(seg_ref, loc_ref, goff_ref, x_ref, mod_ref, slots_ref, gw_ref, sorted_hbm, xo_ref,
                    stage_ref, sems, *, n_slots, n_tiles):
    i = pl.program_id(0)
    par = i % 2

    def copies(tile, parity):
        return _segment_copies(tile, seg_ref, loc_ref, goff_ref, stage_ref.at[parity], sorted_hbm,
                               sems.at[parity], False)

    @pl.when(i == 0)
    def _():
        stage_ref[...] = jnp.zeros_like(stage_ref)
        _start_all(copies(i, par))

    @pl.when(i + 1 < n_tiles)
    def _():
        _start_all(copies(i + 1, 1 - par))

    _wait_all(copies(i, par))
    y = stage_ref[par]
    tm = x_ref.shape[0]
    scol = lax.broadcasted_iota(jnp.int32, (tm, n_slots), 1)
    gates = (jnp.where(scol == slots_ref[:, 0:1], gw_ref[:, 0:1], 0.0)
             + jnp.where(scol == slots_ref[:, 1:2], gw_ref[:, 1:2], 0.0))
    moe = jnp.dot(gates.astype(BF16), y, preferred_element_type=F32)
    xo_ref[...] = x_ref[...] + _mod_part(mod_ref, 5) * moe


def _combine(x, mod, tiles_per_seq, slots_col, gw_col, y_sorted, seg, loc, goff, tm):
    rows, d = x.shape
    mr = mod.shape[1]
    n_tiles = rows // tm
    n_slots = 2 * tm + N_EXPERTS * SEG_ALIGN
    row_spec = pl.BlockSpec((tm, d), lambda i, *_: (i, 0))
    pair_spec = pl.BlockSpec((tm, 2), lambda i, *_: (i, 0))
    return pl.pallas_call(
        functools.partial(_combine_kernel, n_slots=n_slots, n_tiles=n_tiles),
        grid_spec=pltpu.PrefetchScalarGridSpec(
            num_scalar_prefetch=3,
            grid=(n_tiles,),
            in_specs=[row_spec,
                      pl.BlockSpec((1, mr, 6 * d), lambda i, *_: (i // tiles_per_seq, 0, 0)),
                      pair_spec, pair_spec,
                      pl.BlockSpec(memory_space=pl.ANY)],
            out_specs=row_spec,
            scratch_shapes=[pltpu.VMEM((2, n_slots, d), BF16), pltpu.SemaphoreType.DMA((2,))]),
        out_shape=jax.ShapeDtypeStruct((rows, d), F32),
        compiler_params=_cparams("arbitrary"),
        name="moe_combine",
    )(seg, loc, goff, x, mod, slots_col, gw_col, y_sorted)


def _moe_sparse_layer(parts, g, w_router, b_router, w_gu, w_down, index):
    tmf = EXPERT_ROW_TILE
    d = parts[0][0].shape[1]
    routed = [_route(x, mod, tps, g, w_router, b_router, tm) for x, mod, tps, tm in parts]
    tiles = [x.shape[0] // tm for x, _, _, tm in parts]
    seg = jnp.concatenate([r[3][:, :, 0] for r in routed], axis=0)
    loc = jnp.cumsum(seg, axis=1) - seg
    region = ((jnp.sum(seg, axis=0) + tmf - 1) // tmf) * tmf
    region_end = jnp.cumsum(region)
    goff = (region_end - region)[None, :] + jnp.cumsum(seg, axis=0) - seg
    n_sorted = sum(2 * x.shape[0] for x, _, _, _ in parts) + sum(tiles) * N_EXPERTS * (SEG_ALIGN - 1)
    n_sorted += N_EXPERTS * (tmf - SEG_ALIGN)
    n_sorted = ((n_sorted + tmf - 1) // tmf) * tmf
    n_row_tiles = n_sorted // tmf
    n_active = region_end[-1] // tmf
    tile_ids = jnp.arange(n_row_tiles, dtype=jnp.int32)
    tile_block = jnp.maximum(jnp.minimum(tile_ids, n_active - 1), 0)
    tile_expert = jnp.sum((region_end // tmf)[None, :] <= tile_block[:, None], axis=1)
    tile_expert = jnp.minimum(tile_expert, N_EXPERTS - 1).astype(jnp.int32)
    ids = jnp.arange(N_EXPERTS, dtype=jnp.int32)
    later = jnp.where((ids[None, :] > ids[:, None]) & (region[None, :] > 0), ids[None, :], N_EXPERTS)
    next_expert = jnp.min(later, axis=1)
    tile_next = jnp.where(next_expert < N_EXPERTS, next_expert, -1)[tile_expert].astype(jnp.int32)
    from_end = (region_end // tmf)[tile_expert] - 1 - tile_block
    single = (region // tmf)[tile_expert] == 1
    tile_phase = jnp.where(from_end == 1, 1, jnp.where(from_end == 0, jnp.where(single, 3, 2), 0))
    tile_phase = jnp.where(tile_next >= 0, tile_phase, 0).astype(jnp.int32)
    tables, first = [], 0
    for n in tiles:
        cut = lambda a: a[first:first + n].reshape(-1).astype(jnp.int32)
        tables.append((cut(seg), cut(loc), cut(goff)))
        first += n
    h_sorted = jnp.zeros((n_sorted, d), BF16)
    for (_, _, _, tm), (h, slots, _, _), tab in zip(parts, routed, tables):
        h_sorted = _dispatch(h, slots, *tab, h_sorted, tm)
    y_sorted = _experts(h_sorted, tile_expert, tile_block.astype(jnp.int32),
                        n_active.reshape(1).astype(jnp.int32), tile_phase, tile_next, w_gu, w_down, index, tmf)
    outs = []
    for (x, mod, tps, tm), (_, slots, gw, _), tab in zip(parts, routed, tables):
        rows = x.shape[0]
        slots_col = jnp.transpose(slots, (0, 2, 1)).reshape(rows, 2)
        gw_col = jnp.transpose(gw, (0, 2, 1)).reshape(rows, 2)
        outs.append(_combine(x, mod, tps, slots_col, gw_col, y_sorted, *tab, tm))
    return outs


def _rope_tables(pos):
    half = ROT_DIM // 2
    inv = jnp.float32(ROPE_THETA) ** (-jnp.arange(half, dtype=jnp.float32) / half)
    ang = pos.astype(jnp.float32)[:, None] * inv[None, :]
    cos, sin = jnp.cos(ang), jnp.sin(ang)
    t = pos.shape[0]
    pad1 = jnp.ones((t, HEAD_DIM - ROT_DIM), F32)
    pad0 = jnp.zeros((t, HEAD_DIM - ROT_DIM), F32)
    return (jnp.concatenate([cos, cos, pad1], axis=1), jnp.concatenate([-sin, sin, pad0], axis=1))


def _swap_rot_halves(v):
    half = ROT_DIM // 2
    return jnp.concatenate([v[half:ROT_DIM], v[:half], v[ROT_DIM:]])


def _rope_gain_tables(cos_ref, sin_ref, g_ref, gs_ref, scale):
    return cos_ref[...] * (g_ref[...] * scale), sin_ref[...] * (gs_ref[...] * scale)


def _head_norm_rope(xh, a, b):
    half = ROT_DIM // 2
    ones = jnp.ones((HEAD_DIM, HEAD_DIM), BF16)
    ss = jnp.dot((xh * xh).astype(BF16), ones, preferred_element_type=F32)
    rs = lax.rsqrt(ss * (1.0 / HEAD_DIM) + EPS)
    lane = lax.broadcasted_iota(jnp.int32, xh.shape, 1)
    swapped = jnp.where(lane < half, pltpu.roll(xh, HEAD_DIM - half, 1), pltpu.roll(xh, half, 1))
    return (xh * a + swapped * b) * rs


SPLIT = 4
SPLIT_TILE = SPLIT * Q_BLOCK
SPLIT_GROUPS = (1, 2)


def _kv_kernel(x_ref, mod_ref, g_ref, wkv_ref, gk_ref, cos_ref, sin_ref, *out_refs, split):
    x = x_ref[...]
    h = _norm_mod(x, g_ref[...], mod_ref[0, :, 0:D_MODEL], mod_ref[0, :, D_MODEL:2 * D_MODEL]).astype(BF16)
    rope_a, rope_b = _rope_gain_tables(cos_ref, sin_ref, gk_ref.at[0:1], gk_ref.at[1:2], 1.0)
    k_refs, v_refs = out_refs[:N_GROUPS], out_refs[N_GROUPS:2 * N_GROUPS]
    kv_all = jnp.dot(h, wkv_ref[...], preferred_element_type=F32)
    for hd in range(N_KV_HEADS):
        kh = kv_all[:, hd * HEAD_DIM:(hd + 1) * HEAD_DIM]
        vh = kv_all[:, KV_WIDTH + hd * HEAD_DIM:KV_WIDTH + (hd + 1) * HEAD_DIM]
        g_idx, k_idx = divmod(hd, HKV_G)
        sl = slice(k_idx * HEAD_DIM, (k_idx + 1) * HEAD_DIM)
        kh = _head_norm_rope(kh, rope_a, rope_b)
        rows_of_head = pl.ds(k_idx, kh.shape[0], stride=HKV_G)
        k_refs[g_idx][rows_of_head, :] = kh
        v_refs[g_idx][rows_of_head, :] = vh
        if split and g_idx in SPLIT_GROUPS:
            n = SPLIT_GROUPS.index(g_idx)
            scr = out_refs[-1]
            for j, (val, dst) in enumerate(((kh, out_refs[2 * N_GROUPS + 2 * n]),
                                            (vh, out_refs[2 * N_GROUPS + 2 * n + 1]))):
                slot = (n * HKV_G + k_idx) * 2 + j
                scr[slot] = val
                for c in range(SPLIT):
                    dst[c * Q_BLOCK:(c + 1) * Q_BLOCK, sl] = (
                        scr[slot, pl.ds(c, Q_BLOCK, stride=SPLIT), :].astype(BF16))


def _kv_proj(x, mod_kv, tiles_per_seq, g, w_kv, g_k, cos, sin, tm, split):
    rows, d = x.shape
    mr = mod_kv.shape[1]
    assert not split or tm == SPLIT_TILE
    row_spec = pl.BlockSpec((tm, d), lambda i: (i, 0))
    tab_spec = pl.BlockSpec((tm, HEAD_DIM), lambda i: (i % tiles_per_seq, 0))
    out_spec = pl.BlockSpec((tm, GROUP_KV_WIDTH), lambda i: (i, 0))
    nat_spec = pl.BlockSpec((HKV_G * tm, HEAD_DIM), lambda i: (i, 0))
    n_split = 2 * len(SPLIT_GROUPS) if split else 0
    return pl.pallas_call(
        functools.partial(_kv_kernel, split=split),
        grid=(rows // tm,),
        in_specs=[row_spec,
                  pl.BlockSpec((1, mr, 2 * d), lambda i: (i // tiles_per_seq, 0, 0)),
                  pl.BlockSpec((1, d), lambda i: (0, 0)),
                  _resident((d, 2 * KV_WIDTH), lambda i: (0, 0)),
                  pl.BlockSpec((2, HEAD_DIM), lambda i: (0, 0)),
                  tab_spec, tab_spec],
        out_specs=[nat_spec] * (2 * N_GROUPS) + [out_spec] * n_split,
        out_shape=[jax.ShapeDtypeStruct((HKV_G * rows, HEAD_DIM), F32)] * (2 * N_GROUPS)
                  + [jax.ShapeDtypeStruct((rows, GROUP_KV_WIDTH), BF16)] * n_split,
        scratch_shapes=[pltpu.VMEM((n_split * HKV_G, tm, HEAD_DIM), F32)] if split else [],
        compiler_params=_cparams("parallel"),
        name="kv_proj",
    )(x, mod_kv, g.reshape(1, d), w_kv, jnp.stack([g_k, _swap_rot_halves(g_k)]), cos, sin)


def _q_kernel(x_ref, mod_ref, g_ref, wq_ref, gq_ref, cos_ref, sin_ref, *refs, split):
    q_refs = refs[:N_GROUPS]
    x = x_ref[...]
    h = _norm_mod(x, g_ref[...], _mod_part(mod_ref, 0), _mod_part(mod_ref, 1)).astype(BF16)
    rope_a, rope_b = _rope_gain_tables(cos_ref, sin_ref, gq_ref.at[0:1], gq_ref.at[1:2], ATTN_SCALE)
    q_all = jnp.dot(h, wq_ref[...], preferred_element_type=F32)
    for hd in range(N_Q_HEADS):
        g_idx, h_idx = divmod(hd, HQ_G)
        sl = slice(h_idx * HEAD_DIM, (h_idx + 1) * HEAD_DIM)
        qh = _head_norm_rope(q_all[:, hd * HEAD_DIM:(hd + 1) * HEAD_DIM], rope_a, rope_b)
        if split and g_idx in SPLIT_GROUPS:
            scr = refs[N_GROUPS]
            n = SPLIT_GROUPS.index(g_idx) * HQ_G + h_idx
            scr[n] = qh
            for c in range(SPLIT):
                q_refs[g_idx][c * Q_BLOCK:(c + 1) * Q_BLOCK, sl] = (
                    scr[n, pl.ds(c, Q_BLOCK, stride=SPLIT), :].astype(BF16))
        else:
            q_refs[g_idx][:, sl] = qh.astype(BF16)


def _q_proj(x, mod, tiles_per_seq, g, w_q, index, g_q, cos, sin, tm, split):
    rows, d = x.shape
    mr = mod.shape[1]
    assert not split or tm == SPLIT_TILE
    row_spec = pl.BlockSpec((tm, d), lambda i: (i, 0))
    tab_spec = pl.BlockSpec((tm, HEAD_DIM), lambda i: (i % tiles_per_seq, 0))
    return pl.pallas_call(
        functools.partial(_q_kernel, split=split),
        grid=(rows // tm,),
        in_specs=[row_spec,
                  pl.BlockSpec((1, mr, 6 * d), lambda i: (i // tiles_per_seq, 0, 0)),
                  pl.BlockSpec((1, d), lambda i: (0, 0)),
                  _stacked(w_q, index),
                  pl.BlockSpec((2, HEAD_DIM), lambda i: (0, 0)),
                  tab_spec, tab_spec],
        out_specs=[pl.BlockSpec((tm, O_WIDTH), lambda i: (i, 0))] * N_GROUPS,
        out_shape=[jax.ShapeDtypeStruct((rows, O_WIDTH), BF16)] * N_GROUPS,
        scratch_shapes=[pltpu.VMEM((len(SPLIT_GROUPS) * HQ_G, tm, HEAD_DIM), F32)] if split else [],
        compiler_params=_cparams("parallel"),
        name="q_proj",
    )(x, mod, g.reshape(1, d), w_q, jnp.stack([g_q, _swap_rot_halves(g_q)]), cos, sin)


def _attn_prompt_kernel(*refs, step, has_prev, n_blk):
    if has_prev:
        q_ref, k_ref, v_ref, kp_ref, vp_ref, o_ref, st_ref = refs
        not_first = pl.program_id(1) > 0
    else:
        q_ref, k_ref, v_ref, o_ref, st_ref = refs
    nq = REP * Q_BLOCK
    lane = lax.broadcasted_iota(jnp.int32, (Q_BLOCK, HEAD_DIM), 1)
    max_kb = 2 if step == 1 else n_blk
    col = lax.broadcasted_iota(jnp.int32, (nq, max_kb * Q_BLOCK), 1)
    dist = (max_kb - 1) * Q_BLOCK + (lax.broadcasted_iota(jnp.int32, col.shape, 0) & (Q_BLOCK - 1)) - col
    reach = (dist >= 0) & (dist <= step * Q_BLOCK) & ((dist & (step - 1)) == 0)
    for i in range(n_blk):
        first_kb = i - 1 if step == 1 else 0
        if first_kb < 0 and not has_prev:
            first_kb = 0
        kbs = list(range(first_kb, i + 1))
        valid = reach[:, (max_kb - len(kbs)) * Q_BLOCK:]
        if first_kb < 0:
            valid = valid & (not_first | (col >= Q_BLOCK))
        for k in range(HKV_G):
            ksl = slice(k * HEAD_DIM, (k + 1) * HEAD_DIM)
            q = jnp.concatenate(
                [q_ref[i, :, (k * REP + r) * HEAD_DIM:(k * REP + r + 1) * HEAD_DIM] for r in range(REP)], axis=0)
            if has_prev:
                hrows = pl.ds(k, Q_BLOCK, stride=HKV_G)
                keys = jnp.concatenate([kp_ref[hrows, :] if j < 0 else k_ref[j, hrows, :] for j in kbs], axis=0)
                vals = jnp.concatenate([vp_ref[hrows, :] if j < 0 else v_ref[j, hrows, :] for j in kbs], axis=0)
            else:
                keys = jnp.concatenate([k_ref[j, :, ksl] for j in kbs], axis=0)
                vals = jnp.concatenate([v_ref[j, :, ksl] for j in kbs], axis=0)
            s = lax.dot_general(q, keys.astype(BF16), (((1,), (1,)), ((), ())), preferred_element_type=F32)
            s = jnp.where(valid, s, NEG)
            m = jnp.max(s, axis=1, keepdims=True)
            p = jnp.exp(s - m)
            den = jnp.sum(p, axis=1, keepdims=True)
            out = jnp.dot(p.astype(BF16), vals.astype(BF16), preferred_element_type=F32) / den
            lse = m + jnp.log(den)
            stats = jnp.zeros((Q_BLOCK, HEAD_DIM), F32)
            for r in range(REP):
                rs = slice(r * Q_BLOCK, (r + 1) * Q_BLOCK)
                o_ref[i, :, (k * REP + r) * HEAD_DIM:(k * REP + r + 1) * HEAD_DIM] = out[rs].astype(BF16)
                stats = jnp.where(lane == r, lse[rs], stats)
            st_ref[i, :, ksl] = stats


def _attn_prompt_group(q, k_g, v_g, group, batch, seq):
    _, dil = DIL_GROUPS[group]
    rows = batch * seq
    n_tiles = seq // SPLIT_TILE
    if group in SPLIT_GROUPS:
        step = dil // SPLIT
        lead = (batch, n_tiles, SPLIT, Q_BLOCK)
        view = lambda a: a.reshape(*lead, a.shape[-1])
        spec = lambda w: pl.BlockSpec((None, n_tiles, None, Q_BLOCK, w), lambda b, c: (b, 0, c, 0, 0))
        out_shape = [jax.ShapeDtypeStruct((*lead, O_WIDTH), BF16),
                     jax.ShapeDtypeStruct((*lead, GROUP_KV_WIDTH), F32)]
        out, stats = pl.pallas_call(
            functools.partial(_attn_prompt_kernel, step=step, has_prev=False, n_blk=n_tiles),
            grid=(batch, SPLIT),
            in_specs=[spec(O_WIDTH), spec(GROUP_KV_WIDTH), spec(GROUP_KV_WIDTH)],
            out_specs=[spec(O_WIDTH), spec(GROUP_KV_WIDTH)],
            out_shape=out_shape,
            compiler_params=_cparams("parallel", "parallel"),
            name=f"attn_prompt_g{group}",
        )(view(q), view(k_g), view(v_g))
    else:
        lead = (batch * n_tiles, SPLIT)
        view = lambda a, r: a.reshape(*lead, r, a.shape[-1])
        spec = lambda r, w: pl.BlockSpec((None, SPLIT, r, w), lambda b, i: (b * n_tiles + i, 0, 0, 0))
        kv_rows = HKV_G * Q_BLOCK
        kv_spec = spec(kv_rows, HEAD_DIM)
        prev = pl.BlockSpec((None, None, kv_rows, HEAD_DIM),
                            lambda b, i: (jnp.maximum(b * n_tiles + i - 1, 0), SPLIT - 1, 0, 0))
        out_shape = [jax.ShapeDtypeStruct((*lead, Q_BLOCK, O_WIDTH), BF16),
                     jax.ShapeDtypeStruct((*lead, Q_BLOCK, GROUP_KV_WIDTH), F32)]
        k4, v4 = view(k_g, kv_rows), view(v_g, kv_rows)
        out, stats = pl.pallas_call(
            functools.partial(_attn_prompt_kernel, step=dil, has_prev=True, n_blk=SPLIT),
            grid=(batch, n_tiles),
            in_specs=[spec(Q_BLOCK, O_WIDTH), kv_spec, kv_spec, prev, prev],
            out_specs=[spec(Q_BLOCK, O_WIDTH), spec(Q_BLOCK, GROUP_KV_WIDTH)],
            out_shape=out_shape,
            compiler_params=_cparams("parallel", "arbitrary"),
            name=f"attn_prompt_g{group}",
        )(view(q, Q_BLOCK), k4, v4, k4, v4)
    return out.reshape(rows, O_WIDTH), stats.reshape(rows, GROUP_KV_WIDTH)


def _merge_o_kernel(x_ref, mod_ref, o0_ref, o1_ref, o2_ref, s0_ref, s1_ref, s2_ref, wo_ref, xo_ref,
                    on_ref, sn_ref):
    for n, (o_ref, s_ref) in enumerate(((o1_ref, s1_ref), (o2_ref, s2_ref))):
        for c in range(SPLIT):
            rows = slice(c * Q_BLOCK, (c + 1) * Q_BLOCK)
            dst = pl.ds(c, Q_BLOCK, stride=SPLIT)
            for hd in range(HQ_G):
                on_ref[n, hd, dst, :] = o_ref[rows, hd * HEAD_DIM:(hd + 1) * HEAD_DIM].astype(F32)
            for k in range(HKV_G):
                sn_ref[n, k, dst, :] = s_ref[rows, k * HEAD_DIM:(k + 1) * HEAD_DIM]
    heads = []
    for k in range(HKV_G):
        for r in range(REP):
            hd = k * REP + r
            lses = [s0_ref[:, k * HEAD_DIM + r:k * HEAD_DIM + r + 1],
                    sn_ref[0, k, :, r:r + 1], sn_ref[1, k, :, r:r + 1]]
            m = jnp.maximum(jnp.maximum(lses[0], lses[1]), lses[2])
            es = [jnp.exp(l - m) for l in lses]
            den = es[0] + es[1] + es[2]
            sl = slice(hd * HEAD_DIM, (hd + 1) * HEAD_DIM)
            o0 = o0_ref[:, sl].astype(F32)
            o = o0 + (es[1] / den) * (on_ref[0, hd] - o0) + (es[2] / den) * (on_ref[1, hd] - o0)
            heads.append(o.astype(BF16))
    o = jnp.concatenate(heads, axis=1)
    y = jnp.dot(o, wo_ref[...], preferred_element_type=F32)
    xo_ref[...] = x_ref[...] + _mod_part(mod_ref, 2) * y


def _merge_o(x, mod, tiles_per_seq, outs, stats, w_o, index, tm):
    rows, d = x.shape
    mr = mod.shape[1]
    assert tm == SPLIT_TILE
    row_spec = pl.BlockSpec((tm, d), lambda i: (i, 0))
    o_spec = pl.BlockSpec((tm, O_WIDTH), lambda i: (i, 0))
    st_spec = pl.BlockSpec((tm, GROUP_KV_WIDTH), lambda i: (i, 0))
    return pl.pallas_call(
        _merge_o_kernel,
        grid=(rows // tm,),
        in_specs=[row_spec,
                  pl.BlockSpec((1, mr, 6 * d), lambda i: (i // tiles_per_seq, 0, 0))]
                 + [o_spec] * 3 + [st_spec] * 3 + [_stacked(w_o, index)],
        out_specs=row_spec,
        out_shape=jax.ShapeDtypeStruct((rows, d), F32),
        scratch_shapes=[pltpu.VMEM((len(SPLIT_GROUPS), HQ_G, tm, HEAD_DIM), F32),
                        pltpu.VMEM((len(SPLIT_GROUPS), HKV_G, tm, HEAD_DIM), F32)],
        compiler_params=_cparams("parallel"),
        name="merge_o_proj",
    )(x, mod, *outs, *stats, w_o)


def _attn_sample_kernel(*refs, n_new, write_buffers):
    q_ref = refs[0]
    nk_refs = refs[1:1 + N_GROUPS]
    nv_refs = refs[1 + N_GROUPS:1 + 2 * N_GROUPS]
    ck_refs = refs[1 + 2 * N_GROUPS:1 + 3 * N_GROUPS]
    cv_refs = refs[1 + 3 * N_GROUPS:1 + 4 * N_GROUPS]
    o_ref = refs[1 + 4 * N_GROUPS]
    nq = REP * n_new
    n_new_rows = HKV_G * n_new
    qi = lax.broadcasted_iota(jnp.int32, (nq, 1), 0) & (n_new - 1)

    if write_buffers:
        ok_refs = refs[2 + 4 * N_GROUPS:2 + 5 * N_GROUPS]
        ov_refs = refs[2 + 5 * N_GROUPS:2 + 6 * N_GROUPS]
        for g in range(N_GROUPS):
            for c_ref, n_ref, o_buf in ((ck_refs[g], nk_refs[g], ok_refs[g]), (cv_refs[g], nv_refs[g], ov_refs[g])):
                n_rows = c_ref.shape[1]
                o_buf[0, 0:n_rows - n_new_rows, :] = c_ref[0, n_new_rows:n_rows, :]
                o_buf[0, n_rows - n_new_rows:n_rows, :] = n_ref[0]

    for k in range(HKV_G):
        outs, lses = [], []
        for g, (window, dil) in enumerate(DIL_GROUPS):
            length = ck_refs[g].shape[1] // HKV_G
            q = q_ref[0, g, k]
            qf = q.astype(F32)
            keys = ck_refs[g][0, pl.ds(k, length, stride=HKV_G), :].astype(BF16)
            vals = cv_refs[g][0, pl.ds(k, length, stride=HKV_G), :].astype(BF16)
            s = lax.dot_general(q, keys, (((1,), (1,)), ((), ())), preferred_element_type=F32)
            idx = lax.broadcasted_iota(jnp.int32, (nq, length), 1)
            diff = length + qi - idx
            s = jnp.where(((diff & (dil - 1)) == 0) & (diff <= window), s, NEG)
            new_k = nk_refs[g][0]
            new_v = nv_refs[g][0]
            s_new = []
            for j in range(n_new):
                sj = jnp.sum(qf * new_k[HKV_G * j + k:HKV_G * j + k + 1, :], axis=1, keepdims=True)
                ok = (qi >= j) & (((qi - j) & (dil - 1)) == 0)
                s_new.append(jnp.where(ok, sj, NEG))
            m = jnp.max(s, axis=1, keepdims=True)
            for sj in s_new:
                m = jnp.maximum(m, sj)
            p = jnp.exp(s - m)
            den = jnp.sum(p, axis=1, keepdims=True)
            acc = jnp.dot(p.astype(BF16), vals, preferred_element_type=F32)
            for j, sj in enumerate(s_new):
                pj = jnp.exp(sj - m)
                den = den + pj
                acc = acc + pj * new_v[HKV_G * j + k:HKV_G * j + k + 1, :]
            outs.append(acc / den)
            lses.append(m + jnp.log(den))
        m = jnp.maximum(jnp.maximum(lses[0], lses[1]), lses[2])
        es = [jnp.exp(l - m) for l in lses]
        den = es[0] + es[1] + es[2]
        o_ref[0, k] = sum((e / den) * o for e, o in zip(es, outs))


def _attn_sample(q, new_k, new_v, cache_k, cache_v, n_seq, n_new, write_buffers):
    assert HKV_G * n_new == F32_SUBLANES
    q5 = q.reshape(n_seq, n_new, N_GROUPS, HKV_G, REP, HEAD_DIM)
    q5 = jnp.transpose(q5, (0, 2, 3, 4, 1, 5)).reshape(n_seq, N_GROUPS, HKV_G, REP * n_new, HEAD_DIM)
    rows2 = lambda a, n: a.reshape(n_seq, n * HKV_G, HEAD_DIM)
    nk = [rows2(a, n_new) for a in new_k]
    nv = [rows2(a, n_new) for a in new_v]
    ck = [rows2(c, c.shape[1]) for c in cache_k]
    cv = [rows2(c, c.shape[1]) for c in cache_v]
    new_spec = pl.BlockSpec((1, HKV_G * n_new, HEAD_DIM), lambda b: (b, 0, 0))
    cache_specs = [pl.BlockSpec((1, c.shape[1], HEAD_DIM), lambda b: (b, 0, 0)) for c in ck]
    cache_shapes = [jax.ShapeDtypeStruct(c.shape, F32) for c in ck]
    n_buf = 2 if write_buffers else 0
    res = pl.pallas_call(
        functools.partial(_attn_sample_kernel, n_new=n_new, write_buffers=write_buffers),
        grid=(n_seq,),
        in_specs=[pl.BlockSpec((1, N_GROUPS, HKV_G, REP * n_new, HEAD_DIM), lambda b: (b, 0, 0, 0, 0))]
                 + [new_spec] * (2 * N_GROUPS) + cache_specs + cache_specs,
        out_specs=[pl.BlockSpec((1, HKV_G, REP * n_new, HEAD_DIM), lambda b: (b, 0, 0, 0))]
                  + cache_specs * n_buf,
        out_shape=[jax.ShapeDtypeStruct((n_seq, HKV_G, REP * n_new, HEAD_DIM), F32)] + cache_shapes * n_buf,
        compiler_params=_cparams("parallel"),
        name="attn_sample",
    )(q5, *nk, *nv, *ck, *cv)
    o = res[0].reshape(n_seq, HKV_G, REP, n_new, HEAD_DIM)
    o = jnp.transpose(o, (0, 3, 1, 2, 4)).reshape(n_seq * n_new, O_WIDTH)
    if not write_buffers:
        return o, None, None
    shape4 = lambda a: a.reshape(n_seq, a.shape[1] // HKV_G, HKV_G, HEAD_DIM)
    new_ck = [shape4(a) for a in res[1:1 + N_GROUPS]]
    new_cv = [shape4(a) for a in res[1 + N_GROUPS:1 + 2 * N_GROUPS]]
    return o, new_ck, new_cv


def _o_proj_kernel(x_ref, mod_ref, o_ref, wo_ref, xo_ref):
    y = jnp.dot(o_ref[...].astype(BF16), wo_ref[...], preferred_element_type=F32)
    xo_ref[...] = x_ref[...] + _mod_part(mod_ref, 2) * y


def _o_proj(x, mod, o, w_o, index):
    rows, d = x.shape
    mr = mod.shape[1]
    full = pl.BlockSpec((rows, d), lambda i: (0, 0))
    return pl.pallas_call(
        _o_proj_kernel,
        grid=(1,),
        in_specs=[full, pl.BlockSpec((1, mr, 6 * d), lambda i: (0, 0, 0)),
                  pl.BlockSpec((rows, O_WIDTH), lambda i: (0, 0)),
                  _stacked(w_o, index)],
        out_specs=full,
        out_shape=jax.ShapeDtypeStruct((rows, d), F32),
        compiler_params=_cparams("arbitrary"),
        name="o_proj_sample",
    )(x, mod, o, w_o)


def _trunk(x, mods, mod_kv, pos, weights, tm, conv_prev, kv_prev):
    (g_norm_mix, g_norm_ffn, w_in, w_conv, w_out_conv, g_norm_kv, w_kv, g_k, w_q, g_q, w_o,
     w_gu_dense, w_down_dense) = weights
    batch, seq, d = x.shape
    rows = batch * seq
    depth = g_norm_mix.shape[0]
    n_conv = w_in.shape[0]
    sample = kv_prev is not None
    tiles_per_seq = 1 if sample else seq // tm
    x = x.reshape(rows, d)
    cos, sin = _rope_tables(pos)
    if sample:
        cos, sin = jnp.tile(cos, (batch, 1)), jnp.tile(sin, (batch, 1))
    conv_state = []
    kv_state = None
    for layer in range(depth):
        mod = mods[layer]
        if layer == n_conv:
            kvs = _kv_proj(x, mod_kv, tiles_per_seq, g_norm_kv, w_kv, g_k, cos, sin, tm, not sample)
            k_new, v_new = kvs[:N_GROUPS], kvs[N_GROUPS:2 * N_GROUPS]
            k_att, v_att = list(k_new), list(v_new)
            for n, g_idx in enumerate(SPLIT_GROUPS if not sample else ()):
                k_att[g_idx], v_att[g_idx] = kvs[2 * N_GROUPS + 2 * n], kvs[2 * N_GROUPS + 2 * n + 1]
        if layer < n_conv:
            if sample:
                st = conv_prev[layer]
                zero = jnp.zeros((batch, seq - 2, d), F32)
                pa = jnp.concatenate([st[:, 1:2], jnp.zeros((batch, seq - 1, d), F32)], axis=1)
                pb = jnp.concatenate([st, zero], axis=1)
                prev = (seq, pa.reshape(rows, d), pb.reshape(rows, d))
            else:
                prev = None
            x, u_tail = _conv_layer(x, mod, tiles_per_seq, g_norm_mix[layer], w_in, w_conv, w_out_conv, layer,
                                    tm, prev)
            if sample:
                conv_state.append(u_tail.reshape(batch, seq, d)[:, seq - (CONV_W - 1):])
            else:
                conv_state.append(u_tail[:, F32_SUBLANES - (CONV_W - 1):])
        else:
            lb = layer - n_conv
            qs = _q_proj(x, mod, tiles_per_seq, g_norm_mix[layer], w_q, lb, g_q[lb], cos, sin, tm, not sample)
            if sample:
                o, new_ck, new_cv = _attn_sample(jnp.concatenate(qs, axis=1), k_new, v_new, kv_prev[0],
                                                 kv_prev[1], batch, seq, kv_state is None)
                if kv_state is None:
                    kv_state = (new_ck, new_cv)
                x = _o_proj(x, mod, o, w_o, lb)
            else:
                res = [_attn_prompt_group(qs[g], k_att[g], v_att[g], g, batch, seq) for g in range(N_GROUPS)]
                x = _merge_o(x, mod, tiles_per_seq, [r[0] for r in res], [r[1] for r in res], w_o, lb, tm)
        if layer % 2 == 0:
            x = _ffn_layer(x, mod, tiles_per_seq, g_norm_ffn[layer], w_gu_dense, w_down_dense, layer // 2, tm)
        else:
            tm_moe = tm if sample else min(SPARSE_TILE, seq)
            x = yield (x, mod, 1 if sample else seq // tm_moe, tm_moe), layer
    if not sample:
        shape4 = lambda a: a.reshape(batch, seq, HKV_G, HEAD_DIM)
        kv_state = ([shape4(k)[:, -min(w, seq):] for k, (w, _) in zip(k_new, DIL_GROUPS)],
                    [shape4(v)[:, -min(w, seq):] for v, (w, _) in zip(v_new, DIL_GROUPS)])
    return x.reshape(batch, seq, d), jnp.stack(conv_state, axis=0), kv_state


def _resume(trunk, x):
    try:
        return False, trunk.send(x)
    except StopIteration as done:
        return True, done.value


def kernel(x_prompt, x_sample, state_conv, cache_k_g0, cache_v_g0, cache_k_g1, cache_v_g1, cache_k_g2,
           cache_v_g2, c_prompt, c_sample, g_norm_mix, g_norm_ffn, w_ada, b_ada, w_in, w_conv, w_out_conv,
           g_norm_kv, w_ada_kv, b_ada_kv, w_kv, g_k, w_q, g_q, w_o, w_gu_dense, w_down_dense, w_router,
           b_router, w_gu_moe, w_down_moe):
    b_p, t_p, d = x_prompt.shape
    b_s, t_s, _ = x_sample.shape
    bf = lambda w: w.astype(BF16)
    weights = (g_norm_mix, g_norm_ffn, bf(w_in), w_conv, bf(w_out_conv), g_norm_kv, bf(w_kv), g_k, bf(w_q),
               g_q, bf(w_o), bf(w_gu_dense), bf(w_down_dense))

    c_all = jnp.concatenate([c_prompt, c_sample], axis=0)
    mods_all = _ada(c_all, w_ada, b_ada)
    mod_kv_all = _ada(c_all, w_ada_kv[None], b_ada_kv[None])[0]
    depth = w_ada.shape[0]
    mods_p = [mods_all[l, :b_p, None, :] for l in range(depth)]
    mods_s = [jnp.repeat(mods_all[l, b_p:], t_s, axis=0)[None] for l in range(depth)]
    mod_kv_p = mod_kv_all[:b_p, None, :]
    mod_kv_s = jnp.repeat(mod_kv_all[b_p:], t_s, axis=0)[None]

    pos_p = jnp.arange(t_p, dtype=jnp.int32)
    pos_s = PAST_LEN + jnp.arange(t_s, dtype=jnp.int32)
    kv_prev = ([cache_k_g0, cache_k_g1, cache_k_g2], [cache_v_g0, cache_v_g1, cache_v_g2])
    trunks = [_trunk(x_prompt, mods_p, mod_kv_p, pos_p, weights, min(PROMPT_TILE, t_p), None, None),
              _trunk(x_sample, mods_s, mod_kv_s, pos_s, weights, b_s * t_s, state_conv, kv_prev)]
    requests = [next(t) for t in trunks]
    results = None
    while results is None:
        layer = requests[0][1]
        xs = _moe_sparse_layer([r[0] for r in requests], g_norm_ffn[layer], w_router[layer // 2],
                               b_router[layer // 2], w_gu_moe, w_down_moe, layer // 2)
        steps = [_resume(t, x) for t, x in zip(trunks, xs)]
        if all(done for done, _ in steps):
            results = [value for _, value in steps]
        else:
            requests = [value for _, value in steps]
    (y_prompt, conv_p, (kp, vp)), (y_sample, conv_s, (ksn, vsn)) = results
    return (y_prompt, y_sample, conv_p, conv_s,
            kp[0], vp[0], kp[1], vp[1], kp[2], vp[2],
            ksn[0], vsn[0], ksn[1], vsn[1], ksn[2], vsn[2])
```

```python
import functools

import jax
import jax.numpy as jnp
from jax import lax
from jax.experimental import pallas as pl
from jax.experimental.pallas import tpu as pltpu

D_MODEL = 1024
PAST_LEN = 8192
CONV_W = 3
HEAD_DIM = 128
ROT_DIM = HEAD_DIM // 4
ROPE_THETA = 500000.0
DIL_GROUPS = ((128, 1), (512, 4), (2048, 16))
N_GROUPS = len(DIL_GROUPS)
HQ_G = 8
HKV_G = 2
REP = HQ_G // HKV_G
N_Q_HEADS = N_GROUPS * HQ_G
N_KV_HEADS = N_GROUPS * HKV_G
Q_WIDTH = N_Q_HEADS * HEAD_DIM
KV_WIDTH = N_KV_HEADS * HEAD_DIM
O_WIDTH = HQ_G * HEAD_DIM
GROUP_KV_WIDTH = HKV_G * HEAD_DIM
N_EXPERTS = 8
EPS = 1e-6
ATTN_SCALE = HEAD_DIM ** -0.5
NEG = -1e30
Q_BLOCK = 128

F32 = jnp.float32
BF16 = jnp.bfloat16

V7X_VMEM_LIMIT_BYTES = 56 * 1024 * 1024
V7X_MXU_WIDTH = 256
F32_SUBLANES = 8
PROMPT_TILE = 512


def _cparams(*sem):
    return pltpu.CompilerParams(dimension_semantics=sem, vmem_limit_bytes=V7X_VMEM_LIMIT_BYTES)


def _resident(shape, index_map):
    return pl.BlockSpec(shape, index_map, pipeline_mode=pl.Buffered(1))


def _stacked(w, index):
    shape = w.shape[1:]
    return pl.BlockSpec((None,) + shape, lambda *_: (index,) + (0,) * len(shape), pipeline_mode=pl.Buffered(1))


def _silu(x):
    return x * jax.nn.sigmoid(x)


def _norm_mod(x, g, shift, scale):
    y = x * lax.rsqrt(jnp.mean(x * x, axis=-1, keepdims=True) + EPS) * g
    return y * (1.0 + scale) + shift


def _mod_part(mod_ref, k):
    return mod_ref[0, :, k * D_MODEL:(k + 1) * D_MODEL]


def _split_bf16(v):
    hi = v.astype(BF16)
    return hi, (v - hi.astype(F32)).astype(BF16)


def _ada_kernel(c_ref, w_ref, b_ref, o_ref):
    s_hi, s_lo = _split_bf16(_silu(c_ref[...]))
    w_hi, w_lo = _split_bf16(w_ref[0])
    acc = jnp.dot(s_hi, w_hi, preferred_element_type=F32)
    acc = acc + jnp.dot(s_lo, w_hi, preferred_element_type=F32)
    acc = acc + jnp.dot(s_hi, w_lo, preferred_element_type=F32)
    o_ref[0] = acc + b_ref[0]


def _ada(c, w, b):
    n_layers, d, n = w.shape
    m = c.shape[0]
    tn = 2048
    return pl.pallas_call(
        _ada_kernel,
        grid=(n_layers, n // tn),
        in_specs=[pl.BlockSpec((m, d), lambda l, j: (0, 0)),
                  pl.BlockSpec((1, d, tn), lambda l, j: (l, 0, j)),
                  pl.BlockSpec((1, 1, tn), lambda l, j: (l, 0, j))],
        out_specs=pl.BlockSpec((1, m, tn), lambda l, j: (l, 0, j)),
        out_shape=jax.ShapeDtypeStruct((n_layers, m, n), F32),
        compiler_params=_cparams("parallel", "parallel"),
        name="ada",
    )(c, w, b.reshape(n_layers, 1, n))


def _conv_kernel(*refs, tm, tiles_per_seq, seq_rows):
    if seq_rows is None:
        x_ref, mod_ref, g_ref, win_ref, wc_ref, wout_ref, xo_ref, ust_ref, carry_ref = refs
    else:
        x_ref, mod_ref, g_ref, win_ref, wc_ref, wout_ref, pa_ref, pb_ref, xo_ref, ust_ref = refs
    x = x_ref[...]
    h = _norm_mod(x, g_ref[...], _mod_part(mod_ref, 0), _mod_part(mod_ref, 1)).astype(BF16)
    b_gate = jnp.dot(h, win_ref[:, 0:D_MODEL], preferred_element_type=F32)
    c_gate = jnp.dot(h, win_ref[:, D_MODEL:2 * D_MODEL], preferred_element_type=F32)
    v = jnp.dot(h, win_ref[:, 2 * D_MODEL:3 * D_MODEL], preferred_element_type=F32)
    u = c_gate * v
    row = lax.broadcasted_iota(jnp.int32, (tm, 1), 0)
    r1 = pltpu.roll(u, 1, 0)
    r2 = pltpu.roll(u, 2, 0)
    if seq_rows is None:
        @pl.when(pl.program_id(0) % tiles_per_seq == 0)
        def _():
            carry_ref[...] = jnp.zeros_like(carry_ref)
        last = F32_SUBLANES - 1
        um1 = jnp.where(row == 0, carry_ref[last:last + 1, :], r1)
        um2 = jnp.where(row == 0, carry_ref[last - 1:last, :],
                        jnp.where(row == 1, carry_ref[last:last + 1, :], r2))
        carry_ref[...] = u[tm - F32_SUBLANES:tm, :]
        ust_ref[0] = u[tm - F32_SUBLANES:tm, :]
    else:
        rowm = row & (seq_rows - 1)
        um1 = jnp.where(rowm == 0, pa_ref[...], r1)
        um2 = jnp.where(rowm < 2, pb_ref[...], r2)
        ust_ref[...] = u
    y = um2 * wc_ref[0:1, :] + um1 * wc_ref[1:2, :] + u * wc_ref[2:3, :]
    z = (b_gate * y).astype(BF16)
    out = jnp.dot(z, wout_ref[...], preferred_element_type=F32)
    xo_ref[...] = x + _mod_part(mod_ref, 2) * out


def _conv_layer(x, mod, tiles_per_seq, g, w_in, w_conv, w_out, layer, tm, prev=None):
    rows, d = x.shape
    n_tiles = rows // tm
    mr = mod.shape[1]
    row_spec = pl.BlockSpec((tm, d), lambda i: (i, 0))
    in_specs = [row_spec,
                pl.BlockSpec((1, mr, 6 * d), lambda i: (i // tiles_per_seq, 0, 0)),
                pl.BlockSpec((1, d), lambda i: (0, 0)),
                _stacked(w_in, layer), _stacked(w_conv, layer), _stacked(w_out, layer)]
    args = [x, mod, g.reshape(1, d), w_in, w_conv, w_out]
    if prev is None:
        seq_rows = None
        n_seq = n_tiles // tiles_per_seq
        ust_shape = jax.ShapeDtypeStruct((n_seq, F32_SUBLANES, d), F32)
        ust_spec = pl.BlockSpec((1, F32_SUBLANES, d), lambda i: (i // tiles_per_seq, 0, 0))
        scratch = [pltpu.VMEM((F32_SUBLANES, d), F32)]
    else:
        seq_rows, pa, pb = prev
        in_specs += [row_spec, row_spec]
        args += [pa, pb]
        ust_shape = jax.ShapeDtypeStruct((rows, d), F32)
        ust_spec = row_spec
        scratch = []
    return pl.pallas_call(
        functools.partial(_conv_kernel, tm=tm, tiles_per_seq=tiles_per_seq, seq_rows=seq_rows),
        grid=(n_tiles,),
        in_specs=in_specs,
        out_specs=[row_spec, ust_spec],
        out_shape=[jax.ShapeDtypeStruct((rows, d), F32), ust_shape],
        scratch_shapes=scratch,
        compiler_params=_cparams("arbitrary"),
        name="conv_layer",
    )(*args)


def _ffn_kernel(x_ref, mod_ref, g_ref, wgu_ref, wd_ref, xo_ref, *, d_ff, n_chunks):
    x = x_ref[...]
    h = _norm_mod(x, g_ref[...], _mod_part(mod_ref, 3), _mod_part(mod_ref, 4)).astype(BF16)
    n_mxu = d_ff // V7X_MXU_WIDTH
    edges = [(c * n_mxu // n_chunks) * V7X_MXU_WIDTH for c in range(n_chunks)] + [d_ff]
    acc = None
    for lo, hi in zip(edges[:-1], edges[1:]):
        gate = jnp.dot(h, wgu_ref[:, lo:hi], preferred_element_type=F32)
        up = jnp.dot(h, wgu_ref[:, d_ff + lo:d_ff + hi], preferred_element_type=F32)
        a = (_silu(gate) * up).astype(BF16)
        part = jnp.dot(a, wd_ref[lo:hi, :], preferred_element_type=F32)
        acc = part if acc is None else acc + part
    xo_ref[...] = x + _mod_part(mod_ref, 5) * acc


def _ffn_layer(x, mod, tiles_per_seq, g, w_gu, w_down, index, tm):
    rows, d = x.shape
    d_ff = w_down.shape[1]
    mr = mod.shape[1]
    row_spec = pl.BlockSpec((tm, d), lambda i: (i, 0))
    return pl.pallas_call(
        functools.partial(_ffn_kernel, d_ff=d_ff, n_chunks=2),
        grid=(rows // tm,),
        in_specs=[row_spec,
                  pl.BlockSpec((1, mr, 6 * d), lambda i: (i // tiles_per_seq, 0, 0)),
                  pl.BlockSpec((1, d), lambda i: (0, 0)),
                  _stacked(w_gu, index), _stacked(w_down, index)],
        out_specs=row_spec,
        out_shape=jax.ShapeDtypeStruct((rows, d), F32),
        compiler_params=_cparams("parallel"),
        name="ffn_dense",
    )(x, mod, g.reshape(1, d), w_gu, w_down)


SEG_ALIGN = 16
SEG_SIZES = (512, 256, 128, 64, 32, 16)
SPARSE_TILE = 512
EXPERT_ROW_TILE = 256
EXPERT_FF_UNIT = 2 * V7X_MXU_WIDTH
EXPERT_STAGE_BUFFERS = 4


def _route_kernel(x_ref, mod_ref, g_ref, wrt_ref, br_ref, h_ref, slots_ref, gw_ref, seg_ref, *, tm):
    x = x_ref[...]
    h = _norm_mod(x, g_ref[...], _mod_part(mod_ref, 3), _mod_part(mod_ref, 4))
    h_hi, h_lo = _split_bf16(h)
    h_ref[...] = h_hi
    w_hi, w_lo = _split_bf16(wrt_ref[...])
    nt = (((1,), (1,)), ((), ()))
    logits = (lax.dot_general(w_hi, h_hi, nt, preferred_element_type=F32)
              + lax.dot_general(w_lo, h_hi, nt, preferred_element_type=F32)
              + lax.dot_general(w_hi, h_lo, nt, preferred_element_type=F32)) + br_ref[...]
    eidx = lax.broadcasted_iota(jnp.int32, logits.shape, 0).astype(F32)
    m1 = jnp.max(logits, axis=0, keepdims=True)
    i1 = jnp.min(jnp.where(logits == m1, eidx, float(N_EXPERTS)), axis=0, keepdims=True)
    sel1 = eidx == i1
    rest = jnp.where(sel1, -jnp.inf, logits)
    m2 = jnp.max(rest, axis=0, keepdims=True)
    i2 = jnp.min(jnp.where(rest == m2, eidx, float(N_EXPERTS)), axis=0, keepdims=True)
    sel2 = eidx == i2
    e2 = jnp.exp(m2 - m1)
    den = 1.0 + e2
    gw_ref[0] = jnp.concatenate([1.0 / den, e2 / den], axis=0)
    mask = jnp.where(sel1 | sel2, 1.0, 0.0)
    before = (lax.broadcasted_iota(jnp.int32, (tm, tm), 0) < lax.broadcasted_iota(jnp.int32, (tm, tm), 1))
    rank = jnp.dot(mask.astype(BF16), jnp.where(before, 1.0, 0.0).astype(BF16), preferred_element_type=F32)
    count = jnp.sum(mask, axis=1, keepdims=True)
    seg = jnp.floor((count + (SEG_ALIGN - 1)) * (1.0 / SEG_ALIGN)) * SEG_ALIGN
    start = jnp.zeros_like(seg)
    for e in range(N_EXPERTS - 1):
        start = start + jnp.where(eidx[:, 0:1] > e, seg[e:e + 1, :], 0.0)
    slot = start + rank
    slot1 = jnp.sum(jnp.where(sel1, slot, 0.0), axis=0, keepdims=True)
    slot2 = jnp.sum(jnp.where(sel2, slot, 0.0), axis=0, keepdims=True)
    slots_ref[0] = jnp.concatenate([slot1, slot2], axis=0).astype(jnp.int32)
    seg_ref[0] = jnp.broadcast_to(seg, (N_EXPERTS, HEAD_DIM)).astype(jnp.int32)


def _route(x, mod, tiles_per_seq, g, w_router, b_router, tm):
    rows, d = x.shape
    mr = mod.shape[1]
    n_tiles = rows // tm
    row_spec = pl.BlockSpec((tm, d), lambda i: (i, 0))
    pair_spec = pl.BlockSpec((1, 2, tm), lambda i: (i, 0, 0))
    return pl.pallas_call(
        functools.partial(_route_kernel, tm=tm),
        grid=(n_tiles,),
        in_specs=[row_spec,
                  pl.BlockSpec((1, mr, 6 * d), lambda i: (i // tiles_per_seq, 0, 0)),
                  pl.BlockSpec((1, d), lambda i: (0, 0)),
                  pl.BlockSpec((N_EXPERTS, d), lambda i: (0, 0)),
                  pl.BlockSpec((N_EXPERTS, 1), lambda i: (0, 0))],
        out_specs=[row_spec, pair_spec, pair_spec,
                   pl.BlockSpec((1, N_EXPERTS, HEAD_DIM), lambda i: (i, 0, 0))],
        out_shape=[jax.ShapeDtypeStruct((rows, d), BF16),
                   jax.ShapeDtypeStruct((n_tiles, 2, tm), jnp.int32),
                   jax.ShapeDtypeStruct((n_tiles, 2, tm), F32),
                   jax.ShapeDtypeStruct((n_tiles, N_EXPERTS, HEAD_DIM), jnp.int32)],
        compiler_params=_cparams("parallel"),
        name="moe_route",
    )(x, mod, g.reshape(1, d), w_router.T, b_router.reshape(N_EXPERTS, 1))


def _segment_copies(tile, seg_ref, loc_ref, goff_ref, packed, sorted_hbm, sem, to_sorted):
    copies = []
    tokens = (packed.shape[0] - N_EXPERTS * SEG_ALIGN) // 2
    max_seg = -(-tokens // SEG_ALIGN) * SEG_ALIGN
    for e in range(N_EXPERTS):
        seg = seg_ref[tile * N_EXPERTS + e]
        loc = loc_ref[tile * N_EXPERTS + e]
        goff = goff_ref[tile * N_EXPERTS + e]
        done = jnp.int32(0)
        for size in (s for s in SEG_SIZES if s <= max_seg):
            vm = packed.at[pl.ds(pl.multiple_of(loc + done, SEG_ALIGN), size), :]
            hb = sorted_hbm.at[pl.ds(pl.multiple_of(goff + done, SEG_ALIGN), size), :]
            src, dst = (vm, hb) if to_sorted else (hb, vm)
            copies.append(((seg & size) != 0, pltpu.make_async_copy(src, dst, sem)))
            done = done + (seg & size)
    return copies


def _start_all(copies):
    for cond, cp in copies:
        pl.when(cond)(cp.start)


def _wait_all(copies):
    for cond, cp in copies:
        pl.when(cond)(cp.wait)


def _dispatch_kernel(seg_ref, loc_ref, goff_ref, h_ref, slots_ref, sorted_in, sorted_out, stage_ref, sems,
                     *, n_slots, n_tiles):
    del sorted_in
    i = pl.program_id(0)
    par = i % 2

    def copies(tile, parity):
        return _segment_copies(tile, seg_ref, loc_ref, goff_ref, stage_ref.at[parity], sorted_out,
                               sems.at[parity], True)

    @pl.when(i >= 2)
    def _():
        _wait_all(copies(i - 2, par))

    tm = h_ref.shape[0]
    srow = lax.broadcasted_iota(jnp.int32, (n_slots, tm), 0)
    onehot = (srow == slots_ref[0, 0:1, :]) | (srow == slots_ref[0, 1:2, :])
    packed = jnp.dot(jnp.where(onehot, 1.0, 0.0).astype(BF16), h_ref[...], preferred_element_type=F32)
    stage_ref[par] = packed.astype(BF16)
    _start_all(copies(i, par))

    @pl.when(i == n_tiles - 1)
    def _():
        if n_tiles >= 2:
            _wait_all(copies(i - 1, 1 - par))
        _wait_all(copies(i, par))


def _dispatch(h, slots, seg, loc, goff, h_sorted, tm):
    rows, d = h.shape
    n_tiles = rows // tm
    n_slots = 2 * tm + N_EXPERTS * SEG_ALIGN
    return pl.pallas_call(
        functools.partial(_dispatch_kernel, n_slots=n_slots, n_tiles=n_tiles),
        grid_spec=pltpu.PrefetchScalarGridSpec(
            num_scalar_prefetch=3,
            grid=(n_tiles,),
            in_specs=[pl.BlockSpec((tm, d), lambda i, *_: (i, 0)),
                      pl.BlockSpec((1, 2, tm), lambda i, *_: (i, 0, 0)),
                      pl.BlockSpec(memory_space=pl.ANY)],
            out_specs=pl.BlockSpec(memory_space=pl.ANY),
            scratch_shapes=[pltpu.VMEM((2, n_slots, d), BF16), pltpu.SemaphoreType.DMA((2,))]),
        out_shape=jax.ShapeDtypeStruct(h_sorted.shape, BF16),
        input_output_aliases={5: 0},
        compiler_params=_cparams("arbitrary"),
        name="moe_dispatch",
    )(seg, loc, goff, h, slots, h_sorted)


def _experts_kernel(te_ref, jb_ref, nact_ref, phase_ref, nxt_ref, h_ref, wgu_hbm, wd_hbm, y_ref,
                    wg_ref, wd_ref, sg_ref, su_ref, sd_ref, sems, *, first_expert, d_ff, unit):
    del jb_ref
    j = pl.program_id(0)
    n_units = d_ff // unit
    depth = sg_ref.shape[0]
    active = j < nact_ref[0]

    def fetch(expert, u, buf):
        return (pltpu.make_async_copy(wgu_hbm.at[expert, :, pl.ds(u * unit, unit)], sg_ref.at[buf], sems.at[buf]),
                pltpu.make_async_copy(wgu_hbm.at[expert, :, pl.ds(d_ff + u * unit, unit)], su_ref.at[buf],
                                      sems.at[buf]),
                pltpu.make_async_copy(wd_hbm.at[expert, pl.ds(u * unit, unit), :], sd_ref.at[buf], sems.at[buf]))

    def start(expert, u):
        for cp in fetch(expert, u, u % depth):
            cp.start()

    def start_first(expert):
        for u in range(min(depth, n_units)):
            start(expert, u)

    def finish(expert, u):
        for cp in fetch(expert, u, u % depth):
            cp.wait()
        wg_ref[u, :, 0:unit] = sg_ref[u % depth].astype(BF16)
        wg_ref[u, :, unit:2 * unit] = su_ref[u % depth].astype(BF16)
        wd_ref[u] = sd_ref[u % depth].astype(BF16)
        if u + depth < n_units:
            start(expert, u + depth)

    def unit_out(h, u):
        gate = jnp.dot(h, wg_ref[u, :, 0:unit], preferred_element_type=F32)
        up = jnp.dot(h, wg_ref[u, :, unit:2 * unit], preferred_element_type=F32)
        a = (_silu(gate) * up).astype(BF16)
        return jnp.dot(a, wd_ref[u], preferred_element_type=F32)

    @pl.when(active & (j == 0))
    def _():
        expert = first_expert + te_ref[0]
        start_first(expert)
        for u in range(n_units):
            finish(expert, u)

    phase = phase_ref[j]
    swap = active & (phase >= 2)

    @pl.when(active & (phase == 1))
    def _():
        start_first(first_expert + nxt_ref[j])

    @pl.when(active & jnp.logical_not(swap))
    def _():
        h = h_ref[...]
        acc = unit_out(h, 0)
        for u in range(1, n_units):
            acc = acc + unit_out(h, u)
        y_ref[...] = acc.astype(BF16)

    @pl.when(swap)
    def _():
        nxt = first_expert + nxt_ref[j]
        pl.when(phase == 3)(functools.partial(start_first, nxt))
        h = h_ref[...]
        acc = None
        for u in range(n_units):
            part = unit_out(h, u)
            acc = part if acc is None else acc + part
            finish(nxt, u)
        y_ref[...] = acc.astype(BF16)

    @pl.when(jnp.logical_not(active))
    def _():
        y_ref[...] = jnp.zeros_like(y_ref)


def _experts(h_sorted, tile_expert, tile_block, n_active, tile_phase, tile_next, w_gu, w_down, index, tmf):
    n_rows, d = h_sorted.shape
    d_ff = w_down.shape[2]
    unit = EXPERT_FF_UNIT
    n_units = d_ff // unit
    depth = EXPERT_STAGE_BUFFERS
    w_gu = w_gu.reshape(-1, d, 2 * d_ff)
    w_down = w_down.reshape(-1, d_ff, d)
    return pl.pallas_call(
        functools.partial(_experts_kernel, first_expert=index * N_EXPERTS, d_ff=d_ff, unit=unit),
        grid_spec=pltpu.PrefetchScalarGridSpec(
            num_scalar_prefetch=5,
            grid=(n_rows // tmf,),
            in_specs=[pl.BlockSpec((tmf, d), lambda j, te, jb, *_: (jb[j], 0)),
                      pl.BlockSpec(memory_space=pl.ANY),
                      pl.BlockSpec(memory_space=pl.ANY)],
            out_specs=pl.BlockSpec((tmf, d), lambda j, *_: (j, 0)),
            scratch_shapes=[pltpu.VMEM((n_units, d, 2 * unit), BF16),
                            pltpu.VMEM((n_units, unit, d), BF16),
                            pltpu.VMEM((depth, d, unit), F32),
                            pltpu.VMEM((depth, d, unit), F32),
                            pltpu.VMEM((depth, unit, d), F32),
                            pltpu.SemaphoreType.DMA((depth,))]),
        out_shape=jax.ShapeDtypeStruct((n_rows, d), BF16),
        compiler_params=_cparams("arbitrary"),
        name="moe_experts_sorted",
    )(tile_expert, tile_block, n_active, tile_phase, tile_next, h_sorted, w_gu, w_down)


def _combine_kernel(seg_ref, loc_ref, goff_ref, x_ref, mod_ref, slots_ref, gw_ref, sorted_hbm, xo_ref,
                    stage_ref, sems, *, n_slots, n_tiles):
    i = pl.program_id(0)
    par = i % 2

    def copies(tile, parity):
        return _segment_copies(tile, seg_ref, loc_ref, goff_ref, stage_ref.at[parity], sorted_hbm,
                               sems.at[parity], False)

    @pl.when(i == 0)
    def _():
        stage_ref[...] = jnp.zeros_like(stage_ref)
        _start_all(copies(i, par))

    @pl.when(i + 1 < n_tiles)
    def _():
        _start_all(copies(i + 1, 1 - par))

    _wait_all(copies(i, par))
    y = stage_ref[par]
    tm = x_ref.shape[0]
    scol = lax.broadcasted_iota(jnp.int32, (tm, n_slots), 1)
    gates = (jnp.where(scol == slots_ref[:, 0:1], gw_ref[:, 0:1], 0.0)
             + jnp.where(scol == slots_ref[:, 1:2], gw_ref[:, 1:2], 0.0))
    moe = jnp.dot(gates.astype(BF16), y, preferred_element_type=F32)
    xo_ref[...] = x_ref[...] + _mod_part(mod_ref, 5) * moe


def _combine(x, mod, tiles_per_seq, slots_col, gw_col, y_sorted, seg, loc, goff, tm):
    rows, d = x.shape
    mr = mod.shape[1]
    n_tiles = rows // tm
    n_slots = 2 * tm + N_EXPERTS * SEG_ALIGN
    row_spec = pl.BlockSpec((tm, d), lambda i, *_: (i, 0))
    pair_spec = pl.BlockSpec((tm, 2), lambda i, *_: (i, 0))
    return pl.pallas_call(
        functools.partial(_combine_kernel, n_slots=n_slots, n_tiles=n_tiles),
        grid_spec=pltpu.PrefetchScalarGridSpec(
            num_scalar_prefetch=3,
            grid=(n_tiles,),
            in_specs=[row_spec,
                      pl.BlockSpec((1, mr, 6 * d), lambda i, *_: (i // tiles_per_seq, 0, 0)),
                      pair_spec, pair_spec,
                      pl.BlockSpec(memory_space=pl.ANY)],
            out_specs=row_spec,
            scratch_shapes=[pltpu.VMEM((2, n_slots, d), BF16), pltpu.SemaphoreType.DMA((2,))]),
        out_shape=jax.ShapeDtypeStruct((rows, d), F32),
        compiler_params=_cparams("arbitrary"),
        name="moe_combine",
    )(seg, loc, goff, x, mod, slots_col, gw_col, y_sorted)


def _moe_sparse_layer(parts, g, w_router, b_router, w_gu, w_down, index):
    tmf = EXPERT_ROW_TILE
    d = parts[0][0].shape[1]
    routed = [_route(x, mod, tps, g, w_router, b_router, tm) for x, mod, tps, tm in parts]
    tiles = [x.shape[0] // tm for x, _, _, tm in parts]
    seg = jnp.concatenate([r[3][:, :, 0] for r in routed], axis=0)
    loc = jnp.cumsum(seg, axis=1) - seg
    region = ((jnp.sum(seg, axis=0) + tmf - 1) // tmf) * tmf
    region_end = jnp.cumsum(region)
    goff = (region_end - region)[None, :] + jnp.cumsum(seg, axis=0) - seg
    n_sorted = sum(2 * x.shape[0] for x, _, _, _ in parts) + sum(tiles) * N_EXPERTS * (SEG_ALIGN - 1)
    n_sorted += N_EXPERTS * (tmf - SEG_ALIGN)
    n_sorted = ((n_sorted + tmf - 1) // tmf) * tmf
    n_row_tiles = n_sorted // tmf
    n_active = region_end[-1] // tmf
    tile_ids = jnp.arange(n_row_tiles, dtype=jnp.int32)
    tile_block = jnp.maximum(jnp.minimum(tile_ids, n_active - 1), 0)
    tile_expert = jnp.sum((region_end // tmf)[None, :] <= tile_block[:, None], axis=1)
    tile_expert = jnp.minimum(tile_expert, N_EXPERTS - 1).astype(jnp.int32)
    ids = jnp.arange(N_EXPERTS, dtype=jnp.int32)
    later = jnp.where((ids[None, :] > ids[:, None]) & (region[None, :] > 0), ids[None, :], N_EXPERTS)
    next_expert = jnp.min(later, axis=1)
    tile_next = jnp.where(next_expert < N_EXPERTS, next_expert, -1)[tile_expert].astype(jnp.int32)
    from_end = (region_end // tmf)[tile_expert] - 1 - tile_block
    single = (region // tmf)[tile_expert] == 1
    tile_phase = jnp.where(from_end == 1, 1, jnp.where(from_end == 0, jnp.where(single, 3, 2), 0))
    tile_phase = jnp.where(tile_next >= 0, tile_phase, 0).astype(jnp.int32)
    tables, first = [], 0
    for n in tiles:
        cut = lambda a: a[first:first + n].reshape(-1).astype(jnp.int32)
        tables.append((cut(seg), cut(loc), cut(goff)))
        first += n
    h_sorted = jnp.zeros((n_sorted, d), BF16)
    for (_, _, _, tm), (h, slots, _, _), tab in zip(parts, routed, tables):
        h_sorted = _dispatch(h, slots, *tab, h_sorted, tm)
    y_sorted = _experts(h_sorted, tile_expert, tile_block.astype(jnp.int32),
                        n_active.reshape(1).astype(jnp.int32), tile_phase, tile_next, w_gu, w_down, index, tmf)
    outs = []
    for (x, mod, tps, tm), (_, slots, gw, _), tab in zip(parts, routed, tables):
        rows = x.shape[0]
        slots_col = jnp.transpose(slots, (0, 2, 1)).reshape(rows, 2)
        gw_col = jnp.transpose(gw, (0, 2, 1)).reshape(rows, 2)
        outs.append(_combine(x, mod, tps, slots_col, gw_col, y_sorted, *tab, tm))
    return outs


def _rope_tables(pos):
    half = ROT_DIM // 2
    inv = jnp.float32(ROPE_THETA) ** (-jnp.arange(half, dtype=jnp.float32) / half)
    ang = pos.astype(jnp.float32)[:, None] * inv[None, :]
    cos, sin = jnp.cos(ang), jnp.sin(ang)
    t = pos.shape[0]
    pad1 = jnp.ones((t, HEAD_DIM - ROT_DIM), F32)
    pad0 = jnp.zeros((t, HEAD_DIM - ROT_DIM), F32)
    return (jnp.concatenate([cos, cos, pad1], axis=1), jnp.concatenate([-sin, sin, pad0], axis=1))


def _swap_rot_halves(v):
    half = ROT_DIM // 2
    return jnp.concatenate([v[half:ROT_DIM], v[:half], v[ROT_DIM:]])


def _rope_gain_tables(cos_ref, sin_ref, g_ref, gs_ref, scale):
    return cos_ref[...] * (g_ref[...] * scale), sin_ref[...] * (gs_ref[...] * scale)


def _head_norm_rope(xh, a, b):
    half = ROT_DIM // 2
    ones = jnp.ones((HEAD_DIM, HEAD_DIM), BF16)
    ss = jnp.dot((xh * xh).astype(BF16), ones, preferred_element_type=F32)
    rs = lax.rsqrt(ss * (1.0 / HEAD_DIM) + EPS)
    lane = lax.broadcasted_iota(jnp.int32, xh.shape, 1)
    swapped = jnp.where(lane < half, pltpu.roll(xh, HEAD_DIM - half, 1), pltpu.roll(xh, half, 1))
    return (xh * a + swapped * b) * rs


SPLIT = 4
SPLIT_TILE = SPLIT * Q_BLOCK
SPLIT_GROUPS = (1, 2)


def _kv_kernel(x_ref, mod_ref, g_ref, wkv_ref, gk_ref, cos_ref, sin_ref, *out_refs, split):
    x = x_ref[...]
    h = _norm_mod(x, g_ref[...], mod_ref[0, :, 0:D_MODEL], mod_ref[0, :, D_MODEL:2 * D_MODEL]).astype(BF16)
    rope_a, rope_b = _rope_gain_tables(cos_ref, sin_ref, gk_ref.at[0:1], gk_ref.at[1:2], 1.0)
    k_refs, v_refs = out_refs[:N_GROUPS], out_refs[N_GROUPS:2 * N_GROUPS]
    kv_all = jnp.dot(h, wkv_ref[...], preferred_element_type=F32)
    for hd in range(N_KV_HEADS):
        kh = kv_all[:, hd * HEAD_DIM:(hd + 1) * HEAD_DIM]
        vh = kv_all[:, KV_WIDTH + hd * HEAD_DIM:KV_WIDTH + (hd + 1) * HEAD_DIM]
        g_idx, k_idx = divmod(hd, HKV_G)
        sl = slice(k_idx * HEAD_DIM, (k_idx + 1) * HEAD_DIM)
        kh = _head_norm_rope(kh, rope_a, rope_b)
        rows_of_head = pl.ds(k_idx, kh.shape[0], stride=HKV_G)
        k_refs[g_idx][rows_of_head, :] = kh
        v_refs[g_idx][rows_of_head, :] = vh
        if split and g_idx in SPLIT_GROUPS:
            n = SPLIT_GROUPS.index(g_idx)
            scr = out_refs[-1]
            for j, (val, dst) in enumerate(((kh, out_refs[2 * N_GROUPS + 2 * n]),
                                            (vh, out_refs[2 * N_GROUPS + 2 * n + 1]))):
                slot = (n * HKV_G + k_idx) * 2 + j
                scr[slot] = val
                for c in range(SPLIT):
                    dst[c * Q_BLOCK:(c + 1) * Q_BLOCK, sl] = (
                        scr[slot, pl.ds(c, Q_BLOCK, stride=SPLIT), :].astype(BF16))


def _kv_proj(x, mod_kv, tiles_per_seq, g, w_kv, g_k, cos, sin, tm, split):
    rows, d = x.shape
    mr = mod_kv.shape[1]
    assert not split or tm == SPLIT_TILE
    row_spec = pl.BlockSpec((tm, d), lambda i: (i, 0))
    tab_spec = pl.BlockSpec((tm, HEAD_DIM), lambda i: (i % tiles_per_seq, 0))
    out_spec = pl.BlockSpec((tm, GROUP_KV_WIDTH), lambda i: (i, 0))
    nat_spec = pl.BlockSpec((HKV_G * tm, HEAD_DIM), lambda i: (i, 0))
    n_split = 2 * len(SPLIT_GROUPS) if split else 0
    return pl.pallas_call(
        functools.partial(_kv_kernel, split=split),
        grid=(rows // tm,),
        in_specs=[row_spec,
                  pl.BlockSpec((1, mr, 2 * d), lambda i: (i // tiles_per_seq, 0, 0)),
                  pl.BlockSpec((1, d), lambda i: (0, 0)),
                  _resident((d, 2 * KV_WIDTH), lambda i: (0, 0)),
                  pl.BlockSpec((2, HEAD_DIM), lambda i: (0, 0)),
                  tab_spec, tab_spec],
        out_specs=[nat_spec] * (2 * N_GROUPS) + [out_spec] * n_split,
        out_shape=[jax.ShapeDtypeStruct((HKV_G * rows, HEAD_DIM), F32)] * (2 * N_GROUPS)
                  + [jax.ShapeDtypeStruct((rows, GROUP_KV_WIDTH), BF16)] * n_split,
        scratch_shapes=[pltpu.VMEM((n_split * HKV_G, tm, HEAD_DIM), F32)] if split else [],
        compiler_params=_cparams("parallel"),
        name="kv_proj",
    )(x, mod_kv, g.reshape(1, d), w_kv, jnp.stack([g_k, _swap_rot_halves(g_k)]), cos, sin)


def _q_kernel(x_ref, mod_ref, g_ref, wq_ref, gq_ref, cos_ref, sin_ref, *refs, split):
    q_refs = refs[:N_GROUPS]
    x = x_ref[...]
    h = _norm_mod(x, g_ref[...], _mod_part(mod_ref, 0), _mod_part(mod_ref, 1)).astype(BF16)
    rope_a, rope_b = _rope_gain_tables(cos_ref, sin_ref, gq_ref.at[0:1], gq_ref.at[1:2], ATTN_SCALE)
    q_all = jnp.dot(h, wq_ref[...], preferred_element_type=F32)
    for hd in range(N_Q_HEADS):
        g_idx, h_idx = divmod(hd, HQ_G)
        sl = slice(h_idx * HEAD_DIM, (h_idx + 1) * HEAD_DIM)
        qh = _head_norm_rope(q_all[:, hd * HEAD_DIM:(hd + 1) * HEAD_DIM], rope_a, rope_b)
        if split and g_idx in SPLIT_GROUPS:
            scr = refs[N_GROUPS]
            n = SPLIT_GROUPS.index(g_idx) * HQ_G + h_idx
            scr[n] = qh
            for c in range(SPLIT):
                q_refs[g_idx][c * Q_BLOCK:(c + 1) * Q_BLOCK, sl] = (
                    scr[n, pl.ds(c, Q_BLOCK, stride=SPLIT), :].astype(BF16))
        else:
            q_refs[g_idx][:, sl] = qh.astype(BF16)


def _q_proj(x, mod, tiles_per_seq, g, w_q, index, g_q, cos, sin, tm, split):
    rows, d = x.shape
    mr = mod.shape[1]
    assert not split or tm == SPLIT_TILE
    row_spec = pl.BlockSpec((tm, d), lambda i: (i, 0))
    tab_spec = pl.BlockSpec((tm, HEAD_DIM), lambda i: (i % tiles_per_seq, 0))
    return pl.pallas_call(
        functools.partial(_q_kernel, split=split),
        grid=(rows // tm,),
        in_specs=[row_spec,
                  pl.BlockSpec((1, mr, 6 * d), lambda i: (i // tiles_per_seq, 0, 0)),
                  pl.BlockSpec((1, d), lambda i: (0, 0)),
                  _stacked(w_q, index),
                  pl.BlockSpec((2, HEAD_DIM), lambda i: (0, 0)),
                  tab_spec, tab_spec],
        out_specs=[pl.BlockSpec((tm, O_WIDTH), lambda i: (i, 0))] * N_GROUPS,
        out_shape=[jax.ShapeDtypeStruct((rows, O_WIDTH), BF16)] * N_GROUPS,
        scratch_shapes=[pltpu.VMEM((len(SPLIT_GROUPS) * HQ_G, tm, HEAD_DIM), F32)] if split else [],
        compiler_params=_cparams("parallel"),
        name="q_proj",
    )(x, mod, g.reshape(1, d), w_q, jnp.stack([g_q, _swap_rot_halves(g_q)]), cos, sin)


def _attn_prompt_kernel(*refs, step, has_prev, n_blk):
    if has_prev:
        q_ref, k_ref, v_ref, kp_ref, vp_ref, o_ref, st_ref = refs
        not_first = pl.program_id(1) > 0
    else:
        q_ref, k_ref, v_ref, o_ref, st_ref = refs
    nq = REP * Q_BLOCK
    lane = lax.broadcasted_iota(jnp.int32, (Q_BLOCK, HEAD_DIM), 1)
    max_kb = 2 if step == 1 else n_blk
    col = lax.broadcasted_iota(jnp.int32, (nq, max_kb * Q_BLOCK), 1)
    dist = (max_kb - 1) * Q_BLOCK + (lax.broadcasted_iota(jnp.int32, col.shape, 0) & (Q_BLOCK - 1)) - col
    reach = (dist >= 0) & (dist <= step * Q_BLOCK) & ((dist & (step - 1)) == 0)
    for i in range(n_blk):
        first_kb = i - 1 if step == 1 else 0
        if first_kb < 0 and not has_prev:
            first_kb = 0
        kbs = list(range(first_kb, i + 1))
        valid = reach[:, (max_kb - len(kbs)) * Q_BLOCK:]
        if first_kb < 0:
            valid = valid & (not_first | (col >= Q_BLOCK))
        for k in range(HKV_G):
            ksl = slice(k * HEAD_DIM, (k + 1) * HEAD_DIM)
            q = jnp.concatenate(
                [q_ref[i, :, (k * REP + r) * HEAD_DIM:(k * REP + r + 1) * HEAD_DIM] for r in range(REP)], axis=0)
            if has_prev:
                hrows = pl.ds(k, Q_BLOCK, stride=HKV_G)
                keys = jnp.concatenate([kp_ref[hrows, :] if j < 0 else k_ref[j, hrows, :] for j in kbs], axis=0)
                vals = jnp.concatenate([vp_ref[hrows, :] if j < 0 else v_ref[j, hrows, :] for j in kbs], axis=0)
            else:
                keys = jnp.concatenate([k_ref[j, :, ksl] for j in kbs], axis=0)
                vals = jnp.concatenate([v_ref[j, :, ksl] for j in kbs], axis=0)
            s = lax.dot_general(q, keys.astype(BF16), (((1,), (1,)), ((), ())), preferred_element_type=F32)
            s = jnp.where(valid, s, NEG)
            m = jnp.max(s, axis=1, keepdims=True)
            p = jnp.exp(s - m)
            den = jnp.sum(p, axis=1, keepdims=True)
            out = jnp.dot(p.astype(BF16), vals.astype(BF16), preferred_element_type=F32) / den
            lse = m + jnp.log(den)
            stats = jnp.zeros((Q_BLOCK, HEAD_DIM), F32)
            for r in range(REP):
                rs = slice(r * Q_BLOCK, (r + 1) * Q_BLOCK)
                o_ref[i, :, (k * REP + r) * HEAD_DIM:(k * REP + r + 1) * HEAD_DIM] = out[rs].astype(BF16)
                stats = jnp.where(lane == r, lse[rs], stats)
            st_ref[i, :, ksl] = stats


def _attn_prompt_group(q, k_g, v_g, group, batch, seq):
    _, dil = DIL_GROUPS[group]
    rows = batch * seq
    n_tiles = seq // SPLIT_TILE
    if group in SPLIT_GROUPS:
        step = dil // SPLIT
        lead = (batch, n_tiles, SPLIT, Q_BLOCK)
        view = lambda a: a.reshape(*lead, a.shape[-1])
        spec = lambda w: pl.BlockSpec((None, n_tiles, None, Q_BLOCK, w), lambda b, c: (b, 0, c, 0, 0))
        out_shape = [jax.ShapeDtypeStruct((*lead, O_WIDTH), BF16),
                     jax.ShapeDtypeStruct((*lead, GROUP_KV_WIDTH), F32)]
        out, stats = pl.pallas_call(
            functools.partial(_attn_prompt_kernel, step=step, has_prev=False, n_blk=n_tiles),
            grid=(batch, SPLIT),
            in_specs=[spec(O_WIDTH), spec(GROUP_KV_WIDTH), spec(GROUP_KV_WIDTH)],
            out_specs=[spec(O_WIDTH), spec(GROUP_KV_WIDTH)],
            out_shape=out_shape,
            compiler_params=_cparams("parallel", "parallel"),
            name=f"attn_prompt_g{group}",
        )(view(q), view(k_g), view(v_g))
    else:
        lead = (batch * n_tiles, SPLIT)
        view = lambda a, r: a.reshape(*lead, r, a.shape[-1])
        spec = lambda r, w: pl.BlockSpec((None, SPLIT, r, w), lambda b, i: (b * n_tiles + i, 0, 0, 0))
        kv_rows = HKV_G * Q_BLOCK
        kv_spec = spec(kv_rows, HEAD_DIM)
        prev = pl.BlockSpec((None, None, kv_rows, HEAD_DIM),
                            lambda b, i: (jnp.maximum(b * n_tiles + i - 1, 0), SPLIT - 1, 0, 0))
        out_shape = [jax.ShapeDtypeStruct((*lead, Q_BLOCK, O_WIDTH), BF16),
                     jax.ShapeDtypeStruct((*lead, Q_BLOCK, GROUP_KV_WIDTH), F32)]
        k4, v4 = view(k_g, kv_rows), view(v_g, kv_rows)
        out, stats = pl.pallas_call(
            functools.partial(_attn_prompt_kernel, step=dil, has_prev=True, n_blk=SPLIT),
            grid=(batch, n_tiles),
            in_specs=[spec(Q_BLOCK, O_WIDTH), kv_spec, kv_spec, prev, prev],
            out_specs=[spec(Q_BLOCK, O_WIDTH), spec(Q_BLOCK, GROUP_KV_WIDTH)],
            out_shape=out_shape,
            compiler_params=_cparams("parallel", "arbitrary"),
            name=f"attn_prompt_g{group}",
        )(view(q, Q_BLOCK), k4, v4, k4, v4)
    return out.reshape(rows, O_WIDTH), stats.reshape(rows, GROUP_KV_WIDTH)


def _merge_o_kernel(x_ref, mod_ref, o0_ref, o1_ref, o2_ref, s0_ref, s1_ref, s2_ref, wo_ref, xo_ref,
                    on_ref, sn_ref):
    for n, (o_ref, s_ref) in enumerate(((o1_ref, s1_ref), (o2_ref, s2_ref))):
        for c in range(SPLIT):
            rows = slice(c * Q_BLOCK, (c + 1) * Q_BLOCK)
            dst = pl.ds(c, Q_BLOCK, stride=SPLIT)
            for hd in range(HQ_G):
                on_ref[n, hd, dst, :] = o_ref[rows, hd * HEAD_DIM:(hd + 1) * HEAD_DIM].astype(F32)
            for k in range(HKV_G):
                sn_ref[n, k, dst, :] = s_ref[rows, k * HEAD_DIM:(k + 1) * HEAD_DIM]
    heads = []
    for k in range(HKV_G):
        for r in range(REP):
            hd = k * REP + r
            lses = [s0_ref[:, k * HEAD_DIM + r:k * HEAD_DIM + r + 1],
                    sn_ref[0, k, :, r:r + 1], sn_ref[1, k, :, r:r + 1]]
            m = jnp.maximum(jnp.maximum(lses[0], lses[1]), lses[2])
            es = [jnp.exp(l - m) for l in lses]
            den = es[0] + es[1] + es[2]
            sl = slice(hd * HEAD_DIM, (hd + 1) * HEAD_DIM)
            o0 = o0_ref[:, sl].astype(F32)
            o = o0 + (es[1] / den) * (on_ref[0, hd] - o0) + (es[2] / den) * (on_ref[1, hd] - o0)
            heads.append(o.astype(BF16))
    o = jnp.concatenate(heads, axis=1)
    y = jnp.dot(o, wo_ref[...], preferred_element_type=F32)
    xo_ref[...] = x_ref[...] + _mod_part(mod_ref, 2) * y


def _merge_o(x, mod, tiles_per_seq, outs, stats, w_o, index, tm):
    rows, d = x.shape
    mr = mod.shape[1]
    assert tm == SPLIT_TILE
    row_spec = pl.BlockSpec((tm, d), lambda i: (i, 0))
    o_spec = pl.BlockSpec((tm, O_WIDTH), lambda i: (i, 0))
    st_spec = pl.BlockSpec((tm, GROUP_KV_WIDTH), lambda i: (i, 0))
    return pl.pallas_call(
        _merge_o_kernel,
        grid=(rows // tm,),
        in_specs=[row_spec,
                  pl.BlockSpec((1, mr, 6 * d), lambda i: (i // tiles_per_seq, 0, 0))]
                 + [o_spec] * 3 + [st_spec] * 3 + [_stacked(w_o, index)],
        out_specs=row_spec,
        out_shape=jax.ShapeDtypeStruct((rows, d), F32),
        scratch_shapes=[pltpu.VMEM((len(SPLIT_GROUPS), HQ_G, tm, HEAD_DIM), F32),
                        pltpu.VMEM((len(SPLIT_GROUPS), HKV_G, tm, HEAD_DIM), F32)],
        compiler_params=_cparams("parallel"),
        name="merge_o_proj",
    )(x, mod, *outs, *stats, w_o)


def _attn_sample_kernel(*refs, n_new, write_buffers):
    q_ref = refs[0]
    nk_refs = refs[1:1 + N_GROUPS]
    nv_refs = refs[1 + N_GROUPS:1 + 2 * N_GROUPS]
    ck_refs = refs[1 + 2 * N_GROUPS:1 + 3 * N_GROUPS]
    cv_refs = refs[1 + 3 * N_GROUPS:1 + 4 * N_GROUPS]
    o_ref = refs[1 + 4 * N_GROUPS]
    nq = REP * n_new
    n_new_rows = HKV_G * n_new
    qi = lax.broadcasted_iota(jnp.int32, (nq, 1), 0) & (n_new - 1)

    if write_buffers:
        ok_refs = refs[2 + 4 * N_GROUPS:2 + 5 * N_GROUPS]
        ov_refs = refs[2 + 5 * N_GROUPS:2 + 6 * N_GROUPS]
        for g in range(N_GROUPS):
            for c_ref, n_ref, o_buf in ((ck_refs[g], nk_refs[g], ok_refs[g]), (cv_refs[g], nv_refs[g], ov_refs[g])):
                n_rows = c_ref.shape[1]
                o_buf[0, 0:n_rows - n_new_rows, :] = c_ref[0, n_new_rows:n_rows, :]
                o_buf[0, n_rows - n_new_rows:n_rows, :] = n_ref[0]

    for k in range(HKV_G):
        outs, lses = [], []
        for g, (window, dil) in enumerate(DIL_GROUPS):
            length = ck_refs[g].shape[1] // HKV_G
            q = q_ref[0, g, k]
            qf = q.astype(F32)
            keys = ck_refs[g][0, pl.ds(k, length, stride=HKV_G), :].astype(BF16)
            vals = cv_refs[g][0, pl.ds(k, length, stride=HKV_G), :].astype(BF16)
            s = lax.dot_general(q, keys, (((1,), (1,)), ((), ())), preferred_element_type=F32)
            idx = lax.broadcasted_iota(jnp.int32, (nq, length), 1)
            diff = length + qi - idx
            s = jnp.where(((diff & (dil - 1)) == 0) & (diff <= window), s, NEG)
            new_k = nk_refs[g][0]
            new_v = nv_refs[g][0]
            s_new = []
            for j in range(n_new):
                sj = jnp.sum(qf * new_k[HKV_G * j + k:HKV_G * j + k + 1, :], axis=1, keepdims=True)
                ok = (qi >= j) & (((qi - j) & (dil - 1)) == 0)
                s_new.append(jnp.where(ok, sj, NEG))
            m = jnp.max(s, axis=1, keepdims=True)
            for sj in s_new:
                m = jnp.maximum(m, sj)
            p = jnp.exp(s - m)
            den = jnp.sum(p, axis=1, keepdims=True)
            acc = jnp.dot(p.astype(BF16), vals, preferred_element_type=F32)
            for j, sj in enumerate(s_new):
                pj = jnp.exp(sj - m)
                den = den + pj
                acc = acc + pj * new_v[HKV_G * j + k:HKV_G * j + k + 1, :]
            outs.append(acc / den)
            lses.append(m + jnp.log(den))
        m = jnp.maximum(jnp.maximum(lses[0], lses[1]), lses[2])
        es = [jnp.exp(l - m) for l in lses]
        den = es[0] + es[1] + es[2]
        o_ref[0, k] = sum((e / den) * o for e, o in zip(es, outs))


def _attn_sample(q, new_k, new_v, cache_k, cache_v, n_seq, n_new, write_buffers):
    assert HKV_G * n_new == F32_SUBLANES
    q5 = q.reshape(n_seq, n_new, N_GROUPS, HKV_G, REP, HEAD_DIM)
    q5 = jnp.transpose(q5, (0, 2, 3, 4, 1, 5)).reshape(n_seq, N_GROUPS, HKV_G, REP * n_new, HEAD_DIM)
    rows2 = lambda a, n: a.reshape(n_seq, n * HKV_G, HEAD_DIM)
    nk = [rows2(a, n_new) for a in new_k]
    nv = [rows2(a, n_new) for a in new_v]
    ck = [rows2(c, c.shape[1]) for c in cache_k]
    cv = [rows2(c, c.shape[1]) for c in cache_v]
    new_spec = pl.BlockSpec((1, HKV_G * n_new, HEAD_DIM), lambda b: (b, 0, 0))
    cache_specs = [pl.BlockSpec((1, c.shape[1], HEAD_DIM), lambda b: (b, 0, 0)) for c in ck]
    cache_shapes = [jax.ShapeDtypeStruct(c.shape, F32) for c in ck]
    n_buf = 2 if write_buffers else 0
    res = pl.pallas_call(
        functools.partial(_attn_sample_kernel, n_new=n_new, write_buffers=write_buffers),
        grid=(n_seq,),
        in_specs=[pl.BlockSpec((1, N_GROUPS, HKV_G, REP * n_new, HEAD_DIM), lambda b: (b, 0, 0, 0, 0))]
                 + [new_spec] * (2 * N_GROUPS) + cache_specs + cache_specs,
        out_specs=[pl.BlockSpec((1, HKV_G, REP * n_new, HEAD_DIM), lambda b: (b, 0, 0, 0))]
                  + cache_specs * n_buf,
        out_shape=[jax.ShapeDtypeStruct((n_seq, HKV_G, REP * n_new, HEAD_DIM), F32)] + cache_shapes * n_buf,
        compiler_params=_cparams("parallel"),
        name="attn_sample",
    )(q5, *nk, *nv, *ck, *cv)
    o = res[0].reshape(n_seq, HKV_G, REP, n_new, HEAD_DIM)
    o = jnp.transpose(o, (0, 3, 1, 2, 4)).reshape(n_seq * n_new, O_WIDTH)
    if not write_buffers:
        return o, None, None
    shape4 = lambda a: a.reshape(n_seq, a.shape[1] // HKV_G, HKV_G, HEAD_DIM)
    new_ck = [shape4(a) for a in res[1:1 + N_GROUPS]]
    new_cv = [shape4(a) for a in res[1 + N_GROUPS:1 + 2 * N_GROUPS]]
    return o, new_ck, new_cv


def _o_proj_kernel(x_ref, mod_ref, o_ref, wo_ref, xo_ref):
    y = jnp.dot(o_ref[...].astype(BF16), wo_ref[...], preferred_element_type=F32)
    xo_ref[...] = x_ref[...] + _mod_part(mod_ref, 2) * y


def _o_proj(x, mod, o, w_o, index):
    rows, d = x.shape
    mr = mod.shape[1]
    full = pl.BlockSpec((rows, d), lambda i: (0, 0))
    return pl.pallas_call(
        _o_proj_kernel,
        grid=(1,),
        in_specs=[full, pl.BlockSpec((1, mr, 6 * d), lambda i: (0, 0, 0)),
                  pl.BlockSpec((rows, O_WIDTH), lambda i: (0, 0)),
                  _stacked(w_o, index)],
        out_specs=full,
        out_shape=jax.ShapeDtypeStruct((rows, d), F32),
        compiler_params=_cparams("arbitrary"),
        name="o_proj_sample",
    )(x, mod, o, w_o)


def _trunk(x, mods, mod_kv, pos, weights, tm, conv_prev, kv_prev):
    (g_norm_mix, g_norm_ffn, w_in, w_conv, w_out_conv, g_norm_kv, w_kv, g_k, w_q, g_q, w_o,
     w_gu_dense, w_down_dense) = weights
    batch, seq, d = x.shape
    rows = batch * seq
    depth = g_norm_mix.shape[0]
    n_conv = w_in.shape[0]
    sample = kv_prev is not None
    tiles_per_seq = 1 if sample else seq // tm
    x = x.reshape(rows, d)
    cos, sin = _rope_tables(pos)
    if sample:
        cos, sin = jnp.tile(cos, (batch, 1)), jnp.tile(sin, (batch, 1))
    conv_state = []
    kv_state = None
    for layer in range(depth):
        mod = mods[layer]
        if layer == n_conv:
            kvs = _kv_proj(x, mod_kv, tiles_per_seq, g_norm_kv, w_kv, g_k, cos, sin, tm, not sample)
            k_new, v_new = kvs[:N_GROUPS], kvs[N_GROUPS:2 * N_GROUPS]
            k_att, v_att = list(k_new), list(v_new)
            for n, g_idx in enumerate(SPLIT_GROUPS if not sample else ()):
                k_att[g_idx], v_att[g_idx] = kvs[2 * N_GROUPS + 2 * n], kvs[2 * N_GROUPS + 2 * n + 1]
        if layer < n_conv:
            if sample:
                st = conv_prev[layer]
                zero = jnp.zeros((batch, seq - 2, d), F32)
                pa = jnp.concatenate([st[:, 1:2], jnp.zeros((batch, seq - 1, d), F32)], axis=1)
                pb = jnp.concatenate([st, zero], axis=1)
                prev = (seq, pa.reshape(rows, d), pb.reshape(rows, d))
            else:
                prev = None
            x, u_tail = _conv_layer(x, mod, tiles_per_seq, g_norm_mix[layer], w_in, w_conv, w_out_conv, layer,
                                    tm, prev)
            if sample:
                conv_state.append(u_tail.reshape(batch, seq, d)[:, seq - (CONV_W - 1):])
            else:
                conv_state.append(u_tail[:, F32_SUBLANES - (CONV_W - 1):])
        else:
            lb = layer - n_conv
            qs = _q_proj(x, mod, tiles_per_seq, g_norm_mix[layer], w_q, lb, g_q[lb], cos, sin, tm, not sample)
            if sample:
                o, new_ck, new_cv = _attn_sample(jnp.concatenate(qs, axis=1), k_new, v_new, kv_prev[0],
                                                 kv_prev[1], batch, seq, kv_state is None)
                if kv_state is None:
                    kv_state = (new_ck, new_cv)
                x = _o_proj(x, mod, o, w_o, lb)
            else:
                res = [_attn_prompt_group(qs[g], k_att[g], v_att[g], g, batch, seq) for g in range(N_GROUPS)]
                x = _merge_o(x, mod, tiles_per_seq, [r[0] for r in res], [r[1] for r in res], w_o, lb, tm)
        if layer % 2 == 0:
            x = _ffn_layer(x, mod, tiles_per_seq, g_norm_ffn[layer], w_gu_dense, w_down_dense, layer // 2, tm)
        else:
            tm_moe = tm if sample else min(SPARSE_TILE, seq)
            x = yield (x, mod, 1 if sample else seq // tm_moe, tm_moe), layer
    if not sample:
        shape4 = lambda a: a.reshape(batch, seq, HKV_G, HEAD_DIM)
        kv_state = ([shape4(k)[:, -min(w, seq):] for k, (w, _) in zip(k_new, DIL_GROUPS)],
                    [shape4(v)[:, -min(w, seq):] for v, (w, _) in zip(v_new, DIL_GROUPS)])
    return x.reshape(batch, seq, d), jnp.stack(conv_state, axis=0), kv_state


def _resume(trunk, x):
    try:
        return False, trunk.send(x)
    except StopIteration as done:
        return True, done.value


def kernel(x_prompt, x_sample, state_conv, cache_k_g0, cache_v_g0, cache_k_g1, cache_v_g1, cache_k_g2,
           cache_v_g2, c_prompt, c_sample, g_norm_mix, g_norm_ffn, w_ada, b_ada, w_in, w_conv, w_out_conv,
           g_norm_kv, w_ada_kv, b_ada_kv, w_kv, g_k, w_q, g_q, w_o, w_gu_dense, w_down_dense, w_router,
           b_router, w_gu_moe, w_down_moe):
    b_p, t_p, d = x_prompt.shape
    b_s, t_s, _ = x_sample.shape
    bf = lambda w: w.astype(BF16)
    weights = (g_norm_mix, g_norm_ffn, bf(w_in), w_conv, bf(w_out_conv), g_norm_kv, bf(w_kv), g_k, bf(w_q),
               g_q, bf(w_o), bf(w_gu_dense), bf(w_down_dense))

    c_all = jnp.concatenate([c_prompt, c_sample], axis=0)
    mods_all = _ada(c_all, w_ada, b_ada)
    mod_kv_all = _ada(c_all, w_ada_kv[None], b_ada_kv[None])[0]
    depth = w_ada.shape[0]
    mods_p = [mods_all[l, :b_p, None, :] for l in range(depth)]
    mods_s = [jnp.repeat(mods_all[l, b_p:], t_s, axis=0)[None] for l in range(depth)]
    mod_kv_p = mod_kv_all[:b_p, None, :]
    mod_kv_s = jnp.repeat(mod_kv_all[b_p:], t_s, axis=0)[None]

    pos_p = jnp.arange(t_p, dtype=jnp.int32)
    pos_s = PAST_LEN + jnp.arange(t_s, dtype=jnp.int32)
    kv_prev = ([cache_k_g0, cache_k_g1, cache_k_g2], [cache_v_g0, cache_v_g1, cache_v_g2])
    trunks = [_trunk(x_prompt, mods_p, mod_kv_p, pos_p, weights, min(PROMPT_TILE, t_p), None, None),
              _trunk(x_sample, mods_s, mod_kv_s, pos_s, weights, b_s * t_s, state_conv, kv_prev)]
    requests = [next(t) for t in trunks]
    results = None
    while results is None:
        layer = requests[0][1]
        xs = _moe_sparse_layer([r[0] for r in requests], g_norm_ffn[layer], w_router[layer // 2],
                               b_router[layer // 2], w_gu_moe, w_down_moe, layer // 2)
        steps = [_resume(t, x) for t, x in zip(trunks, xs)]
        if all(done for done, _ in steps):
            results = [value for _, value in steps]
        else:
            requests = [value for _, value in steps]
    (y_prompt, conv_p, (kp, vp)), (y_sample, conv_s, (ksn, vsn)) = results
    return (y_prompt, y_sample, conv_p, conv_s,
            kp[0], vp[0], kp[1], vp[1], kp[2], vp[2],
            ksn[0], vsn[0], ksn[1], vsn[1], ksn[2], vsn[2])
```

```python
import functools

import jax
import jax.numpy as jnp
from jax import lax
from jax.experimental import pallas as pl
from jax.experimental.pallas import tpu as pltpu

D_MODEL = 1024
PAST_LEN = 8192
CONV_W = 3
HEAD_DIM = 128
ROT_DIM = HEAD_DIM // 4
ROPE_THETA = 500000.0
DIL_GROUPS = ((128, 1), (512, 4), (2048, 16))
N_GROUPS = len(DIL_GROUPS)
HQ_G = 8
HKV_G = 2
REP = HQ_G // HKV_G
N_Q_HEADS = N_GROUPS * HQ_G
N_KV_HEADS = N_GROUPS * HKV_G
Q_WIDTH = N_Q_HEADS * HEAD_DIM
KV_WIDTH = N_KV_HEADS * HEAD_DIM
O_WIDTH = HQ_G * HEAD_DIM
GROUP_KV_WIDTH = HKV_G * HEAD_DIM
N_EXPERTS = 8
EPS = 1e-6
ATTN_SCALE = HEAD_DIM ** -0.5
NEG = -1e30
Q_BLOCK = 128

F32 = jnp.float32
BF16 = jnp.bfloat16

V7X_VMEM_LIMIT_BYTES = 56 * 1024 * 1024
V7X_MXU_WIDTH = 256
F32_SUBLANES = 8
PROMPT_TILE = 512


def _cparams(*sem):
    return pltpu.CompilerParams(dimension_semantics=sem, vmem_limit_bytes=V7X_VMEM_LIMIT_BYTES)


def _resident(shape, index_map):
    return pl.BlockSpec(shape, index_map, pipeline_mode=pl.Buffered(1))


def _stacked(w, index):
    shape = w.shape[1:]
    return pl.BlockSpec((None,) + shape, lambda *_: (index,) + (0,) * len(shape), pipeline_mode=pl.Buffered(1))


def _silu(x):
    return x * jax.nn.sigmoid(x)


def _norm_mod(x, g, shift, scale):
    y = x * lax.rsqrt(jnp.mean(x * x, axis=-1, keepdims=True) + EPS) * g
    return y * (1.0 + scale) + shift


def _mod_part(mod_ref, k):
    return mod_ref[0, :, k * D_MODEL:(k + 1) * D_MODEL]


def _split_bf16(v):
    hi = v.astype(BF16)
    return hi, (v - hi.astype(F32)).astype(BF16)


def _ada_kernel(c_ref, w_ref, b_ref, o_ref):
    s_hi, s_lo = _split_bf16(_silu(c_ref[...]))
    w_hi, w_lo = _split_bf16(w_ref[0])
    acc = jnp.dot(s_hi, w_hi, preferred_element_type=F32)
    acc = acc + jnp.dot(s_lo, w_hi, preferred_element_type=F32)
    acc = acc + jnp.dot(s_hi, w_lo, preferred_element_type=F32)
    o_ref[0] = acc + b_ref[0]


def _ada(c, w, b):
    n_layers, d, n = w.shape
    m = c.shape[0]
    tn = 2048
    return pl.pallas_call(
        _ada_kernel,
        grid=(n_layers, n // tn),
        in_specs=[pl.BlockSpec((m, d), lambda l, j: (0, 0)),
                  pl.BlockSpec((1, d, tn), lambda l, j: (l, 0, j)),
                  pl.BlockSpec((1, 1, tn), lambda l, j: (l, 0, j))],
        out_specs=pl.BlockSpec((1, m, tn), lambda l, j: (l, 0, j)),
        out_shape=jax.ShapeDtypeStruct((n_layers, m, n), F32),
        compiler_params=_cparams("parallel", "parallel"),
        name="ada",
    )(c, w, b.reshape(n_layers, 1, n))


def _conv_kernel(*refs, tm, tiles_per_seq, seq_rows):
    if seq_rows is None:
        x_ref, mod_ref, g_ref, win_ref, wc_ref, wout_ref, xo_ref, ust_ref, carry_ref = refs
    else:
        x_ref, mod_ref, g_ref, win_ref, wc_ref, wout_ref, pa_ref, pb_ref, xo_ref, ust_ref = refs
    x = x_ref[...]
    h = _norm_mod(x, g_ref[...], _mod_part(mod_ref, 0), _mod_part(mod_ref, 1)).astype(BF16)
    b_gate = jnp.dot(h, win_ref[:, 0:D_MODEL], preferred_element_type=F32)
    c_gate = jnp.dot(h, win_ref[:, D_MODEL:2 * D_MODEL], preferred_element_type=F32)
    v = jnp.dot(h, win_ref[:, 2 * D_MODEL:3 * D_MODEL], preferred_element_type=F32)
    u = c_gate * v
    row = lax.broadcasted_iota(jnp.int32, (tm, 1), 0)
    r1 = pltpu.roll(u, 1, 0)
    r2 = pltpu.roll(u, 2, 0)
    if seq_rows is None:
        @pl.when(pl.program_id(0) % tiles_per_seq == 0)
        def _():
            carry_ref[...] = jnp.zeros_like(carry_ref)
        last = F32_SUBLANES - 1
        um1 = jnp.where(row == 0, carry_ref[last:last + 1, :], r1)
        um2 = jnp.where(row == 0, carry_ref[last - 1:last, :],
                        jnp.where(row == 1, carry_ref[last:last + 1, :], r2))
        carry_ref[...] = u[tm - F32_SUBLANES:tm, :]
        ust_ref[0] = u[tm - F32_SUBLANES:tm, :]
    else:
        rowm = row & (seq_rows - 1)
        um1 = jnp.where(rowm == 0, pa_ref[...], r1)
        um2 = jnp.where(rowm < 2, pb_ref[...], r2)
        ust_ref[...] = u
    y = um2 * wc_ref[0:1, :] + um1 * wc_ref[1:2, :] + u * wc_ref[2:3, :]
    z = (b_gate * y).astype(BF16)
    out = jnp.dot(z, wout_ref[...], preferred_element_type=F32)
    xo_ref[...] = x + _mod_part(mod_ref, 2) * out


def _conv_layer(x, mod, tiles_per_seq, g, w_in, w_conv, w_out, layer, tm, prev=None):
    rows, d = x.shape
    n_tiles = rows // tm
    mr = mod.shape[1]
    row_spec = pl.BlockSpec((tm, d), lambda i: (i, 0))
    in_specs = [row_spec,
                pl.BlockSpec((1, mr, 6 * d), lambda i: (i // tiles_per_seq, 0, 0)),
                pl.BlockSpec((1, d), lambda i: (0, 0)),
                _stacked(w_in, layer), _stacked(w_conv, layer), _stacked(w_out, layer)]
    args = [x, mod, g.reshape(1, d), w_in, w_conv, w_out]
    if prev is None:
        seq_rows = None
        n_seq = n_tiles // tiles_per_seq
        ust_shape = jax.ShapeDtypeStruct((n_seq, F32_SUBLANES, d), F32)
        ust_spec = pl.BlockSpec((1, F32_SUBLANES, d), lambda i: (i // tiles_per_seq, 0, 0))
        scratch = [pltpu.VMEM((F32_SUBLANES, d), F32)]
    else:
        seq_rows, pa, pb = prev
        in_specs += [row_spec, row_spec]
        args += [pa, pb]
        ust_shape = jax.ShapeDtypeStruct((rows, d), F32)
        ust_spec = row_spec
        scratch = []
    return pl.pallas_call(
        functools.partial(_conv_kernel, tm=tm, tiles_per_seq=tiles_per_seq, seq_rows=seq_rows),
        grid=(n_tiles,),
        in_specs=in_specs,
        out_specs=[row_spec, ust_spec],
        out_shape=[jax.ShapeDtypeStruct((rows, d), F32), ust_shape],
        scratch_shapes=scratch,
        compiler_params=_cparams("arbitrary"),
        name="conv_layer",
    )(*args)


def _ffn_kernel(x_ref, mod_ref, g_ref, wgu_ref, wd_ref, xo_ref, *, d_ff, n_chunks):
    x = x_ref[...]
    h = _norm_mod(x, g_ref[...], _mod_part(mod_ref, 3), _mod_part(mod_ref, 4)).astype(BF16)
    n_mxu = d_ff // V7X_MXU_WIDTH
    edges = [(c * n_mxu // n_chunks) * V7X_MXU_WIDTH for c in range(n_chunks)] + [d_ff]
    acc = None
    for lo, hi in zip(edges[:-1], edges[1:]):
        gate = jnp.dot(h, wgu_ref[:, lo:hi], preferred_element_type=F32)
        up = jnp.dot(h, wgu_ref[:, d_ff + lo:d_ff + hi], preferred_element_type=F32)
        a = (_silu(gate) * up).astype(BF16)
        part = jnp.dot(a, wd_ref[lo:hi, :], preferred_element_type=F32)
        acc = part if acc is None else acc + part
    xo_ref[...] = x + _mod_part(mod_ref, 5) * acc


def _ffn_layer(x, mod, tiles_per_seq, g, w_gu, w_down, index, tm):
    rows, d = x.shape
    d_ff = w_down.shape[1]
    mr = mod.shape[1]
    row_spec = pl.BlockSpec((tm, d), lambda i: (i, 0))
    return pl.pallas_call(
        functools.partial(_ffn_kernel, d_ff=d_ff, n_chunks=2),
        grid=(rows // tm,),
        in_specs=[row_spec,
                  pl.BlockSpec((1, mr, 6 * d), lambda i: (i // tiles_per_seq, 0, 0)),
                  pl.BlockSpec((1, d), lambda i: (0, 0)),
                  _stacked(w_gu, index), _stacked(w_down, index)],
        out_specs=row_spec,
        out_shape=jax.ShapeDtypeStruct((rows, d), F32),
        compiler_params=_cparams("parallel"),
        name="ffn_dense",
    )(x, mod, g.reshape(1, d), w_gu, w_down)


SEG_ALIGN = 16
SEG_SIZES = (512, 256, 128, 64, 32, 16)
SPARSE_TILE = 512
EXPERT_ROW_TILE = 256
EXPERT_FF_UNIT = 2 * V7X_MXU_WIDTH
EXPERT_STAGE_BUFFERS = 4


def _route_kernel(x_ref, mod_ref, g_ref, wrt_ref, br_ref, h_ref, slots_ref, gw_ref, seg_ref, *, tm):
    x = x_ref[...]
    h = _norm_mod(x, g_ref[...], _mod_part(mod_ref, 3), _mod_part(mod_ref, 4))
    h_hi, h_lo = _split_bf16(h)
    h_ref[...] = h_hi
    w_hi, w_lo = _split_bf16(wrt_ref[...])
    nt = (((1,), (1,)), ((), ()))
    logits = (lax.dot_general(w_hi, h_hi, nt, preferred_element_type=F32)
              + lax.dot_general(w_lo, h_hi, nt, preferred_element_type=F32)
              + lax.dot_general(w_hi, h_lo, nt, preferred_element_type=F32)) + br_ref[...]
    eidx = lax.broadcasted_iota(jnp.int32, logits.shape, 0).astype(F32)
    m1 = jnp.max(logits, axis=0, keepdims=True)
    i1 = jnp.min(jnp.where(logits == m1, eidx, float(N_EXPERTS)), axis=0, keepdims=True)
    sel1 = eidx == i1
    rest = jnp.where(sel1, -jnp.inf, logits)
    m2 = jnp.max(rest, axis=0, keepdims=True)
    i2 = jnp.min(jnp.where(rest == m2, eidx, float(N_EXPERTS)), axis=0, keepdims=True)
    sel2 = eidx == i2
    e2 = jnp.exp(m2 - m1)
    den = 1.0 + e2
    gw_ref[0] = jnp.concatenate([1.0 / den, e2 / den], axis=0)
    mask = jnp.where(sel1 | sel2, 1.0, 0.0)
    before = (lax.broadcasted_iota(jnp.int32, (tm, tm), 0) < lax.broadcasted_iota(jnp.int32, (tm, tm), 1))
    rank = jnp.dot(mask.astype(BF16), jnp.where(before, 1.0, 0.0).astype(BF16), preferred_element_type=F32)
    count = jnp.sum(mask, axis=1, keepdims=True)
    seg = jnp.floor((count + (SEG_ALIGN - 1)) * (1.0 / SEG_ALIGN)) * SEG_ALIGN
    start = jnp.zeros_like(seg)
    for e in range(N_EXPERTS - 1):
        start = start + jnp.where(eidx[:, 0:1] > e, seg[e:e + 1, :], 0.0)
    slot = start + rank
    slot1 = jnp.sum(jnp.where(sel1, slot, 0.0), axis=0, keepdims=True)
    slot2 = jnp.sum(jnp.where(sel2, slot, 0.0), axis=0, keepdims=True)
    slots_ref[0] = jnp.concatenate([slot1, slot2], axis=0).astype(jnp.int32)
    seg_ref[0] = jnp.broadcast_to(seg, (N_EXPERTS, HEAD_DIM)).astype(jnp.int32)


def _route(x, mod, tiles_per_seq, g, w_router, b_router, tm):
    rows, d = x.shape
    mr = mod.shape[1]
    n_tiles = rows // tm
    row_spec = pl.BlockSpec((tm, d), lambda i: (i, 0))
    pair_spec = pl.BlockSpec((1, 2, tm), lambda i: (i, 0, 0))
    return pl.pallas_call(
        functools.partial(_route_kernel, tm=tm),
        grid=(n_tiles,),
        in_specs=[row_spec,
                  pl.BlockSpec((1, mr, 6 * d), lambda i: (i // tiles_per_seq, 0, 0)),
                  pl.BlockSpec((1, d), lambda i: (0, 0)),
                  pl.BlockSpec((N_EXPERTS, d), lambda i: (0, 0)),
                  pl.BlockSpec((N_EXPERTS, 1), lambda i: (0, 0))],
        out_specs=[row_spec, pair_spec, pair_spec,
                   pl.BlockSpec((1, N_EXPERTS, HEAD_DIM), lambda i: (i, 0, 0))],
        out_shape=[jax.ShapeDtypeStruct((rows, d), BF16),
                   jax.ShapeDtypeStruct((n_tiles, 2, tm), jnp.int32),
                   jax.ShapeDtypeStruct((n_tiles, 2, tm), F32),
                   jax.ShapeDtypeStruct((n_tiles, N_EXPERTS, HEAD_DIM), jnp.int32)],
        compiler_params=_cparams("parallel"),
        name="moe_route",
    )(x, mod, g.reshape(1, d), w_router.T, b_router.reshape(N_EXPERTS, 1))


def _segment_copies(tile, seg_ref, loc_ref, goff_ref, packed, sorted_hbm, sem, to_sorted):
    copies = []
    tokens = (packed.shape[0] - N_EXPERTS * SEG_ALIGN) // 2
    max_seg = -(-tokens // SEG_ALIGN) * SEG_ALIGN
    for e in range(N_EXPERTS):
        seg = seg_ref[tile * N_EXPERTS + e]
        loc = loc_ref[tile * N_EXPERTS + e]
        goff = goff_ref[tile * N_EXPERTS + e]
        done = jnp.int32(0)
        for size in (s for s in SEG_SIZES if s <= max_seg):
            vm = packed.at[pl.ds(pl.multiple_of(loc + done, SEG_ALIGN), size), :]
            hb = sorted_hbm.at[pl.ds(pl.multiple_of(goff + done, SEG_ALIGN), size), :]
            src, dst = (vm, hb) if to_sorted else (hb, vm)
            copies.append(((seg & size) != 0, pltpu.make_async_copy(src, dst, sem)))
            done = done + (seg & size)
    return copies


def _start_all(copies):
    for cond, cp in copies:
        pl.when(cond)(cp.start)


def _wait_all(copies):
    for cond, cp in copies:
        pl.when(cond)(cp.wait)


def _dispatch_kernel(seg_ref, loc_ref, goff_ref, h_ref, slots_ref, sorted_in, sorted_out, stage_ref, sems,
                     *, n_slots, n_tiles):
    del sorted_in
    i = pl.program_id(0)
    par = i % 2

    def copies(tile, parity):
        return _segment_copies(tile, seg_ref, loc_ref, goff_ref, stage_ref.at[parity], sorted_out,
                               sems.at[parity], True)

    @pl.when(i >= 2)
    def _():
        _wait_all(copies(i - 2, par))

    tm = h_ref.shape[0]
    srow = lax.broadcasted_iota(jnp.int32, (n_slots, tm), 0)
    onehot = (srow == slots_ref[0, 0:1, :]) | (srow == slots_ref[0, 1:2, :])
    packed = jnp.dot(jnp.where(onehot, 1.0, 0.0).astype(BF16), h_ref[...], preferred_element_type=F32)
    stage_ref[par] = packed.astype(BF16)
    _start_all(copies(i, par))

    @pl.when(i == n_tiles - 1)
    def _():
        if n_tiles >= 2:
            _wait_all(copies(i - 1, 1 - par))
        _wait_all(copies(i, par))


def _dispatch(h, slots, seg, loc, goff, h_sorted, tm):
    rows, d = h.shape
    n_tiles = rows // tm
    n_slots = 2 * tm + N_EXPERTS * SEG_ALIGN
    return pl.pallas_call(
        functools.partial(_dispatch_kernel, n_slots=n_slots, n_tiles=n_tiles),
        grid_spec=pltpu.PrefetchScalarGridSpec(
            num_scalar_prefetch=3,
            grid=(n_tiles,),
            in_specs=[pl.BlockSpec((tm, d), lambda i, *_: (i, 0)),
                      pl.BlockSpec((1, 2, tm), lambda i, *_: (i, 0, 0)),
                      pl.BlockSpec(memory_space=pl.ANY)],
            out_specs=pl.BlockSpec(memory_space=pl.ANY),
            scratch_shapes=[pltpu.VMEM((2, n_slots, d), BF16), pltpu.SemaphoreType.DMA((2,))]),
        out_shape=jax.ShapeDtypeStruct(h_sorted.shape, BF16),
        input_output_aliases={5: 0},
        compiler_params=_cparams("arbitrary"),
        name="moe_dispatch",
    )(seg, loc, goff, h, slots, h_sorted)


def _experts_kernel(te_ref, jb_ref, nact_ref, phase_ref, nxt_ref, h_ref, wgu_hbm, wd_hbm, y_ref,
                    wg_ref, wd_ref, sg_ref, su_ref, sd_ref, sems, *, first_expert, d_ff, unit):
    del jb_ref
    j = pl.program_id(0)
    n_units = d_ff // unit
    depth = sg_ref.shape[0]
    active = j < nact_ref[0]

    def fetch(expert, u, buf):
        return (pltpu.make_async_copy(wgu_hbm.at[expert, :, pl.ds(u * unit, unit)], sg_ref.at[buf], sems.at[buf]),
                pltpu.make_async_copy(wgu_hbm.at[expert, :, pl.ds(d_ff + u * unit, unit)], su_ref.at[buf],
                                      sems.at[buf]),
                pltpu.make_async_copy(wd_hbm.at[expert, pl.ds(u * unit, unit), :], sd_ref.at[buf], sems.at[buf]))

    def start(expert, u):
        for cp in fetch(expert, u, u % depth):
            cp.start()

    def start_first(expert):
        for u in range(min(depth, n_units)):
            start(expert, u)

    def finish(expert, u):
        for cp in fetch(expert, u, u % depth):
            cp.wait()
        wg_ref[u, :, 0:unit] = sg_ref[u % depth].astype(BF16)
        wg_ref[u, :, unit:2 * unit] = su_ref[u % depth].astype(BF16)
        wd_ref[u] = sd_ref[u % depth].astype(BF16)
        if u + depth < n_units:
            start(expert, u + depth)

    def unit_out(h, u):
        gate = jnp.dot(h, wg_ref[u, :, 0:unit], preferred_element_type=F32)
        up = jnp.dot(h, wg_ref[u, :, unit:2 * unit], preferred_element_type=F32)
        a = (_silu(gate) * up).astype(BF16)
        return jnp.dot(a, wd_ref[u], preferred_element_type=F32)

    @pl.when(active & (j == 0))
    def _():
        expert = first_expert + te_ref[0]
        start_first(expert)
        for u in range(n_units):
            finish(expert, u)

    phase = phase_ref[j]
    swap = active & (phase >= 2)

    @pl.when(active & (phase == 1))
    def _():
        start_first(first_expert + nxt_ref[j])

    @pl.when(active & jnp.logical_not(swap))
    def _():
        h = h_ref[...]
        acc = unit_out(h, 0)
        for u in range(1, n_units):
            acc = acc + unit_out(h, u)
        y_ref[...] = acc.astype(BF16)

    @pl.when(swap)
    def _():
        nxt = first_expert + nxt_ref[j]
        pl.when(phase == 3)(functools.partial(start_first, nxt))
        h = h_ref[...]
        acc = None
        for u in range(n_units):
            part = unit_out(h, u)
            acc = part if acc is None else acc + part
            finish(nxt, u)
        y_ref[...] = acc.astype(BF16)

    @pl.when(jnp.logical_not(active))
    def _():
        y_ref[...] = jnp.zeros_like(y_ref)


def _experts(h_sorted, tile_expert, tile_block, n_active, tile_phase, tile_next, w_gu, w_down, index, tmf):
    n_rows, d = h_sorted.shape
    d_ff = w_down.shape[2]
    unit = EXPERT_FF_UNIT
    n_units = d_ff // unit
    depth = EXPERT_STAGE_BUFFERS
    w_gu = w_gu.reshape(-1, d, 2 * d_ff)
    w_down = w_down.reshape(-1, d_ff, d)
    return pl.pallas_call(
        functools.partial(_experts_kernel, first_expert=index * N_EXPERTS, d_ff=d_ff, unit=unit),
        grid_spec=pltpu.PrefetchScalarGridSpec(
            num_scalar_prefetch=5,
            grid=(n_rows // tmf,),
            in_specs=[pl.BlockSpec((tmf, d), lambda j, te, jb, *_: (jb[j], 0)),
                      pl.BlockSpec(memory_space=pl.ANY),
                      pl.BlockSpec(memory_space=pl.ANY)],
            out_specs=pl.BlockSpec((tmf, d), lambda j, *_: (j, 0)),
            scratch_shapes=[pltpu.VMEM((n_units, d, 2 * unit), BF16),
                            pltpu.VMEM((n_units, unit, d), BF16),
                            pltpu.VMEM((depth, d, unit), F32),
                            pltpu.VMEM((depth, d, unit), F32),
                            pltpu.VMEM((depth, unit, d), F32),
                            pltpu.SemaphoreType.DMA((depth,))]),
        out_shape=jax.ShapeDtypeStruct((n_rows, d), BF16),
        compiler_params=_cparams("arbitrary"),
        name="moe_experts_sorted",
    )(tile_expert, tile_block, n_active, tile_phase, tile_next, h_sorted, w_gu, w_down)


def _combine_kernel(seg_ref, loc_ref, goff_ref, x_ref, mod_ref, slots_ref, gw_ref, sorted_hbm, xo_ref,
                    stage_ref, sems, *, n_slots, n_tiles):
    i = pl.program_id(0)
    par = i % 2

    def copies(tile, parity):
        return _segment_copies(tile, seg_ref, loc_ref, goff_ref, stage_ref.at[parity], sorted_hbm,
                               sems.at[parity], False)

    @pl.when(i == 0)
    def _():
        stage_ref[...] = jnp.zeros_like(stage_ref)
        _start_all(copies(i, par))

    @pl.when(i + 1 < n_tiles)
    def _():
        _start_all(copies(i + 1, 1 - par))

    _wait_all(copies(i, par))
    y = stage_ref[par]
    tm = x_ref.shape[0]
    scol = lax.broadcasted_iota(jnp.int32, (tm, n_slots), 1)
    gates = (jnp.where(scol == slots_ref[:, 0:1], gw_ref[:, 0:1], 0.0)
             + jnp.where(scol == slots_ref[:, 1:2], gw_ref[:, 1:2], 0.0))
    moe = jnp.dot(gates.astype(BF16), y, preferred_element_type=F32)
    xo_ref[...] = x_ref[...] + _mod_part(mod_ref, 5) * moe


def _combine(x, mod, tiles_per_seq, slots_col, gw_col, y_sorted, seg, loc, goff, tm):
    rows, d = x.shape
    mr = mod.shape[1]
    n_tiles = rows // tm
    n_slots = 2 * tm + N_EXPERTS * SEG_ALIGN
    row_spec = pl.BlockSpec((tm, d), lambda i, *_: (i, 0))
    pair_spec = pl.BlockSpec((tm, 2), lambda i, *_: (i, 0))
    return pl.pallas_call(
        functools.partial(_combine_kernel, n_slots=n_slots, n_tiles=n_tiles),
        grid_spec=pltpu.PrefetchScalarGridSpec(
            num_scalar_prefetch=3,
            grid=(n_tiles,),
            in_specs=[row_spec,
                      pl.BlockSpec((1, mr, 6 * d), lambda i, *_: (i // tiles_per_seq, 0, 0)),
                      pair_spec, pair_spec,
                      pl.BlockSpec(memory_space=pl.ANY)],
            out_specs=row_spec,
            scratch_shapes=[pltpu.VMEM((2, n_slots, d), BF16), pltpu.SemaphoreType.DMA((2,))]),
        out_shape=jax.ShapeDtypeStruct((rows, d), F32),
        compiler_params=_cparams("arbitrary"),
        name="moe_combine",
    )(seg, loc, goff, x, mod, slots_col, gw_col, y_sorted)


def _moe_sparse_layer(parts, g, w_router, b_router, w_gu, w_down, index):
    tmf = EXPERT_ROW_TILE
    d = parts[0][0].shape[1]
    routed = [_route(x, mod, tps, g, w_router, b_router, tm) for x, mod, tps, tm in parts]
    tiles = [x.shape[0] // tm for x, _, _, tm in parts]
    seg = jnp.concatenate([r[3][:, :, 0] for r in routed], axis=0)
    loc = jnp.cumsum(seg, axis=1) - seg
    region = ((jnp.sum(seg, axis=0) + tmf - 1) // tmf) * tmf
    region_end = jnp.cumsum(region)
    goff = (region_end - region)[None, :] + jnp.cumsum(seg, axis=0) - seg
    n_sorted = sum(2 * x.shape[0] for x, _, _, _ in parts) + sum(tiles) * N_EXPERTS * (SEG_ALIGN - 1)
    n_sorted += N_EXPERTS * (tmf - SEG_ALIGN)
    n_sorted = ((n_sorted + tmf - 1) // tmf) * tmf
    n_row_tiles = n_sorted // tmf
    n_active = region_end[-1] // tmf
    tile_ids = jnp.arange(n_row_tiles, dtype=jnp.int32)
    tile_block = jnp.maximum(jnp.minimum(tile_ids, n_active - 1), 0)
    tile_expert = jnp.sum((region_end // tmf)[None, :] <= tile_block[:, None], axis=1)
    tile_expert = jnp.minimum(tile_expert, N_EXPERTS - 1).astype(jnp.int32)
    ids = jnp.arange(N_EXPERTS, dtype=jnp.int32)
    later = jnp.where((ids[None, :] > ids[:, None]) & (region[None, :] > 0), ids[None, :], N_EXPERTS)
    next_expert = jnp.min(later, axis=1)
    tile_next = jnp.where(next_expert < N_EXPERTS, next_expert, -1)[tile_expert].astype(jnp.int32)
    from_end = (region_end // tmf)[tile_expert] - 1 - tile_block
    single = (region // tmf)[tile_expert] == 1
    tile_phase = jnp.where(from_end == 1, 1, jnp.where(from_end == 0, jnp.where(single, 3, 2), 0))
    tile_phase = jnp.where(tile_next >= 0, tile_phase, 0).astype(jnp.int32)
    tables, first = [], 0
    for n in tiles:
        cut = lambda a: a[first:first + n].reshape(-1).astype(jnp.int32)
        tables.append((cut(seg), cut(loc), cut(goff)))
        first += n
    h_sorted = jnp.zeros((n_sorted, d), BF16)
    for (_, _, _, tm), (h, slots, _, _), tab in zip(parts, routed, tables):
        h_sorted = _dispatch(h, slots, *tab, h_sorted, tm)
    y_sorted = _experts(h_sorted, tile_expert, tile_block.astype(jnp.int32),
                        n_active.reshape(1).astype(jnp.int32), tile_phase, tile_next, w_gu, w_down, index, tmf)
    outs = []
    for (x, mod, tps, tm), (_, slots, gw, _), tab in zip(parts, routed, tables):
        rows = x.shape[0]
        slots_col = jnp.transpose(slots, (0, 2, 1)).reshape(rows, 2)
        gw_col = jnp.transpose(gw, (0, 2, 1)).reshape(rows, 2)
        outs.append(_combine(x, mod, tps, slots_col, gw_col, y_sorted, *tab, tm))
    return outs


def _rope_tables(pos):
    half = ROT_DIM // 2
    inv = jnp.float32(ROPE_THETA) ** (-jnp.arange(half, dtype=jnp.float32) / half)
    ang = pos.astype(jnp.float32)[:, None] * inv[None, :]
    cos, sin = jnp.cos(ang), jnp.sin(ang)
    t = pos.shape[0]
    pad1 = jnp.ones((t, HEAD_DIM - ROT_DIM), F32)
    pad0 = jnp.zeros((t, HEAD_DIM - ROT_DIM), F32)
    return (jnp.concatenate([cos, cos, pad1], axis=1), jnp.concatenate([-sin, sin, pad0], axis=1))


def _swap_rot_halves(v):
    half = ROT_DIM // 2
    return jnp.concatenate([v[half:ROT_DIM], v[:half], v[ROT_DIM:]])


def _rope_gain_tables(cos_ref, sin_ref, g_ref, gs_ref, scale):
    return cos_ref[...] * (g_ref[...] * scale), sin_ref[...] * (gs_ref[...] * scale)


def _head_norm_rope(xh, a, b):
    half = ROT_DIM // 2
    ones = jnp.ones((HEAD_DIM, HEAD_DIM), BF16)
    ss = jnp.dot((xh * xh).astype(BF16), ones, preferred_element_type=F32)
    rs = lax.rsqrt(ss * (1.0 / HEAD_DIM) + EPS)
    lane = lax.broadcasted_iota(jnp.int32, xh.shape, 1)
    swapped = jnp.where(lane < half, pltpu.roll(xh, HEAD_DIM - half, 1), pltpu.roll(xh, half, 1))
    return (xh * a + swapped * b) * rs


SPLIT = 4
SPLIT_TILE = SPLIT * Q_BLOCK
SPLIT_GROUPS = (1, 2)


def _kv_kernel(x_ref, mod_ref, g_ref, wkv_ref, gk_ref, cos_ref, sin_ref, *out_refs, split):
    x = x_ref[...]
    h = _norm_mod(x, g_ref[...], mod_ref[0, :, 0:D_MODEL], mod_ref[0, :, D_MODEL:2 * D_MODEL]).astype(BF16)
    rope_a, rope_b = _rope_gain_tables(cos_ref, sin_ref, gk_ref.at[0:1], gk_ref.at[1:2], 1.0)
    k_refs, v_refs = out_refs[:N_GROUPS], out_refs[N_GROUPS:2 * N_GROUPS]
    kv_all = jnp.dot(h, wkv_ref[...], preferred_element_type=F32)
    for hd in range(N_KV_HEADS):
        kh = kv_all[:, hd * HEAD_DIM:(hd + 1) * HEAD_DIM]
        vh = kv_all[:, KV_WIDTH + hd * HEAD_DIM:KV_WIDTH + (hd + 1) * HEAD_DIM]
        g_idx, k_idx = divmod(hd, HKV_G)
        sl = slice(k_idx * HEAD_DIM, (k_idx + 1) * HEAD_DIM)
        kh = _head_norm_rope(kh, rope_a, rope_b)
        rows_of_head = pl.ds(k_idx, kh.shape[0], stride=HKV_G)
        k_refs[g_idx][rows_of_head, :] = kh
        v_refs[g_idx][rows_of_head, :] = vh
        if split and g_idx in SPLIT_GROUPS:
            n = SPLIT_GROUPS.index(g_idx)
            scr = out_refs[-1]
            for j, (val, dst) in enumerate(((kh, out_refs[2 * N_GROUPS + 2 * n]),
                                            (vh, out_refs[2 * N_GROUPS + 2 * n + 1]))):
                slot = (n * HKV_G + k_idx) * 2 + j
                scr[slot] = val
                for c in range(SPLIT):
                    dst[c * Q_BLOCK:(c + 1) * Q_BLOCK, sl] = (
                        scr[slot, pl.ds(c, Q_BLOCK, stride=SPLIT), :].astype(BF16))


def _kv_proj(x, mod_kv, tiles_per_seq, g, w_kv, g_k, cos, sin, tm, split):
    rows, d = x.shape
    mr = mod_kv.shape[1]
    assert not split or tm == SPLIT_TILE
    row_spec = pl.BlockSpec((tm, d), lambda i: (i, 0))
    tab_spec = pl.BlockSpec((tm, HEAD_DIM), lambda i: (i % tiles_per_seq, 0))
    out_spec = pl.BlockSpec((tm, GROUP_KV_WIDTH), lambda i: (i, 0))
    nat_spec = pl.BlockSpec((HKV_G * tm, HEAD_DIM), lambda i: (i, 0))
    n_split = 2 * len(SPLIT_GROUPS) if split else 0
    return pl.pallas_call(
        functools.partial(_kv_kernel, split=split),
        grid=(rows // tm,),
        in_specs=[row_spec,
                  pl.BlockSpec((1, mr, 2 * d), lambda i: (i // tiles_per_seq, 0, 0)),
                  pl.BlockSpec((1, d), lambda i: (0, 0)),
                  _resident((d, 2 * KV_WIDTH), lambda i: (0, 0)),
                  pl.BlockSpec((2, HEAD_DIM), lambda i: (0, 0)),
                  tab_spec, tab_spec],
        out_specs=[nat_spec] * (2 * N_GROUPS) + [out_spec] * n_split,
        out_shape=[jax.ShapeDtypeStruct((HKV_G * rows, HEAD_DIM), F32)] * (2 * N_GROUPS)
                  + [jax.ShapeDtypeStruct((rows, GROUP_KV_WIDTH), BF16)] * n_split,
        scratch_shapes=[pltpu.VMEM((n_split * HKV_G, tm, HEAD_DIM), F32)] if split else [],
        compiler_params=_cparams("parallel"),
        name="kv_proj",
    )(x, mod_kv, g.reshape(1, d), w_kv, jnp.stack([g_k, _swap_rot_halves(g_k)]), cos, sin)


def _q_kernel(x_ref, mod_ref, g_ref, wq_ref, gq_ref, cos_ref, sin_ref, *refs, split):
    q_refs = refs[:N_GROUPS]
    x = x_ref[...]
    h = _norm_mod(x, g_ref[...], _mod_part(mod_ref, 0), _mod_part(mod_ref, 1)).astype(BF16)
    rope_a, rope_b = _rope_gain_tables(cos_ref, sin_ref, gq_ref.at[0:1], gq_ref.at[1:2], ATTN_SCALE)
    q_all = jnp.dot(h, wq_ref[...], preferred_element_type=F32)
    for hd in range(N_Q_HEADS):
        g_idx, h_idx = divmod(hd, HQ_G)
        sl = slice(h_idx * HEAD_DIM, (h_idx + 1) * HEAD_DIM)
        qh = _head_norm_rope(q_all[:, hd * HEAD_DIM:(hd + 1) * HEAD_DIM], rope_a, rope_b)
        if split and g_idx in SPLIT_GROUPS:
            scr = refs[N_GROUPS]
            n = SPLIT_GROUPS.index(g_idx) * HQ_G + h_idx
            scr[n] = qh
            for c in range(SPLIT):
                q_refs[g_idx][c * Q_BLOCK:(c + 1) * Q_BLOCK, sl] = (
                    scr[n, pl.ds(c, Q_BLOCK, stride=SPLIT), :].astype(BF16))
        else:
            q_refs[g_idx][:, sl] = qh.astype(BF16)


def _q_proj(x, mod, tiles_per_seq, g, w_q, index, g_q, cos, sin, tm, split):
    rows, d = x.shape
    mr = mod.shape[1]
    assert not split or tm == SPLIT_TILE
    row_spec = pl.BlockSpec((tm, d), lambda i: (i, 0))
    tab_spec = pl.BlockSpec((tm, HEAD_DIM), lambda i: (i % tiles_per_seq, 0))
    return pl.pallas_call(
        functools.partial(_q_kernel, split=split),
        grid=(rows // tm,),
        in_specs=[row_spec,
                  pl.BlockSpec((1, mr, 6 * d), lambda i: (i // tiles_per_seq, 0, 0)),
                  pl.BlockSpec((1, d), lambda i: (0, 0)),
                  _stacked(w_q, index),
                  pl.BlockSpec((2, HEAD_DIM), lambda i: (0, 0)),
                  tab_spec, tab_spec],
        out_specs=[pl.BlockSpec((tm, O_WIDTH), lambda i: (i, 0))] * N_GROUPS,
        out_shape=[jax.ShapeDtypeStruct((rows, O_WIDTH), BF16)] * N_GROUPS,
        scratch_shapes=[pltpu.VMEM((len(SPLIT_GROUPS) * HQ_G, tm, HEAD_DIM), F32)] if split else [],
        compiler_params=_cparams("parallel"),
        name="q_proj",
    )(x, mod, g.reshape(1, d), w_q, jnp.stack([g_q, _swap_rot_halves(g_q)]), cos, sin)


def _attn_prompt_kernel(*refs, step, has_prev, n_blk):
    if has_prev:
        q_ref, k_ref, v_ref, kp_ref, vp_ref, o_ref, st_ref = refs
        not_first = pl.program_id(1) > 0
    else:
        q_ref, k_ref, v_ref, o_ref, st_ref = refs
    nq = REP * Q_BLOCK
    lane = lax.broadcasted_iota(jnp.int32, (Q_BLOCK, HEAD_DIM), 1)
    max_kb = 2 if step == 1 else n_blk
    col = lax.broadcasted_iota(jnp.int32, (nq, max_kb * Q_BLOCK), 1)
    dist = (max_kb - 1) * Q_BLOCK + (lax.broadcasted_iota(jnp.int32, col.shape, 0) & (Q_BLOCK - 1)) - col
    reach = (dist >= 0) & (dist <= step * Q_BLOCK) & ((dist & (step - 1)) == 0)
    for i in range(n_blk):
        first_kb = i - 1 if step == 1 else 0
        if first_kb < 0 and not has_prev:
            first_kb = 0
        kbs = list(range(first_kb, i + 1))
        valid = reach[:, (max_kb - len(kbs)) * Q_BLOCK:]
        if first_kb < 0:
            valid = valid & (not_first | (col >= Q_BLOCK))
        for k in range(HKV_G):
            ksl = slice(k * HEAD_DIM, (k + 1) * HEAD_DIM)
            q = jnp.concatenate(
                [q_ref[i, :, (k * REP + r) * HEAD_DIM:(k * REP + r + 1) * HEAD_DIM] for r in range(REP)], axis=0)
            if has_prev:
                hrows = pl.ds(k, Q_BLOCK, stride=HKV_G)
                keys = jnp.concatenate([kp_ref[hrows, :] if j < 0 else k_ref[j, hrows, :] for j in kbs], axis=0)
                vals = jnp.concatenate([vp_ref[hrows, :] if j < 0 else v_ref[j, hrows, :] for j in kbs], axis=0)
            else:
                keys = jnp.concatenate([k_ref[j, :, ksl] for j in kbs], axis=0)
                vals = jnp.concatenate([v_ref[j, :, ksl] for j in kbs], axis=0)
            s = lax.dot_general(q, keys.astype(BF16), (((1,), (1,)), ((), ())), preferred_element_type=F32)
            s = jnp.where(valid, s, NEG)
            m = jnp.max(s, axis=1, keepdims=True)
            p = jnp.exp(s - m)
            den = jnp.sum(p, axis=1, keepdims=True)
            out = jnp.dot(p.astype(BF16), vals.astype(BF16), preferred_element_type=F32) / den
            lse = m + jnp.log(den)
            stats = jnp.zeros((Q_BLOCK, HEAD_DIM), F32)
            for r in range(REP):
                rs = slice(r * Q_BLOCK, (r + 1) * Q_BLOCK)
                o_ref[i, :, (k * REP + r) * HEAD_DIM:(k * REP + r + 1) * HEAD_DIM] = out[rs].astype(BF16)
                stats = jnp.where(lane == r, lse[rs], stats)
            st_ref[i, :, ksl] = stats


def _attn_prompt_group(q, k_g, v_g, group, batch, seq):
    _, dil = DIL_GROUPS[group]
    rows = batch * seq
    n_tiles = seq // SPLIT_TILE
    if group in SPLIT_GROUPS:
        step = dil // SPLIT
        lead = (batch, n_tiles, SPLIT, Q_BLOCK)
        view = lambda a: a.reshape(*lead, a.shape[-1])
        spec = lambda w: pl.BlockSpec((None, n_tiles, None, Q_BLOCK, w), lambda b, c: (b, 0, c, 0, 0))
        out_shape = [jax.ShapeDtypeStruct((*lead, O_WIDTH), BF16),
                     jax.ShapeDtypeStruct((*lead, GROUP_KV_WIDTH), F32)]
        out, stats = pl.pallas_call(
            functools.partial(_attn_prompt_kernel, step=step, has_prev=False, n_blk=n_tiles),
            grid=(batch, SPLIT),
            in_specs=[spec(O_WIDTH), spec(GROUP_KV_WIDTH), spec(GROUP_KV_WIDTH)],
            out_specs=[spec(O_WIDTH), spec(GROUP_KV_WIDTH)],
            out_shape=out_shape,
            compiler_params=_cparams("parallel", "parallel"),
            name=f"attn_prompt_g{group}",
        )(view(q), view(k_g), view(v_g))
    else:
        lead = (batch * n_tiles, SPLIT)
        view = lambda a, r: a.reshape(*lead, r, a.shape[-1])
        spec = lambda r, w: pl.BlockSpec((None, SPLIT, r, w), lambda b, i: (b * n_tiles + i, 0, 0, 0))
        kv_rows = HKV_G * Q_BLOCK
        kv_spec = spec(kv_rows, HEAD_DIM)
        prev = pl.BlockSpec((None, None, kv_rows, HEAD_DIM),
                            lambda b, i: (jnp.maximum(b * n_tiles + i - 1, 0), SPLIT - 1, 0, 0))
        out_shape = [jax.ShapeDtypeStruct((*lead, Q_BLOCK, O_WIDTH), BF16),
                     jax.ShapeDtypeStruct((*lead, Q_BLOCK, GROUP_KV_WIDTH), F32)]
        k4, v4 = view(k_g, kv_rows), view(v_g, kv_rows)
        out, stats = pl.pallas_call(
            functools.partial(_attn_prompt_kernel, step=dil, has_prev=True, n_blk=SPLIT),
            grid=(batch, n_tiles),
            in_specs=[spec(Q_BLOCK, O_WIDTH), kv_spec, kv_spec, prev, prev],
            out_specs=[spec(Q_BLOCK, O_WIDTH), spec(Q_BLOCK, GROUP_KV_WIDTH)],
            out_shape=out_shape,
            compiler_params=_cparams("parallel", "arbitrary"),
            name=f"attn_prompt_g{group}",
        )(view(q, Q_BLOCK), k4, v4, k4, v4)
    return out.reshape(rows, O_WIDTH), stats.reshape(rows, GROUP_KV_WIDTH)


def _merge_o_kernel(x_ref, mod_ref, o0_ref, o1_ref, o2_ref, s0_ref, s1_ref, s2_ref, wo_ref, xo_ref,
                    on_ref, sn_ref):
    for n, (o_ref, s_ref) in enumerate(((o1_ref, s1_ref), (o2_ref, s2_ref))):
        for c in range(SPLIT):
            rows = slice(c * Q_BLOCK, (c + 1) * Q_BLOCK)
            dst = pl.ds(c, Q_BLOCK, stride=SPLIT)
            for hd in range(HQ_G):
                on_ref[n, hd, dst, :] = o_ref[rows, hd * HEAD_DIM:(hd + 1) * HEAD_DIM].astype(F32)
            for k in range(HKV_G):
                sn_ref[n, k, dst, :] = s_ref[rows, k * HEAD_DIM:(k + 1) * HEAD_DIM]
    heads = []
    for k in range(HKV_G):
        for r in range(REP):
            hd = k * REP + r
            lses = [s0_ref[:, k * HEAD_DIM + r:k * HEAD_DIM + r + 1],
                    sn_ref[0, k, :, r:r + 1], sn_ref[1, k, :, r:r + 1]]
            m = jnp.maximum(jnp.maximum(lses[0], lses[1]), lses[2])
            es = [jnp.exp(l - m) for l in lses]
            den = es[0] + es[1] + es[2]
            sl = slice(hd * HEAD_DIM, (hd + 1) * HEAD_DIM)
            o0 = o0_ref[:, sl].astype(F32)
            o = o0 + (es[1] / den) * (on_ref[0, hd] - o0) + (es[2] / den) * (on_ref[1, hd] - o0)
            heads.append(o.astype(BF16))
    o = jnp.concatenate(heads, axis=1)
    y = jnp.dot(o, wo_ref[...], preferred_element_type=F32)
    xo_ref[...] = x_ref[...] + _mod_part(mod_ref, 2) * y


def _merge_o(x, mod, tiles_per_seq, outs, stats, w_o, index, tm):
    rows, d = x.shape
    mr = mod.shape[1]
    assert tm == SPLIT_TILE
    row_spec = pl.BlockSpec((tm, d), lambda i: (i, 0))
    o_spec = pl.BlockSpec((tm, O_WIDTH), lambda i: (i, 0))
    st_spec = pl.BlockSpec((tm, GROUP_KV_WIDTH), lambda i: (i, 0))
    return pl.pallas_call(
        _merge_o_kernel,
        grid=(rows // tm,),
        in_specs=[row_spec,
                  pl.BlockSpec((1, mr, 6 * d), lambda i: (i // tiles_per_seq, 0, 0))]
                 + [o_spec] * 3 + [st_spec] * 3 + [_stacked(w_o, index)],
        out_specs=row_spec,
        out_shape=jax.ShapeDtypeStruct((rows, d), F32),
        scratch_shapes=[pltpu.VMEM((len(SPLIT_GROUPS), HQ_G, tm, HEAD_DIM), F32),
                        pltpu.VMEM((len(SPLIT_GROUPS), HKV_G, tm, HEAD_DIM), F32)],
        compiler_params=_cparams("parallel"),
        name="merge_o_proj",
    )(x, mod, *outs, *stats, w_o)


def _attn_sample_kernel(*refs, n_new, write_buffers):
    q_ref = refs[0]
    nk_refs = refs[1:1 + N_GROUPS]
    nv_refs = refs[1 + N_GROUPS:1 + 2 * N_GROUPS]
    ck_refs = refs[1 + 2 * N_GROUPS:1 + 3 * N_GROUPS]
    cv_refs = refs[1 + 3 * N_GROUPS:1 + 4 * N_GROUPS]
    o_ref = refs[1 + 4 * N_GROUPS]
    nq = REP * n_new
    n_new_rows = HKV_G * n_new
    qi = lax.broadcasted_iota(jnp.int32, (nq, 1), 0) & (n_new - 1)

    if write_buffers:
        ok_refs = refs[2 + 4 * N_GROUPS:2 + 5 * N_GROUPS]
        ov_refs = refs[2 + 5 * N_GROUPS:2 + 6 * N_GROUPS]
        for g in range(N_GROUPS):
            for c_ref, n_ref, o_buf in ((ck_refs[g], nk_refs[g], ok_refs[g]), (cv_refs[g], nv_refs[g], ov_refs[g])):
                n_rows = c_ref.shape[1]
                o_buf[0, 0:n_rows - n_new_rows, :] = c_ref[0, n_new_rows:n_rows, :]
                o_buf[0, n_rows - n_new_rows:n_rows, :] = n_ref[0]

    for k in range(HKV_G):
        outs, lses = [], []
        for g, (window, dil) in enumerate(DIL_GROUPS):
            q = q_ref[0, g, k]
            qf = q.astype(F32)
            if len(ck_refs[g].shape) == 4:
                keys = ck_refs[g][0].reshape(-1, HEAD_DIM).astype(BF16)
                vals = cv_refs[g][0].reshape(-1, HEAD_DIM).astype(BF16)
                s = lax.dot_general(q, keys, (((1,), (1,)), ((), ())), preferred_element_type=F32)
                idx = lax.broadcasted_iota(jnp.int32, s.shape, 1)
                residue = (idx >> (HKV_G.bit_length() - 1)) & (n_new - 1)
                s = jnp.where(((idx & (HKV_G - 1)) == k) & (residue == qi), s, NEG)
            else:
                length = ck_refs[g].shape[1] // HKV_G
                keys = ck_refs[g][0, pl.ds(k, length, stride=HKV_G), :].astype(BF16)
                vals = cv_refs[g][0, pl.ds(k, length, stride=HKV_G), :].astype(BF16)
                s = lax.dot_general(q, keys, (((1,), (1,)), ((), ())), preferred_element_type=F32)
                idx = lax.broadcasted_iota(jnp.int32, (nq, length), 1)
                diff = length + qi - idx
                s = jnp.where(((diff & (dil - 1)) == 0) & (diff <= window), s, NEG)
            new_k = nk_refs[g][0]
            new_v = nv_refs[g][0]
            s_new = []
            for j in range(n_new):
                sj = jnp.sum(qf * new_k[HKV_G * j + k:HKV_G * j + k + 1, :], axis=1, keepdims=True)
                ok = (qi >= j) & (((qi - j) & (dil - 1)) == 0)
                s_new.append(jnp.where(ok, sj, NEG))
            m = jnp.max(s, axis=1, keepdims=True)
            for sj in s_new:
                m = jnp.maximum(m, sj)
            p = jnp.exp(s - m)
            den = jnp.sum(p, axis=1, keepdims=True)
            acc = jnp.dot(p.astype(BF16), vals, preferred_element_type=F32)
            for j, sj in enumerate(s_new):
                pj = jnp.exp(sj - m)
                den = den + pj
                acc = acc + pj * new_v[HKV_G * j + k:HKV_G * j + k + 1, :]
            outs.append(acc / den)
            lses.append(m + jnp.log(den))
        m = jnp.maximum(jnp.maximum(lses[0], lses[1]), lses[2])
        es = [jnp.exp(l - m) for l in lses]
        den = es[0] + es[1] + es[2]
        o_ref[0, k] = sum((e / den) * o for e, o in zip(es, outs))


def _attn_sample(q, new_k, new_v, cache_k, cache_v, n_seq, n_new, write_buffers):
    assert HKV_G * n_new == F32_SUBLANES
    q5 = q.reshape(n_seq, n_new, N_GROUPS, HKV_G, REP, HEAD_DIM)
    q5 = jnp.transpose(q5, (0, 2, 3, 4, 1, 5)).reshape(n_seq, N_GROUPS, HKV_G, REP * n_new, HEAD_DIM)
    rows2 = lambda a, n: a.reshape(n_seq, n * HKV_G, HEAD_DIM)
    nk = [rows2(a, n_new) for a in new_k]
    nv = [rows2(a, n_new) for a in new_v]
    ck = [rows2(c, c.shape[1]) for c in cache_k]
    cv = [rows2(c, c.shape[1]) for c in cache_v]
    new_spec = pl.BlockSpec((1, HKV_G * n_new, HEAD_DIM), lambda b: (b, 0, 0))
    cache_specs = [pl.BlockSpec((1, c.shape[1], HEAD_DIM), lambda b: (b, 0, 0)) for c in ck]
    cache_shapes = [jax.ShapeDtypeStruct(c.shape, F32) for c in ck]
    n_buf = 2 if write_buffers else 0
    in_cache_specs = list(cache_specs)
    if not write_buffers:
        for g, (window, dil) in enumerate(DIL_GROUPS):
            length = cache_k[g].shape[1]
            if dil > n_new and length == window and window % dil == 0:
                thin = lambda c: c.reshape(n_seq, length // dil, dil * HKV_G, HEAD_DIM)
                ck[g], cv[g] = thin(ck[g]), thin(cv[g])
                in_cache_specs[g] = pl.BlockSpec((1, length // dil, n_new * HKV_G, HEAD_DIM),
                                                 lambda b: (b, 0, 0, 0))
    res = pl.pallas_call(
        functools.partial(_attn_sample_kernel, n_new=n_new, write_buffers=write_buffers),
        grid=(n_seq,),
        in_specs=[pl.BlockSpec((1, N_GROUPS, HKV_G, REP * n_new, HEAD_DIM), lambda b: (b, 0, 0, 0, 0))]
                 + [new_spec] * (2 * N_GROUPS) + in_cache_specs + in_cache_specs,
        out_specs=[pl.BlockSpec((1, HKV_G, REP * n_new, HEAD_DIM), lambda b: (b, 0, 0, 0))]
                  + cache_specs * n_buf,
        out_shape=[jax.ShapeDtypeStruct((n_seq, HKV_G, REP * n_new, HEAD_DIM), F32)] + cache_shapes * n_buf,
        compiler_params=_cparams("parallel"),
        name="attn_sample",
    )(q5, *nk, *nv, *ck, *cv)
    o = res[0].reshape(n_seq, HKV_G, REP, n_new, HEAD_DIM)
    o = jnp.transpose(o, (0, 3, 1, 2, 4)).reshape(n_seq * n_new, O_WIDTH)
    if not write_buffers:
        return o, None, None
    shape4 = lambda a: a.reshape(n_seq, a.shape[1] // HKV_G, HKV_G, HEAD_DIM)
    new_ck = [shape4(a) for a in res[1:1 + N_GROUPS]]
    new_cv = [shape4(a) for a in res[1 + N_GROUPS:1 + 2 * N_GROUPS]]
    return o, new_ck, new_cv


def _o_proj_kernel(x_ref, mod_ref, o_ref, wo_ref, xo_ref):
    y = jnp.dot(o_ref[...].astype(BF16), wo_ref[...], preferred_element_type=F32)
    xo_ref[...] = x_ref[...] + _mod_part(mod_ref, 2) * y


def _o_proj(x, mod, o, w_o, index):
    rows, d = x.shape
    mr = mod.shape[1]
    full = pl.BlockSpec((rows, d), lambda i: (0, 0))
    return pl.pallas_call(
        _o_proj_kernel,
        grid=(1,),
        in_specs=[full, pl.BlockSpec((1, mr, 6 * d), lambda i: (0, 0, 0)),
                  pl.BlockSpec((rows, O_WIDTH), lambda i: (0, 0)),
                  _stacked(w_o, index)],
        out_specs=full,
        out_shape=jax.ShapeDtypeStruct((rows, d), F32),
        compiler_params=_cparams("arbitrary"),
        name="o_proj_sample",
    )(x, mod, o, w_o)


def _trunk(x, mods, mod_kv, pos, weights, tm, conv_prev, kv_prev):
    (g_norm_mix, g_norm_ffn, w_in, w_conv, w_out_conv, g_norm_kv, w_kv, g_k, w_q, g_q, w_o,
     w_gu_dense, w_down_dense) = weights
    batch, seq, d = x.shape
    rows = batch * seq
    depth = g_norm_mix.shape[0]
    n_conv = w_in.shape[0]
    sample = kv_prev is not None
    tiles_per_seq = 1 if sample else seq // tm
    x = x.reshape(rows, d)
    cos, sin = _rope_tables(pos)
    if sample:
        cos, sin = jnp.tile(cos, (batch, 1)), jnp.tile(sin, (batch, 1))
    conv_state = []
    kv_state = None
    for layer in range(depth):
        mod = mods[layer]
        if layer == n_conv:
            kvs = _kv_proj(x, mod_kv, tiles_per_seq, g_norm_kv, w_kv, g_k, cos, sin, tm, not sample)
            k_new, v_new = kvs[:N_GROUPS], kvs[N_GROUPS:2 * N_GROUPS]
            k_att, v_att = list(k_new), list(v_new)
            for n, g_idx in enumerate(SPLIT_GROUPS if not sample else ()):
                k_att[g_idx], v_att[g_idx] = kvs[2 * N_GROUPS + 2 * n], kvs[2 * N_GROUPS + 2 * n + 1]
        if layer < n_conv:
            if sample:
                st = conv_prev[layer]
                zero = jnp.zeros((batch, seq - 2, d), F32)
                pa = jnp.concatenate([st[:, 1:2], jnp.zeros((batch, seq - 1, d), F32)], axis=1)
                pb = jnp.concatenate([st, zero], axis=1)
                prev = (seq, pa.reshape(rows, d), pb.reshape(rows, d))
            else:
                prev = None
            x, u_tail = _conv_layer(x, mod, tiles_per_seq, g_norm_mix[layer], w_in, w_conv, w_out_conv, layer,
                                    tm, prev)
            if sample:
                conv_state.append(u_tail.reshape(batch, seq, d)[:, seq - (CONV_W - 1):])
            else:
                conv_state.append(u_tail[:, F32_SUBLANES - (CONV_W - 1):])
        else:
            lb = layer - n_conv
            qs = _q_proj(x, mod, tiles_per_seq, g_norm_mix[layer], w_q, lb, g_q[lb], cos, sin, tm, not sample)
            if sample:
                o, new_ck, new_cv = _attn_sample(jnp.concatenate(qs, axis=1), k_new, v_new, kv_prev[0],
                                                 kv_prev[1], batch, seq, kv_state is None)
                if kv_state is None:
                    kv_state = (new_ck, new_cv)
                x = _o_proj(x, mod, o, w_o, lb)
            else:
                res = [_attn_prompt_group(qs[g], k_att[g], v_att[g], g, batch, seq) for g in range(N_GROUPS)]
                x = _merge_o(x, mod, tiles_per_seq, [r[0] for r in res], [r[1] for r in res], w_o, lb, tm)
        if layer % 2 == 0:
            x = _ffn_layer(x, mod, tiles_per_seq, g_norm_ffn[layer], w_gu_dense, w_down_dense, layer // 2, tm)
        else:
            tm_moe = tm if sample else min(SPARSE_TILE, seq)
            x = yield (x, mod, 1 if sample else seq // tm_moe, tm_moe), layer
    if not sample:
        shape4 = lambda a: a.reshape(batch, seq, HKV_G, HEAD_DIM)
        kv_state = ([shape4(k)[:, -min(w, seq):] for k, (w, _) in zip(k_new, DIL_GROUPS)],
                    [shape4(v)[:, -min(w, seq):] for v, (w, _) in zip(v_new, DIL_GROUPS)])
    return x.reshape(batch, seq, d), jnp.stack(conv_state, axis=0), kv_state


def _resume(trunk, x):
    try:
        return False, trunk.send(x)
    except StopIteration as done:
        return True, done.value


def kernel(x_prompt, x_sample, state_conv, cache_k_g0, cache_v_g0, cache_k_g1, cache_v_g1, cache_k_g2,
           cache_v_g2, c_prompt, c_sample, g_norm_mix, g_norm_ffn, w_ada, b_ada, w_in, w_conv, w_out_conv,
           g_norm_kv, w_ada_kv, b_ada_kv, w_kv, g_k, w_q, g_q, w_o, w_gu_dense, w_down_dense, w_router,
           b_router, w_gu_moe, w_down_moe):
    b_p, t_p, d = x_prompt.shape
    b_s, t_s, _ = x_sample.shape
    bf = lambda w: w.astype(BF16)
    weights = (g_norm_mix, g_norm_ffn, bf(w_in), w_conv, bf(w_out_conv), g_norm_kv, bf(w_kv), g_k, bf(w_q),
               g_q, bf(w_o), bf(w_gu_dense), bf(w_down_dense))

    c_all = jnp.concatenate([c_prompt, c_sample], axis=0)
    mods_all = _ada(c_all, w_ada, b_ada)
    mod_kv_all = _ada(c_all, w_ada_kv[None], b_ada_kv[None])[0]
    depth = w_ada.shape[0]
    mods_p = [mods_all[l, :b_p, None, :] for l in range(depth)]
    mods_s = [jnp.repeat(mods_all[l, b_p:], t_s, axis=0)[None] for l in range(depth)]
    mod_kv_p = mod_kv_all[:b_p, None, :]
    mod_kv_s = jnp.repeat(mod_kv_all[b_p:], t_s, axis=0)[None]

    pos_p = jnp.arange(t_p, dtype=jnp.int32)
    pos_s = PAST_LEN + jnp.arange(t_s, dtype=jnp.int32)
    kv_prev = ([cache_k_g0, cache_k_g1, cache_k_g2], [cache_v_g0, cache_v_g1, cache_v_g2])
    trunks = [_trunk(x_prompt, mods_p, mod_kv_p, pos_p, weights, min(PROMPT_TILE, t_p), None, None),
              _trunk(x_sample, mods_s, mod_kv_s, pos_s, weights, b_s * t_s, state_conv, kv_prev)]
    requests = [next(t) for t in trunks]
    results = None
    while results is None:
        layer = requests[0][1]
        xs = _moe_sparse_layer([r[0] for r in requests], g_norm_ffn[layer], w_router[layer // 2],
                               b_router[layer // 2], w_gu_moe, w_down_moe, layer // 2)
        steps = [_resume(t, x) for t, x in zip(trunks, xs)]
        if all(done for done, _ in steps):
            results = [value for _, value in steps]
        else:
            requests = [value for _, value in steps]
    (y_prompt, conv_p, (kp, vp)), (y_sample, conv_s, (ksn, vsn)) = results
    return (y_prompt, y_sample, conv_p, conv_s,
            kp[0], vp[0], kp[1], vp[1], kp[2], vp[2],
            ksn[0], vsn[0], ksn[1], vsn[1], ksn[2], vsn[2])
```
